```python
import jax, jax.numpy as jnp
from jax import lax
import numpy as np

D_MODEL = 2048
BATCH = 8
SEQ = 2048
DEPTH = 4

CHUNK = 64
Q_BLOCK = 128
N_MIXERS = 3
N_MLA_LAYERS = (DEPTH + 2) // 3
N_SB_LAYERS = (DEPTH + 1) // 3
N_CA_LAYERS = DEPTH // 3

MLA_HEADS = 16
MLA_Q_LORA = 512
MLA_KV_LORA = 512
MLA_NOPE = 128
MLA_ROPE = 64
MLA_V = 128
ROPE_THETA = 10000.0

SB_HEADS = 16
SB_HEAD_DIM = 128

CA_HEADS = 16
CA_HEAD_DIM = 128
CA_LEFT_CHUNKS = 8
REL_CLIP_LEFT = 128
REL_TABLE = REL_CLIP_LEFT + CHUNK

D_FF = 4 * D_MODEL

LN_EPS = 1e-5
RMS_EPS = 1e-6
DEEPNORM_ALPHA = (2.0 * DEPTH) ** 0.25
DEEPNORM_BETA = (8.0 * DEPTH) ** -0.25
NEG = -1e30

kernel_name = "hybrid_streaming_mla_stickbreak_chunkrel_deepnorm"


def layer_norm(x, g, b):
    xf = x.astype(jnp.float32)
    mu = jnp.mean(xf, -1, keepdims=True)
    var = jnp.mean(jnp.square(xf - mu), -1, keepdims=True)
    return ((xf - mu) * lax.rsqrt(var + LN_EPS) * g.astype(jnp.float32) + b.astype(jnp.float32)).astype(x.dtype)


def rms_norm(x, g):
    xf = x.astype(jnp.float32)
    return (xf * lax.rsqrt(jnp.mean(xf * xf, -1, keepdims=True) + RMS_EPS) * g.astype(jnp.float32)).astype(x.dtype)


def rope(x, pos):
    half = x.shape[-1] // 2
    inv = ROPE_THETA ** (-jnp.arange(half, dtype=jnp.float32) / half)
    ang = pos.astype(jnp.float32)[:, None] * inv[None, :]
    cos = jnp.cos(ang)[None, :, None, :]
    sin = jnp.sin(ang)[None, :, None, :]
    x1 = x[..., :half].astype(jnp.float32)
    x2 = x[..., half:].astype(jnp.float32)
    return jnp.concatenate([x1 * cos - x2 * sin, x2 * cos + x1 * sin], -1).astype(x.dtype)


def mla_mixer(x, w_down, q_norm_g, w_uq, kv_norm_g, w_ukv, w_o):
    B, S, _ = x.shape
    H = MLA_HEADS
    down = x @ w_down
    c_q, c_kv, k_rope = jnp.split(down, [MLA_Q_LORA, MLA_Q_LORA + MLA_KV_LORA], axis=-1)
    q = (rms_norm(c_q, q_norm_g) @ w_uq).reshape(B, S, H, MLA_NOPE + MLA_ROPE)
    kv = (rms_norm(c_kv, kv_norm_g) @ w_ukv).reshape(B, S, H, MLA_NOPE + MLA_V)
    q_nope, q_rope = q[..., :MLA_NOPE], q[..., MLA_NOPE:]
    k_nope, v = kv[..., :MLA_NOPE], kv[..., MLA_NOPE:]
    pos = jnp.arange(S)
    q_rope = rope(q_rope, pos)
    k_rope = rope(k_rope[:, :, None, :], pos)[:, :, 0, :]
    scale = (MLA_NOPE + MLA_ROPE) ** -0.5
    chunk_id = pos // CHUNK
    outs = []
    for qs in range(0, S, Q_BLOCK):
        ke = qs + Q_BLOCK
        s = (jnp.einsum('bqhd,bkhd->bhqk', q_nope[:, qs:ke], k_nope[:, :ke])
             + jnp.einsum('bqhr,bkr->bhqk', q_rope[:, qs:ke], k_rope[:, :ke])).astype(jnp.float32) * scale
        mask = chunk_id[None, :ke] <= chunk_id[qs:ke, None]
        p = jax.nn.softmax(jnp.where(mask, s, NEG), axis=-1).astype(v.dtype)
        outs.append(jnp.einsum('bhqk,bkhd->bqhd', p, v[:, :ke]))
    o = jnp.concatenate(outs, axis=1).reshape(B, S, H * MLA_V)
    return o @ w_o


def stick_breaking_mixer(x, w_qkv, w_o):
    B, S, _ = x.shape
    H, Dh = SB_HEADS, SB_HEAD_DIM
    q, k, v = jnp.split((x @ w_qkv).reshape(B, S, 3, H, Dh), 3, axis=2)
    q, k, v = q[:, :, 0], k[:, :, 0], v[:, :, 0]
    scale = Dh ** -0.5
    pos = jnp.arange(S)
    outs = []
    for qs in range(0, S, Q_BLOCK):
        ke = qs + Q_BLOCK
        z = jnp.einsum('bqhd,bkhd->bhqk', q[:, qs:ke], k[:, :ke]).astype(jnp.float32) * scale
        strict = pos[None, :ke] < pos[qs:ke, None]
        log_beta = jax.nn.log_sigmoid(z)
        log_1m = jnp.where(strict, jax.nn.log_sigmoid(-z), 0.0)
        log_surv = lax.cumsum(log_1m, axis=3, reverse=True) - log_1m
        a = jnp.where(strict, jnp.exp(log_beta + log_surv), 0.0).astype(v.dtype)
        outs.append(jnp.einsum('bhqk,bkhd->bqhd', a, v[:, :ke]))
    o = jnp.concatenate(outs, axis=1).reshape(B, S, H * Dh)
    return o @ w_o


def chunked_relpos_mixer(x, w_qkv, rel_bias, w_o):
    B, S, _ = x.shape
    H, Dh = CA_HEADS, CA_HEAD_DIM
    n_chunks = S // CHUNK
    pad = CA_LEFT_CHUNKS * CHUNK
    band = pad + CHUNK
    q, k, v = jnp.split((x @ w_qkv).reshape(B, S, 3, H, Dh), 3, axis=2)
    q, k, v = q[:, :, 0], k[:, :, 0], v[:, :, 0]
    qc = q.reshape(B, n_chunks, CHUNK, H, Dh)
    k_pad = jnp.pad(k, ((0, 0), (pad, 0), (0, 0), (0, 0)))
    v_pad = jnp.pad(v, ((0, 0), (pad, 0), (0, 0), (0, 0)))
    key_idx = jnp.arange(n_chunks)[:, None] * CHUNK + jnp.arange(band)[None, :]
    kb = k_pad[:, key_idx]
    vb = v_pad[:, key_idx]
    s = jnp.einsum('bcqhd,bckhd->bhcqk', qc, kb).astype(jnp.float32) * (Dh ** -0.5)
    rel = jnp.arange(band)[None, :] - pad - jnp.arange(CHUNK)[:, None]
    bias = rel_bias[jnp.clip(rel, -REL_CLIP_LEFT, CHUNK - 1) + REL_CLIP_LEFT]
    s = s + jnp.transpose(bias, (2, 0, 1)).astype(jnp.float32)[None, :, None]
    valid = (key_idx - pad) >= 0
    s = jnp.where(valid[None, None, :, None, :], s, NEG)
    p = jax.nn.softmax(s, axis=-1).astype(vb.dtype)
    o = jnp.einsum('bhcqk,bckhd->bcqhd', p, vb).reshape(B, S, H * Dh)
    return o @ w_o


def sq_relu_mlp(x, w_in, w_out):
    return jnp.square(jax.nn.relu(x @ w_in)) @ w_out


def _fwd_setup_inputs(seed: int = 0) -> dict:
    key = jax.random.key(seed)
    keys = iter(jax.random.split(key, 32))

    def nrm(shape, scale):
        return jax.random.normal(next(keys), shape, jnp.float32) * scale

    x = nrm((BATCH, SEQ, D_MODEL), 1.0)
    ln_mix_g = 1.0 + nrm((DEPTH, D_MODEL), 0.02)
    ln_mix_b = nrm((DEPTH, D_MODEL), 0.02)
    ln_ffn_g = 1.0 + nrm((DEPTH, D_MODEL), 0.02)
    ln_ffn_b = nrm((DEPTH, D_MODEL), 0.02)
    ffn_w_in = nrm((DEPTH, D_MODEL, D_FF), D_MODEL ** -0.5 * DEEPNORM_BETA)
    ffn_w_out = nrm((DEPTH, D_FF, D_MODEL), D_FF ** -0.5 * DEEPNORM_BETA)

    nA = N_MLA_LAYERS
    mla_w_down = nrm((nA, D_MODEL, MLA_Q_LORA + MLA_KV_LORA + MLA_ROPE), D_MODEL ** -0.5)
    mla_q_norm_g = 1.0 + nrm((nA, MLA_Q_LORA), 0.02)
    mla_w_uq = nrm((nA, MLA_Q_LORA, MLA_HEADS * (MLA_NOPE + MLA_ROPE)), MLA_Q_LORA ** -0.5)
    mla_kv_norm_g = 1.0 + nrm((nA, MLA_KV_LORA), 0.02)
    w_uk = nrm((nA, MLA_KV_LORA, MLA_HEADS, MLA_NOPE), MLA_KV_LORA ** -0.5)
    w_uv = nrm((nA, MLA_KV_LORA, MLA_HEADS, MLA_V), MLA_KV_LORA ** -0.5 * DEEPNORM_BETA)
    mla_w_ukv = jnp.concatenate([w_uk, w_uv], -1).reshape(nA, MLA_KV_LORA, MLA_HEADS * (MLA_NOPE + MLA_V))
    mla_w_o = nrm((nA, MLA_HEADS * MLA_V, D_MODEL), (MLA_HEADS * MLA_V) ** -0.5 * DEEPNORM_BETA)

    nB = N_SB_LAYERS
    sb_qk = nrm((nB, D_MODEL, 2, SB_HEADS * SB_HEAD_DIM), D_MODEL ** -0.5)
    sb_v = nrm((nB, D_MODEL, 1, SB_HEADS * SB_HEAD_DIM), D_MODEL ** -0.5 * DEEPNORM_BETA)
    sb_w_qkv = jnp.concatenate([sb_qk, sb_v], 2).reshape(nB, D_MODEL, 3 * SB_HEADS * SB_HEAD_DIM)
    sb_w_o = nrm((nB, SB_HEADS * SB_HEAD_DIM, D_MODEL), (SB_HEADS * SB_HEAD_DIM) ** -0.5 * DEEPNORM_BETA)

    nC = N_CA_LAYERS
    ca_qk = nrm((nC, D_MODEL, 2, CA_HEADS * CA_HEAD_DIM), D_MODEL ** -0.5)
    ca_v = nrm((nC, D_MODEL, 1, CA_HEADS * CA_HEAD_DIM), D_MODEL ** -0.5 * DEEPNORM_BETA)
    ca_w_qkv = jnp.concatenate([ca_qk, ca_v], 2).reshape(nC, D_MODEL, 3 * CA_HEADS * CA_HEAD_DIM)
    ca_rel_bias = nrm((nC, REL_TABLE, CA_HEADS), 0.5)
    ca_w_o = nrm((nC, CA_HEADS * CA_HEAD_DIM, D_MODEL), (CA_HEADS * CA_HEAD_DIM) ** -0.5 * DEEPNORM_BETA)

    return {"x": x, "ln_mix_g": ln_mix_g, "ln_mix_b": ln_mix_b, "ln_ffn_g": ln_ffn_g, "ln_ffn_b": ln_ffn_b,
            "ffn_w_in": ffn_w_in, "ffn_w_out": ffn_w_out,
            "mla_w_down": mla_w_down, "mla_q_norm_g": mla_q_norm_g, "mla_w_uq": mla_w_uq,
            "mla_kv_norm_g": mla_kv_norm_g, "mla_w_ukv": mla_w_ukv, "mla_w_o": mla_w_o,
            "sb_w_qkv": sb_w_qkv, "sb_w_o": sb_w_o,
            "ca_w_qkv": ca_w_qkv, "ca_rel_bias": ca_rel_bias, "ca_w_o": ca_w_o}


def _fwd_reference(x, ln_mix_g, ln_mix_b, ln_ffn_g, ln_ffn_b, ffn_w_in, ffn_w_out,
              mla_w_down, mla_q_norm_g, mla_w_uq, mla_kv_norm_g, mla_w_ukv, mla_w_o,
              sb_w_qkv, sb_w_o, ca_w_qkv, ca_rel_bias, ca_w_o):
    h = x
    for i in range(DEPTH):
        kind = i % N_MIXERS
        slot = i // N_MIXERS
        if kind == 0:
            m = mla_mixer(h, mla_w_down[slot], mla_q_norm_g[slot], mla_w_uq[slot],
                          mla_kv_norm_g[slot], mla_w_ukv[slot], mla_w_o[slot])
        elif kind == 1:
            m = stick_breaking_mixer(h, sb_w_qkv[slot], sb_w_o[slot])
        else:
            m = chunked_relpos_mixer(h, ca_w_qkv[slot], ca_rel_bias[slot], ca_w_o[slot])
        h = layer_norm(DEEPNORM_ALPHA * h + m, ln_mix_g[i], ln_mix_b[i])
        h = layer_norm(DEEPNORM_ALPHA * h + sq_relu_mlp(h, ffn_w_in[i], ffn_w_out[i]),
                       ln_ffn_g[i], ln_ffn_b[i])
    return h


import jax as _jax
import jax.numpy as _jnp

TWIN_FORMAT = 'train_step'
FWD_PARAMS = ['x', 'ln_mix_g', 'ln_mix_b', 'ln_ffn_g', 'ln_ffn_b', 'ffn_w_in', 'ffn_w_out', 'mla_w_down', 'mla_q_norm_g', 'mla_w_uq', 'mla_kv_norm_g', 'mla_w_ukv', 'mla_w_o', 'sb_w_qkv', 'sb_w_o', 'ca_w_qkv', 'ca_rel_bias', 'ca_w_o']
TWIN_WEIGHTS = ['ln_mix_g', 'ln_mix_b', 'ln_ffn_g', 'ln_ffn_b', 'ffn_w_in', 'ffn_w_out', 'mla_w_down', 'mla_q_norm_g', 'mla_w_uq', 'mla_kv_norm_g', 'mla_w_ukv', 'mla_w_o', 'sb_w_qkv', 'sb_w_o', 'ca_w_qkv', 'ca_rel_bias', 'ca_w_o']
TWIN_DIFF_INPUT = 'x'
TWIN_INPUTS = ['x', 'ln_mix_g', 'ln_mix_b', 'ln_ffn_g', 'ln_ffn_b', 'ffn_w_in', 'ffn_w_out', 'mla_w_down', 'mla_q_norm_g', 'mla_w_uq', 'mla_kv_norm_g', 'mla_w_ukv', 'mla_w_o', 'sb_w_qkv', 'sb_w_o', 'ca_w_qkv', 'ca_rel_bias', 'ca_w_o', 'loss_target', 'm_ln_mix_g', 'm_ln_mix_b', 'm_ln_ffn_g', 'm_ln_ffn_b', 'm_ffn_w_in', 'm_ffn_w_out', 'm_mla_w_down', 'm_mla_q_norm_g', 'm_mla_w_uq', 'm_mla_kv_norm_g', 'm_mla_w_ukv', 'm_mla_w_o', 'm_sb_w_qkv', 'm_sb_w_o', 'm_ca_w_qkv', 'm_ca_rel_bias', 'm_ca_w_o', 'v_ln_mix_g', 'v_ln_mix_b', 'v_ln_ffn_g', 'v_ln_ffn_b', 'v_ffn_w_in', 'v_ffn_w_out', 'v_mla_w_down', 'v_mla_q_norm_g', 'v_mla_w_uq', 'v_mla_kv_norm_g', 'v_mla_w_ukv', 'v_mla_w_o', 'v_sb_w_qkv', 'v_sb_w_o', 'v_ca_w_qkv', 'v_ca_rel_bias', 'v_ca_w_o']
TWIN_OUTPUTS = ['loss', 'grad_x', 'grad_ln_mix_g', 'grad_ln_mix_b', 'grad_ln_ffn_g', 'grad_ln_ffn_b', 'grad_ffn_w_in', 'grad_ffn_w_out', 'grad_mla_w_down', 'grad_mla_q_norm_g', 'grad_mla_w_uq', 'grad_mla_kv_norm_g', 'grad_mla_w_ukv', 'grad_mla_w_o', 'grad_sb_w_qkv', 'grad_sb_w_o', 'grad_ca_w_qkv', 'grad_ca_rel_bias', 'grad_ca_w_o', 'delta_ln_mix_g', 'delta_ln_mix_b', 'delta_ln_ffn_g', 'delta_ln_ffn_b', 'delta_ffn_w_in', 'delta_ffn_w_out', 'delta_mla_w_down', 'delta_mla_q_norm_g', 'delta_mla_w_uq', 'delta_mla_kv_norm_g', 'delta_mla_w_ukv', 'delta_mla_w_o', 'delta_sb_w_qkv', 'delta_sb_w_o', 'delta_ca_w_qkv', 'delta_ca_rel_bias', 'delta_ca_w_o', 'new_m_ln_mix_g', 'new_m_ln_mix_b', 'new_m_ln_ffn_g', 'new_m_ln_ffn_b', 'new_m_ffn_w_in', 'new_m_ffn_w_out', 'new_m_mla_w_down', 'new_m_mla_q_norm_g', 'new_m_mla_w_uq', 'new_m_mla_kv_norm_g', 'new_m_mla_w_ukv', 'new_m_mla_w_o', 'new_m_sb_w_qkv', 'new_m_sb_w_o', 'new_m_ca_w_qkv', 'new_m_ca_rel_bias', 'new_m_ca_w_o', 'new_v_ln_mix_g', 'new_v_ln_mix_b', 'new_v_ln_ffn_g', 'new_v_ln_ffn_b', 'new_v_ffn_w_in', 'new_v_ffn_w_out', 'new_v_mla_w_down', 'new_v_mla_q_norm_g', 'new_v_mla_w_uq', 'new_v_mla_kv_norm_g', 'new_v_mla_w_ukv', 'new_v_mla_w_o', 'new_v_sb_w_qkv', 'new_v_sb_w_o', 'new_v_ca_w_qkv', 'new_v_ca_rel_bias', 'new_v_ca_w_o']
TWIN_LEAF_KINDS = {'loss': 'loss', 'grad_x': 'grad_x', 'grad_ln_mix_g': 'grad_w', 'grad_ln_mix_b': 'grad_w', 'grad_ln_ffn_g': 'grad_w', 'grad_ln_ffn_b': 'grad_w', 'grad_ffn_w_in': 'grad_w', 'grad_ffn_w_out': 'grad_w', 'grad_mla_w_down': 'grad_w', 'grad_mla_q_norm_g': 'grad_w', 'grad_mla_w_uq': 'grad_w', 'grad_mla_kv_norm_g': 'grad_w', 'grad_mla_w_ukv': 'grad_w', 'grad_mla_w_o': 'grad_w', 'grad_sb_w_qkv': 'grad_w', 'grad_sb_w_o': 'grad_w', 'grad_ca_w_qkv': 'grad_w', 'grad_ca_rel_bias': 'grad_w', 'grad_ca_w_o': 'grad_w', 'delta_ln_mix_g': 'delta_w', 'delta_ln_mix_b': 'delta_w', 'delta_ln_ffn_g': 'delta_w', 'delta_ln_ffn_b': 'delta_w', 'delta_ffn_w_in': 'delta_w', 'delta_ffn_w_out': 'delta_w', 'delta_mla_w_down': 'delta_w', 'delta_mla_q_norm_g': 'delta_w', 'delta_mla_w_uq': 'delta_w', 'delta_mla_kv_norm_g': 'delta_w', 'delta_mla_w_ukv': 'delta_w', 'delta_mla_w_o': 'delta_w', 'delta_sb_w_qkv': 'delta_w', 'delta_sb_w_o': 'delta_w', 'delta_ca_w_qkv': 'delta_w', 'delta_ca_rel_bias': 'delta_w', 'delta_ca_w_o': 'delta_w', 'new_m_ln_mix_g': 'new_m', 'new_m_ln_mix_b': 'new_m', 'new_m_ln_ffn_g': 'new_m', 'new_m_ln_ffn_b': 'new_m', 'new_m_ffn_w_in': 'new_m', 'new_m_ffn_w_out': 'new_m', 'new_m_mla_w_down': 'new_m', 'new_m_mla_q_norm_g': 'new_m', 'new_m_mla_w_uq': 'new_m', 'new_m_mla_kv_norm_g': 'new_m', 'new_m_mla_w_ukv': 'new_m', 'new_m_mla_w_o': 'new_m', 'new_m_sb_w_qkv': 'new_m', 'new_m_sb_w_o': 'new_m', 'new_m_ca_w_qkv': 'new_m', 'new_m_ca_rel_bias': 'new_m', 'new_m_ca_w_o': 'new_m', 'new_v_ln_mix_g': 'new_v', 'new_v_ln_mix_b': 'new_v', 'new_v_ln_ffn_g': 'new_v', 'new_v_ln_ffn_b': 'new_v', 'new_v_ffn_w_in': 'new_v', 'new_v_ffn_w_out': 'new_v', 'new_v_mla_w_down': 'new_v', 'new_v_mla_q_norm_g': 'new_v', 'new_v_mla_w_uq': 'new_v', 'new_v_mla_kv_norm_g': 'new_v', 'new_v_mla_w_ukv': 'new_v', 'new_v_mla_w_o': 'new_v', 'new_v_sb_w_qkv': 'new_v', 'new_v_sb_w_o': 'new_v', 'new_v_ca_w_qkv': 'new_v', 'new_v_ca_rel_bias': 'new_v', 'new_v_ca_w_o': 'new_v'}


def _forward(args):
    return _fwd_reference(*[args[k] for k in FWD_PARAMS])


def _output_shape():
    out = _jax.eval_shape(lambda: _forward(_fwd_setup_inputs(0)))
    return out.shape, out.dtype

N_MICROBATCH = 1
ADAM_LR = 0.001
ADAM_B1 = 0.9
ADAM_B2 = 0.999
ADAM_EPS = 1e-08
ADAM_WD = 0.01
ADAM_STEP = 10
PER_EXAMPLE_BATCH_AXIS = {'x': 0, 'loss_target': 0}
SHARED_INPUTS = []
_WEIGHT_DTYPES = {'ln_mix_g': _jnp.float32, 'ln_mix_b': _jnp.float32, 'ln_ffn_g': _jnp.float32, 'ln_ffn_b': _jnp.float32, 'ffn_w_in': _jnp.float32, 'ffn_w_out': _jnp.float32, 'mla_w_down': _jnp.float32, 'mla_q_norm_g': _jnp.float32, 'mla_w_uq': _jnp.float32, 'mla_kv_norm_g': _jnp.float32, 'mla_w_ukv': _jnp.float32, 'mla_w_o': _jnp.float32, 'sb_w_qkv': _jnp.float32, 'sb_w_o': _jnp.float32, 'ca_w_qkv': _jnp.float32, 'ca_rel_bias': _jnp.float32, 'ca_w_o': _jnp.float32}
MOMENT_SCALE = {'ln_mix_g': 2.935530e-01, 'ln_mix_b': 1.466333e-01, 'ln_ffn_g': 4.025849e+00, 'ln_ffn_b': 2.930613e-01, 'ffn_w_in': 4.648796e-03, 'ffn_w_out': 1.008095e-02, 'mla_w_down': 2.293407e-03, 'mla_q_norm_g': 1.929936e-03, 'mla_w_uq': 7.682837e-04, 'mla_kv_norm_g': 2.782929e-03, 'mla_w_ukv': 1.858001e-03, 'mla_w_o': 2.508249e-03, 'sb_w_qkv': 6.056869e-03, 'sb_w_o': 1.013651e-02, 'ca_w_qkv': 1.750579e-03, 'ca_rel_bias': 7.585417e-04, 'ca_w_o': 2.686715e-03}


def _to_microbatches(a, axis):
    t = _jnp.moveaxis(a, axis, 0)
    t = t.reshape((N_MICROBATCH, t.shape[0] // N_MICROBATCH) + t.shape[1:])
    return _jnp.moveaxis(t, 1, axis + 1)


def setup_inputs(seed: int = 0) -> dict:
    inp = _fwd_setup_inputs(seed)
    key = _jax.random.fold_in(_jax.random.key(seed), 7919)
    shape, _ = _output_shape()
    out = dict(inp)
    out["loss_target"] = _jax.random.normal(_jax.random.fold_in(key, 0), shape, _jnp.float32)
    for i, name in enumerate(TWIN_WEIGHTS):
        w = inp[name].astype(_jnp.float32)
        if MOMENT_SCALE is None:
            s = _jnp.sqrt(_jnp.mean(_jnp.square(w)) + 1e-30)
        else:
            s = MOMENT_SCALE[name]
        km, kv = _jax.random.split(_jax.random.fold_in(key, i + 1))
        out[name] = w
        out["m_" + name] = s * _jax.random.normal(km, w.shape, _jnp.float32)
        out["v_" + name] = (s * s) * _jax.random.uniform(kv, w.shape, _jnp.float32, 0.5, 1.5)
    if N_MICROBATCH > 1:
        for name, axis in PER_EXAMPLE_BATCH_AXIS.items():
            out[name] = _to_microbatches(out[name], axis)
    return {'x': out['x'], 'ln_mix_g': out['ln_mix_g'], 'ln_mix_b': out['ln_mix_b'], 'ln_ffn_g': out['ln_ffn_g'], 'ln_ffn_b': out['ln_ffn_b'], 'ffn_w_in': out['ffn_w_in'], 'ffn_w_out': out['ffn_w_out'], 'mla_w_down': out['mla_w_down'], 'mla_q_norm_g': out['mla_q_norm_g'], 'mla_w_uq': out['mla_w_uq'], 'mla_kv_norm_g': out['mla_kv_norm_g'], 'mla_w_ukv': out['mla_w_ukv'], 'mla_w_o': out['mla_w_o'], 'sb_w_qkv': out['sb_w_qkv'], 'sb_w_o': out['sb_w_o'], 'ca_w_qkv': out['ca_w_qkv'], 'ca_rel_bias': out['ca_rel_bias'], 'ca_w_o': out['ca_w_o'], 'loss_target': out['loss_target'], 'm_ln_mix_g': out['m_ln_mix_g'], 'm_ln_mix_b': out['m_ln_mix_b'], 'm_ln_ffn_g': out['m_ln_ffn_g'], 'm_ln_ffn_b': out['m_ln_ffn_b'], 'm_ffn_w_in': out['m_ffn_w_in'], 'm_ffn_w_out': out['m_ffn_w_out'], 'm_mla_w_down': out['m_mla_w_down'], 'm_mla_q_norm_g': out['m_mla_q_norm_g'], 'm_mla_w_uq': out['m_mla_w_uq'], 'm_mla_kv_norm_g': out['m_mla_kv_norm_g'], 'm_mla_w_ukv': out['m_mla_w_ukv'], 'm_mla_w_o': out['m_mla_w_o'], 'm_sb_w_qkv': out['m_sb_w_qkv'], 'm_sb_w_o': out['m_sb_w_o'], 'm_ca_w_qkv': out['m_ca_w_qkv'], 'm_ca_rel_bias': out['m_ca_rel_bias'], 'm_ca_w_o': out['m_ca_w_o'], 'v_ln_mix_g': out['v_ln_mix_g'], 'v_ln_mix_b': out['v_ln_mix_b'], 'v_ln_ffn_g': out['v_ln_ffn_g'], 'v_ln_ffn_b': out['v_ln_ffn_b'], 'v_ffn_w_in': out['v_ffn_w_in'], 'v_ffn_w_out': out['v_ffn_w_out'], 'v_mla_w_down': out['v_mla_w_down'], 'v_mla_q_norm_g': out['v_mla_q_norm_g'], 'v_mla_w_uq': out['v_mla_w_uq'], 'v_mla_kv_norm_g': out['v_mla_kv_norm_g'], 'v_mla_w_ukv': out['v_mla_w_ukv'], 'v_mla_w_o': out['v_mla_w_o'], 'v_sb_w_qkv': out['v_sb_w_qkv'], 'v_sb_w_o': out['v_sb_w_o'], 'v_ca_w_qkv': out['v_ca_w_qkv'], 'v_ca_rel_bias': out['v_ca_rel_bias'], 'v_ca_w_o': out['v_ca_w_o']}


def _loss(weights, diff, rest, loss_target):
    with _jax.named_scope("forward"):
        args = {**rest, TWIN_DIFF_INPUT: diff, **{k: w.astype(_WEIGHT_DTYPES[k]) for k, w in weights.items()}}
        y = _forward(args)
    with _jax.named_scope("loss_head"):
        err = _jnp.square(y.astype(_jnp.float32) - loss_target)
        return 0.5 * _jnp.sum(_jnp.mean(err, axis=-1)) if err.ndim else 0.5 * err


def _adamw(w, g, m, v):
    m = ADAM_B1 * m + (1.0 - ADAM_B1) * g
    v = ADAM_B2 * v + (1.0 - ADAM_B2) * _jnp.square(g)
    m_hat = m / (1.0 - ADAM_B1 ** ADAM_STEP)
    v_hat = v / (1.0 - ADAM_B2 ** ADAM_STEP)
    delta = -ADAM_LR * (m_hat / (_jnp.sqrt(v_hat) + ADAM_EPS) + ADAM_WD * w)
    return delta, m, v


def reference(x, ln_mix_g, ln_mix_b, ln_ffn_g, ln_ffn_b, ffn_w_in, ffn_w_out, mla_w_down, mla_q_norm_g, mla_w_uq, mla_kv_norm_g, mla_w_ukv, mla_w_o, sb_w_qkv, sb_w_o, ca_w_qkv, ca_rel_bias, ca_w_o, loss_target, m_ln_mix_g, m_ln_mix_b, m_ln_ffn_g, m_ln_ffn_b, m_ffn_w_in, m_ffn_w_out, m_mla_w_down, m_mla_q_norm_g, m_mla_w_uq, m_mla_kv_norm_g, m_mla_w_ukv, m_mla_w_o, m_sb_w_qkv, m_sb_w_o, m_ca_w_qkv, m_ca_rel_bias, m_ca_w_o, v_ln_mix_g, v_ln_mix_b, v_ln_ffn_g, v_ln_ffn_b, v_ffn_w_in, v_ffn_w_out, v_mla_w_down, v_mla_q_norm_g, v_mla_w_uq, v_mla_kv_norm_g, v_mla_w_ukv, v_mla_w_o, v_sb_w_qkv, v_sb_w_o, v_ca_w_qkv, v_ca_rel_bias, v_ca_w_o):
    given = dict(x=x, ln_mix_g=ln_mix_g, ln_mix_b=ln_mix_b, ln_ffn_g=ln_ffn_g, ln_ffn_b=ln_ffn_b, ffn_w_in=ffn_w_in, ffn_w_out=ffn_w_out, mla_w_down=mla_w_down, mla_q_norm_g=mla_q_norm_g, mla_w_uq=mla_w_uq, mla_kv_norm_g=mla_kv_norm_g, mla_w_ukv=mla_w_ukv, mla_w_o=mla_w_o, sb_w_qkv=sb_w_qkv, sb_w_o=sb_w_o, ca_w_qkv=ca_w_qkv, ca_rel_bias=ca_rel_bias, ca_w_o=ca_w_o, loss_target=loss_target, m_ln_mix_g=m_ln_mix_g, m_ln_mix_b=m_ln_mix_b, m_ln_ffn_g=m_ln_ffn_g, m_ln_ffn_b=m_ln_ffn_b, m_ffn_w_in=m_ffn_w_in, m_ffn_w_out=m_ffn_w_out, m_mla_w_down=m_mla_w_down, m_mla_q_norm_g=m_mla_q_norm_g, m_mla_w_uq=m_mla_w_uq, m_mla_kv_norm_g=m_mla_kv_norm_g, m_mla_w_ukv=m_mla_w_ukv, m_mla_w_o=m_mla_w_o, m_sb_w_qkv=m_sb_w_qkv, m_sb_w_o=m_sb_w_o, m_ca_w_qkv=m_ca_w_qkv, m_ca_rel_bias=m_ca_rel_bias, m_ca_w_o=m_ca_w_o, v_ln_mix_g=v_ln_mix_g, v_ln_mix_b=v_ln_mix_b, v_ln_ffn_g=v_ln_ffn_g, v_ln_ffn_b=v_ln_ffn_b, v_ffn_w_in=v_ffn_w_in, v_ffn_w_out=v_ffn_w_out, v_mla_w_down=v_mla_w_down, v_mla_q_norm_g=v_mla_q_norm_g, v_mla_w_uq=v_mla_w_uq, v_mla_kv_norm_g=v_mla_kv_norm_g, v_mla_w_ukv=v_mla_w_ukv, v_mla_w_o=v_mla_w_o, v_sb_w_qkv=v_sb_w_qkv, v_sb_w_o=v_sb_w_o, v_ca_w_qkv=v_ca_w_qkv, v_ca_rel_bias=v_ca_rel_bias, v_ca_w_o=v_ca_w_o)
    weights = {n: given[n] for n in TWIN_WEIGHTS}
    shared = {n: given[n] for n in SHARED_INPUTS}
    per_example = {n: given[n] for n in ['x']}
    grad_fn = _jax.value_and_grad(_loss, argnums=(0, 1))

    def one_microbatch(ex, loss_target):
        ex = dict(ex)
        diff = ex.pop(TWIN_DIFF_INPUT)
        return grad_fn(weights, diff, {**shared, **ex}, loss_target)

    if N_MICROBATCH == 1:
        loss, (grad_w, grad_x) = one_microbatch(per_example, given["loss_target"])
    else:
        def body(carry, xs):
            loss_sum, grad_sum = carry
            l_k, (gw_k, gx_k) = one_microbatch(xs[0], xs[1])
            with _jax.named_scope("update"):
                return (loss_sum + l_k, _jax.tree.map(_jnp.add, grad_sum, gw_k)), gx_k

        init = (_jnp.zeros((), _jnp.float32), _jax.tree.map(_jnp.zeros_like, weights))
        (loss, grad_w), grad_x = _jax.lax.scan(body, init, (per_example, given["loss_target"]))
    with _jax.named_scope("update"):
        delta_w, new_m, new_v = {}, {}, {}
        for n in TWIN_WEIGHTS:
            delta_w[n], new_m[n], new_v[n] = _adamw(weights[n], grad_w[n], given["m_" + n], given["v_" + n])
    return (loss, grad_x, *[grad_w[n] for n in TWIN_WEIGHTS], *[delta_w[n] for n in TWIN_WEIGHTS],
            *[new_m[n] for n in TWIN_WEIGHTS], *[new_v[n] for n in TWIN_WEIGHTS])
```

```python
import functools

import numpy as np
import jax
import jax.numpy as jnp
from jax import lax
from jax.experimental import pallas as pl
from jax.experimental.pallas import tpu as pltpu

F32, BF16 = jnp.float32, jnp.bfloat16
MXU_DTYPE = BF16
WIRE_DTYPE = BF16

DEPTH = 4
HEADS = 16
HEAD_DIM = 128
CHUNK_SHIFT = 6
TQ = 128
MLA_ROPE = 64
MLA_QK_DIM = 192
HEAD_PAD = 256
CA_LEFT_CHUNKS = 8
CA_LEFT_BLOCKS = (CA_LEFT_CHUNKS << CHUNK_SHIFT) // TQ
REL_CLIP_LEFT = 128
REL_TABLE = 192
ROPE_THETA = 10000.0
LN_EPS = 1e-5
RMS_EPS = 1e-6
ALPHA = (2.0 * DEPTH) ** 0.25
NEG = -1e30
ADAM_LR, ADAM_B1, ADAM_B2, ADAM_EPS, ADAM_WD, ADAM_STEP = 0.001, 0.9, 0.999, 1e-08, 0.01, 10
N_CHIPS = 4
N_DEV = 8
VMEM_LIMIT = 48 << 20
MESH = pl.DeviceIdType.MESH
ANY = pl.BlockSpec(memory_space=pl.ANY)
VMEM_SPEC = pl.BlockSpec(memory_space=pltpu.VMEM)

NN = (((1,), (0,)), ((), ()))
NT = (((1,), (1,)), ((), ()))
TN = (((0,), (0,)), ((), ()))


def _dot(a, b, dims=NN):
    return lax.dot_general(a, b, dims, preferred_element_type=F32)


def _exact_dot(x, u):
    hi = x.astype(BF16)
    r1 = x - hi.astype(F32)
    mid = r1.astype(BF16)
    lo = (r1 - mid.astype(F32)).astype(BF16)
    return _dot(hi, u) + _dot(mid, u) + _dot(lo, u)


def _params(*sem):
    return pltpu.CompilerParams(dimension_semantics=sem, vmem_limit_bytes=VMEM_LIMIT)


def _tile(n, pref):
    for t in (1024, 768, 512, 384, 256, 128):
        if t <= pref and n % t == 0:
            return t
    return n


def _mm(name, a, b, extras, *, grid, a_spec, b_spec, extra_specs, out_specs, out_shape, dims,
        epilogue, acc_shape, aliases=None):
    nk = grid[2]
    n_ex = len(extras)

    def body(*refs):
        a_ref, b_ref = refs[:2]
        ex = refs[2:2 + n_ex]
        outs = refs[2 + n_ex:-1]
        acc = refs[-1]
        k = pl.program_id(2)

        @pl.when(k == 0)
        def _():
            acc[...] = jnp.zeros_like(acc)

        acc[...] += lax.dot_general(a_ref[...].astype(MXU_DTYPE), b_ref[...].astype(MXU_DTYPE), dims,
                                    preferred_element_type=F32)

        @pl.when(k == nk - 1)
        def _():
            epilogue(acc[...], ex, outs)

    return pl.pallas_call(
        body, grid=grid, in_specs=[a_spec, b_spec, *extra_specs], out_specs=out_specs, out_shape=out_shape,
        scratch_shapes=[pltpu.VMEM(acc_shape, F32)], name=name, input_output_aliases=aliases or {},
        compiler_params=_params("parallel", "parallel", "arbitrary"))(a, b, *extras)


def _epi_store(acc, ex, outs):
    outs[0][...] = acc.astype(outs[0].dtype)


def _epi_relu2(acc, ex, outs):
    outs[0][...] = acc
    r = jnp.maximum(acc, 0.0)
    outs[1][...] = (r * r).astype(outs[1].dtype)


def _epi_drelu2(acc, ex, outs):
    outs[0][...] = (acc * (2.0 * jnp.maximum(ex[0][...], 0.0))).astype(outs[0].dtype)


def _epi_residual(acc, ex, outs):
    outs[0][...] = acc + ALPHA * ex[0][...]


def _fwd_col(name, x, wg, l, rows, cols, dtype, epilogue=_epi_store, n_out=1, dtypes=None):
    m = x.shape[0]
    tm, tn, tk = _tile(m, 1024), _tile(cols, 1024), _tile(rows, 512)
    nps, kt = cols // tn, rows // tk
    dtypes = dtypes or (dtype,)
    out = pl.BlockSpec((tm, tn), lambda i, j, k: (i, j))
    res = _mm(name, x, wg, (), grid=(m // tm, N_CHIPS * nps, kt),
              a_spec=pl.BlockSpec((tm, tk), lambda i, j, k: (i, k)),
              b_spec=pl.BlockSpec((None, tk, tn), lambda i, j, k: (j // nps, l * kt + k, j % nps)),
              extra_specs=(), out_specs=[out] * len(dtypes),
              out_shape=[jax.ShapeDtypeStruct((m, N_CHIPS * cols), d) for d in dtypes],
              dims=NN, epilogue=epilogue, acc_shape=(tm, tn))
    return res if len(dtypes) > 1 else res[0]


def _fwd_row(name, x, wg, l, rows, cols, dtype):
    m = x.shape[0]
    tm, tn, tk = _tile(m, 1024), _tile(cols, 1024), _tile(rows, 512)
    kps = rows // tk
    return _mm(name, x, wg, (), grid=(m // tm, cols // tn, N_CHIPS * kps),
               a_spec=pl.BlockSpec((tm, tk), lambda i, j, k: (i, k)),
               b_spec=pl.BlockSpec((None, tk, tn), lambda i, j, k: (k // kps, l * kps + k % kps, j)),
               extra_specs=(), out_specs=[pl.BlockSpec((tm, tn), lambda i, j, k: (i, j))],
               out_shape=[jax.ShapeDtypeStruct((m, cols), dtype)],
               dims=NN, epilogue=_epi_store, acc_shape=(tm, tn))[0]


def _dx_col(name, dy, wg, l, rows, cols, dtype, epilogue=_epi_store, extra=None):
    m = dy.shape[0]
    tm, tn, tk = _tile(m, 1024), _tile(rows, 1024), _tile(cols, 512)
    kps, nt = cols // tk, rows // tn
    tile = pl.BlockSpec((tm, tn), lambda i, j, k: (i, j))
    return _mm(name, dy, wg, () if extra is None else (extra,), grid=(m // tm, nt, N_CHIPS * kps),
               a_spec=pl.BlockSpec((tm, tk), lambda i, j, k: (i, k)),
               b_spec=pl.BlockSpec((None, tn, tk), lambda i, j, k: (k // kps, l * nt + j, k % kps)),
               extra_specs=() if extra is None else (tile,), out_specs=[tile],
               out_shape=[jax.ShapeDtypeStruct((m, rows), dtype)],
               dims=NT, epilogue=epilogue, acc_shape=(tm, tn))[0]


def _dx_row(name, dy, wg, l, rows, cols, dtype, epilogue=_epi_store, extra=None):
    m = dy.shape[0]
    tm, tn, tk = _tile(m, 1024), _tile(rows, 1024), _tile(cols, 512)
    nps = rows // tn
    tile = pl.BlockSpec((tm, tn), lambda i, j, k: (i, j))
    return _mm(name, dy, wg, () if extra is None else (extra,), grid=(m // tm, N_CHIPS * nps, cols // tk),
               a_spec=pl.BlockSpec((tm, tk), lambda i, j, k: (i, k)),
               b_spec=pl.BlockSpec((None, tn, tk), lambda i, j, k: (j // nps, l * nps + j % nps, k)),
               extra_specs=() if extra is None else (tile,), out_specs=[tile],
               out_shape=[jax.ShapeDtypeStruct((m, N_CHIPS * rows), dtype)],
               dims=NT, epilogue=epilogue, acc_shape=(tm, tn))[0]


def _dw(name, x, dy, gbuf, l, layers, rows, cols, col_sharded):
    s_tok = x.shape[0]
    tm, tn, tk = _tile(rows, 1024), _tile(cols, 1024), _tile(s_tok, 512)
    mt, nps = rows // tm, cols // tn
    if col_sharded:
        grid = (mt, N_CHIPS * nps, s_tok // tk)
        out = pl.BlockSpec((None, tm, tn), lambda i, j, k: (j // nps, l * mt + i, j % nps))
    else:
        grid = (N_CHIPS * mt, nps, s_tok // tk)
        out = pl.BlockSpec((None, tm, tn), lambda i, j, k: (i // mt, l * mt + i % mt, j))
    return _mm(name, x, dy, () if gbuf is None else (gbuf,), grid=grid,
               a_spec=pl.BlockSpec((tk, tm), lambda i, j, k: (k, i)),
               b_spec=pl.BlockSpec((tk, tn), lambda i, j, k: (k, j)),
               extra_specs=() if gbuf is None else (ANY,), out_specs=[out],
               out_shape=[jax.ShapeDtypeStruct((N_CHIPS, layers * rows, cols), WIRE_DTYPE)],
               dims=TN, epilogue=_epi_store, acc_shape=(tm, tn),
               aliases=None if gbuf is None else {2: 0})[0]


def _ln_fwd(name, h, m, g, b):
    s, d = h.shape
    tm = _tile(s, 256)
    row = pl.BlockSpec((tm, d), lambda i: (i, 0))
    vec = pl.BlockSpec((1, d), lambda i: (0, 0))

    def body(h_ref, m_ref, g_ref, b_ref, y_ref, xh_ref, r_ref):
        z = ALPHA * h_ref[...] + m_ref[...]
        mu = jnp.mean(z, -1, keepdims=True)
        zc = z - mu
        r = lax.rsqrt(jnp.mean(zc * zc, -1, keepdims=True) + LN_EPS)
        xh = zc * r
        xh_ref[...] = xh
        r_ref[...] = r
        y_ref[...] = xh * g_ref[...] + b_ref[...]

    return pl.pallas_call(
        body, grid=(s // tm,), in_specs=[row, row, vec, vec],
        out_specs=[row, row, pl.BlockSpec((tm, 1), lambda i: (i, 0))],
        out_shape=[jax.ShapeDtypeStruct((s, d), F32), jax.ShapeDtypeStruct((s, d), F32),
                   jax.ShapeDtypeStruct((s, 1), F32)],
        name=name, compiler_params=_params("parallel"))(h, m, g, b)


def _ln_bwd(name, dy, xh, r, g):
    s, d = dy.shape
    tm = _tile(s, 256)
    row = pl.BlockSpec((tm, d), lambda i: (i, 0))
    vec = pl.BlockSpec((1, d), lambda i: (0, 0))

    def body(dy_ref, xh_ref, r_ref, g_ref, dz_ref, dg_ref, db_ref):
        i = pl.program_id(0)
        dy_, xh_ = dy_ref[...], xh_ref[...]
        dyg = dy_ * g_ref[...]
        m1 = jnp.mean(dyg, -1, keepdims=True)
        m2 = jnp.mean(dyg * xh_, -1, keepdims=True)
        dz_ref[...] = r_ref[...] * (dyg - m1 - xh_ * m2)
        pg = jnp.sum(dy_ * xh_, 0, keepdims=True)
        pb = jnp.sum(dy_, 0, keepdims=True)

        @pl.when(i == 0)
        def _():
            dg_ref[...] = pg
            db_ref[...] = pb

        @pl.when(i > 0)
        def _():
            dg_ref[...] += pg
            db_ref[...] += pb

    return pl.pallas_call(
        body, grid=(s // tm,), in_specs=[row, row, pl.BlockSpec((tm, 1), lambda i: (i, 0)), vec],
        out_specs=[row, vec, vec],
        out_shape=[jax.ShapeDtypeStruct((s, d), F32), jax.ShapeDtypeStruct((1, d), F32),
                   jax.ShapeDtypeStruct((1, d), F32)],
        name=name, compiler_params=_params("arbitrary"))(dy, xh, r, g)


def _loss_head(y, t):
    s, d = y.shape
    tm = _tile(s, 256)
    row = pl.BlockSpec((tm, d), lambda i: (i, 0))

    def body(y_ref, t_ref, l_ref, dy_ref):
        i = pl.program_id(0)
        e = y_ref[...] - t_ref[...]
        dy_ref[...] = e * (1.0 / d)
        part = 0.5 * jnp.sum(jnp.mean(e * e, -1, keepdims=True), 0, keepdims=True)

        @pl.when(i == 0)
        def _():
            l_ref[...] = part

        @pl.when(i > 0)
        def _():
            l_ref[...] += part

    return pl.pallas_call(
        body, grid=(s // tm,), in_specs=[row, row],
        out_specs=[pl.BlockSpec((1, 1), lambda i: (0, 0)), row],
        out_shape=[jax.ShapeDtypeStruct((1, 1), F32), jax.ShapeDtypeStruct((s, d), F32)],
        name="loss_head", compiler_params=_params("arbitrary"))(y, t)


def _rope_tables(s):
    half = MLA_ROPE // 2
    inv = ROPE_THETA ** (-jnp.arange(half, dtype=F32) / half)
    ang = jnp.arange(s).astype(F32)[:, None] * inv[None, :]
    cos, sin = jnp.cos(ang), jnp.sin(ang)
    c = jnp.concatenate([cos, cos, jnp.ones((s, 128 - MLA_ROPE), F32)], 1)
    s1 = jnp.concatenate([-sin, jnp.zeros((s, 128 - half), F32)], 1)
    s2 = jnp.concatenate([jnp.zeros((s, half), F32), sin, jnp.zeros((s, 128 - MLA_ROPE), F32)], 1)
    return c, s1, s2


def _rope(x, c, s1, s2):
    half = MLA_ROPE // 2
    return x * c + pltpu.roll(x, 128 - half, 1) * s1 + pltpu.roll(x, half, 1) * s2


def _rope_t(dy, c, s1, s2):
    half = MLA_ROPE // 2
    return dy * c + pltpu.roll(dy * s1, half, 1) + pltpu.roll(dy * s2, 128 - half, 1)


def _mla_mid_fwd(down, gq, gkv, tabs):
    s, w = down.shape
    ql, kvl = gq.shape[1], gkv.shape[1]
    tm = _tile(s, 256)

    def body(d_ref, gq_ref, gkv_ref, c_ref, s1_ref, s2_ref, cq_ref, ckv_ref, kr_ref):
        cq = d_ref[:, :ql]
        ckv = d_ref[:, ql:ql + kvl]
        cq_ref[...] = (cq * lax.rsqrt(jnp.mean(cq * cq, -1, keepdims=True) + RMS_EPS)
                       * gq_ref[...]).astype(cq_ref.dtype)
        ckv_ref[...] = (ckv * lax.rsqrt(jnp.mean(ckv * ckv, -1, keepdims=True) + RMS_EPS)
                        * gkv_ref[...]).astype(ckv_ref.dtype)
        kr_ref[...] = _rope(d_ref[:, ql + kvl:], c_ref[...], s1_ref[...], s2_ref[...]).astype(kr_ref.dtype)

    tab = pl.BlockSpec((tm, 128), lambda i: (i, 0))
    return pl.pallas_call(
        body, grid=(s // tm,),
        in_specs=[pl.BlockSpec((tm, w), lambda i: (i, 0)), pl.BlockSpec((1, ql), lambda i: (0, 0)),
                  pl.BlockSpec((1, kvl), lambda i: (0, 0)), tab, tab, tab],
        out_specs=[pl.BlockSpec((tm, ql), lambda i: (i, 0)), pl.BlockSpec((tm, kvl), lambda i: (i, 0)), tab],
        out_shape=[jax.ShapeDtypeStruct((s, ql), MXU_DTYPE), jax.ShapeDtypeStruct((s, kvl), MXU_DTYPE),
                   jax.ShapeDtypeStruct((s, 128), MXU_DTYPE)],
        name="mla_mid_fwd", compiler_params=_params("parallel"))(down, gq, gkv, *tabs)


def _mla_mid_bwd(down, dcq, dckv, dkr, gq, gkv, tabs):
    s, w = down.shape
    ql, kvl = gq.shape[1], gkv.shape[1]
    tm = _tile(s, 256)

    def rms_bwd(x, dy, g):
        n = x.shape[1]
        r = lax.rsqrt(jnp.mean(x * x, -1, keepdims=True) + RMS_EPS)
        dyg = dy * g
        dx = r * dyg - x * (r * r * r * (1.0 / n)) * jnp.sum(dyg * x, -1, keepdims=True)
        return dx, jnp.sum(dy * x * r, 0, keepdims=True)

    def body(d_ref, dcq_ref, dckv_ref, dkr_ref, gq_ref, gkv_ref, c_ref, s1_ref, s2_ref, o_ref, dgq_ref, dgkv_ref):
        i = pl.program_id(0)
        dxq, pq = rms_bwd(d_ref[:, :ql], dcq_ref[...], gq_ref[...])
        dxkv, pkv = rms_bwd(d_ref[:, ql:ql + kvl], dckv_ref[...], gkv_ref[...])
        o_ref[:, :ql] = dxq.astype(o_ref.dtype)
        o_ref[:, ql:ql + kvl] = dxkv.astype(o_ref.dtype)
        o_ref[:, ql + kvl:] = _rope_t(dkr_ref[...], c_ref[...], s1_ref[...], s2_ref[...]).astype(o_ref.dtype)

        @pl.when(i == 0)
        def _():
            dgq_ref[...] = pq
            dgkv_ref[...] = pkv

        @pl.when(i > 0)
        def _():
            dgq_ref[...] += pq
            dgkv_ref[...] += pkv

    tab = pl.BlockSpec((tm, 128), lambda i: (i, 0))
    vq = pl.BlockSpec((1, ql), lambda i: (0, 0))
    vkv = pl.BlockSpec((1, kvl), lambda i: (0, 0))
    full = pl.BlockSpec((tm, w), lambda i: (i, 0))
    return pl.pallas_call(
        body, grid=(s // tm,),
        in_specs=[full, pl.BlockSpec((tm, ql), lambda i: (i, 0)), pl.BlockSpec((tm, kvl), lambda i: (i, 0)), tab,
                  vq, vkv, tab, tab, tab],
        out_specs=[full, vq, vkv],
        out_shape=[jax.ShapeDtypeStruct((s, w), MXU_DTYPE), jax.ShapeDtypeStruct((1, ql), F32),
                   jax.ShapeDtypeStruct((1, kvl), F32)],
        name="mla_mid_bwd", compiler_params=_params("arbitrary"))(down, dcq, dckv, dkr, gq, gkv, *tabs)


def _iota2():
    return (lax.broadcasted_iota(jnp.int32, (TQ, TQ), 0), lax.broadcasted_iota(jnp.int32, (TQ, TQ), 1))


def _mla_attn_fwd(q, kv, kr, tabs):
    s = q.shape[0]
    nq = s // TQ
    scale = MLA_QK_DIM ** -0.5

    def body(q_ref, kn_ref, v_ref, kr_ref, c_ref, s1_ref, s2_ref, o_ref, lse_ref):
        i = pl.program_id(1)
        row, col = _iota2()
        qn = q_ref[:, :HEAD_DIM].astype(MXU_DTYPE)
        qr = _rope(q_ref[:, HEAD_DIM:], c_ref[...], s1_ref[...], s2_ref[...]).astype(MXU_DTYPE)
        qc = jnp.right_shift(i * TQ + row, CHUNK_SHIFT)

        def step(kb, carry):
            m, l, acc = carry
            ks = pl.multiple_of(kb * TQ, TQ)
            sc = (_dot(qn, kn_ref[pl.ds(ks, TQ), :].astype(MXU_DTYPE), NT)
                  + _dot(qr, kr_ref[pl.ds(ks, TQ), :].astype(MXU_DTYPE), NT)) * scale
            sc = jnp.where(jnp.right_shift(ks + col, CHUNK_SHIFT) <= qc, sc, NEG)
            m_new = jnp.maximum(m, jnp.max(sc, -1, keepdims=True))
            p = jnp.exp(sc - m_new)
            corr = jnp.exp(m - m_new)
            l = corr * l + jnp.sum(p, -1, keepdims=True)
            acc = corr * acc + _dot(p.astype(MXU_DTYPE), v_ref[pl.ds(ks, TQ), :].astype(MXU_DTYPE))
            return m_new, l, acc

        m, l, acc = lax.fori_loop(0, i + 1, step, (jnp.full((TQ, 1), NEG, F32), jnp.zeros((TQ, 1), F32),
                                                   jnp.zeros((TQ, HEAD_DIM), F32)))
        o_ref[...] = acc / l
        lse_ref[...] = m + jnp.log(l)

    tab = pl.BlockSpec((TQ, 128), lambda h, i: (i, 0))
    return pl.pallas_call(
        body, grid=(HEADS, nq),
        in_specs=[pl.BlockSpec((TQ, HEAD_PAD), lambda h, i: (i, h)),
                  pl.BlockSpec((s, HEAD_DIM), lambda h, i: (0, 2 * h)),
                  pl.BlockSpec((s, HEAD_DIM), lambda h, i: (0, 2 * h + 1)),
                  pl.BlockSpec((s, 128), lambda h, i: (0, 0)), tab, tab, tab],
        out_specs=[pl.BlockSpec((TQ, HEAD_DIM), lambda h, i: (i, h)),
                   pl.BlockSpec((None, TQ, 1), lambda h, i: (h, i, 0))],
        out_shape=[jax.ShapeDtypeStruct((s, HEADS * HEAD_DIM), F32), jax.ShapeDtypeStruct((HEADS, s, 1), F32)],
        name="mla_attn_fwd", compiler_params=_params("parallel", "parallel"))(q, kv, kv, kr, *tabs)


def _mla_attn_bwd(q, kv, kr, tabs, do, o, lse):
    s = q.shape[0]
    nq = s // TQ
    scale = MLA_QK_DIM ** -0.5

    def body(q_ref, kn_ref, v_ref, kr_ref, c_ref, s1_ref, s2_ref, do_ref, o_ref, lse_ref,
             dq_ref, dkv_ref, dkr_ref, dkv_acc, dkr_acc):
        h, i = pl.program_id(0), pl.program_id(1)
        row, col = _iota2()

        @pl.when(i == 0)
        def _():
            dkv_acc[...] = jnp.zeros_like(dkv_acc)

        @pl.when((h == 0) & (i == 0))
        def _():
            dkr_acc[...] = jnp.zeros_like(dkr_acc)

        tabs_i = (c_ref[...], s1_ref[...], s2_ref[...])
        qn = q_ref[:, :HEAD_DIM].astype(MXU_DTYPE)
        qr = _rope(q_ref[:, HEAD_DIM:], *tabs_i).astype(MXU_DTYPE)
        qc = jnp.right_shift(i * TQ + row, CHUNK_SHIFT)
        do_ = do_ref[...]
        delta = jnp.sum(do_ * o_ref[...], -1, keepdims=True)
        lse_ = lse_ref[...]
        dob = do_.astype(MXU_DTYPE)

        def step(kb, carry):
            dqn, dqr = carry
            ks = pl.multiple_of(kb * TQ, TQ)
            kn = kn_ref[pl.ds(ks, TQ), :].astype(MXU_DTYPE)
            krb = kr_ref[pl.ds(ks, TQ), :].astype(MXU_DTYPE)
            v = v_ref[pl.ds(ks, TQ), :].astype(MXU_DTYPE)
            sc = (_dot(qn, kn, NT) + _dot(qr, krb, NT)) * scale
            sc = jnp.where(jnp.right_shift(ks + col, CHUNK_SHIFT) <= qc, sc, NEG)
            p = jnp.exp(sc - lse_)
            ds = (p * (_dot(dob, v, NT) - delta) * scale).astype(MXU_DTYPE)
            dkv_acc[pl.ds(ks, TQ), :HEAD_DIM] += _dot(ds, qn, TN)
            dkv_acc[pl.ds(ks, TQ), HEAD_DIM:] += _dot(p.astype(MXU_DTYPE), dob, TN)
            dkr_acc[pl.ds(ks, TQ), :] += _dot(ds, qr, TN)
            return dqn + _dot(ds, kn), dqr + _dot(ds, krb)

        dqn, dqr = lax.fori_loop(0, i + 1, step, (jnp.zeros((TQ, HEAD_DIM), F32), jnp.zeros((TQ, 128), F32)))
        dq_ref[:, :HEAD_DIM] = dqn
        dq_ref[:, HEAD_DIM:] = _rope_t(dqr, *tabs_i)

        @pl.when(i == nq - 1)
        def _():
            dkv_ref[...] = dkv_acc[...].astype(dkv_ref.dtype)

        @pl.when((h == HEADS - 1) & (i == nq - 1))
        def _():
            dkr_ref[...] = dkr_acc[...]

    tab = pl.BlockSpec((TQ, 128), lambda h, i: (i, 0))
    qblk = pl.BlockSpec((TQ, HEAD_PAD), lambda h, i: (i, h))
    oblk = pl.BlockSpec((TQ, HEAD_DIM), lambda h, i: (i, h))
    return pl.pallas_call(
        body, grid=(HEADS, nq),
        in_specs=[qblk, pl.BlockSpec((s, HEAD_DIM), lambda h, i: (0, 2 * h)),
                  pl.BlockSpec((s, HEAD_DIM), lambda h, i: (0, 2 * h + 1)),
                  pl.BlockSpec((s, 128), lambda h, i: (0, 0)), tab, tab, tab, oblk, oblk,
                  pl.BlockSpec((None, TQ, 1), lambda h, i: (h, i, 0))],
        out_specs=[qblk, pl.BlockSpec((s, HEAD_PAD), lambda h, i: (0, h)),
                   pl.BlockSpec((s, 128), lambda h, i: (0, 0))],
        out_shape=[jax.ShapeDtypeStruct((s, HEADS * HEAD_PAD), F32), jax.ShapeDtypeStruct((s, HEADS * HEAD_PAD), MXU_DTYPE),
                   jax.ShapeDtypeStruct((s, 128), F32)],
        scratch_shapes=[pltpu.VMEM((s, HEAD_PAD), F32), pltpu.VMEM((s, 128), F32)],
        name="mla_attn_bwd", compiler_params=_params("arbitrary", "arbitrary"))(q, kv, kv, kr, *tabs, do, o, lse)


def _qkv_specs(s):
    return [pl.BlockSpec((TQ, HEAD_DIM), lambda h, i: (i, h)),
            pl.BlockSpec((s, HEAD_DIM), lambda h, i: (0, HEADS + h)),
            pl.BlockSpec((s, HEAD_DIM), lambda h, i: (0, 2 * HEADS + h))]


def _sb_terms(z):
    sp = jnp.log(1.0 + jnp.exp(-jnp.abs(z)))
    return jnp.minimum(z, 0.0) - sp, jnp.minimum(-z, 0.0) - sp


def _sb_attn_fwd(qkv):
    s = qkv.shape[0]
    nq = s // TQ
    scale = HEAD_DIM ** -0.5

    def body(q_ref, k_ref, v_ref, o_ref):
        i = pl.program_id(1)
        row, col = _iota2()
        after = (row > col).astype(BF16)
        q = q_ref[...].astype(MXU_DTYPE)
        qpos = i * TQ + row

        def step(n, carry):
            tail, acc = carry
            ks = pl.multiple_of((i - n) * TQ, TQ)
            z = _dot(q, k_ref[pl.ds(ks, TQ), :].astype(MXU_DTYPE), NT) * scale
            strict = (ks + col) < qpos
            lb, l1 = _sb_terms(z)
            l1 = jnp.where(strict, l1, 0.0)
            a = jnp.where(strict, jnp.exp(lb + tail + _exact_dot(l1, after)), 0.0)
            acc = acc + _dot(a.astype(MXU_DTYPE), v_ref[pl.ds(ks, TQ), :].astype(MXU_DTYPE))
            return tail + jnp.sum(l1, -1, keepdims=True), acc

        _, acc = lax.fori_loop(0, i + 1, step, (jnp.zeros((TQ, 1), F32), jnp.zeros((TQ, HEAD_DIM), F32)))
        o_ref[...] = acc

    return pl.pallas_call(
        body, grid=(HEADS, nq), in_specs=_qkv_specs(s),
        out_specs=pl.BlockSpec((TQ, HEAD_DIM), lambda h, i: (i, h)),
        out_shape=jax.ShapeDtypeStruct((s, HEADS * HEAD_DIM), F32),
        name="sb_attn_fwd", compiler_params=_params("parallel", "parallel"))(qkv, qkv, qkv)


def _sb_attn_bwd(qkv, do):
    s = qkv.shape[0]
    nq = s // TQ
    scale = HEAD_DIM ** -0.5

    def body(q_ref, k_ref, v_ref, do_ref, dq_ref, dk_ref, dv_ref, a_buf, dk_acc, dv_acc):
        i = pl.program_id(1)
        row, col = _iota2()
        after = (row > col).astype(BF16)
        before = (row < col).astype(BF16)

        @pl.when(i == 0)
        def _():
            dk_acc[...] = jnp.zeros_like(dk_acc)
            dv_acc[...] = jnp.zeros_like(dv_acc)

        q = q_ref[...].astype(MXU_DTYPE)
        dob = do_ref[...].astype(MXU_DTYPE)
        qpos = i * TQ + row

        def weights(n, tail):
            kb = i - n
            ks = pl.multiple_of(kb * TQ, TQ)
            z = _dot(q, k_ref[pl.ds(ks, TQ), :].astype(MXU_DTYPE), NT) * scale
            strict = (ks + col) < qpos
            lb, l1 = _sb_terms(z)
            l1 = jnp.where(strict, l1, 0.0)
            a = jnp.where(strict, jnp.exp(lb + tail + _exact_dot(l1, after)), 0.0)
            a_buf[kb] = a
            dv_acc[pl.ds(ks, TQ), :] += _dot(a.astype(MXU_DTYPE), dob, TN)
            return tail + jnp.sum(l1, -1, keepdims=True)

        lax.fori_loop(0, i + 1, weights, jnp.zeros((TQ, 1), F32))

        def grads(kb, carry):
            head, dq = carry
            ks = pl.multiple_of(kb * TQ, TQ)
            k = k_ref[pl.ds(ks, TQ), :].astype(MXU_DTYPE)
            z = _dot(q, k, NT) * scale
            strict = (ks + col) < qpos
            e = jnp.exp(-jnp.abs(z))
            beta = jnp.where(z >= 0.0, 1.0, e) / (1.0 + e)
            w = _dot(dob, v_ref[pl.ds(ks, TQ), :].astype(MXU_DTYPE), NT) * a_buf[kb]
            dz = jnp.where(strict, w * (1.0 - beta) - beta * (head + _exact_dot(w, before)), 0.0) * scale
            dzb = dz.astype(MXU_DTYPE)
            dk_acc[pl.ds(ks, TQ), :] += _dot(dzb, q, TN)
            return head + jnp.sum(w, -1, keepdims=True), dq + _dot(dzb, k)

        _, dq = lax.fori_loop(0, i + 1, grads, (jnp.zeros((TQ, 1), F32), jnp.zeros((TQ, HEAD_DIM), F32)))
        dq_ref[...] = dq.astype(dq_ref.dtype)

        @pl.when(i == nq - 1)
        def _():
            dk_ref[...] = dk_acc[...].astype(dk_ref.dtype)
            dv_ref[...] = dv_acc[...].astype(dv_ref.dtype)

    blk = pl.BlockSpec((TQ, HEAD_DIM), lambda h, i: (i, h))
    col_h = pl.BlockSpec((s, HEAD_DIM), lambda h, i: (0, h))
    shp = jax.ShapeDtypeStruct((s, HEADS * HEAD_DIM), MXU_DTYPE)
    return pl.pallas_call(
        body, grid=(HEADS, nq), in_specs=_qkv_specs(s) + [blk],
        out_specs=[blk, col_h, col_h], out_shape=[shp, shp, shp],
        scratch_shapes=[pltpu.VMEM((nq, TQ, TQ), F32), pltpu.VMEM((s, HEAD_DIM), F32), pltpu.VMEM((s, HEAD_DIM), F32)],
        name="sb_attn_bwd", compiler_params=_params("arbitrary", "arbitrary"))(qkv, qkv, qkv, do)


def _ca_tile_index():
    r = np.arange(TQ)[:, None]
    c = np.arange(TQ)[None, :]
    clip = lambda rel: np.clip(rel, -REL_CLIP_LEFT, (1 << CHUNK_SHIFT) - 1) + REL_CLIP_LEFT
    return np.stack([clip(c - r), clip(c - r - TQ), np.zeros((TQ, TQ), np.int64)]).astype(np.int32)


def _ca_bias_tiles(rel_bias):
    return jnp.transpose(rel_bias[_ca_tile_index()], (3, 0, 1, 2))


def _ca_mask(i, ks, row, col):
    qc = jnp.right_shift(i * TQ + row, CHUNK_SHIFT)
    kc = jnp.right_shift(ks + col, CHUNK_SHIFT)
    return (kc <= qc) & (kc >= qc - CA_LEFT_CHUNKS)


def _ca_attn_fwd(qkv, tiles):
    s = qkv.shape[0]
    nq = s // TQ
    scale = HEAD_DIM ** -0.5

    def body(q_ref, k_ref, v_ref, bt_ref, o_ref, lse_ref):
        i = pl.program_id(1)
        row, col = _iota2()
        q = q_ref[...].astype(MXU_DTYPE)

        def step(kb, carry):
            m, l, acc = carry
            ks = pl.multiple_of(kb * TQ, TQ)
            sc = _dot(q, k_ref[pl.ds(ks, TQ), :].astype(MXU_DTYPE), NT) * scale + bt_ref[jnp.minimum(i - kb, 2)]
            sc = jnp.where(_ca_mask(i, ks, row, col), sc, NEG)
            m_new = jnp.maximum(m, jnp.max(sc, -1, keepdims=True))
            p = jnp.exp(sc - m_new)
            corr = jnp.exp(m - m_new)
            l = corr * l + jnp.sum(p, -1, keepdims=True)
            acc = corr * acc + _dot(p.astype(MXU_DTYPE), v_ref[pl.ds(ks, TQ), :].astype(MXU_DTYPE))
            return m_new, l, acc

        m, l, acc = lax.fori_loop(jnp.maximum(i - CA_LEFT_BLOCKS, 0), i + 1, step,
                                  (jnp.full((TQ, 1), NEG, F32), jnp.zeros((TQ, 1), F32),
                                   jnp.zeros((TQ, HEAD_DIM), F32)))
        o_ref[...] = acc / l
        lse_ref[...] = m + jnp.log(l)

    return pl.pallas_call(
        body, grid=(HEADS, nq),
        in_specs=_qkv_specs(s) + [pl.BlockSpec((None, 3, TQ, TQ), lambda h, i: (h, 0, 0, 0))],
        out_specs=[pl.BlockSpec((TQ, HEAD_DIM), lambda h, i: (i, h)),
                   pl.BlockSpec((None, TQ, 1), lambda h, i: (h, i, 0))],
        out_shape=[jax.ShapeDtypeStruct((s, HEADS * HEAD_DIM), F32), jax.ShapeDtypeStruct((HEADS, s, 1), F32)],
        name="ca_attn_fwd", compiler_params=_params("parallel", "parallel"))(qkv, qkv, qkv, tiles)


def _ca_attn_bwd(qkv, tiles, do, o, lse):
    s = qkv.shape[0]
    nq = s // TQ
    scale = HEAD_DIM ** -0.5

    def body(q_ref, k_ref, v_ref, bt_ref, do_ref, o_ref, lse_ref, dq_ref, dk_ref, dv_ref, dbt_ref, dk_acc, dv_acc):
        i = pl.program_id(1)
        row, col = _iota2()

        @pl.when(i == 0)
        def _():
            dk_acc[...] = jnp.zeros_like(dk_acc)
            dv_acc[...] = jnp.zeros_like(dv_acc)
            dbt_ref[...] = jnp.zeros_like(dbt_ref)

        q = q_ref[...].astype(MXU_DTYPE)
        do_ = do_ref[...]
        delta = jnp.sum(do_ * o_ref[...], -1, keepdims=True)
        lse_ = lse_ref[...]
        dob = do_.astype(MXU_DTYPE)

        def step(kb, dq):
            ks = pl.multiple_of(kb * TQ, TQ)
            k = k_ref[pl.ds(ks, TQ), :].astype(MXU_DTYPE)
            v = v_ref[pl.ds(ks, TQ), :].astype(MXU_DTYPE)
            t = jnp.minimum(i - kb, 2)
            sc = _dot(q, k, NT) * scale + bt_ref[t]
            sc = jnp.where(_ca_mask(i, ks, row, col), sc, NEG)
            p = jnp.exp(sc - lse_)
            dsc = p * (_dot(dob, v, NT) - delta)
            dbt_ref[t] += dsc
            ds = (dsc * scale).astype(MXU_DTYPE)
            dk_acc[pl.ds(ks, TQ), :] += _dot(ds, q, TN)
            dv_acc[pl.ds(ks, TQ), :] += _dot(p.astype(MXU_DTYPE), dob, TN)
            return dq + _dot(ds, k)

        dq = lax.fori_loop(jnp.maximum(i - CA_LEFT_BLOCKS, 0), i + 1, step, jnp.zeros((TQ, HEAD_DIM), F32))
        dq_ref[...] = dq.astype(dq_ref.dtype)

        @pl.when(i == nq - 1)
        def _():
            dk_ref[...] = dk_acc[...].astype(dk_ref.dtype)
            dv_ref[...] = dv_acc[...].astype(dv_ref.dtype)

    blk = pl.BlockSpec((TQ, HEAD_DIM), lambda h, i: (i, h))
    col_h = pl.BlockSpec((s, HEAD_DIM), lambda h, i: (0, h))
    tile = pl.BlockSpec((None, 3, TQ, TQ), lambda h, i: (h, 0, 0, 0))
    shp = jax.ShapeDtypeStruct((s, HEADS * HEAD_DIM), MXU_DTYPE)
    return pl.pallas_call(
        body, grid=(HEADS, nq),
        in_specs=_qkv_specs(s) + [tile, blk, blk, pl.BlockSpec((None, TQ, 1), lambda h, i: (h, i, 0))],
        out_specs=[blk, col_h, col_h, tile],
        out_shape=[shp, shp, shp, jax.ShapeDtypeStruct((HEADS, 3, TQ, TQ), F32)],
        scratch_shapes=[pltpu.VMEM((s, HEAD_DIM), F32), pltpu.VMEM((s, HEAD_DIM), F32)],
        name="ca_attn_bwd", compiler_params=_params("arbitrary", "arbitrary"))(qkv, qkv, qkv, tiles, do, o, lse)


def _ca_table_grad(dtiles):
    def skew(x, row):
        for b in range(7):
            x = jnp.where((jnp.right_shift(row, b) & 1) == 1, pltpu.roll(x, TQ - (1 << b), 1), x)
        return x

    def body(t_ref, o_ref):
        row, col = _iota2()
        wrapped = (row + col) >= TQ
        lane = col[:1]
        y0, y1 = skew(t_ref[0], row), skew(t_ref[1], row)
        pos0 = jnp.sum(jnp.where(wrapped, 0.0, y0), 0, keepdims=True)
        neg0 = jnp.sum(jnp.where(wrapped, y0, 0.0), 0, keepdims=True)
        pos1 = jnp.sum(jnp.where(wrapped, 0.0, y1), 0, keepdims=True)
        neg1 = jnp.sum(jnp.where(wrapped, y1, 0.0), 0, keepdims=True)
        far = jnp.sum(neg1, -1, keepdims=True) + jnp.sum(jnp.sum(t_ref[2], 0, keepdims=True), -1, keepdims=True)
        last = (1 << CHUNK_SHIFT) - 1
        clipped = jnp.sum(jnp.where(lane > last, pos0, 0.0), -1, keepdims=True)
        o_ref[:, :TQ] = neg0 + pos1 + jnp.where(lane == 0, far, 0.0)
        o_ref[:, TQ:] = jnp.where(lane <= last, pos0, 0.0) + jnp.where(lane == last, clipped, 0.0)

    return pl.pallas_call(
        body, grid=(HEADS,), in_specs=[pl.BlockSpec((None, 3, TQ, TQ), lambda h: (h, 0, 0, 0))],
        out_specs=pl.BlockSpec((None, 1, 2 * TQ), lambda h: (h, 0, 0)),
        out_shape=jax.ShapeDtypeStruct((HEADS, 1, 2 * TQ), F32),
        name="ca_table_grad", compiler_params=_params("parallel"))(dtiles)


def _row_tile(rows, cols):
    if rows % 128:
        return rows
    tr = 128
    while rows % (2 * tr) == 0 and 2 * tr * cols * 4 <= (1 << 20):
        tr *= 2
    return tr


def _adamw(name, w, g, m, v):
    rows, cols = w.shape
    tr = _row_tile(rows, cols)
    blk = pl.BlockSpec((tr, cols), lambda i: (i, 0))

    def body(w_ref, g_ref, m_ref, v_ref, d_ref, m2_ref, v2_ref):
        g_ = g_ref[...]
        m2 = ADAM_B1 * m_ref[...] + (1.0 - ADAM_B1) * g_
        v2 = ADAM_B2 * v_ref[...] + (1.0 - ADAM_B2) * jnp.square(g_)
        m_hat = m2 / (1.0 - ADAM_B1 ** ADAM_STEP)
        v_hat = v2 / (1.0 - ADAM_B2 ** ADAM_STEP)
        d_ref[...] = -ADAM_LR * (m_hat / (jnp.sqrt(v_hat) + ADAM_EPS) + ADAM_WD * w_ref[...])
        m2_ref[...] = m2
        v2_ref[...] = v2

    shp = jax.ShapeDtypeStruct((rows, cols), F32)
    return pl.pallas_call(body, grid=(rows // tr,), in_specs=[blk] * 4, out_specs=[blk] * 3, out_shape=[shp] * 3,
                          name=name, compiler_params=_params("parallel"))(w, g, m, v)


def _pair_sum(name, g, r1, c):
    _, rh, cols = r1.shape
    tr = _row_tile(rh, cols)
    nb = rh // tr

    def body(c_ref, g_ref, r_ref, o_ref):
        o_ref[...] = (g_ref[...].astype(F32) + r_ref[...].astype(F32)).astype(o_ref.dtype)

    return pl.pallas_call(
        body, grid_spec=pltpu.PrefetchScalarGridSpec(
            num_scalar_prefetch=1, grid=(N_CHIPS, nb),
            in_specs=[pl.BlockSpec((None, tr, cols), lambda k, i, c_ref: (k, c_ref[0] * nb + i, 0)),
                      pl.BlockSpec((None, tr, cols), lambda k, i, c_ref: (k, i, 0))],
            out_specs=pl.BlockSpec((None, tr, cols), lambda k, i, c_ref: (k, i, 0))),
        out_shape=jax.ShapeDtypeStruct(r1.shape, WIRE_DTYPE), name=name,
        compiler_params=_params("parallel", "parallel"))(c, g, r1)


def _chip_sum(name, a1, r2, shard):
    _, rh, cols = a1.shape
    tr = _row_tile(rh, cols)

    def body(s_ref, a_ref, r_ref, o_ref):
        o_ref[...] = ((a_ref[...].astype(F32) + r_ref[0].astype(F32)) + r_ref[1].astype(F32)) + r_ref[2].astype(F32)

    return pl.pallas_call(
        body, grid_spec=pltpu.PrefetchScalarGridSpec(
            num_scalar_prefetch=1, grid=(rh // tr,),
            in_specs=[pl.BlockSpec((None, tr, cols), lambda i, s_ref: (s_ref[0], i, 0)),
                      pl.BlockSpec((N_CHIPS - 1, tr, cols), lambda i, s_ref: (0, i, 0))],
            out_specs=pl.BlockSpec((tr, cols), lambda i, s_ref: (i, 0))),
        out_shape=jax.ShapeDtypeStruct((rh, cols), F32), name=name,
        compiler_params=_params("parallel"))(shard, a1, r2)


def _place():
    x, y, c = lax.axis_index("x"), lax.axis_index("y"), lax.axis_index("c")
    chips = [(1 - x, y), (x, 1 - y), (1 - x, 1 - y)]
    return x, y, c, chips


def _remote(src, dst, send_sem, recv_sem, to):
    return pltpu.make_async_remote_copy(src_ref=src, dst_ref=dst, send_sem=send_sem, recv_sem=recv_sem,
                                        device_id=to, device_id_type=MESH)


def _all_gather(shards):
    n = len(shards)

    def body(*refs):
        ins, outs = refs[:n], refs[n:2 * n]
        send, recv, local = refs[2 * n:]
        x, y, c, chips = _place()
        me, sib = 2 * x + y, (x, y, 1 - c)
        own = [pltpu.make_async_copy(ins[t], outs[t].at[me], local.at[t]) for t in range(n)]
        for cp in own:
            cp.start()

        def half(t, which):
            return pl.ds(which * (ins[t].shape[0] // 2), ins[t].shape[0] // 2)

        first, passed = [], []
        for t in range(n):
            for j, chip in enumerate(chips):
                cp = _remote(ins[t].at[half(t, c)], outs[t].at[me, half(t, c)], send.at[6 * t + j],
                             recv.at[6 * t + j], (*chip, c))
                cp.start()
                first.append(cp)
        for t in range(n):
            for j, (px, py) in enumerate(chips):
                got = outs[t].at[2 * px + py, half(t, c)]
                _remote(got, got, send.at[6 * t + j], recv.at[6 * t + j], (px, py, c)).wait_recv()
                cp = _remote(got, got, send.at[6 * t + 3 + j], recv.at[6 * t + 3 + j], sib)
                cp.start()
                passed.append(cp)
        for t in range(n):
            for j, (px, py) in enumerate(chips):
                got = outs[t].at[2 * px + py, half(t, 1 - c)]
                _remote(got, got, send.at[6 * t + 3 + j], recv.at[6 * t + 3 + j], sib).wait_recv()
        for cp in first + passed:
            cp.wait_send()
        for cp in own:
            cp.wait()

    return pl.pallas_call(
        body, in_specs=[ANY] * n, out_specs=[ANY] * n,
        out_shape=[jax.ShapeDtypeStruct((N_CHIPS, *s.shape), s.dtype) for s in shards],
        scratch_shapes=[pltpu.SemaphoreType.DMA((6 * n,)), pltpu.SemaphoreType.DMA((6 * n,)),
                        pltpu.SemaphoreType.DMA((n,))],
        name="weight_all_gather")(*shards)


def _sibling_exchange(grads):
    n = len(grads)

    def body(*refs):
        ins, outs = refs[:n], refs[n:2 * n]
        send, recv = refs[2 * n:]
        x, y, c, _ = _place()
        cps = []
        for t in range(n):
            rh = ins[t].shape[1] // 2
            cp = _remote(ins[t].at[:, pl.ds((1 - c) * rh, rh)], outs[t], send.at[t], recv.at[t], (x, y, 1 - c))
            cp.start()
            cps.append(cp)
        for cp in cps:
            cp.wait()

    return pl.pallas_call(
        body, in_specs=[ANY] * n, out_specs=[ANY] * n,
        out_shape=[jax.ShapeDtypeStruct((N_CHIPS, g.shape[1] // 2, g.shape[2]), g.dtype) for g in grads],
        scratch_shapes=[pltpu.SemaphoreType.DMA((n,)), pltpu.SemaphoreType.DMA((n,))],
        name="grad_sibling_exchange")(*grads)


def _chip_exchange(parts):
    n = len(parts)

    def body(*refs):
        ins, outs = refs[:n], refs[n:2 * n]
        send, recv = refs[2 * n:]
        x, y, c, chips = _place()
        cps = []
        for t in range(n):
            for j, (px, py) in enumerate(chips):
                cp = _remote(ins[t].at[2 * px + py], outs[t].at[j], send.at[3 * t + j], recv.at[3 * t + j],
                             (px, py, c))
                cp.start()
                cps.append(cp)
        for cp in cps:
            cp.wait()

    return pl.pallas_call(
        body, in_specs=[ANY] * n, out_specs=[ANY] * n,
        out_shape=[jax.ShapeDtypeStruct((N_CHIPS - 1, *p.shape[1:]), p.dtype) for p in parts],
        scratch_shapes=[pltpu.SemaphoreType.DMA((3 * n,)), pltpu.SemaphoreType.DMA((3 * n,))],
        name="grad_chip_exchange")(*parts)


def _sibling_share(halves):
    n = len(halves)

    def body(*refs):
        ins, outs = refs[:n], refs[n:2 * n]
        send, recv, local = refs[2 * n:]
        x, y, c, _ = _place()
        cps = []
        for t in range(n):
            own = pltpu.make_async_copy(ins[t], outs[t].at[c], local.at[t])
            own.start()
            cp = _remote(ins[t], outs[t].at[c], send.at[t], recv.at[t], (x, y, 1 - c))
            cp.start()
            cps += [own, cp]
        for cp in cps:
            cp.wait()

    return pl.pallas_call(
        body, in_specs=[ANY] * n, out_specs=[ANY] * n,
        out_shape=[jax.ShapeDtypeStruct((2, *h.shape), h.dtype) for h in halves],
        scratch_shapes=[pltpu.SemaphoreType.DMA((n,)), pltpu.SemaphoreType.DMA((n,)), pltpu.SemaphoreType.DMA((n,))],
        name="grad_sibling_share")(*halves)


def _all_reduce_small(name, pack):
    rows, cols = pack.shape

    def body(p_ref, o_ref, slots, send, recv):
        x, y, c, _ = _place()
        me = 4 * x + 2 * y + c
        slots[me] = p_ref[...]
        cps = []
        for r in range(1, N_DEV):
            to = ((1 - x) if r & 4 else x, (1 - y) if r & 2 else y, (1 - c) if r & 1 else c)
            cp = _remote(p_ref, slots.at[me], send.at[r - 1], recv.at[r - 1], to)
            cp.start()
            cps.append(cp)
        for cp in cps:
            cp.wait()
        acc = slots[0]
        for d in range(1, N_DEV):
            acc = acc + slots[d]
        o_ref[...] = acc

    return pl.pallas_call(
        body, in_specs=[VMEM_SPEC], out_specs=VMEM_SPEC, out_shape=jax.ShapeDtypeStruct((rows, cols), F32),
        scratch_shapes=[pltpu.VMEM((N_DEV, rows, cols), F32), pltpu.SemaphoreType.DMA((N_DEV - 1,)),
                        pltpu.SemaphoreType.DMA((N_DEV - 1,))],
        name=name)(pack)


BIG = ("ffn_w_in", "ffn_w_out", "mla_w_down", "mla_w_uq", "mla_w_ukv", "mla_w_o", "sb_w_qkv", "sb_w_o", "ca_w_qkv",
       "ca_w_o")
COL_SHARDED = {"ffn_w_in": True, "ffn_w_out": False, "mla_w_down": False, "mla_w_uq": True, "mla_w_ukv": True,
               "mla_w_o": False, "sb_w_qkv": True, "sb_w_o": False, "ca_w_qkv": True, "ca_w_o": False}
WEIGHTS = ("ln_mix_g", "ln_mix_b", "ln_ffn_g", "ln_ffn_b", "ffn_w_in", "ffn_w_out", "mla_w_down", "mla_q_norm_g",
           "mla_w_uq", "mla_kv_norm_g", "mla_w_ukv", "mla_w_o", "sb_w_qkv", "sb_w_o", "ca_w_qkv", "ca_rel_bias",
           "ca_w_o")
HEADS_PER_CHIP = HEADS // N_CHIPS


def _mxu_shards(w):
    down = w["mla_w_down"]
    uq = w["mla_w_uq"]
    n, ql = uq.shape[:2]
    lane_pad = 128 - MLA_ROPE
    shaped = dict(w)
    shaped["mla_w_down"] = jnp.pad(down, ((0, 0), (0, 0), (0, lane_pad)))
    shaped["mla_w_uq"] = jnp.pad(uq.reshape(n, ql, HEADS_PER_CHIP, MLA_QK_DIM),
                                 ((0, 0), (0, 0), (0, 0), (0, HEAD_PAD - MLA_QK_DIM))).reshape(n, ql, -1)
    return {k: shaped[k].reshape(-1, shaped[k].shape[-1]).astype(MXU_DTYPE) for k in BIG}


def _unpad_grad(name, g, like):
    if name == "mla_w_down":
        g = g[:, :like.shape[-1]]
    elif name == "mla_w_uq":
        g = g.reshape(g.shape[0], HEADS_PER_CHIP, HEAD_PAD)[:, :, :MLA_QK_DIM]
    return g.reshape(like.shape)


class _Grads:
    def __init__(self, dims):
        self.dims, self.buf = dims, {}

    def add(self, name, x, dy, l):
        layers, rows, cols = self.dims[name]
        self.buf[name] = _dw(f"dw_{name}_{l}", x, dy, self.buf.get(name), l, layers, rows, cols, COL_SHARDED[name])


def _local_step(x, target, wg, dims, p, tabs):
    d = x.shape[1]
    grads = _Grads(dims)
    rows = {k: v[1] for k, v in dims.items()}
    cols = {k: v[2] for k, v in dims.items()}
    saved = []
    h = x
    for i in range(DEPTH):
        kind, slot = i % 3, i // 3
        sv = {"h0": h}
        if kind == 0:
            down = _fwd_row(f"mla_down_{i}", h, wg["mla_w_down"], slot, rows["mla_w_down"], cols["mla_w_down"], F32)
            gq, gkv = p["mla_q_norm_g"][slot][None], p["mla_kv_norm_g"][slot][None]
            cq, ckv, kr = _mla_mid_fwd(down, gq, gkv, tabs)
            q = _fwd_col(f"mla_uq_{i}", cq, wg["mla_w_uq"], slot, rows["mla_w_uq"], cols["mla_w_uq"], F32)
            kv = _fwd_col(f"mla_ukv_{i}", ckv, wg["mla_w_ukv"], slot, rows["mla_w_ukv"], cols["mla_w_ukv"],
                          MXU_DTYPE)
            o, lse = _mla_attn_fwd(q, kv, kr, tabs)
            m = _fwd_row(f"mla_o_{i}", o, wg["mla_w_o"], slot, rows["mla_w_o"], cols["mla_w_o"], F32)
            sv.update(down=down, gq=gq, gkv=gkv, cq=cq, ckv=ckv, kr=kr, q=q, kv=kv, o=o, lse=lse)
        else:
            pre = "sb" if kind == 1 else "ca"
            qkv = _fwd_col(f"{pre}_qkv_{i}", h, wg[f"{pre}_w_qkv"], slot, rows[f"{pre}_w_qkv"], cols[f"{pre}_w_qkv"],
                           MXU_DTYPE)
            if kind == 1:
                o = _sb_attn_fwd(qkv)
            else:
                tiles = _ca_bias_tiles(p["ca_rel_bias"][slot])
                o, lse = _ca_attn_fwd(qkv, tiles)
                sv.update(tiles=tiles, lse=lse)
            m = _fwd_row(f"{pre}_o_{i}", o, wg[f"{pre}_w_o"], slot, rows[f"{pre}_w_o"], cols[f"{pre}_w_o"], F32)
            sv.update(qkv=qkv, o=o)
        h1, sv["xh1"], sv["r1"] = _ln_fwd(f"ln_mix_{i}", h, m, p["ln_mix_g"][i][None], p["ln_mix_b"][i][None])
        u, a = _fwd_col(f"ffn_in_{i}", h1, wg["ffn_w_in"], i, rows["ffn_w_in"], cols["ffn_w_in"], None,
                        epilogue=_epi_relu2, dtypes=(F32, MXU_DTYPE))
        y = _fwd_row(f"ffn_out_{i}", a, wg["ffn_w_out"], i, rows["ffn_w_out"], cols["ffn_w_out"], F32)
        h2, sv["xh2"], sv["r2"] = _ln_fwd(f"ln_ffn_{i}", h1, y, p["ln_ffn_g"][i][None], p["ln_ffn_b"][i][None])
        sv.update(h1=h1, u=u, a=a)
        saved.append(sv)
        h = h2

    loss, dy = _loss_head(h, target)
    small = {k: [None] * DEPTH for k in ("ln_mix_g", "ln_mix_b", "ln_ffn_g", "ln_ffn_b")}
    n_mla = p["mla_q_norm_g"].shape[0]
    small["mla_q_norm_g"], small["mla_kv_norm_g"] = [None] * n_mla, [None] * n_mla
    for i in reversed(range(DEPTH)):
        kind, slot = i % 3, i // 3
        sv = saved[i]
        dz, small["ln_ffn_g"][i], small["ln_ffn_b"][i] = _ln_bwd(f"ln_ffn_bwd_{i}", dy, sv["xh2"], sv["r2"],
                                                                 p["ln_ffn_g"][i][None])
        grads.add("ffn_w_out", sv["a"], dz, i)
        du = _dx_row(f"ffn_du_{i}", dz, wg["ffn_w_out"], i, rows["ffn_w_out"], cols["ffn_w_out"], MXU_DTYPE,
                     epilogue=_epi_drelu2, extra=sv["u"])
        grads.add("ffn_w_in", sv["h1"], du, i)
        dy = _dx_col(f"ffn_dh_{i}", du, wg["ffn_w_in"], i, rows["ffn_w_in"], cols["ffn_w_in"], F32,
                     epilogue=_epi_residual, extra=dz)
        dz, small["ln_mix_g"][i], small["ln_mix_b"][i] = _ln_bwd(f"ln_mix_bwd_{i}", dy, sv["xh1"], sv["r1"],
                                                                 p["ln_mix_g"][i][None])
        if kind == 0:
            grads.add("mla_w_o", sv["o"], dz, slot)
            do = _dx_row(f"mla_do_{i}", dz, wg["mla_w_o"], slot, rows["mla_w_o"], cols["mla_w_o"], F32)
            dq, dkv, dkr = _mla_attn_bwd(sv["q"], sv["kv"], sv["kr"], tabs, do, sv["o"], sv["lse"])
            grads.add("mla_w_uq", sv["cq"], dq, slot)
            dcq = _dx_col(f"mla_dcq_{i}", dq, wg["mla_w_uq"], slot, rows["mla_w_uq"], cols["mla_w_uq"], F32)
            grads.add("mla_w_ukv", sv["ckv"], dkv, slot)
            dckv = _dx_col(f"mla_dckv_{i}", dkv, wg["mla_w_ukv"], slot, rows["mla_w_ukv"], cols["mla_w_ukv"], F32)
            ddown, small["mla_q_norm_g"][slot], small["mla_kv_norm_g"][slot] = _mla_mid_bwd(
                sv["down"], dcq, dckv, dkr, sv["gq"], sv["gkv"], tabs)
            grads.add("mla_w_down", sv["h0"], ddown, slot)
            dy = _dx_row(f"mla_dh_{i}", ddown, wg["mla_w_down"], slot, rows["mla_w_down"], cols["mla_w_down"], F32,
                         epilogue=_epi_residual, extra=dz)
        else:
            pre = "sb" if kind == 1 else "ca"
            grads.add(f"{pre}_w_o", sv["o"], dz, slot)
            do = _dx_row(f"{pre}_do_{i}", dz, wg[f"{pre}_w_o"], slot, rows[f"{pre}_w_o"], cols[f"{pre}_w_o"], F32)
            if kind == 1:
                dq, dk, dv = _sb_attn_bwd(sv["qkv"], do)
            else:
                dq, dk, dv, dtiles = _ca_attn_bwd(sv["qkv"], sv["tiles"], do, sv["o"], sv["lse"])
                small["ca_rel_bias"] = [jnp.transpose(_ca_table_grad(dtiles)[:, 0, :REL_TABLE])]
            dqkv = jnp.concatenate([dq, dk, dv], 1)
            grads.add(f"{pre}_w_qkv", sv["h0"], dqkv, slot)
            dy = _dx_col(f"{pre}_dh_{i}", dqkv, wg[f"{pre}_w_qkv"], slot, rows[f"{pre}_w_qkv"], cols[f"{pre}_w_qkv"],
                         F32, epilogue=_epi_residual, extra=dz)
    small = {k: jnp.stack([g.reshape(g.shape[-2:]) if k == "ca_rel_bias" else g[0] for g in v]) for k, v in small.items()}
    return loss, dy, grads.buf, small


SMALL = ("ln_mix_g", "ln_mix_b", "ln_ffn_g", "ln_ffn_b", "mla_q_norm_g", "mla_kv_norm_g", "ca_rel_bias")


def _pack_small(parts, width):
    flat = jnp.concatenate([parts[k].reshape(-1) for k in SMALL])
    rows = -(-flat.shape[0] // width)
    rows += -rows % 8
    return jnp.pad(flat, (0, rows * width - flat.shape[0])).reshape(rows, width)


def _unpack_small(pack, like):
    flat, out, at = pack.reshape(-1), {}, 0
    for k in SMALL:
        n = int(np.prod(like[k].shape))
        out[k] = flat[at:at + n].reshape(like[k].shape)
        at += n
    return out


def kernel(x, ln_mix_g, ln_mix_b, ln_ffn_g, ln_ffn_b, ffn_w_in, ffn_w_out, mla_w_down, mla_q_norm_g, mla_w_uq, mla_kv_norm_g, mla_w_ukv, mla_w_o, sb_w_qkv, sb_w_o, ca_w_qkv, ca_rel_bias, ca_w_o, loss_target, m_ln_mix_g, m_ln_mix_b, m_ln_ffn_g, m_ln_ffn_b, m_ffn_w_in, m_ffn_w_out, m_mla_w_down, m_mla_q_norm_g, m_mla_w_uq, m_mla_kv_norm_g, m_mla_w_ukv, m_mla_w_o, m_sb_w_qkv, m_sb_w_o, m_ca_w_qkv, m_ca_rel_bias, m_ca_w_o, v_ln_mix_g, v_ln_mix_b, v_ln_ffn_g, v_ln_ffn_b, v_ffn_w_in, v_ffn_w_out, v_mla_w_down, v_mla_q_norm_g, v_mla_w_uq, v_mla_kv_norm_g, v_mla_w_ukv, v_mla_w_o, v_sb_w_qkv, v_sb_w_o, v_ca_w_qkv, v_ca_rel_bias, v_ca_w_o):
    w = dict(zip(WEIGHTS, (ln_mix_g, ln_mix_b, ln_ffn_g, ln_ffn_b, ffn_w_in, ffn_w_out, mla_w_down, mla_q_norm_g,
                           mla_w_uq, mla_kv_norm_g, mla_w_ukv, mla_w_o, sb_w_qkv, sb_w_o, ca_w_qkv, ca_rel_bias,
                           ca_w_o)))
    mom1 = dict(zip(WEIGHTS, (m_ln_mix_g, m_ln_mix_b, m_ln_ffn_g, m_ln_ffn_b, m_ffn_w_in, m_ffn_w_out, m_mla_w_down,
                              m_mla_q_norm_g, m_mla_w_uq, m_mla_kv_norm_g, m_mla_w_ukv, m_mla_w_o, m_sb_w_qkv,
                              m_sb_w_o, m_ca_w_qkv, m_ca_rel_bias, m_ca_w_o)))
    mom2 = dict(zip(WEIGHTS, (v_ln_mix_g, v_ln_mix_b, v_ln_ffn_g, v_ln_ffn_b, v_ffn_w_in, v_ffn_w_out, v_mla_w_down,
                              v_mla_q_norm_g, v_mla_w_uq, v_mla_kv_norm_g, v_mla_w_ukv, v_mla_w_o, v_sb_w_qkv,
                              v_sb_w_o, v_ca_w_qkv, v_ca_rel_bias, v_ca_w_o)))
    xi, yi, ci = lax.axis_index("x"), lax.axis_index("y"), lax.axis_index("c")
    chip = 2 * xi + yi
    d_model = x.shape[-1]

    shards = _mxu_shards(w)
    layers = {k: w[k].shape[0] for k in BIG}
    dims = {k: (layers[k], shards[k].shape[0] // layers[k], shards[k].shape[1]) for k in BIG}
    wg = dict(zip(BIG, _all_gather([shards[k] for k in BIG])))
    gains = jnp.stack([w["mla_q_norm_g"], w["mla_kv_norm_g"]])
    gains = jnp.where(ci == 0, gains, 0.0)
    placed = lax.dynamic_update_slice_in_dim(jnp.zeros((*gains.shape[:2], N_CHIPS, gains.shape[2]), F32),
                                             gains[:, :, None], chip, 2)
    full_gains = _all_reduce_small("norm_gain_gather", placed.reshape(2 * gains.shape[1], -1))
    full_gains = full_gains.reshape(2, gains.shape[1], -1)
    p = {"ln_mix_g": ln_mix_g, "ln_mix_b": ln_mix_b, "ln_ffn_g": ln_ffn_g, "ln_ffn_b": ln_ffn_b,
         "mla_q_norm_g": full_gains[0], "mla_kv_norm_g": full_gains[1], "ca_rel_bias": ca_rel_bias}

    loss, grad_x, gbuf, small = _local_step(x[0], loss_target[0], wg, dims, p, _rope_tables(x.shape[1]))
    loss = lax.psum(loss[0, 0], ("x", "y", "c"))

    small = _unpack_small(_all_reduce_small("small_grad_all_reduce", _pack_small(small, d_model)), small)
    grad = {k: small[k] for k in ("ln_mix_g", "ln_mix_b", "ln_ffn_g", "ln_ffn_b", "ca_rel_bias")}
    for k in ("mla_q_norm_g", "mla_kv_norm_g"):
        g = small[k].reshape(small[k].shape[0], N_CHIPS, -1)
        grad[k] = lax.dynamic_index_in_dim(g, chip, 1, keepdims=False)

    c1, chip1 = jnp.reshape(ci, (1,)).astype(jnp.int32), jnp.reshape(chip, (1,)).astype(jnp.int32)
    got = _sibling_exchange([gbuf[k] for k in BIG])
    pair = [_pair_sum(f"pair_sum_{k}", gbuf[k], r, c1) for k, r in zip(BIG, got)]
    arrived = _chip_exchange(pair)
    halves = [_chip_sum(f"chip_sum_{k}", a, r, chip1) for k, a, r in zip(BIG, pair, arrived)]
    for k, g in zip(BIG, _sibling_share(halves)):
        grad[k] = _unpad_grad(k, g.reshape(-1, g.shape[-1]), w[k])

    delta, new_m, new_v = {}, {}, {}
    for k in WEIGHTS:
        flat = lambda a: a.reshape(-1, a.shape[-1])
        dl, m2, v2 = _adamw(f"adamw_{k}", flat(w[k]), flat(grad[k]), flat(mom1[k]), flat(mom2[k]))
        delta[k], new_m[k], new_v[k] = dl.reshape(w[k].shape), m2.reshape(w[k].shape), v2.reshape(w[k].shape)
    return (loss, grad_x[None], *[grad[k] for k in WEIGHTS], *[delta[k] for k in WEIGHTS],
            *[new_m[k] for k in WEIGHTS], *[new_v[k] for k in WEIGHTS])
```

```python
import functools

import numpy as np
import jax
import jax.numpy as jnp
from jax import lax
from jax.experimental import pallas as pl
from jax.experimental.pallas import tpu as pltpu

F32, BF16 = jnp.float32, jnp.bfloat16
MXU_DTYPE = BF16
WIRE_DTYPE = BF16

DEPTH = 4
HEADS = 16
HEAD_DIM = 128
CHUNK_SHIFT = 6
TQ = 256
MLA_ROPE = 64
MLA_QK_DIM = 192
HEAD_PAD = 256
CA_LEFT_CHUNKS = 8
CA_LEFT_BLOCKS = (CA_LEFT_CHUNKS << CHUNK_SHIFT) // TQ
REL_CLIP_LEFT = 128
REL_TABLE = 192
ROPE_THETA = 10000.0
LN_EPS = 1e-5
RMS_EPS = 1e-6
ALPHA = (2.0 * DEPTH) ** 0.25
NEG = -1e30
ADAM_LR, ADAM_B1, ADAM_B2, ADAM_EPS, ADAM_WD, ADAM_STEP = 0.001, 0.9, 0.999, 1e-08, 0.01, 10
N_CHIPS = 4
N_DEV = 8
VMEM_LIMIT = 48 << 20
MESH = pl.DeviceIdType.MESH
ANY = pl.BlockSpec(memory_space=pl.ANY)
VMEM_SPEC = pl.BlockSpec(memory_space=pltpu.VMEM)

NN = (((1,), (0,)), ((), ()))
NT = (((1,), (1,)), ((), ()))
TN = (((0,), (0,)), ((), ()))


def _dot(a, b, dims=NN):
    return lax.dot_general(a, b, dims, preferred_element_type=F32)


def _exact_dot(x, u):
    hi = x.astype(BF16)
    r1 = x - hi.astype(F32)
    mid = r1.astype(BF16)
    lo = (r1 - mid.astype(F32)).astype(BF16)
    return _dot(hi, u) + _dot(mid, u) + _dot(lo, u)


def _params(*sem):
    return pltpu.CompilerParams(dimension_semantics=sem, vmem_limit_bytes=VMEM_LIMIT)


def _tile(n, pref):
    for t in (1024, 768, 512, 384, 256, 128):
        if t <= pref and n % t == 0:
            return t
    return n


def _mm(name, a, b, extras, *, grid, a_spec, b_spec, extra_specs, out_specs, out_shape, dims,
        epilogue, acc_shape, aliases=None):
    nk = grid[2]
    n_ex = len(extras)

    def body(*refs):
        a_ref, b_ref = refs[:2]
        ex = refs[2:2 + n_ex]
        outs = refs[2 + n_ex:-1]
        acc = refs[-1]
        k = pl.program_id(2)

        @pl.when(k == 0)
        def _():
            acc[...] = jnp.zeros_like(acc)

        acc[...] += lax.dot_general(a_ref[...].astype(MXU_DTYPE), b_ref[...].astype(MXU_DTYPE), dims,
                                    preferred_element_type=F32)

        @pl.when(k == nk - 1)
        def _():
            epilogue(acc[...], ex, outs)

    return pl.pallas_call(
        body, grid=grid, in_specs=[a_spec, b_spec, *extra_specs], out_specs=out_specs, out_shape=out_shape,
        scratch_shapes=[pltpu.VMEM(acc_shape, F32)], name=name, input_output_aliases=aliases or {},
        compiler_params=_params("parallel", "parallel", "arbitrary"))(a, b, *extras)


def _epi_store(acc, ex, outs):
    outs[0][...] = acc.astype(outs[0].dtype)


def _epi_relu2(acc, ex, outs):
    outs[0][...] = acc
    r = jnp.maximum(acc, 0.0)
    outs[1][...] = (r * r).astype(outs[1].dtype)


def _epi_drelu2(acc, ex, outs):
    outs[0][...] = (acc * (2.0 * jnp.maximum(ex[0][...], 0.0))).astype(outs[0].dtype)


def _epi_residual(acc, ex, outs):
    outs[0][...] = acc + ALPHA * ex[0][...]


def _fwd_col(name, x, wg, l, rows, cols, dtype, epilogue=_epi_store, n_out=1, dtypes=None):
    m = x.shape[0]
    tm, tn, tk = _tile(m, 1024), _tile(cols, 1024), _tile(rows, 512)
    nps, kt = cols // tn, rows // tk
    dtypes = dtypes or (dtype,)
    out = pl.BlockSpec((tm, tn), lambda i, j, k: (i, j))
    res = _mm(name, x, wg, (), grid=(m // tm, N_CHIPS * nps, kt),
              a_spec=pl.BlockSpec((tm, tk), lambda i, j, k: (i, k)),
              b_spec=pl.BlockSpec((None, tk, tn), lambda i, j, k: (j // nps, l * kt + k, j % nps)),
              extra_specs=(), out_specs=[out] * len(dtypes),
              out_shape=[jax.ShapeDtypeStruct((m, N_CHIPS * cols), d) for d in dtypes],
              dims=NN, epilogue=epilogue, acc_shape=(tm, tn))
    return res if len(dtypes) > 1 else res[0]


def _fwd_row(name, x, wg, l, rows, cols, dtype):
    m = x.shape[0]
    tm, tn, tk = _tile(m, 1024), _tile(cols, 1024), _tile(rows, 512)
    kps = rows // tk
    return _mm(name, x, wg, (), grid=(m // tm, cols // tn, N_CHIPS * kps),
               a_spec=pl.BlockSpec((tm, tk), lambda i, j, k: (i, k)),
               b_spec=pl.BlockSpec((None, tk, tn), lambda i, j, k: (k // kps, l * kps + k % kps, j)),
               extra_specs=(), out_specs=[pl.BlockSpec((tm, tn), lambda i, j, k: (i, j))],
               out_shape=[jax.ShapeDtypeStruct((m, cols), dtype)],
               dims=NN, epilogue=_epi_store, acc_shape=(tm, tn))[0]


def _dx_col(name, dy, wg, l, rows, cols, dtype, epilogue=_epi_store, extra=None):
    m = dy.shape[0]
    tm, tn, tk = _tile(m, 1024), _tile(rows, 1024), _tile(cols, 512)
    kps, nt = cols // tk, rows // tn
    tile = pl.BlockSpec((tm, tn), lambda i, j, k: (i, j))
    return _mm(name, dy, wg, () if extra is None else (extra,), grid=(m // tm, nt, N_CHIPS * kps),
               a_spec=pl.BlockSpec((tm, tk), lambda i, j, k: (i, k)),
               b_spec=pl.BlockSpec((None, tn, tk), lambda i, j, k: (k // kps, l * nt + j, k % kps)),
               extra_specs=() if extra is None else (tile,), out_specs=[tile],
               out_shape=[jax.ShapeDtypeStruct((m, rows), dtype)],
               dims=NT, epilogue=epilogue, acc_shape=(tm, tn))[0]


def _dx_row(name, dy, wg, l, rows, cols, dtype, epilogue=_epi_store, extra=None):
    m = dy.shape[0]
    tm, tn, tk = _tile(m, 1024), _tile(rows, 1024), _tile(cols, 512)
    nps = rows // tn
    tile = pl.BlockSpec((tm, tn), lambda i, j, k: (i, j))
    return _mm(name, dy, wg, () if extra is None else (extra,), grid=(m // tm, N_CHIPS * nps, cols // tk),
               a_spec=pl.BlockSpec((tm, tk), lambda i, j, k: (i, k)),
               b_spec=pl.BlockSpec((None, tn, tk), lambda i, j, k: (j // nps, l * nps + j % nps, k)),
               extra_specs=() if extra is None else (tile,), out_specs=[tile],
               out_shape=[jax.ShapeDtypeStruct((m, N_CHIPS * rows), dtype)],
               dims=NT, epilogue=epilogue, acc_shape=(tm, tn))[0]


def _dw(name, x, dy, gbuf, l, layers, rows, cols, col_sharded):
    s_tok = x.shape[0]
    tm, tn, tk = _tile(rows, 1024), _tile(cols, 1024), _tile(s_tok, 512)
    mt, nps = rows // tm, cols // tn
    if col_sharded:
        grid = (mt, N_CHIPS * nps, s_tok // tk)
        out = pl.BlockSpec((None, tm, tn), lambda i, j, k: (j // nps, l * mt + i, j % nps))
    else:
        grid = (N_CHIPS * mt, nps, s_tok // tk)
        out = pl.BlockSpec((None, tm, tn), lambda i, j, k: (i // mt, l * mt + i % mt, j))
    return _mm(name, x, dy, () if gbuf is None else (gbuf,), grid=grid,
               a_spec=pl.BlockSpec((tk, tm), lambda i, j, k: (k, i)),
               b_spec=pl.BlockSpec((tk, tn), lambda i, j, k: (k, j)),
               extra_specs=() if gbuf is None else (ANY,), out_specs=[out],
               out_shape=[jax.ShapeDtypeStruct((N_CHIPS, layers * rows, cols), WIRE_DTYPE)],
               dims=TN, epilogue=_epi_store, acc_shape=(tm, tn),
               aliases=None if gbuf is None else {2: 0})[0]


def _ln_fwd(name, h, m, g, b):
    s, d = h.shape
    tm = _tile(s, 256)
    row = pl.BlockSpec((tm, d), lambda i: (i, 0))
    vec = pl.BlockSpec((1, d), lambda i: (0, 0))

    def body(h_ref, m_ref, g_ref, b_ref, y_ref, xh_ref, r_ref):
        z = ALPHA * h_ref[...] + m_ref[...]
        mu = jnp.mean(z, -1, keepdims=True)
        zc = z - mu
        r = lax.rsqrt(jnp.mean(zc * zc, -1, keepdims=True) + LN_EPS)
        xh = zc * r
        xh_ref[...] = xh
        r_ref[...] = r
        y_ref[...] = xh * g_ref[...] + b_ref[...]

    return pl.pallas_call(
        body, grid=(s // tm,), in_specs=[row, row, vec, vec],
        out_specs=[row, row, pl.BlockSpec((tm, 1), lambda i: (i, 0))],
        out_shape=[jax.ShapeDtypeStruct((s, d), F32), jax.ShapeDtypeStruct((s, d), F32),
                   jax.ShapeDtypeStruct((s, 1), F32)],
        name=name, compiler_params=_params("parallel"))(h, m, g, b)


def _ln_bwd(name, dy, xh, r, g):
    s, d = dy.shape
    tm = _tile(s, 256)
    row = pl.BlockSpec((tm, d), lambda i: (i, 0))
    vec = pl.BlockSpec((1, d), lambda i: (0, 0))

    def body(dy_ref, xh_ref, r_ref, g_ref, dz_ref, dg_ref, db_ref):
        i = pl.program_id(0)
        dy_, xh_ = dy_ref[...], xh_ref[...]
        dyg = dy_ * g_ref[...]
        m1 = jnp.mean(dyg, -1, keepdims=True)
        m2 = jnp.mean(dyg * xh_, -1, keepdims=True)
        dz_ref[...] = r_ref[...] * (dyg - m1 - xh_ * m2)
        pg = jnp.sum(dy_ * xh_, 0, keepdims=True)
        pb = jnp.sum(dy_, 0, keepdims=True)

        @pl.when(i == 0)
        def _():
            dg_ref[...] = pg
            db_ref[...] = pb

        @pl.when(i > 0)
        def _():
            dg_ref[...] += pg
            db_ref[...] += pb

    return pl.pallas_call(
        body, grid=(s // tm,), in_specs=[row, row, pl.BlockSpec((tm, 1), lambda i: (i, 0)), vec],
        out_specs=[row, vec, vec],
        out_shape=[jax.ShapeDtypeStruct((s, d), F32), jax.ShapeDtypeStruct((1, d), F32),
                   jax.ShapeDtypeStruct((1, d), F32)],
        name=name, compiler_params=_params("arbitrary"))(dy, xh, r, g)


def _loss_head(y, t):
    s, d = y.shape
    tm = _tile(s, 256)
    row = pl.BlockSpec((tm, d), lambda i: (i, 0))

    def body(y_ref, t_ref, l_ref, dy_ref):
        i = pl.program_id(0)
        e = y_ref[...] - t_ref[...]
        dy_ref[...] = e * (1.0 / d)
        part = 0.5 * jnp.sum(jnp.mean(e * e, -1, keepdims=True), 0, keepdims=True)

        @pl.when(i == 0)
        def _():
            l_ref[...] = part

        @pl.when(i > 0)
        def _():
            l_ref[...] += part

    return pl.pallas_call(
        body, grid=(s // tm,), in_specs=[row, row],
        out_specs=[pl.BlockSpec((1, 1), lambda i: (0, 0)), row],
        out_shape=[jax.ShapeDtypeStruct((1, 1), F32), jax.ShapeDtypeStruct((s, d), F32)],
        name="loss_head", compiler_params=_params("arbitrary"))(y, t)


def _rope_tables(s):
    half = MLA_ROPE // 2
    inv = ROPE_THETA ** (-jnp.arange(half, dtype=F32) / half)
    ang = jnp.arange(s).astype(F32)[:, None] * inv[None, :]
    cos, sin = jnp.cos(ang), jnp.sin(ang)
    c = jnp.concatenate([cos, cos, jnp.ones((s, 128 - MLA_ROPE), F32)], 1)
    s1 = jnp.concatenate([-sin, jnp.zeros((s, 128 - half), F32)], 1)
    s2 = jnp.concatenate([jnp.zeros((s, half), F32), sin, jnp.zeros((s, 128 - MLA_ROPE), F32)], 1)
    return c, s1, s2


def _rope(x, c, s1, s2):
    half = MLA_ROPE // 2
    return x * c + pltpu.roll(x, 128 - half, 1) * s1 + pltpu.roll(x, half, 1) * s2


def _rope_t(dy, c, s1, s2):
    half = MLA_ROPE // 2
    return dy * c + pltpu.roll(dy * s1, half, 1) + pltpu.roll(dy * s2, 128 - half, 1)


def _mla_mid_fwd(down, gq, gkv, tabs):
    s, w = down.shape
    ql, kvl = gq.shape[1], gkv.shape[1]
    tm = _tile(s, 256)

    def body(d_ref, gq_ref, gkv_ref, c_ref, s1_ref, s2_ref, cq_ref, ckv_ref, kr_ref):
        cq = d_ref[:, :ql]
        ckv = d_ref[:, ql:ql + kvl]
        cq_ref[...] = (cq * lax.rsqrt(jnp.mean(cq * cq, -1, keepdims=True) + RMS_EPS)
                       * gq_ref[...]).astype(cq_ref.dtype)
        ckv_ref[...] = (ckv * lax.rsqrt(jnp.mean(ckv * ckv, -1, keepdims=True) + RMS_EPS)
                        * gkv_ref[...]).astype(ckv_ref.dtype)
        kr_ref[...] = _rope(d_ref[:, ql + kvl:], c_ref[...], s1_ref[...], s2_ref[...]).astype(kr_ref.dtype)

    tab = pl.BlockSpec((tm, 128), lambda i: (i, 0))
    return pl.pallas_call(
        body, grid=(s // tm,),
        in_specs=[pl.BlockSpec((tm, w), lambda i: (i, 0)), pl.BlockSpec((1, ql), lambda i: (0, 0)),
                  pl.BlockSpec((1, kvl), lambda i: (0, 0)), tab, tab, tab],
        out_specs=[pl.BlockSpec((tm, ql), lambda i: (i, 0)), pl.BlockSpec((tm, kvl), lambda i: (i, 0)), tab],
        out_shape=[jax.ShapeDtypeStruct((s, ql), MXU_DTYPE), jax.ShapeDtypeStruct((s, kvl), MXU_DTYPE),
                   jax.ShapeDtypeStruct((s, 128), MXU_DTYPE)],
        name="mla_mid_fwd", compiler_params=_params("parallel"))(down, gq, gkv, *tabs)


def _mla_mid_bwd(down, dcq, dckv, dkr, gq, gkv, tabs):
    s, w = down.shape
    ql, kvl = gq.shape[1], gkv.shape[1]
    tm = _tile(s, 256)

    def rms_bwd(x, dy, g):
        n = x.shape[1]
        r = lax.rsqrt(jnp.mean(x * x, -1, keepdims=True) + RMS_EPS)
        dyg = dy * g
        dx = r * dyg - x * (r * r * r * (1.0 / n)) * jnp.sum(dyg * x, -1, keepdims=True)
        return dx, jnp.sum(dy * x * r, 0, keepdims=True)

    def body(d_ref, dcq_ref, dckv_ref, dkr_ref, gq_ref, gkv_ref, c_ref, s1_ref, s2_ref, o_ref, dgq_ref, dgkv_ref):
        i = pl.program_id(0)
        dxq, pq = rms_bwd(d_ref[:, :ql], dcq_ref[...], gq_ref[...])
        dxkv, pkv = rms_bwd(d_ref[:, ql:ql + kvl], dckv_ref[...], gkv_ref[...])
        o_ref[:, :ql] = dxq.astype(o_ref.dtype)
        o_ref[:, ql:ql + kvl] = dxkv.astype(o_ref.dtype)
        o_ref[:, ql + kvl:] = _rope_t(dkr_ref[...], c_ref[...], s1_ref[...], s2_ref[...]).astype(o_ref.dtype)

        @pl.when(i == 0)
        def _():
            dgq_ref[...] = pq
            dgkv_ref[...] = pkv

        @pl.when(i > 0)
        def _():
            dgq_ref[...] += pq
            dgkv_ref[...] += pkv

    tab = pl.BlockSpec((tm, 128), lambda i: (i, 0))
    vq = pl.BlockSpec((1, ql), lambda i: (0, 0))
    vkv = pl.BlockSpec((1, kvl), lambda i: (0, 0))
    full = pl.BlockSpec((tm, w), lambda i: (i, 0))
    return pl.pallas_call(
        body, grid=(s // tm,),
        in_specs=[full, pl.BlockSpec((tm, ql), lambda i: (i, 0)), pl.BlockSpec((tm, kvl), lambda i: (i, 0)), tab,
                  vq, vkv, tab, tab, tab],
        out_specs=[full, vq, vkv],
        out_shape=[jax.ShapeDtypeStruct((s, w), MXU_DTYPE), jax.ShapeDtypeStruct((1, ql), F32),
                   jax.ShapeDtypeStruct((1, kvl), F32)],
        name="mla_mid_bwd", compiler_params=_params("arbitrary"))(down, dcq, dckv, dkr, gq, gkv, *tabs)


def _iota2():
    return (lax.broadcasted_iota(jnp.int32, (TQ, TQ), 0), lax.broadcasted_iota(jnp.int32, (TQ, TQ), 1))


def _mla_attn_fwd(q, kv, kr, tabs):
    s = q.shape[0]
    nq = s // TQ
    scale = MLA_QK_DIM ** -0.5

    def body(q_ref, kn_ref, v_ref, kr_ref, c_ref, s1_ref, s2_ref, o_ref, lse_ref):
        i = pl.program_id(1)
        row, col = _iota2()
        qn = q_ref[:, :HEAD_DIM].astype(MXU_DTYPE)
        qr = _rope(q_ref[:, HEAD_DIM:], c_ref[...], s1_ref[...], s2_ref[...]).astype(MXU_DTYPE)
        qc = jnp.right_shift(i * TQ + row, CHUNK_SHIFT)

        def step(kb, carry):
            m, l, acc = carry
            ks = pl.multiple_of(kb * TQ, TQ)
            sc = (_dot(qn, kn_ref[pl.ds(ks, TQ), :].astype(MXU_DTYPE), NT)
                  + _dot(qr, kr_ref[pl.ds(ks, TQ), :].astype(MXU_DTYPE), NT)) * scale
            sc = jnp.where(jnp.right_shift(ks + col, CHUNK_SHIFT) <= qc, sc, NEG)
            m_new = jnp.maximum(m, jnp.max(sc, -1, keepdims=True))
            p = jnp.exp(sc - m_new)
            corr = jnp.exp(m - m_new)
            l = corr * l + jnp.sum(p, -1, keepdims=True)
            acc = corr * acc + _dot(p.astype(MXU_DTYPE), v_ref[pl.ds(ks, TQ), :].astype(MXU_DTYPE))
            return m_new, l, acc

        m, l, acc = lax.fori_loop(0, i + 1, step, (jnp.full((TQ, 1), NEG, F32), jnp.zeros((TQ, 1), F32),
                                                   jnp.zeros((TQ, HEAD_DIM), F32)))
        o_ref[...] = acc / l
        lse_ref[...] = m + jnp.log(l)

    tab = pl.BlockSpec((TQ, 128), lambda h, i: (i, 0))
    return pl.pallas_call(
        body, grid=(HEADS, nq),
        in_specs=[pl.BlockSpec((TQ, HEAD_PAD), lambda h, i: (i, h)),
                  pl.BlockSpec((s, HEAD_DIM), lambda h, i: (0, 2 * h)),
                  pl.BlockSpec((s, HEAD_DIM), lambda h, i: (0, 2 * h + 1)),
                  pl.BlockSpec((s, 128), lambda h, i: (0, 0)), tab, tab, tab],
        out_specs=[pl.BlockSpec((TQ, HEAD_DIM), lambda h, i: (i, h)),
                   pl.BlockSpec((None, TQ, 1), lambda h, i: (h, i, 0))],
        out_shape=[jax.ShapeDtypeStruct((s, HEADS * HEAD_DIM), F32), jax.ShapeDtypeStruct((HEADS, s, 1), F32)],
        name="mla_attn_fwd", compiler_params=_params("parallel", "parallel"))(q, kv, kv, kr, *tabs)


def _mla_attn_bwd(q, kv, kr, tabs, do, o, lse):
    s = q.shape[0]
    nq = s // TQ
    scale = MLA_QK_DIM ** -0.5

    def body(q_ref, kn_ref, v_ref, kr_ref, c_ref, s1_ref, s2_ref, do_ref, o_ref, lse_ref,
             dq_ref, dkv_ref, dkr_ref, dkv_acc, dkr_acc):
        h, i = pl.program_id(0), pl.program_id(1)
        row, col = _iota2()

        @pl.when(i == 0)
        def _():
            dkv_acc[...] = jnp.zeros_like(dkv_acc)

        @pl.when((h == 0) & (i == 0))
        def _():
            dkr_acc[...] = jnp.zeros_like(dkr_acc)

        tabs_i = (c_ref[...], s1_ref[...], s2_ref[...])
        qn = q_ref[:, :HEAD_DIM].astype(MXU_DTYPE)
        qr = _rope(q_ref[:, HEAD_DIM:], *tabs_i).astype(MXU_DTYPE)
        qc = jnp.right_shift(i * TQ + row, CHUNK_SHIFT)
        do_ = do_ref[...]
        delta = jnp.sum(do_ * o_ref[...], -1, keepdims=True)
        lse_ = lse_ref[...]
        dob = do_.astype(MXU_DTYPE)

        def step(kb, carry):
            dqn, dqr = carry
            ks = pl.multiple_of(kb * TQ, TQ)
            kn = kn_ref[pl.ds(ks, TQ), :].astype(MXU_DTYPE)
            krb = kr_ref[pl.ds(ks, TQ), :].astype(MXU_DTYPE)
            v = v_ref[pl.ds(ks, TQ), :].astype(MXU_DTYPE)
            sc = (_dot(qn, kn, NT) + _dot(qr, krb, NT)) * scale
            sc = jnp.where(jnp.right_shift(ks + col, CHUNK_SHIFT) <= qc, sc, NEG)
            p = jnp.exp(sc - lse_)
            ds = (p * (_dot(dob, v, NT) - delta) * scale).astype(MXU_DTYPE)
            dkv_acc[pl.ds(ks, TQ), :HEAD_DIM] += _dot(ds, qn, TN)
            dkv_acc[pl.ds(ks, TQ), HEAD_DIM:] += _dot(p.astype(MXU_DTYPE), dob, TN)
            dkr_acc[pl.ds(ks, TQ), :] += _dot(ds, qr, TN)
            return dqn + _dot(ds, kn), dqr + _dot(ds, krb)

        dqn, dqr = lax.fori_loop(0, i + 1, step, (jnp.zeros((TQ, HEAD_DIM), F32), jnp.zeros((TQ, 128), F32)))
        dq_ref[:, :HEAD_DIM] = dqn
        dq_ref[:, HEAD_DIM:] = _rope_t(dqr, *tabs_i)

        @pl.when(i == nq - 1)
        def _():
            dkv_ref[...] = dkv_acc[...].astype(dkv_ref.dtype)

        @pl.when((h == HEADS - 1) & (i == nq - 1))
        def _():
            dkr_ref[...] = dkr_acc[...]

    tab = pl.BlockSpec((TQ, 128), lambda h, i: (i, 0))
    qblk = pl.BlockSpec((TQ, HEAD_PAD), lambda h, i: (i, h))
    oblk = pl.BlockSpec((TQ, HEAD_DIM), lambda h, i: (i, h))
    return pl.pallas_call(
        body, grid=(HEADS, nq),
        in_specs=[qblk, pl.BlockSpec((s, HEAD_DIM), lambda h, i: (0, 2 * h)),
                  pl.BlockSpec((s, HEAD_DIM), lambda h, i: (0, 2 * h + 1)),
                  pl.BlockSpec((s, 128), lambda h, i: (0, 0)), tab, tab, tab, oblk, oblk,
                  pl.BlockSpec((None, TQ, 1), lambda h, i: (h, i, 0))],
        out_specs=[qblk, pl.BlockSpec((s, HEAD_PAD), lambda h, i: (0, h)),
                   pl.BlockSpec((s, 128), lambda h, i: (0, 0))],
        out_shape=[jax.ShapeDtypeStruct((s, HEADS * HEAD_PAD), F32), jax.ShapeDtypeStruct((s, HEADS * HEAD_PAD), MXU_DTYPE),
                   jax.ShapeDtypeStruct((s, 128), F32)],
        scratch_shapes=[pltpu.VMEM((s, HEAD_PAD), F32), pltpu.VMEM((s, 128), F32)],
        name="mla_attn_bwd", compiler_params=_params("arbitrary", "arbitrary"))(q, kv, kv, kr, *tabs, do, o, lse)


def _qkv_specs(s):
    return [pl.BlockSpec((TQ, HEAD_DIM), lambda h, i: (i, h)),
            pl.BlockSpec((s, HEAD_DIM), lambda h, i: (0, HEADS + h)),
            pl.BlockSpec((s, HEAD_DIM), lambda h, i: (0, 2 * HEADS + h))]


def _sb_terms(z):
    sp = jnp.log(1.0 + jnp.exp(-jnp.abs(z)))
    return jnp.minimum(z, 0.0) - sp, jnp.minimum(-z, 0.0) - sp


def _sb_attn_fwd(qkv):
    s = qkv.shape[0]
    nq = s // TQ
    scale = HEAD_DIM ** -0.5

    def body(q_ref, k_ref, v_ref, o_ref):
        i = pl.program_id(1)
        row, col = _iota2()
        after = (row > col).astype(BF16)
        q = q_ref[...].astype(MXU_DTYPE)
        qpos = i * TQ + row

        def step(n, carry):
            tail, acc = carry
            ks = pl.multiple_of((i - n) * TQ, TQ)
            z = _dot(q, k_ref[pl.ds(ks, TQ), :].astype(MXU_DTYPE), NT) * scale
            strict = (ks + col) < qpos
            lb, l1 = _sb_terms(z)
            l1 = jnp.where(strict, l1, 0.0)
            a = jnp.where(strict, jnp.exp(lb + tail + _exact_dot(l1, after)), 0.0)
            acc = acc + _dot(a.astype(MXU_DTYPE), v_ref[pl.ds(ks, TQ), :].astype(MXU_DTYPE))
            return tail + jnp.sum(l1, -1, keepdims=True), acc

        _, acc = lax.fori_loop(0, i + 1, step, (jnp.zeros((TQ, 1), F32), jnp.zeros((TQ, HEAD_DIM), F32)))
        o_ref[...] = acc

    return pl.pallas_call(
        body, grid=(HEADS, nq), in_specs=_qkv_specs(s),
        out_specs=pl.BlockSpec((TQ, HEAD_DIM), lambda h, i: (i, h)),
        out_shape=jax.ShapeDtypeStruct((s, HEADS * HEAD_DIM), F32),
        name="sb_attn_fwd", compiler_params=_params("parallel", "parallel"))(qkv, qkv, qkv)


def _sb_attn_bwd(qkv, do):
    s = qkv.shape[0]
    nq = s // TQ
    scale = HEAD_DIM ** -0.5

    def body(q_ref, k_ref, v_ref, do_ref, dq_ref, dk_ref, dv_ref, a_buf, dk_acc, dv_acc):
        i = pl.program_id(1)
        row, col = _iota2()
        after = (row > col).astype(BF16)
        before = (row < col).astype(BF16)

        @pl.when(i == 0)
        def _():
            dk_acc[...] = jnp.zeros_like(dk_acc)
            dv_acc[...] = jnp.zeros_like(dv_acc)

        q = q_ref[...].astype(MXU_DTYPE)
        dob = do_ref[...].astype(MXU_DTYPE)
        qpos = i * TQ + row

        def weights(n, tail):
            kb = i - n
            ks = pl.multiple_of(kb * TQ, TQ)
            z = _dot(q, k_ref[pl.ds(ks, TQ), :].astype(MXU_DTYPE), NT) * scale
            strict = (ks + col) < qpos
            lb, l1 = _sb_terms(z)
            l1 = jnp.where(strict, l1, 0.0)
            a = jnp.where(strict, jnp.exp(lb + tail + _exact_dot(l1, after)), 0.0)
            a_buf[kb] = a
            dv_acc[pl.ds(ks, TQ), :] += _dot(a.astype(MXU_DTYPE), dob, TN)
            return tail + jnp.sum(l1, -1, keepdims=True)

        lax.fori_loop(0, i + 1, weights, jnp.zeros((TQ, 1), F32))

        def grads(kb, carry):
            head, dq = carry
            ks = pl.multiple_of(kb * TQ, TQ)
            k = k_ref[pl.ds(ks, TQ), :].astype(MXU_DTYPE)
            z = _dot(q, k, NT) * scale
            strict = (ks + col) < qpos
            e = jnp.exp(-jnp.abs(z))
            beta = jnp.where(z >= 0.0, 1.0, e) / (1.0 + e)
            w = _dot(dob, v_ref[pl.ds(ks, TQ), :].astype(MXU_DTYPE), NT) * a_buf[kb]
            dz = jnp.where(strict, w * (1.0 - beta) - beta * (head + _exact_dot(w, before)), 0.0) * scale
            dzb = dz.astype(MXU_DTYPE)
            dk_acc[pl.ds(ks, TQ), :] += _dot(dzb, q, TN)
            return head + jnp.sum(w, -1, keepdims=True), dq + _dot(dzb, k)

        _, dq = lax.fori_loop(0, i + 1, grads, (jnp.zeros((TQ, 1), F32), jnp.zeros((TQ, HEAD_DIM), F32)))
        dq_ref[...] = dq.astype(dq_ref.dtype)

        @pl.when(i == nq - 1)
        def _():
            dk_ref[...] = dk_acc[...].astype(dk_ref.dtype)
            dv_ref[...] = dv_acc[...].astype(dv_ref.dtype)

    blk = pl.BlockSpec((TQ, HEAD_DIM), lambda h, i: (i, h))
    col_h = pl.BlockSpec((s, HEAD_DIM), lambda h, i: (0, h))
    shp = jax.ShapeDtypeStruct((s, HEADS * HEAD_DIM), MXU_DTYPE)
    return pl.pallas_call(
        body, grid=(HEADS, nq), in_specs=_qkv_specs(s) + [blk],
        out_specs=[blk, col_h, col_h], out_shape=[shp, shp, shp],
        scratch_shapes=[pltpu.VMEM((nq, TQ, TQ), F32), pltpu.VMEM((s, HEAD_DIM), F32), pltpu.VMEM((s, HEAD_DIM), F32)],
        name="sb_attn_bwd", compiler_params=_params("arbitrary", "arbitrary"))(qkv, qkv, qkv, do)


LANES = 128
LAST_REL = (1 << CHUNK_SHIFT) - 1


def _ca_subtiles():
    nb = TQ // LANES
    return [(d, a, b, -d * TQ + (b - a) * LANES) for d in range(3) for a in range(nb) for b in range(nb)]


def _ca_bias_tiles(rel_bias):
    table = jnp.pad(jnp.transpose(rel_bias), ((0, 0), (0, 2 * LANES - REL_TABLE)))[:, None]

    def unskew(x, row):
        for b in range(7):
            x = jnp.where((jnp.right_shift(row, b) & 1) == 1, pltpu.roll(x, 1 << b, 1), x)
        return x

    def body(t_ref, o_ref):
        row = lax.broadcasted_iota(jnp.int32, (LANES, LANES), 0)
        col = lax.broadcasted_iota(jnp.int32, (LANES, LANES), 1)
        lane = col[:1]
        lo, hi = t_ref[:, :LANES], t_ref[:, LANES:]
        first = jnp.sum(jnp.where(lane == 0, lo, 0.0), -1, keepdims=True)
        last = jnp.sum(jnp.where(lane == LAST_REL, hi, 0.0), -1, keepdims=True)
        hi = jnp.where(lane <= LAST_REL, hi, last)
        r_lo = unskew(jnp.broadcast_to(lo, (LANES, LANES)), row)
        r_hi = unskew(jnp.broadcast_to(hi, (LANES, LANES)), row)
        upper = col >= row
        for d, a, b, o in _ca_subtiles():
            if o >= LANES:
                piece = jnp.broadcast_to(last, (LANES, LANES))
            elif o == 0:
                piece = jnp.where(upper, r_hi, r_lo)
            elif o == -LANES:
                piece = jnp.where(upper, r_lo, first)
            else:
                piece = jnp.broadcast_to(first, (LANES, LANES))
            o_ref[d, a * LANES:(a + 1) * LANES, b * LANES:(b + 1) * LANES] = piece

    return pl.pallas_call(
        body, grid=(HEADS,), in_specs=[pl.BlockSpec((None, 1, 2 * LANES), lambda h: (h, 0, 0))],
        out_specs=pl.BlockSpec((None, 3, TQ, TQ), lambda h: (h, 0, 0, 0)),
        out_shape=jax.ShapeDtypeStruct((HEADS, 3, TQ, TQ), F32),
        name="ca_bias_tiles", compiler_params=_params("parallel"))(table)


def _ca_mask(i, ks, row, col):
    qc = jnp.right_shift(i * TQ + row, CHUNK_SHIFT)
    kc = jnp.right_shift(ks + col, CHUNK_SHIFT)
    return (kc <= qc) & (kc >= qc - CA_LEFT_CHUNKS)


def _ca_attn_fwd(qkv, tiles):
    s = qkv.shape[0]
    nq = s // TQ
    scale = HEAD_DIM ** -0.5

    def body(q_ref, k_ref, v_ref, bt_ref, o_ref, lse_ref):
        i = pl.program_id(1)
        row, col = _iota2()
        q = q_ref[...].astype(MXU_DTYPE)

        def step(kb, carry):
            m, l, acc = carry
            ks = pl.multiple_of(kb * TQ, TQ)
            sc = _dot(q, k_ref[pl.ds(ks, TQ), :].astype(MXU_DTYPE), NT) * scale + bt_ref[jnp.minimum(i - kb, 2)]
            sc = jnp.where(_ca_mask(i, ks, row, col), sc, NEG)
            m_new = jnp.maximum(m, jnp.max(sc, -1, keepdims=True))
            p = jnp.exp(sc - m_new)
            corr = jnp.exp(m - m_new)
            l = corr * l + jnp.sum(p, -1, keepdims=True)
            acc = corr * acc + _dot(p.astype(MXU_DTYPE), v_ref[pl.ds(ks, TQ), :].astype(MXU_DTYPE))
            return m_new, l, acc

        m, l, acc = lax.fori_loop(jnp.maximum(i - CA_LEFT_BLOCKS, 0), i + 1, step,
                                  (jnp.full((TQ, 1), NEG, F32), jnp.zeros((TQ, 1), F32),
                                   jnp.zeros((TQ, HEAD_DIM), F32)))
        o_ref[...] = acc / l
        lse_ref[...] = m + jnp.log(l)

    return pl.pallas_call(
        body, grid=(HEADS, nq),
        in_specs=_qkv_specs(s) + [pl.BlockSpec((None, 3, TQ, TQ), lambda h, i: (h, 0, 0, 0))],
        out_specs=[pl.BlockSpec((TQ, HEAD_DIM), lambda h, i: (i, h)),
                   pl.BlockSpec((None, TQ, 1), lambda h, i: (h, i, 0))],
        out_shape=[jax.ShapeDtypeStruct((s, HEADS * HEAD_DIM), F32), jax.ShapeDtypeStruct((HEADS, s, 1), F32)],
        name="ca_attn_fwd", compiler_params=_params("parallel", "parallel"))(qkv, qkv, qkv, tiles)


def _ca_attn_bwd(qkv, tiles, do, o, lse):
    s = qkv.shape[0]
    nq = s // TQ
    scale = HEAD_DIM ** -0.5

    def body(q_ref, k_ref, v_ref, bt_ref, do_ref, o_ref, lse_ref, dq_ref, dk_ref, dv_ref, dbt_ref, dk_acc, dv_acc):
        i = pl.program_id(1)
        row, col = _iota2()

        @pl.when(i == 0)
        def _():
            dk_acc[...] = jnp.zeros_like(dk_acc)
            dv_acc[...] = jnp.zeros_like(dv_acc)
            dbt_ref[...] = jnp.zeros_like(dbt_ref)

        q = q_ref[...].astype(MXU_DTYPE)
        do_ = do_ref[...]
        delta = jnp.sum(do_ * o_ref[...], -1, keepdims=True)
        lse_ = lse_ref[...]
        dob = do_.astype(MXU_DTYPE)

        def step(kb, dq):
            ks = pl.multiple_of(kb * TQ, TQ)
            k = k_ref[pl.ds(ks, TQ), :].astype(MXU_DTYPE)
            v = v_ref[pl.ds(ks, TQ), :].astype(MXU_DTYPE)
            t = jnp.minimum(i - kb, 2)
            sc = _dot(q, k, NT) * scale + bt_ref[t]
            sc = jnp.where(_ca_mask(i, ks, row, col), sc, NEG)
            p = jnp.exp(sc - lse_)
            dsc = p * (_dot(dob, v, NT) - delta)
            dbt_ref[t] += dsc
            ds = (dsc * scale).astype(MXU_DTYPE)
            dk_acc[pl.ds(ks, TQ), :] += _dot(ds, q, TN)
            dv_acc[pl.ds(ks, TQ), :] += _dot(p.astype(MXU_DTYPE), dob, TN)
            return dq + _dot(ds, k)

        dq = lax.fori_loop(jnp.maximum(i - CA_LEFT_BLOCKS, 0), i + 1, step, jnp.zeros((TQ, HEAD_DIM), F32))
        dq_ref[...] = dq.astype(dq_ref.dtype)

        @pl.when(i == nq - 1)
        def _():
            dk_ref[...] = dk_acc[...].astype(dk_ref.dtype)
            dv_ref[...] = dv_acc[...].astype(dv_ref.dtype)

    blk = pl.BlockSpec((TQ, HEAD_DIM), lambda h, i: (i, h))
    col_h = pl.BlockSpec((s, HEAD_DIM), lambda h, i: (0, h))
    tile = pl.BlockSpec((None, 3, TQ, TQ), lambda h, i: (h, 0, 0, 0))
    shp = jax.ShapeDtypeStruct((s, HEADS * HEAD_DIM), MXU_DTYPE)
    return pl.pallas_call(
        body, grid=(HEADS, nq),
        in_specs=_qkv_specs(s) + [tile, blk, blk, pl.BlockSpec((None, TQ, 1), lambda h, i: (h, i, 0))],
        out_specs=[blk, col_h, col_h, tile],
        out_shape=[shp, shp, shp, jax.ShapeDtypeStruct((HEADS, 3, TQ, TQ), F32)],
        scratch_shapes=[pltpu.VMEM((s, HEAD_DIM), F32), pltpu.VMEM((s, HEAD_DIM), F32)],
        name="ca_attn_bwd", compiler_params=_params("arbitrary", "arbitrary"))(qkv, qkv, qkv, tiles, do, o, lse)


def _ca_table_grad(dtiles):
    def skew(x, row):
        for b in range(7):
            x = jnp.where((jnp.right_shift(row, b) & 1) == 1, pltpu.roll(x, LANES - (1 << b), 1), x)
        return x

    def body(t_ref, o_ref):
        row = lax.broadcasted_iota(jnp.int32, (LANES, LANES), 0)
        col = lax.broadcasted_iota(jnp.int32, (LANES, LANES), 1)
        wrapped = (row + col) >= LANES
        lane = col[:1]
        total = lambda x: jnp.sum(jnp.sum(x, 0, keepdims=True), -1, keepdims=True)
        lo = jnp.zeros((1, LANES), F32)
        hi = jnp.zeros((1, LANES), F32)
        for d, a, b, o in _ca_subtiles():
            x = t_ref[d, a * LANES:(a + 1) * LANES, b * LANES:(b + 1) * LANES]
            if o >= LANES:
                hi = hi + jnp.where(lane == LAST_REL, total(x), 0.0)
            elif o <= -2 * LANES:
                lo = lo + jnp.where(lane == 0, total(x), 0.0)
            else:
                y = skew(x, row)
                pos = jnp.sum(jnp.where(wrapped, 0.0, y), 0, keepdims=True)
                neg = jnp.sum(jnp.where(wrapped, y, 0.0), 0, keepdims=True)
                if o == 0:
                    clipped = jnp.sum(jnp.where(lane > LAST_REL, pos, 0.0), -1, keepdims=True)
                    hi = hi + jnp.where(lane <= LAST_REL, pos, 0.0) + jnp.where(lane == LAST_REL, clipped, 0.0)
                    lo = lo + neg
                else:
                    lo = lo + pos + jnp.where(lane == 0, jnp.sum(neg, -1, keepdims=True), 0.0)
        o_ref[:, :LANES] = lo
        o_ref[:, LANES:] = hi

    return pl.pallas_call(
        body, grid=(HEADS,), in_specs=[pl.BlockSpec((None, 3, TQ, TQ), lambda h: (h, 0, 0, 0))],
        out_specs=pl.BlockSpec((None, 1, 2 * LANES), lambda h: (h, 0, 0)),
        out_shape=jax.ShapeDtypeStruct((HEADS, 1, 2 * LANES), F32),
        name="ca_table_grad", compiler_params=_params("parallel"))(dtiles)


def _row_tile(rows, cols):
    if rows % 128:
        return rows
    tr = 128
    while rows % (2 * tr) == 0 and 2 * tr * cols * 4 <= (1 << 20):
        tr *= 2
    return tr


def _adamw(name, w, g, m, v):
    rows, cols = w.shape
    tr = _row_tile(rows, cols)
    blk = pl.BlockSpec((tr, cols), lambda i: (i, 0))

    def body(w_ref, g_ref, m_ref, v_ref, d_ref, m2_ref, v2_ref):
        g_ = g_ref[...]
        m2 = ADAM_B1 * m_ref[...] + (1.0 - ADAM_B1) * g_
        v2 = ADAM_B2 * v_ref[...] + (1.0 - ADAM_B2) * jnp.square(g_)
        m_hat = m2 / (1.0 - ADAM_B1 ** ADAM_STEP)
        v_hat = v2 / (1.0 - ADAM_B2 ** ADAM_STEP)
        d_ref[...] = -ADAM_LR * (m_hat / (jnp.sqrt(v_hat) + ADAM_EPS) + ADAM_WD * w_ref[...])
        m2_ref[...] = m2
        v2_ref[...] = v2

    shp = jax.ShapeDtypeStruct((rows, cols), F32)
    return pl.pallas_call(body, grid=(rows // tr,), in_specs=[blk] * 4, out_specs=[blk] * 3, out_shape=[shp] * 3,
                          name=name, compiler_params=_params("parallel"))(w, g, m, v)


def _pair_sum(name, g, r1, c):
    _, rh, cols = r1.shape
    tr = _row_tile(rh, cols)
    nb = rh // tr

    def body(c_ref, g_ref, r_ref, o_ref):
        o_ref[...] = (g_ref[...].astype(F32) + r_ref[...].astype(F32)).astype(o_ref.dtype)

    return pl.pallas_call(
        body, grid_spec=pltpu.PrefetchScalarGridSpec(
            num_scalar_prefetch=1, grid=(N_CHIPS, nb),
            in_specs=[pl.BlockSpec((None, tr, cols), lambda k, i, c_ref: (k, c_ref[0] * nb + i, 0)),
                      pl.BlockSpec((None, tr, cols), lambda k, i, c_ref: (k, i, 0))],
            out_specs=pl.BlockSpec((None, tr, cols), lambda k, i, c_ref: (k, i, 0))),
        out_shape=jax.ShapeDtypeStruct(r1.shape, WIRE_DTYPE), name=name,
        compiler_params=_params("parallel", "parallel"))(c, g, r1)


def _chip_sum(name, a1, r2, place):
    _, rh, cols = a1.shape
    tr = _row_tile(rh, cols)

    def body(p_ref, a_ref, r_ref, o_ref):
        o_ref[...] = ((a_ref[...].astype(F32) + r_ref[0].astype(F32)) + r_ref[1].astype(F32)) + r_ref[2].astype(F32)

    return pl.pallas_call(
        body, grid_spec=pltpu.PrefetchScalarGridSpec(
            num_scalar_prefetch=1, grid=(rh // tr,),
            in_specs=[pl.BlockSpec((None, tr, cols), lambda i, p_ref: (p_ref[0], i, 0)),
                      pl.BlockSpec((N_CHIPS - 1, tr, cols), lambda i, p_ref: (0, i, 0))],
            out_specs=pl.BlockSpec((None, tr, cols), lambda i, p_ref: (p_ref[1], i, 0))),
        out_shape=jax.ShapeDtypeStruct((2, rh, cols), F32), name=name,
        compiler_params=_params("parallel"))(place, a1, r2)


def _cast_place(name, w, chip):
    rows, cols = w.shape
    tr = _row_tile(rows, cols)

    def body(c_ref, w_ref, o_ref):
        o_ref[...] = w_ref[...].astype(o_ref.dtype)

    return pl.pallas_call(
        body, grid_spec=pltpu.PrefetchScalarGridSpec(
            num_scalar_prefetch=1, grid=(rows // tr,),
            in_specs=[pl.BlockSpec((tr, cols), lambda i, c_ref: (i, 0))],
            out_specs=pl.BlockSpec((None, tr, cols), lambda i, c_ref: (c_ref[0], i, 0))),
        out_shape=jax.ShapeDtypeStruct((N_CHIPS, rows, cols), MXU_DTYPE), name=name,
        compiler_params=_params("parallel"))(chip, w)


def _place():
    x, y, c = lax.axis_index("x"), lax.axis_index("y"), lax.axis_index("c")
    chips = [(1 - x, y), (x, 1 - y), (1 - x, 1 - y)]
    return x, y, c, chips


def _remote(src, dst, send_sem, recv_sem, to):
    return pltpu.make_async_remote_copy(src_ref=src, dst_ref=dst, send_sem=send_sem, recv_sem=recv_sem,
                                        device_id=to, device_id_type=MESH)


def _all_gather(placed):
    n = len(placed)

    def body(*refs):
        outs = refs[n:2 * n]
        send, recv = refs[2 * n:]
        x, y, c, chips = _place()
        me, sib = 2 * x + y, (x, y, 1 - c)

        def half(t, which):
            return pl.ds(which * (outs[t].shape[1] // 2), outs[t].shape[1] // 2)

        first, passed = [], []
        for t in range(n):
            for j, chip in enumerate(chips):
                mine = outs[t].at[me, half(t, c)]
                cp = _remote(mine, mine, send.at[6 * t + j], recv.at[6 * t + j], (*chip, c))
                cp.start()
                first.append(cp)
        for t in range(n):
            for j, (px, py) in enumerate(chips):
                got = outs[t].at[2 * px + py, half(t, c)]
                _remote(got, got, send.at[6 * t + j], recv.at[6 * t + j], (px, py, c)).wait_recv()
                cp = _remote(got, got, send.at[6 * t + 3 + j], recv.at[6 * t + 3 + j], sib)
                cp.start()
                passed.append(cp)
        for t in range(n):
            for j, (px, py) in enumerate(chips):
                got = outs[t].at[2 * px + py, half(t, 1 - c)]
                _remote(got, got, send.at[6 * t + 3 + j], recv.at[6 * t + 3 + j], sib).wait_recv()
        for cp in first + passed:
            cp.wait_send()

    return pl.pallas_call(
        body, in_specs=[ANY] * n, out_specs=[ANY] * n,
        out_shape=[jax.ShapeDtypeStruct(p.shape, p.dtype) for p in placed],
        scratch_shapes=[pltpu.SemaphoreType.DMA((6 * n,)), pltpu.SemaphoreType.DMA((6 * n,))],
        input_output_aliases={t: t for t in range(n)},
        name="weight_all_gather")(*placed)


def _sibling_exchange(grads):
    n = len(grads)

    def body(*refs):
        ins, outs = refs[:n], refs[n:2 * n]
        send, recv = refs[2 * n:]
        x, y, c, _ = _place()
        cps = []
        for t in range(n):
            rh = ins[t].shape[1] // 2
            cp = _remote(ins[t].at[:, pl.ds((1 - c) * rh, rh)], outs[t], send.at[t], recv.at[t], (x, y, 1 - c))
            cp.start()
            cps.append(cp)
        for cp in cps:
            cp.wait()

    return pl.pallas_call(
        body, in_specs=[ANY] * n, out_specs=[ANY] * n,
        out_shape=[jax.ShapeDtypeStruct((N_CHIPS, g.shape[1] // 2, g.shape[2]), g.dtype) for g in grads],
        scratch_shapes=[pltpu.SemaphoreType.DMA((n,)), pltpu.SemaphoreType.DMA((n,))],
        name="grad_sibling_exchange")(*grads)


def _chip_exchange(parts):
    n = len(parts)

    def body(*refs):
        ins, outs = refs[:n], refs[n:2 * n]
        send, recv = refs[2 * n:]
        x, y, c, chips = _place()
        cps = []
        for t in range(n):
            for j, (px, py) in enumerate(chips):
                cp = _remote(ins[t].at[2 * px + py], outs[t].at[j], send.at[3 * t + j], recv.at[3 * t + j],
                             (px, py, c))
                cp.start()
                cps.append(cp)
        for cp in cps:
            cp.wait()

    return pl.pallas_call(
        body, in_specs=[ANY] * n, out_specs=[ANY] * n,
        out_shape=[jax.ShapeDtypeStruct((N_CHIPS - 1, *p.shape[1:]), p.dtype) for p in parts],
        scratch_shapes=[pltpu.SemaphoreType.DMA((3 * n,)), pltpu.SemaphoreType.DMA((3 * n,))],
        name="grad_chip_exchange")(*parts)


def _sibling_share(halves):
    n = len(halves)

    def body(*refs):
        outs = refs[n:2 * n]
        send, recv = refs[2 * n:]
        x, y, c, _ = _place()
        cps = []
        for t in range(n):
            cp = _remote(outs[t].at[c], outs[t].at[c], send.at[t], recv.at[t], (x, y, 1 - c))
            cp.start()
            cps.append(cp)
        for cp in cps:
            cp.wait()

    return pl.pallas_call(
        body, in_specs=[ANY] * n, out_specs=[ANY] * n,
        out_shape=[jax.ShapeDtypeStruct(h.shape, h.dtype) for h in halves],
        scratch_shapes=[pltpu.SemaphoreType.DMA((n,)), pltpu.SemaphoreType.DMA((n,))],
        input_output_aliases={t: t for t in range(n)},
        name="grad_sibling_share")(*halves)


def _all_reduce_small(name, pack):
    rows, cols = pack.shape

    def body(p_ref, o_ref, slots, send, recv):
        x, y, c, _ = _place()
        me = 4 * x + 2 * y + c
        slots[me] = p_ref[...]
        cps = []
        for r in range(1, N_DEV):
            to = ((1 - x) if r & 4 else x, (1 - y) if r & 2 else y, (1 - c) if r & 1 else c)
            cp = _remote(p_ref, slots.at[me], send.at[r - 1], recv.at[r - 1], to)
            cp.start()
            cps.append(cp)
        for cp in cps:
            cp.wait()
        acc = slots[0]
        for d in range(1, N_DEV):
            acc = acc + slots[d]
        o_ref[...] = acc

    return pl.pallas_call(
        body, in_specs=[VMEM_SPEC], out_specs=VMEM_SPEC, out_shape=jax.ShapeDtypeStruct((rows, cols), F32),
        scratch_shapes=[pltpu.VMEM((N_DEV, rows, cols), F32), pltpu.SemaphoreType.DMA((N_DEV - 1,)),
                        pltpu.SemaphoreType.DMA((N_DEV - 1,))],
        name=name)(pack)


BIG = ("ffn_w_in", "ffn_w_out", "mla_w_down", "mla_w_uq", "mla_w_ukv", "mla_w_o", "sb_w_qkv", "sb_w_o", "ca_w_qkv",
       "ca_w_o")
COL_SHARDED = {"ffn_w_in": True, "ffn_w_out": False, "mla_w_down": False, "mla_w_uq": True, "mla_w_ukv": True,
               "mla_w_o": False, "sb_w_qkv": True, "sb_w_o": False, "ca_w_qkv": True, "ca_w_o": False}
WEIGHTS = ("ln_mix_g", "ln_mix_b", "ln_ffn_g", "ln_ffn_b", "ffn_w_in", "ffn_w_out", "mla_w_down", "mla_q_norm_g",
           "mla_w_uq", "mla_kv_norm_g", "mla_w_ukv", "mla_w_o", "sb_w_qkv", "sb_w_o", "ca_w_qkv", "ca_rel_bias",
           "ca_w_o")
HEADS_PER_CHIP = HEADS // N_CHIPS


def _mxu_shards(w):
    down = w["mla_w_down"]
    uq = w["mla_w_uq"]
    n, ql = uq.shape[:2]
    lane_pad = 128 - MLA_ROPE
    shaped = dict(w)
    shaped["mla_w_down"] = jnp.pad(down, ((0, 0), (0, 0), (0, lane_pad)))
    shaped["mla_w_uq"] = jnp.pad(uq.reshape(n, ql, HEADS_PER_CHIP, MLA_QK_DIM),
                                 ((0, 0), (0, 0), (0, 0), (0, HEAD_PAD - MLA_QK_DIM))).reshape(n, ql, -1)
    return {k: shaped[k].reshape(-1, shaped[k].shape[-1]) for k in BIG}


def _unpad_grad(name, g, like):
    if name == "mla_w_down":
        g = g[:, :like.shape[-1]]
    elif name == "mla_w_uq":
        g = g.reshape(g.shape[0], HEADS_PER_CHIP, HEAD_PAD)[:, :, :MLA_QK_DIM]
    return g.reshape(like.shape)


class _Grads:
    def __init__(self, dims):
        self.dims, self.buf = dims, {}

    def add(self, name, x, dy, l):
        layers, rows, cols = self.dims[name]
        self.buf[name] = _dw(f"dw_{name}_{l}", x, dy, self.buf.get(name), l, layers, rows, cols, COL_SHARDED[name])


def _local_step(x, target, wg, dims, p, tabs):
    d = x.shape[1]
    grads = _Grads(dims)
    rows = {k: v[1] for k, v in dims.items()}
    cols = {k: v[2] for k, v in dims.items()}
    saved = []
    h = x
    for i in range(DEPTH):
        kind, slot = i % 3, i // 3
        sv = {"h0": h}
        if kind == 0:
            down = _fwd_row(f"mla_down_{i}", h, wg["mla_w_down"], slot, rows["mla_w_down"], cols["mla_w_down"], F32)
            gq, gkv = p["mla_q_norm_g"][slot][None], p["mla_kv_norm_g"][slot][None]
            cq, ckv, kr = _mla_mid_fwd(down, gq, gkv, tabs)
            q = _fwd_col(f"mla_uq_{i}", cq, wg["mla_w_uq"], slot, rows["mla_w_uq"], cols["mla_w_uq"], F32)
            kv = _fwd_col(f"mla_ukv_{i}", ckv, wg["mla_w_ukv"], slot, rows["mla_w_ukv"], cols["mla_w_ukv"],
                          MXU_DTYPE)
            o, lse = _mla_attn_fwd(q, kv, kr, tabs)
            m = _fwd_row(f"mla_o_{i}", o, wg["mla_w_o"], slot, rows["mla_w_o"], cols["mla_w_o"], F32)
            sv.update(down=down, gq=gq, gkv=gkv, cq=cq, ckv=ckv, kr=kr, q=q, kv=kv, o=o, lse=lse)
        else:
            pre = "sb" if kind == 1 else "ca"
            qkv = _fwd_col(f"{pre}_qkv_{i}", h, wg[f"{pre}_w_qkv"], slot, rows[f"{pre}_w_qkv"], cols[f"{pre}_w_qkv"],
                           MXU_DTYPE)
            if kind == 1:
                o = _sb_attn_fwd(qkv)
            else:
                tiles = _ca_bias_tiles(p["ca_rel_bias"][slot])
                o, lse = _ca_attn_fwd(qkv, tiles)
                sv.update(tiles=tiles, lse=lse)
            m = _fwd_row(f"{pre}_o_{i}", o, wg[f"{pre}_w_o"], slot, rows[f"{pre}_w_o"], cols[f"{pre}_w_o"], F32)
            sv.update(qkv=qkv, o=o)
        h1, sv["xh1"], sv["r1"] = _ln_fwd(f"ln_mix_{i}", h, m, p["ln_mix_g"][i][None], p["ln_mix_b"][i][None])
        u, a = _fwd_col(f"ffn_in_{i}", h1, wg["ffn_w_in"], i, rows["ffn_w_in"], cols["ffn_w_in"], None,
                        epilogue=_epi_relu2, dtypes=(F32, MXU_DTYPE))
        y = _fwd_row(f"ffn_out_{i}", a, wg["ffn_w_out"], i, rows["ffn_w_out"], cols["ffn_w_out"], F32)
        h2, sv["xh2"], sv["r2"] = _ln_fwd(f"ln_ffn_{i}", h1, y, p["ln_ffn_g"][i][None], p["ln_ffn_b"][i][None])
        sv.update(h1=h1, u=u, a=a)
        saved.append(sv)
        h = h2

    loss, dy = _loss_head(h, target)
    small = {k: [None] * DEPTH for k in ("ln_mix_g", "ln_mix_b", "ln_ffn_g", "ln_ffn_b")}
    n_mla = p["mla_q_norm_g"].shape[0]
    small["mla_q_norm_g"], small["mla_kv_norm_g"] = [None] * n_mla, [None] * n_mla
    for i in reversed(range(DEPTH)):
        kind, slot = i % 3, i // 3
        sv = saved[i]
        dz, small["ln_ffn_g"][i], small["ln_ffn_b"][i] = _ln_bwd(f"ln_ffn_bwd_{i}", dy, sv["xh2"], sv["r2"],
                                                                 p["ln_ffn_g"][i][None])
        grads.add("ffn_w_out", sv["a"], dz, i)
        du = _dx_row(f"ffn_du_{i}", dz, wg["ffn_w_out"], i, rows["ffn_w_out"], cols["ffn_w_out"], MXU_DTYPE,
                     epilogue=_epi_drelu2, extra=sv["u"])
        grads.add("ffn_w_in", sv["h1"], du, i)
        dy = _dx_col(f"ffn_dh_{i}", du, wg["ffn_w_in"], i, rows["ffn_w_in"], cols["ffn_w_in"], F32,
                     epilogue=_epi_residual, extra=dz)
        dz, small["ln_mix_g"][i], small["ln_mix_b"][i] = _ln_bwd(f"ln_mix_bwd_{i}", dy, sv["xh1"], sv["r1"],
                                                                 p["ln_mix_g"][i][None])
        if kind == 0:
            grads.add("mla_w_o", sv["o"], dz, slot)
            do = _dx_row(f"mla_do_{i}", dz, wg["mla_w_o"], slot, rows["mla_w_o"], cols["mla_w_o"], F32)
            dq, dkv, dkr = _mla_attn_bwd(sv["q"], sv["kv"], sv["kr"], tabs, do, sv["o"], sv["lse"])
            grads.add("mla_w_uq", sv["cq"], dq, slot)
            dcq = _dx_col(f"mla_dcq_{i}", dq, wg["mla_w_uq"], slot, rows["mla_w_uq"], cols["mla_w_uq"], F32)
            grads.add("mla_w_ukv", sv["ckv"], dkv, slot)
            dckv = _dx_col(f"mla_dckv_{i}", dkv, wg["mla_w_ukv"], slot, rows["mla_w_ukv"], cols["mla_w_ukv"], F32)
            ddown, small["mla_q_norm_g"][slot], small["mla_kv_norm_g"][slot] = _mla_mid_bwd(
                sv["down"], dcq, dckv, dkr, sv["gq"], sv["gkv"], tabs)
            grads.add("mla_w_down", sv["h0"], ddown, slot)
            dy = _dx_row(f"mla_dh_{i}", ddown, wg["mla_w_down"], slot, rows["mla_w_down"], cols["mla_w_down"], F32,
                         epilogue=_epi_residual, extra=dz)
        else:
            pre = "sb" if kind == 1 else "ca"
            grads.add(f"{pre}_w_o", sv["o"], dz, slot)
            do = _dx_row(f"{pre}_do_{i}", dz, wg[f"{pre}_w_o"], slot, rows[f"{pre}_w_o"], cols[f"{pre}_w_o"], F32)
            if kind == 1:
                dq, dk, dv = _sb_attn_bwd(sv["qkv"], do)
            else:
                dq, dk, dv, dtiles = _ca_attn_bwd(sv["qkv"], sv["tiles"], do, sv["o"], sv["lse"])
                small["ca_rel_bias"] = [jnp.transpose(_ca_table_grad(dtiles)[:, 0, :REL_TABLE])]
            dqkv = jnp.concatenate([dq, dk, dv], 1)
            grads.add(f"{pre}_w_qkv", sv["h0"], dqkv, slot)
            dy = _dx_col(f"{pre}_dh_{i}", dqkv, wg[f"{pre}_w_qkv"], slot, rows[f"{pre}_w_qkv"], cols[f"{pre}_w_qkv"],
                         F32, epilogue=_epi_residual, extra=dz)
    small = {k: jnp.stack([g.reshape(g.shape[-2:]) if k == "ca_rel_bias" else g[0] for g in v]) for k, v in small.items()}
    return loss, dy, grads.buf, small


SMALL = ("ln_mix_g", "ln_mix_b", "ln_ffn_g", "ln_ffn_b", "mla_q_norm_g", "mla_kv_norm_g", "ca_rel_bias")


def _pack_small(parts, width):
    flat = jnp.concatenate([parts[k].reshape(-1) for k in SMALL])
    rows = -(-flat.shape[0] // width)
    rows += -rows % 8
    return jnp.pad(flat, (0, rows * width - flat.shape[0])).reshape(rows, width)


def _unpack_small(pack, like):
    flat, out, at = pack.reshape(-1), {}, 0
    for k in SMALL:
        n = int(np.prod(like[k].shape))
        out[k] = flat[at:at + n].reshape(like[k].shape)
        at += n
    return out


def kernel(x, ln_mix_g, ln_mix_b, ln_ffn_g, ln_ffn_b, ffn_w_in, ffn_w_out, mla_w_down, mla_q_norm_g, mla_w_uq, mla_kv_norm_g, mla_w_ukv, mla_w_o, sb_w_qkv, sb_w_o, ca_w_qkv, ca_rel_bias, ca_w_o, loss_target, m_ln_mix_g, m_ln_mix_b, m_ln_ffn_g, m_ln_ffn_b, m_ffn_w_in, m_ffn_w_out, m_mla_w_down, m_mla_q_norm_g, m_mla_w_uq, m_mla_kv_norm_g, m_mla_w_ukv, m_mla_w_o, m_sb_w_qkv, m_sb_w_o, m_ca_w_qkv, m_ca_rel_bias, m_ca_w_o, v_ln_mix_g, v_ln_mix_b, v_ln_ffn_g, v_ln_ffn_b, v_ffn_w_in, v_ffn_w_out, v_mla_w_down, v_mla_q_norm_g, v_mla_w_uq, v_mla_kv_norm_g, v_mla_w_ukv, v_mla_w_o, v_sb_w_qkv, v_sb_w_o, v_ca_w_qkv, v_ca_rel_bias, v_ca_w_o):
    w = dict(zip(WEIGHTS, (ln_mix_g, ln_mix_b, ln_ffn_g, ln_ffn_b, ffn_w_in, ffn_w_out, mla_w_down, mla_q_norm_g,
                           mla_w_uq, mla_kv_norm_g, mla_w_ukv, mla_w_o, sb_w_qkv, sb_w_o, ca_w_qkv, ca_rel_bias,
                           ca_w_o)))
    mom1 = dict(zip(WEIGHTS, (m_ln_mix_g, m_ln_mix_b, m_ln_ffn_g, m_ln_ffn_b, m_ffn_w_in, m_ffn_w_out, m_mla_w_down,
                              m_mla_q_norm_g, m_mla_w_uq, m_mla_kv_norm_g, m_mla_w_ukv, m_mla_w_o, m_sb_w_qkv,
                              m_sb_w_o, m_ca_w_qkv, m_ca_rel_bias, m_ca_w_o)))
    mom2 = dict(zip(WEIGHTS, (v_ln_mix_g, v_ln_mix_b, v_ln_ffn_g, v_ln_ffn_b, v_ffn_w_in, v_ffn_w_out, v_mla_w_down,
                              v_mla_q_norm_g, v_mla_w_uq, v_mla_kv_norm_g, v_mla_w_ukv, v_mla_w_o, v_sb_w_qkv,
                              v_sb_w_o, v_ca_w_qkv, v_ca_rel_bias, v_ca_w_o)))
    xi, yi, ci = lax.axis_index("x"), lax.axis_index("y"), lax.axis_index("c")
    chip = 2 * xi + yi
    d_model = x.shape[-1]

    shards = _mxu_shards(w)
    layers = {k: w[k].shape[0] for k in BIG}
    dims = {k: (layers[k], shards[k].shape[0] // layers[k], shards[k].shape[1]) for k in BIG}
    c1, chip1 = jnp.reshape(ci, (1,)).astype(jnp.int32), jnp.reshape(chip, (1,)).astype(jnp.int32)
    wg = dict(zip(BIG, _all_gather([_cast_place(f"cast_{k}", shards[k], chip1) for k in BIG])))
    gains =jnp.stack([w["mla_q_norm_g"], w["mla_kv_norm_g"]])
    gains = jnp.where(ci == 0, gains, 0.0)
    placed = lax.dynamic_update_slice_in_dim(jnp.zeros((*gains.shape[:2], N_CHIPS, gains.shape[2]), F32),
                                             gains[:, :, None], chip, 2)
    full_gains = _all_reduce_small("norm_gain_gather", placed.reshape(2 * gains.shape[1], -1))
    full_gains = full_gains.reshape(2, gains.shape[1], -1)
    p = {"ln_mix_g": ln_mix_g, "ln_mix_b": ln_mix_b, "ln_ffn_g": ln_ffn_g, "ln_ffn_b": ln_ffn_b,
         "mla_q_norm_g": full_gains[0], "mla_kv_norm_g": full_gains[1], "ca_rel_bias": ca_rel_bias}

    loss, grad_x, gbuf, small = _local_step(x[0], loss_target[0], wg, dims, p, _rope_tables(x.shape[1]))
    loss = lax.psum(loss[0, 0], ("x", "y", "c"))

    small = _unpack_small(_all_reduce_small("small_grad_all_reduce", _pack_small(small, d_model)), small)
    grad = {k: small[k] for k in ("ln_mix_g", "ln_mix_b", "ln_ffn_g", "ln_ffn_b", "ca_rel_bias")}
    for k in ("mla_q_norm_g", "mla_kv_norm_g"):
        g = small[k].reshape(small[k].shape[0], N_CHIPS, -1)
        grad[k] = lax.dynamic_index_in_dim(g, chip, 1, keepdims=False)

    place = jnp.stack([chip, ci]).astype(jnp.int32)
    got = _sibling_exchange([gbuf[k] for k in BIG])
    pair = [_pair_sum(f"pair_sum_{k}", gbuf[k], r, c1) for k, r in zip(BIG, got)]
    arrived = _chip_exchange(pair)
    halves = [_chip_sum(f"chip_sum_{k}", a, r, place) for k, a, r in zip(BIG, pair, arrived)]
    for k, g in zip(BIG, _sibling_share(halves)):
        grad[k] = _unpad_grad(k, g.reshape(-1, g.shape[-1]), w[k])

    delta, new_m, new_v = {}, {}, {}
    for k in WEIGHTS:
        flat = lambda a: a.reshape(-1, a.shape[-1])
        dl, m2, v2 = _adamw(f"adamw_{k}", flat(w[k]), flat(grad[k]), flat(mom1[k]), flat(mom2[k]))
        delta[k], new_m[k], new_v[k] = dl.reshape(w[k].shape), m2.reshape(w[k].shape), v2.reshape(w[k].shape)
    return (loss, grad_x[None], *[grad[k] for k in WEIGHTS], *[delta[k] for k in WEIGHTS],
            *[new_m[k] for k in WEIGHTS], *[new_v[k] for k in WEIGHTS])
```

```python
import functools

import numpy as np
import jax
import jax.numpy as jnp
from jax import lax
from jax.experimental import pallas as pl
from jax.experimental.pallas import tpu as pltpu

F32, BF16 = jnp.float32, jnp.bfloat16
MXU_DTYPE = BF16
WIRE_DTYPE = BF16

DEPTH = 4
HEADS = 16
HEAD_DIM = 128
CHUNK_SHIFT = 6
TQ = 256
MLA_ROPE = 64
MLA_QK_DIM = 192
HEAD_PAD = 256
CA_LEFT_CHUNKS = 8
CA_LEFT_BLOCKS = (CA_LEFT_CHUNKS << CHUNK_SHIFT) // TQ
REL_CLIP_LEFT = 128
REL_TABLE = 192
ROPE_THETA = 10000.0
LN_EPS = 1e-5
RMS_EPS = 1e-6
ALPHA = (2.0 * DEPTH) ** 0.25
NEG = -1e30
ADAM_LR, ADAM_B1, ADAM_B2, ADAM_EPS, ADAM_WD, ADAM_STEP = 0.001, 0.9, 0.999, 1e-08, 0.01, 10
N_CHIPS = 4
N_DEV = 8
VMEM_LIMIT = 48 << 20
MESH = pl.DeviceIdType.MESH
ANY = pl.BlockSpec(memory_space=pl.ANY)
VMEM_SPEC = pl.BlockSpec(memory_space=pltpu.VMEM)

NN = (((1,), (0,)), ((), ()))
NT = (((1,), (1,)), ((), ()))
TN = (((0,), (0,)), ((), ()))


def _dot(a, b, dims=NN):
    return lax.dot_general(a, b, dims, preferred_element_type=F32)


def _exact_dot(x, u):
    hi = x.astype(BF16)
    r1 = x - hi.astype(F32)
    mid = r1.astype(BF16)
    lo = (r1 - mid.astype(F32)).astype(BF16)
    return _dot(hi, u) + _dot(mid, u) + _dot(lo, u)


def _params(*sem):
    return pltpu.CompilerParams(dimension_semantics=sem, vmem_limit_bytes=VMEM_LIMIT)


def _tile(n, pref):
    for t in (1024, 768, 512, 384, 256, 128):
        if t <= pref and n % t == 0:
            return t
    return n


class _Comm:
    def __init__(self, srcs, dsts, alias, copies, n):
        self.srcs, self.dsts, self.alias, self.copies, self.n = list(srcs), list(dsts), dict(alias), copies, n
        self.out = None


def _call(body, args, *, name, grid, in_specs, out_specs, out_shape, scratch_shapes=(), semantics, comm=None):
    in_specs, out_specs, out_shape = list(in_specs), list(out_specs), list(out_shape)
    scratch_shapes = list(scratch_shapes)
    if comm is None:
        return pl.pallas_call(body, grid=grid, in_specs=in_specs, out_specs=out_specs, out_shape=out_shape,
                              scratch_shapes=scratch_shapes, name=name, compiler_params=_params(*semantics))(*args)
    n_in, n_out, n_scr, ns, nd = len(in_specs), len(out_specs), len(scratch_shapes), len(comm.srcs), len(comm.dsts)

    def carrier(*refs):
        ins, refs = refs[:n_in], refs[n_in:]
        srcs, refs = refs[:ns], refs[ns:]
        outs, refs = refs[:n_out], refs[n_out:]
        dsts, refs = refs[:nd], refs[nd:]
        scratch, (send, recv) = refs[:n_scr], refs[n_scr:]
        ids = [pl.program_id(a) for a in range(len(grid))]
        first = functools.reduce(jnp.logical_and, [i == 0 for i in ids])
        last = functools.reduce(jnp.logical_and, [i == g - 1 for i, g in zip(ids, grid)])

        @pl.when(first)
        def _():
            for cp in comm.copies(srcs, dsts, send, recv):
                cp.start()

        body(*ins, *outs, *scratch)

        @pl.when(last)
        def _():
            for cp in comm.copies(srcs, dsts, send, recv):
                cp.wait()

    res = pl.pallas_call(
        carrier, grid=grid, in_specs=in_specs + [ANY] * ns, out_specs=out_specs + [ANY] * nd,
        out_shape=out_shape + comm.dsts,
        scratch_shapes=scratch_shapes + [pltpu.SemaphoreType.DMA((comm.n,)), pltpu.SemaphoreType.DMA((comm.n,))],
        input_output_aliases={n_in + s: n_out + d for s, d in comm.alias.items()}, name=name,
        compiler_params=_params(*["arbitrary"] * len(grid)))(*args, *comm.srcs)
    comm.out = res[n_out:]
    return res[:n_out]


def _mm(name, a, b, extras, *, grid, a_spec, b_spec, extra_specs, out_specs, out_shape, dims,
        epilogue, acc_shape, comm=None):
    nk = grid[2]
    n_ex = len(extras)

    def body(*refs):
        a_ref, b_ref = refs[:2]
        ex = refs[2:2 + n_ex]
        outs = refs[2 + n_ex:-1]
        acc = refs[-1]
        k = pl.program_id(2)

        @pl.when(k == 0)
        def _():
            acc[...] = jnp.zeros_like(acc)

        acc[...] += lax.dot_general(a_ref[...].astype(MXU_DTYPE), b_ref[...].astype(MXU_DTYPE), dims,
                                    preferred_element_type=F32)

        @pl.when(k == nk - 1)
        def _():
            epilogue(acc[...], ex, outs)

    return _call(body, (a, b, *extras), name=name, grid=grid, in_specs=[a_spec, b_spec, *extra_specs],
                 out_specs=out_specs, out_shape=out_shape, scratch_shapes=[pltpu.VMEM(acc_shape, F32)],
                 semantics=("parallel", "parallel", "arbitrary"), comm=comm)


def _epi_store(acc, ex, outs):
    outs[0][...] = acc.astype(outs[0].dtype)


def _epi_relu2(acc, ex, outs):
    outs[0][...] = acc
    r = jnp.maximum(acc, 0.0)
    outs[1][...] = (r * r).astype(outs[1].dtype)


def _epi_drelu2(acc, ex, outs):
    outs[0][...] = (acc * (2.0 * jnp.maximum(ex[0][...], 0.0))).astype(outs[0].dtype)


def _epi_residual(acc, ex, outs):
    outs[0][...] = acc + ALPHA * ex[0][...]


def _fwd_col(name, x, wg, dtype, epilogue=_epi_store, dtypes=None, comm=None):
    m, (_, rows, cols) = x.shape[0], wg.shape
    tm, tn, tk = _tile(m, 1024), _tile(cols, 1024), _tile(rows, 512)
    nps = cols // tn
    dtypes = dtypes or (dtype,)
    out = pl.BlockSpec((tm, tn), lambda i, j, k: (i, j))
    res = _mm(name, x, wg, (), grid=(m // tm, N_CHIPS * nps, rows // tk),
              a_spec=pl.BlockSpec((tm, tk), lambda i, j, k: (i, k)),
              b_spec=pl.BlockSpec((None, tk, tn), lambda i, j, k: (j // nps, k, j % nps)),
              extra_specs=(), out_specs=[out] * len(dtypes),
              out_shape=[jax.ShapeDtypeStruct((m, N_CHIPS * cols), d) for d in dtypes],
              dims=NN, epilogue=epilogue, acc_shape=(tm, tn), comm=comm)
    return res if len(dtypes) > 1 else res[0]


def _fwd_row(name, x, wg, dtype, comm=None):
    m, (_, rows, cols) = x.shape[0], wg.shape
    tm, tn, tk = _tile(m, 1024), _tile(cols, 1024), _tile(rows, 512)
    kps = rows // tk
    return _mm(name, x, wg, (), grid=(m // tm, cols // tn, N_CHIPS * kps),
               a_spec=pl.BlockSpec((tm, tk), lambda i, j, k: (i, k)),
               b_spec=pl.BlockSpec((None, tk, tn), lambda i, j, k: (k // kps, k % kps, j)),
               extra_specs=(), out_specs=[pl.BlockSpec((tm, tn), lambda i, j, k: (i, j))],
               out_shape=[jax.ShapeDtypeStruct((m, cols), dtype)],
               dims=NN, epilogue=_epi_store, acc_shape=(tm, tn), comm=comm)[0]


def _dx_col(name, dy, wg, dtype, epilogue=_epi_store, extra=None, comm=None):
    m, (_, rows, cols) = dy.shape[0], wg.shape
    tm, tn, tk = _tile(m, 1024), _tile(rows, 1024), _tile(cols, 512)
    kps = cols // tk
    tile = pl.BlockSpec((tm, tn), lambda i, j, k: (i, j))
    return _mm(name, dy, wg, () if extra is None else (extra,), grid=(m // tm, rows // tn, N_CHIPS * kps),
               a_spec=pl.BlockSpec((tm, tk), lambda i, j, k: (i, k)),
               b_spec=pl.BlockSpec((None, tn, tk), lambda i, j, k: (k // kps, j, k % kps)),
               extra_specs=() if extra is None else (tile,), out_specs=[tile],
               out_shape=[jax.ShapeDtypeStruct((m, rows), dtype)],
               dims=NT, epilogue=epilogue, acc_shape=(tm, tn), comm=comm)[0]


def _dx_row(name, dy, wg, dtype, epilogue=_epi_store, extra=None, comm=None):
    m, (_, rows, cols) = dy.shape[0], wg.shape
    tm, tn, tk = _tile(m, 1024), _tile(rows, 1024), _tile(cols, 512)
    nps = rows // tn
    tile = pl.BlockSpec((tm, tn), lambda i, j, k: (i, j))
    return _mm(name, dy, wg, () if extra is None else (extra,), grid=(m // tm, N_CHIPS * nps, cols // tk),
               a_spec=pl.BlockSpec((tm, tk), lambda i, j, k: (i, k)),
               b_spec=pl.BlockSpec((None, tn, tk), lambda i, j, k: (j // nps, j % nps, k)),
               extra_specs=() if extra is None else (tile,), out_specs=[tile],
               out_shape=[jax.ShapeDtypeStruct((m, N_CHIPS * rows), dtype)],
               dims=NT, epilogue=epilogue, acc_shape=(tm, tn), comm=comm)[0]


def _dw(name, x, dy, rows, cols, col_sharded, comm=None):
    s_tok = x.shape[0]
    tm, tn, tk = _tile(rows, 1024), _tile(cols, 1024), _tile(s_tok, 512)
    mt, nps = rows // tm, cols // tn
    if col_sharded:
        grid = (mt, N_CHIPS * nps, s_tok // tk)
        out = pl.BlockSpec((None, tm, tn), lambda i, j, k: (j // nps, i, j % nps))
    else:
        grid = (N_CHIPS * mt, nps, s_tok // tk)
        out = pl.BlockSpec((None, tm, tn), lambda i, j, k: (i // mt, i % mt, j))
    return _mm(name, x, dy, (), grid=grid,
               a_spec=pl.BlockSpec((tk, tm), lambda i, j, k: (k, i)),
               b_spec=pl.BlockSpec((tk, tn), lambda i, j, k: (k, j)),
               extra_specs=(), out_specs=[out],
               out_shape=[jax.ShapeDtypeStruct((N_CHIPS, rows, cols), WIRE_DTYPE)],
               dims=TN, epilogue=_epi_store, acc_shape=(tm, tn), comm=comm)[0]


def _ln_fwd(name, h, m, g, b):
    s, d = h.shape
    tm = _tile(s, 256)
    row = pl.BlockSpec((tm, d), lambda i: (i, 0))
    vec = pl.BlockSpec((1, d), lambda i: (0, 0))

    def body(h_ref, m_ref, g_ref, b_ref, y_ref, xh_ref, r_ref):
        z = ALPHA * h_ref[...] + m_ref[...]
        mu = jnp.mean(z, -1, keepdims=True)
        zc = z - mu
        r = lax.rsqrt(jnp.mean(zc * zc, -1, keepdims=True) + LN_EPS)
        xh = zc * r
        xh_ref[...] = xh
        r_ref[...] = r
        y_ref[...] = xh * g_ref[...] + b_ref[...]

    return pl.pallas_call(
        body, grid=(s // tm,), in_specs=[row, row, vec, vec],
        out_specs=[row, row, pl.BlockSpec((tm, 1), lambda i: (i, 0))],
        out_shape=[jax.ShapeDtypeStruct((s, d), F32), jax.ShapeDtypeStruct((s, d), F32),
                   jax.ShapeDtypeStruct((s, 1), F32)],
        name=name, compiler_params=_params("parallel"))(h, m, g, b)


def _ln_bwd(name, dy, xh, r, g):
    s, d = dy.shape
    tm = _tile(s, 256)
    row = pl.BlockSpec((tm, d), lambda i: (i, 0))
    vec = pl.BlockSpec((1, d), lambda i: (0, 0))

    def body(dy_ref, xh_ref, r_ref, g_ref, dz_ref, dg_ref, db_ref):
        i = pl.program_id(0)
        dy_, xh_ = dy_ref[...], xh_ref[...]
        dyg = dy_ * g_ref[...]
        m1 = jnp.mean(dyg, -1, keepdims=True)
        m2 = jnp.mean(dyg * xh_, -1, keepdims=True)
        dz_ref[...] = r_ref[...] * (dyg - m1 - xh_ * m2)
        pg = jnp.sum(dy_ * xh_, 0, keepdims=True)
        pb = jnp.sum(dy_, 0, keepdims=True)

        @pl.when(i == 0)
        def _():
            dg_ref[...] = pg
            db_ref[...] = pb

        @pl.when(i > 0)
        def _():
            dg_ref[...] += pg
            db_ref[...] += pb

    return pl.pallas_call(
        body, grid=(s // tm,), in_specs=[row, row, pl.BlockSpec((tm, 1), lambda i: (i, 0)), vec],
        out_specs=[row, vec, vec],
        out_shape=[jax.ShapeDtypeStruct((s, d), F32), jax.ShapeDtypeStruct((1, d), F32),
                   jax.ShapeDtypeStruct((1, d), F32)],
        name=name, compiler_params=_params("arbitrary"))(dy, xh, r, g)


def _loss_head(y, t):
    s, d = y.shape
    tm = _tile(s, 256)
    row = pl.BlockSpec((tm, d), lambda i: (i, 0))

    def body(y_ref, t_ref, l_ref, dy_ref):
        i = pl.program_id(0)
        e = y_ref[...] - t_ref[...]
        dy_ref[...] = e * (1.0 / d)
        part = 0.5 * jnp.sum(jnp.mean(e * e, -1, keepdims=True), 0, keepdims=True)

        @pl.when(i == 0)
        def _():
            l_ref[...] = part

        @pl.when(i > 0)
        def _():
            l_ref[...] += part

    return pl.pallas_call(
        body, grid=(s // tm,), in_specs=[row, row],
        out_specs=[pl.BlockSpec((1, 1), lambda i: (0, 0)), row],
        out_shape=[jax.ShapeDtypeStruct((1, 1), F32), jax.ShapeDtypeStruct((s, d), F32)],
        name="loss_head", compiler_params=_params("arbitrary"))(y, t)


def _rope_tables(s):
    half = MLA_ROPE // 2
    inv = ROPE_THETA ** (-jnp.arange(half, dtype=F32) / half)
    ang = jnp.arange(s).astype(F32)[:, None] * inv[None, :]
    cos, sin = jnp.cos(ang), jnp.sin(ang)
    c = jnp.concatenate([cos, cos, jnp.ones((s, 128 - MLA_ROPE), F32)], 1)
    s1 = jnp.concatenate([-sin, jnp.zeros((s, 128 - half), F32)], 1)
    s2 = jnp.concatenate([jnp.zeros((s, half), F32), sin, jnp.zeros((s, 128 - MLA_ROPE), F32)], 1)
    return c, s1, s2


def _rope(x, c, s1, s2):
    half = MLA_ROPE // 2
    return x * c + pltpu.roll(x, 128 - half, 1) * s1 + pltpu.roll(x, half, 1) * s2


def _rope_t(dy, c, s1, s2):
    half = MLA_ROPE // 2
    return dy * c + pltpu.roll(dy * s1, half, 1) + pltpu.roll(dy * s2, 128 - half, 1)


def _mla_mid_fwd(down, gq, gkv, tabs):
    s, w = down.shape
    ql, kvl = gq.shape[1], gkv.shape[1]
    tm = _tile(s, 256)

    def body(d_ref, gq_ref, gkv_ref, c_ref, s1_ref, s2_ref, cq_ref, ckv_ref, kr_ref):
        cq = d_ref[:, :ql]
        ckv = d_ref[:, ql:ql + kvl]
        cq_ref[...] = (cq * lax.rsqrt(jnp.mean(cq * cq, -1, keepdims=True) + RMS_EPS)
                       * gq_ref[...]).astype(cq_ref.dtype)
        ckv_ref[...] = (ckv * lax.rsqrt(jnp.mean(ckv * ckv, -1, keepdims=True) + RMS_EPS)
                        * gkv_ref[...]).astype(ckv_ref.dtype)
        kr_ref[...] = _rope(d_ref[:, ql + kvl:], c_ref[...], s1_ref[...], s2_ref[...]).astype(kr_ref.dtype)

    tab = pl.BlockSpec((tm, 128), lambda i: (i, 0))
    return pl.pallas_call(
        body, grid=(s // tm,),
        in_specs=[pl.BlockSpec((tm, w), lambda i: (i, 0)), pl.BlockSpec((1, ql), lambda i: (0, 0)),
                  pl.BlockSpec((1, kvl), lambda i: (0, 0)), tab, tab, tab],
        out_specs=[pl.BlockSpec((tm, ql), lambda i: (i, 0)), pl.BlockSpec((tm, kvl), lambda i: (i, 0)), tab],
        out_shape=[jax.ShapeDtypeStruct((s, ql), MXU_DTYPE), jax.ShapeDtypeStruct((s, kvl), MXU_DTYPE),
                   jax.ShapeDtypeStruct((s, 128), MXU_DTYPE)],
        name="mla_mid_fwd", compiler_params=_params("parallel"))(down, gq, gkv, *tabs)


def _mla_mid_bwd(down, dcq, dckv, dkr, gq, gkv, tabs):
    s, w = down.shape
    ql, kvl = gq.shape[1], gkv.shape[1]
    tm = _tile(s, 256)

    def rms_bwd(x, dy, g):
        n = x.shape[1]
        r = lax.rsqrt(jnp.mean(x * x, -1, keepdims=True) + RMS_EPS)
        dyg = dy * g
        dx = r * dyg - x * (r * r * r * (1.0 / n)) * jnp.sum(dyg * x, -1, keepdims=True)
        return dx, jnp.sum(dy * x * r, 0, keepdims=True)

    def body(d_ref, dcq_ref, dckv_ref, dkr_ref, gq_ref, gkv_ref, c_ref, s1_ref, s2_ref, o_ref, dgq_ref, dgkv_ref):
        i = pl.program_id(0)
        dxq, pq = rms_bwd(d_ref[:, :ql], dcq_ref[...], gq_ref[...])
        dxkv, pkv = rms_bwd(d_ref[:, ql:ql + kvl], dckv_ref[...], gkv_ref[...])
        o_ref[:, :ql] = dxq.astype(o_ref.dtype)
        o_ref[:, ql:ql + kvl] = dxkv.astype(o_ref.dtype)
        o_ref[:, ql + kvl:] = _rope_t(dkr_ref[...], c_ref[...], s1_ref[...], s2_ref[...]).astype(o_ref.dtype)

        @pl.when(i == 0)
        def _():
            dgq_ref[...] = pq
            dgkv_ref[...] = pkv

        @pl.when(i > 0)
        def _():
            dgq_ref[...] += pq
            dgkv_ref[...] += pkv

    tab = pl.BlockSpec((tm, 128), lambda i: (i, 0))
    vq = pl.BlockSpec((1, ql), lambda i: (0, 0))
    vkv = pl.BlockSpec((1, kvl), lambda i: (0, 0))
    full = pl.BlockSpec((tm, w), lambda i: (i, 0))
    return pl.pallas_call(
        body, grid=(s // tm,),
        in_specs=[full, pl.BlockSpec((tm, ql), lambda i: (i, 0)), pl.BlockSpec((tm, kvl), lambda i: (i, 0)), tab,
                  vq, vkv, tab, tab, tab],
        out_specs=[full, vq, vkv],
        out_shape=[jax.ShapeDtypeStruct((s, w), MXU_DTYPE), jax.ShapeDtypeStruct((1, ql), F32),
                   jax.ShapeDtypeStruct((1, kvl), F32)],
        name="mla_mid_bwd", compiler_params=_params("arbitrary"))(down, dcq, dckv, dkr, gq, gkv, *tabs)


def _iota2():
    return (lax.broadcasted_iota(jnp.int32, (TQ, TQ), 0), lax.broadcasted_iota(jnp.int32, (TQ, TQ), 1))


def _mla_attn_fwd(name, q, kv, kr, tabs, comm=None):
    s = q.shape[0]
    nq = s // TQ
    scale = MLA_QK_DIM ** -0.5

    def body(q_ref, kn_ref, v_ref, kr_ref, c_ref, s1_ref, s2_ref, o_ref, lse_ref):
        i = pl.program_id(1)
        row, col = _iota2()
        qn = q_ref[:, :HEAD_DIM].astype(MXU_DTYPE)
        qr = _rope(q_ref[:, HEAD_DIM:], c_ref[...], s1_ref[...], s2_ref[...]).astype(MXU_DTYPE)
        qc = jnp.right_shift(i * TQ + row, CHUNK_SHIFT)

        def step(kb, carry):
            m, l, acc = carry
            ks = pl.multiple_of(kb * TQ, TQ)
            sc = (_dot(qn, kn_ref[pl.ds(ks, TQ), :].astype(MXU_DTYPE), NT)
                  + _dot(qr, kr_ref[pl.ds(ks, TQ), :].astype(MXU_DTYPE), NT)) * scale
            sc = jnp.where(jnp.right_shift(ks + col, CHUNK_SHIFT) <= qc, sc, NEG)
            m_new = jnp.maximum(m, jnp.max(sc, -1, keepdims=True))
            p = jnp.exp(sc - m_new)
            corr = jnp.exp(m - m_new)
            l = corr * l + jnp.sum(p, -1, keepdims=True)
            acc = corr * acc + _dot(p.astype(MXU_DTYPE), v_ref[pl.ds(ks, TQ), :].astype(MXU_DTYPE))
            return m_new, l, acc

        m, l, acc = lax.fori_loop(0, i + 1, step, (jnp.full((TQ, 1), NEG, F32), jnp.zeros((TQ, 1), F32),
                                                   jnp.zeros((TQ, HEAD_DIM), F32)))
        o_ref[...] = acc / l
        lse_ref[...] = m + jnp.log(l)

    tab = pl.BlockSpec((TQ, 128), lambda h, i: (i, 0))
    return _call(
        body, (q, kv, kv, kr, *tabs), grid=(HEADS, nq),
        in_specs=[pl.BlockSpec((TQ, HEAD_PAD), lambda h, i: (i, h)),
                  pl.BlockSpec((s, HEAD_DIM), lambda h, i: (0, 2 * h)),
                  pl.BlockSpec((s, HEAD_DIM), lambda h, i: (0, 2 * h + 1)),
                  pl.BlockSpec((s, 128), lambda h, i: (0, 0)), tab, tab, tab],
        out_specs=[pl.BlockSpec((TQ, HEAD_DIM), lambda h, i: (i, h)),
                   pl.BlockSpec((None, TQ, 1), lambda h, i: (h, i, 0))],
        out_shape=[jax.ShapeDtypeStruct((s, HEADS * HEAD_DIM), F32), jax.ShapeDtypeStruct((HEADS, s, 1), F32)],
        name=name, semantics=("parallel", "parallel"), comm=comm)


def _mla_attn_bwd(name, q, kv, kr, tabs, do, o, lse, comm=None):
    s = q.shape[0]
    nq = s // TQ
    scale = MLA_QK_DIM ** -0.5

    def body(q_ref, kn_ref, v_ref, kr_ref, c_ref, s1_ref, s2_ref, do_ref, o_ref, lse_ref,
             dq_ref, dkv_ref, dkr_ref, dkv_acc, dkr_acc):
        h, i = pl.program_id(0), pl.program_id(1)
        row, col = _iota2()

        @pl.when(i == 0)
        def _():
            dkv_acc[...] = jnp.zeros_like(dkv_acc)

        @pl.when((h == 0) & (i == 0))
        def _():
            dkr_acc[...] = jnp.zeros_like(dkr_acc)

        tabs_i = (c_ref[...], s1_ref[...], s2_ref[...])
        qn = q_ref[:, :HEAD_DIM].astype(MXU_DTYPE)
        qr = _rope(q_ref[:, HEAD_DIM:], *tabs_i).astype(MXU_DTYPE)
        qc = jnp.right_shift(i * TQ + row, CHUNK_SHIFT)
        do_ = do_ref[...]
        delta = jnp.sum(do_ * o_ref[...], -1, keepdims=True)
        lse_ = lse_ref[...]
        dob = do_.astype(MXU_DTYPE)

        def step(kb, carry):
            dqn, dqr = carry
            ks = pl.multiple_of(kb * TQ, TQ)
            kn = kn_ref[pl.ds(ks, TQ), :].astype(MXU_DTYPE)
            krb = kr_ref[pl.ds(ks, TQ), :].astype(MXU_DTYPE)
            v = v_ref[pl.ds(ks, TQ), :].astype(MXU_DTYPE)
            sc = (_dot(qn, kn, NT) + _dot(qr, krb, NT)) * scale
            sc = jnp.where(jnp.right_shift(ks + col, CHUNK_SHIFT) <= qc, sc, NEG)
            p = jnp.exp(sc - lse_)
            ds = (p * (_dot(dob, v, NT) - delta) * scale).astype(MXU_DTYPE)
            dkv_acc[pl.ds(ks, TQ), :HEAD_DIM] += _dot(ds, qn, TN)
            dkv_acc[pl.ds(ks, TQ), HEAD_DIM:] += _dot(p.astype(MXU_DTYPE), dob, TN)
            dkr_acc[pl.ds(ks, TQ), :] += _dot(ds, qr, TN)
            return dqn + _dot(ds, kn), dqr + _dot(ds, krb)

        dqn, dqr = lax.fori_loop(0, i + 1, step, (jnp.zeros((TQ, HEAD_DIM), F32), jnp.zeros((TQ, 128), F32)))
        dq_ref[:, :HEAD_DIM] = dqn
        dq_ref[:, HEAD_DIM:] = _rope_t(dqr, *tabs_i)

        @pl.when(i == nq - 1)
        def _():
            dkv_ref[...] = dkv_acc[...].astype(dkv_ref.dtype)

        @pl.when((h == HEADS - 1) & (i == nq - 1))
        def _():
            dkr_ref[...] = dkr_acc[...]

    tab = pl.BlockSpec((TQ, 128), lambda h, i: (i, 0))
    qblk = pl.BlockSpec((TQ, HEAD_PAD), lambda h, i: (i, h))
    oblk = pl.BlockSpec((TQ, HEAD_DIM), lambda h, i: (i, h))
    return _call(
        body, (q, kv, kv, kr, *tabs, do, o, lse), grid=(HEADS, nq),
        in_specs=[qblk, pl.BlockSpec((s, HEAD_DIM), lambda h, i: (0, 2 * h)),
                  pl.BlockSpec((s, HEAD_DIM), lambda h, i: (0, 2 * h + 1)),
                  pl.BlockSpec((s, 128), lambda h, i: (0, 0)), tab, tab, tab, oblk, oblk,
                  pl.BlockSpec((None, TQ, 1), lambda h, i: (h, i, 0))],
        out_specs=[qblk, pl.BlockSpec((s, HEAD_PAD), lambda h, i: (0, h)),
                   pl.BlockSpec((s, 128), lambda h, i: (0, 0))],
        out_shape=[jax.ShapeDtypeStruct((s, HEADS * HEAD_PAD), F32), jax.ShapeDtypeStruct((s, HEADS * HEAD_PAD), MXU_DTYPE),
                   jax.ShapeDtypeStruct((s, 128), F32)],
        scratch_shapes=[pltpu.VMEM((s, HEAD_PAD), F32), pltpu.VMEM((s, 128), F32)],
        name=name, semantics=("arbitrary", "arbitrary"), comm=comm)


def _qkv_specs(s):
    return [pl.BlockSpec((TQ, HEAD_DIM), lambda h, i: (i, h)),
            pl.BlockSpec((s, HEAD_DIM), lambda h, i: (0, HEADS + h)),
            pl.BlockSpec((s, HEAD_DIM), lambda h, i: (0, 2 * HEADS + h))]


def _sb_terms(z):
    sp = jnp.log(1.0 + jnp.exp(-jnp.abs(z)))
    return jnp.minimum(z, 0.0) - sp, jnp.minimum(-z, 0.0) - sp


def _sb_attn_fwd(name, qkv, comm=None):
    s = qkv.shape[0]
    nq = s // TQ
    scale = HEAD_DIM ** -0.5

    def body(q_ref, k_ref, v_ref, o_ref):
        i = pl.program_id(1)
        row, col = _iota2()
        after = (row > col).astype(BF16)
        q = q_ref[...].astype(MXU_DTYPE)
        qpos = i * TQ + row

        def step(n, carry):
            tail, acc = carry
            ks = pl.multiple_of((i - n) * TQ, TQ)
            z = _dot(q, k_ref[pl.ds(ks, TQ), :].astype(MXU_DTYPE), NT) * scale
            strict = (ks + col) < qpos
            lb, l1 = _sb_terms(z)
            l1 = jnp.where(strict, l1, 0.0)
            a = jnp.where(strict, jnp.exp(lb + tail + _exact_dot(l1, after)), 0.0)
            acc = acc + _dot(a.astype(MXU_DTYPE), v_ref[pl.ds(ks, TQ), :].astype(MXU_DTYPE))
            return tail + jnp.sum(l1, -1, keepdims=True), acc

        _, acc = lax.fori_loop(0, i + 1, step, (jnp.zeros((TQ, 1), F32), jnp.zeros((TQ, HEAD_DIM), F32)))
        o_ref[...] = acc

    return _call(
        body, (qkv, qkv, qkv), grid=(HEADS, nq), in_specs=_qkv_specs(s),
        out_specs=[pl.BlockSpec((TQ, HEAD_DIM), lambda h, i: (i, h))],
        out_shape=[jax.ShapeDtypeStruct((s, HEADS * HEAD_DIM), F32)],
        name=name, semantics=("parallel", "parallel"), comm=comm)[0]


def _sb_attn_bwd(name, qkv, do, comm=None):
    s = qkv.shape[0]
    nq = s // TQ
    scale = HEAD_DIM ** -0.5

    def body(q_ref, k_ref, v_ref, do_ref, dq_ref, dk_ref, dv_ref, a_buf, dk_acc, dv_acc):
        i = pl.program_id(1)
        row, col = _iota2()
        after = (row > col).astype(BF16)
        before = (row < col).astype(BF16)

        @pl.when(i == 0)
        def _():
            dk_acc[...] = jnp.zeros_like(dk_acc)
            dv_acc[...] = jnp.zeros_like(dv_acc)

        q = q_ref[...].astype(MXU_DTYPE)
        dob = do_ref[...].astype(MXU_DTYPE)
        qpos = i * TQ + row

        def weights(n, tail):
            kb = i - n
            ks = pl.multiple_of(kb * TQ, TQ)
            z = _dot(q, k_ref[pl.ds(ks, TQ), :].astype(MXU_DTYPE), NT) * scale
            strict = (ks + col) < qpos
            lb, l1 = _sb_terms(z)
            l1 = jnp.where(strict, l1, 0.0)
            a = jnp.where(strict, jnp.exp(lb + tail + _exact_dot(l1, after)), 0.0)
            a_buf[kb] = a
            dv_acc[pl.ds(ks, TQ), :] += _dot(a.astype(MXU_DTYPE), dob, TN)
            return tail + jnp.sum(l1, -1, keepdims=True)

        lax.fori_loop(0, i + 1, weights, jnp.zeros((TQ, 1), F32))

        def grads(kb, carry):
            head, dq = carry
            ks = pl.multiple_of(kb * TQ, TQ)
            k = k_ref[pl.ds(ks, TQ), :].astype(MXU_DTYPE)
            z = _dot(q, k, NT) * scale
            strict = (ks + col) < qpos
            e = jnp.exp(-jnp.abs(z))
            beta = jnp.where(z >= 0.0, 1.0, e) / (1.0 + e)
            w = _dot(dob, v_ref[pl.ds(ks, TQ), :].astype(MXU_DTYPE), NT) * a_buf[kb]
            dz = jnp.where(strict, w * (1.0 - beta) - beta * (head + _exact_dot(w, before)), 0.0) * scale
            dzb = dz.astype(MXU_DTYPE)
            dk_acc[pl.ds(ks, TQ), :] += _dot(dzb, q, TN)
            return head + jnp.sum(w, -1, keepdims=True), dq + _dot(dzb, k)

        _, dq = lax.fori_loop(0, i + 1, grads, (jnp.zeros((TQ, 1), F32), jnp.zeros((TQ, HEAD_DIM), F32)))
        dq_ref[...] = dq.astype(dq_ref.dtype)

        @pl.when(i == nq - 1)
        def _():
            dk_ref[...] = dk_acc[...].astype(dk_ref.dtype)
            dv_ref[...] = dv_acc[...].astype(dv_ref.dtype)

    blk = pl.BlockSpec((TQ, HEAD_DIM), lambda h, i: (i, h))
    col_h = pl.BlockSpec((s, HEAD_DIM), lambda h, i: (0, h))
    shp = jax.ShapeDtypeStruct((s, HEADS * HEAD_DIM), MXU_DTYPE)
    return _call(
        body, (qkv, qkv, qkv, do), grid=(HEADS, nq), in_specs=_qkv_specs(s) + [blk],
        out_specs=[blk, col_h, col_h], out_shape=[shp, shp, shp],
        scratch_shapes=[pltpu.VMEM((nq, TQ, TQ), F32), pltpu.VMEM((s, HEAD_DIM), F32), pltpu.VMEM((s, HEAD_DIM), F32)],
        name=name, semantics=("arbitrary", "arbitrary"), comm=comm)


LANES = 128
LAST_REL = (1 << CHUNK_SHIFT) - 1


def _ca_subtiles():
    nb = TQ // LANES
    return [(d, a, b, -d * TQ + (b - a) * LANES) for d in range(3) for a in range(nb) for b in range(nb)]


def _ca_bias_tiles(rel_bias):
    table = jnp.pad(jnp.transpose(rel_bias), ((0, 0), (0, 2 * LANES - REL_TABLE)))[:, None]

    def unskew(x, row):
        for b in range(7):
            x = jnp.where((jnp.right_shift(row, b) & 1) == 1, pltpu.roll(x, 1 << b, 1), x)
        return x

    def body(t_ref, o_ref):
        row = lax.broadcasted_iota(jnp.int32, (LANES, LANES), 0)
        col = lax.broadcasted_iota(jnp.int32, (LANES, LANES), 1)
        lane = col[:1]
        lo, hi = t_ref[:, :LANES], t_ref[:, LANES:]
        first = jnp.sum(jnp.where(lane == 0, lo, 0.0), -1, keepdims=True)
        last = jnp.sum(jnp.where(lane == LAST_REL, hi, 0.0), -1, keepdims=True)
        hi = jnp.where(lane <= LAST_REL, hi, last)
        r_lo = unskew(jnp.broadcast_to(lo, (LANES, LANES)), row)
        r_hi = unskew(jnp.broadcast_to(hi, (LANES, LANES)), row)
        upper = col >= row
        for d, a, b, o in _ca_subtiles():
            if o >= LANES:
                piece = jnp.broadcast_to(last, (LANES, LANES))
            elif o == 0:
                piece = jnp.where(upper, r_hi, r_lo)
            elif o == -LANES:
                piece = jnp.where(upper, r_lo, first)
            else:
                piece = jnp.broadcast_to(first, (LANES, LANES))
            o_ref[d, a * LANES:(a + 1) * LANES, b * LANES:(b + 1) * LANES] = piece

    return pl.pallas_call(
        body, grid=(HEADS,), in_specs=[pl.BlockSpec((None, 1, 2 * LANES), lambda h: (h, 0, 0))],
        out_specs=pl.BlockSpec((None, 3, TQ, TQ), lambda h: (h, 0, 0, 0)),
        out_shape=jax.ShapeDtypeStruct((HEADS, 3, TQ, TQ), F32),
        name="ca_bias_tiles", compiler_params=_params("parallel"))(table)


def _ca_mask(i, ks, row, col):
    qc = jnp.right_shift(i * TQ + row, CHUNK_SHIFT)
    kc = jnp.right_shift(ks + col, CHUNK_SHIFT)
    return (kc <= qc) & (kc >= qc - CA_LEFT_CHUNKS)


def _ca_attn_fwd(name, qkv, tiles, comm=None):
    s = qkv.shape[0]
    nq = s // TQ
    scale = HEAD_DIM ** -0.5

    def body(q_ref, k_ref, v_ref, bt_ref, o_ref, lse_ref):
        i = pl.program_id(1)
        row, col = _iota2()
        q = q_ref[...].astype(MXU_DTYPE)

        def step(kb, carry):
            m, l, acc = carry
            ks = pl.multiple_of(kb * TQ, TQ)
            sc = _dot(q, k_ref[pl.ds(ks, TQ), :].astype(MXU_DTYPE), NT) * scale + bt_ref[jnp.minimum(i - kb, 2)]
            sc = jnp.where(_ca_mask(i, ks, row, col), sc, NEG)
            m_new = jnp.maximum(m, jnp.max(sc, -1, keepdims=True))
            p = jnp.exp(sc - m_new)
            corr = jnp.exp(m - m_new)
            l = corr * l + jnp.sum(p, -1, keepdims=True)
            acc = corr * acc + _dot(p.astype(MXU_DTYPE), v_ref[pl.ds(ks, TQ), :].astype(MXU_DTYPE))
            return m_new, l, acc

        m, l, acc = lax.fori_loop(jnp.maximum(i - CA_LEFT_BLOCKS, 0), i + 1, step,
                                  (jnp.full((TQ, 1), NEG, F32), jnp.zeros((TQ, 1), F32),
                                   jnp.zeros((TQ, HEAD_DIM), F32)))
        o_ref[...] = acc / l
        lse_ref[...] = m + jnp.log(l)

    return _call(
        body, (qkv, qkv, qkv, tiles), grid=(HEADS, nq),
        in_specs=_qkv_specs(s) + [pl.BlockSpec((None, 3, TQ, TQ), lambda h, i: (h, 0, 0, 0))],
        out_specs=[pl.BlockSpec((TQ, HEAD_DIM), lambda h, i: (i, h)),
                   pl.BlockSpec((None, TQ, 1), lambda h, i: (h, i, 0))],
        out_shape=[jax.ShapeDtypeStruct((s, HEADS * HEAD_DIM), F32), jax.ShapeDtypeStruct((HEADS, s, 1), F32)],
        name=name, semantics=("parallel", "parallel"), comm=comm)


def _ca_attn_bwd(name, qkv, tiles, do, o, lse, comm=None):
    s = qkv.shape[0]
    nq = s // TQ
    scale = HEAD_DIM ** -0.5

    def body(q_ref, k_ref, v_ref, bt_ref, do_ref, o_ref, lse_ref, dq_ref, dk_ref, dv_ref, dbt_ref, dk_acc, dv_acc):
        i = pl.program_id(1)
        row, col = _iota2()

        @pl.when(i == 0)
        def _():
            dk_acc[...] = jnp.zeros_like(dk_acc)
            dv_acc[...] = jnp.zeros_like(dv_acc)
            dbt_ref[...] = jnp.zeros_like(dbt_ref)

        q = q_ref[...].astype(MXU_DTYPE)
        do_ = do_ref[...]
        delta = jnp.sum(do_ * o_ref[...], -1, keepdims=True)
        lse_ = lse_ref[...]
        dob = do_.astype(MXU_DTYPE)

        def step(kb, dq):
            ks = pl.multiple_of(kb * TQ, TQ)
            k = k_ref[pl.ds(ks, TQ), :].astype(MXU_DTYPE)
            v = v_ref[pl.ds(ks, TQ), :].astype(MXU_DTYPE)
            t = jnp.minimum(i - kb, 2)
            sc = _dot(q, k, NT) * scale + bt_ref[t]
            sc = jnp.where(_ca_mask(i, ks, row, col), sc, NEG)
            p = jnp.exp(sc - lse_)
            dsc = p * (_dot(dob, v, NT) - delta)
            dbt_ref[t] += dsc
            ds = (dsc * scale).astype(MXU_DTYPE)
            dk_acc[pl.ds(ks, TQ), :] += _dot(ds, q, TN)
            dv_acc[pl.ds(ks, TQ), :] += _dot(p.astype(MXU_DTYPE), dob, TN)
            return dq + _dot(ds, k)

        dq = lax.fori_loop(jnp.maximum(i - CA_LEFT_BLOCKS, 0), i + 1, step, jnp.zeros((TQ, HEAD_DIM), F32))
        dq_ref[...] = dq.astype(dq_ref.dtype)

        @pl.when(i == nq - 1)
        def _():
            dk_ref[...] = dk_acc[...].astype(dk_ref.dtype)
            dv_ref[...] = dv_acc[...].astype(dv_ref.dtype)

    blk = pl.BlockSpec((TQ, HEAD_DIM), lambda h, i: (i, h))
    col_h = pl.BlockSpec((s, HEAD_DIM), lambda h, i: (0, h))
    tile = pl.BlockSpec((None, 3, TQ, TQ), lambda h, i: (h, 0, 0, 0))
    shp = jax.ShapeDtypeStruct((s, HEADS * HEAD_DIM), MXU_DTYPE)
    return _call(
        body, (qkv, qkv, qkv, tiles, do, o, lse), grid=(HEADS, nq),
        in_specs=_qkv_specs(s) + [tile, blk, blk, pl.BlockSpec((None, TQ, 1), lambda h, i: (h, i, 0))],
        out_specs=[blk, col_h, col_h, tile],
        out_shape=[shp, shp, shp, jax.ShapeDtypeStruct((HEADS, 3, TQ, TQ), F32)],
        scratch_shapes=[pltpu.VMEM((s, HEAD_DIM), F32), pltpu.VMEM((s, HEAD_DIM), F32)],
        name=name, semantics=("arbitrary", "arbitrary"), comm=comm)


def _ca_table_grad(dtiles):
    def skew(x, row):
        for b in range(7):
            x = jnp.where((jnp.right_shift(row, b) & 1) == 1, pltpu.roll(x, LANES - (1 << b), 1), x)
        return x

    def body(t_ref, o_ref):
        row = lax.broadcasted_iota(jnp.int32, (LANES, LANES), 0)
        col = lax.broadcasted_iota(jnp.int32, (LANES, LANES), 1)
        wrapped = (row + col) >= LANES
        lane = col[:1]
        total = lambda x: jnp.sum(jnp.sum(x, 0, keepdims=True), -1, keepdims=True)
        lo = jnp.zeros((1, LANES), F32)
        hi = jnp.zeros((1, LANES), F32)
        for d, a, b, o in _ca_subtiles():
            x = t_ref[d, a * LANES:(a + 1) * LANES, b * LANES:(b + 1) * LANES]
            if o >= LANES:
                hi = hi + jnp.where(lane == LAST_REL, total(x), 0.0)
            elif o <= -2 * LANES:
                lo = lo + jnp.where(lane == 0, total(x), 0.0)
            else:
                y = skew(x, row)
                pos = jnp.sum(jnp.where(wrapped, 0.0, y), 0, keepdims=True)
                neg = jnp.sum(jnp.where(wrapped, y, 0.0), 0, keepdims=True)
                if o == 0:
                    clipped = jnp.sum(jnp.where(lane > LAST_REL, pos, 0.0), -1, keepdims=True)
                    hi = hi + jnp.where(lane <= LAST_REL, pos, 0.0) + jnp.where(lane == LAST_REL, clipped, 0.0)
                    lo = lo + neg
                else:
                    lo = lo + pos + jnp.where(lane == 0, jnp.sum(neg, -1, keepdims=True), 0.0)
        o_ref[:, :LANES] = lo
        o_ref[:, LANES:] = hi

    return pl.pallas_call(
        body, grid=(HEADS,), in_specs=[pl.BlockSpec((None, 3, TQ, TQ), lambda h: (h, 0, 0, 0))],
        out_specs=pl.BlockSpec((None, 1, 2 * LANES), lambda h: (h, 0, 0)),
        out_shape=jax.ShapeDtypeStruct((HEADS, 1, 2 * LANES), F32),
        name="ca_table_grad", compiler_params=_params("parallel"))(dtiles)


def _row_tile(rows, cols):
    if rows % 128:
        return rows
    tr = 128
    while rows % (2 * tr) == 0 and 2 * tr * cols * 4 <= (1 << 20):
        tr *= 2
    return tr


def _adamw(name, w, g, m, v):
    rows, cols = w.shape
    tr = _row_tile(rows, cols)
    blk = pl.BlockSpec((tr, cols), lambda i: (i, 0))

    def body(w_ref, g_ref, m_ref, v_ref, d_ref, m2_ref, v2_ref):
        g_ = g_ref[...]
        m2 = ADAM_B1 * m_ref[...] + (1.0 - ADAM_B1) * g_
        v2 = ADAM_B2 * v_ref[...] + (1.0 - ADAM_B2) * jnp.square(g_)
        m_hat = m2 / (1.0 - ADAM_B1 ** ADAM_STEP)
        v_hat = v2 / (1.0 - ADAM_B2 ** ADAM_STEP)
        d_ref[...] = -ADAM_LR * (m_hat / (jnp.sqrt(v_hat) + ADAM_EPS) + ADAM_WD * w_ref[...])
        m2_ref[...] = m2
        v2_ref[...] = v2

    shp = jax.ShapeDtypeStruct((rows, cols), F32)
    return pl.pallas_call(body, grid=(rows // tr,), in_specs=[blk] * 4, out_specs=[blk] * 3, out_shape=[shp] * 3,
                          name=name, compiler_params=_params("parallel"))(w, g, m, v)


def _pair_sum(name, g, r1, c):
    _, rh, cols = r1.shape
    tr = _row_tile(rh, cols)
    nb = rh // tr

    def body(c_ref, g_ref, r_ref, o_ref):
        o_ref[...] = (g_ref[...].astype(F32) + r_ref[...].astype(F32)).astype(o_ref.dtype)

    return pl.pallas_call(
        body, grid_spec=pltpu.PrefetchScalarGridSpec(
            num_scalar_prefetch=1, grid=(N_CHIPS, nb),
            in_specs=[pl.BlockSpec((None, tr, cols), lambda k, i, c_ref: (k, c_ref[0] * nb + i, 0)),
                      pl.BlockSpec((None, tr, cols), lambda k, i, c_ref: (k, i, 0))],
            out_specs=pl.BlockSpec((None, tr, cols), lambda k, i, c_ref: (k, i, 0))),
        out_shape=jax.ShapeDtypeStruct(r1.shape, WIRE_DTYPE), name=name,
        compiler_params=_params("parallel", "parallel"))(c, g, r1)


def _chip_sum(name, a1, r2, place, total, l, layers):
    _, rh, cols = a1.shape
    tr = _row_tile(rh, cols)
    nb = rh // tr

    def body(p_ref, a_ref, r_ref, *rest):
        rest[-1][...] = ((a_ref[...].astype(F32) + r_ref[0].astype(F32)) + r_ref[1].astype(F32)) + r_ref[2].astype(F32)

    return pl.pallas_call(
        body, grid_spec=pltpu.PrefetchScalarGridSpec(
            num_scalar_prefetch=1, grid=(nb,),
            in_specs=[pl.BlockSpec((None, tr, cols), lambda i, p_ref: (p_ref[0], i, 0)),
                      pl.BlockSpec((N_CHIPS - 1, tr, cols), lambda i, p_ref: (0, i, 0))] + ([] if total is None else [ANY]),
            out_specs=pl.BlockSpec((tr, cols), lambda i, p_ref: ((2 * l + p_ref[1]) * nb + i, 0))),
        out_shape=jax.ShapeDtypeStruct((layers * 2 * rh, cols), F32), name=name,
        input_output_aliases={} if total is None else {3: 0},
        compiler_params=_params("parallel"))(place, a1, r2, *([] if total is None else [total]))


def _cast_place(name, w, chip, l, rows):
    cols = w.shape[1]
    tr = _row_tile(rows, cols)
    nb = rows // tr

    def body(c_ref, w_ref, o_ref):
        o_ref[...] = w_ref[...].astype(o_ref.dtype)

    return pl.pallas_call(
        body, grid_spec=pltpu.PrefetchScalarGridSpec(
            num_scalar_prefetch=1, grid=(nb,),
            in_specs=[pl.BlockSpec((tr, cols), lambda i, c_ref: (l * nb + i, 0))],
            out_specs=pl.BlockSpec((None, tr, cols), lambda i, c_ref: (c_ref[0], i, 0))),
        out_shape=jax.ShapeDtypeStruct((N_CHIPS, rows, cols), MXU_DTYPE), name=name,
        compiler_params=_params("parallel"))(chip, w)


def _place():
    x, y, c = lax.axis_index("x"), lax.axis_index("y"), lax.axis_index("c")
    chips = [(1 - x, y), (x, 1 - y), (1 - x, 1 - y)]
    return x, y, c, chips


def _remote(src, dst, send_sem, recv_sem, to):
    return pltpu.make_async_remote_copy(src_ref=src, dst_ref=dst, send_sem=send_sem, recv_sem=recv_sem,
                                        device_id=to, device_id_type=MESH)


def _all_gather(placed):
    n = len(placed)

    def body(*refs):
        outs = refs[n:2 * n]
        send, recv = refs[2 * n:]
        x, y, c, chips = _place()
        me, sib = 2 * x + y, (x, y, 1 - c)

        def half(t, which):
            return pl.ds(which * (outs[t].shape[1] // 2), outs[t].shape[1] // 2)

        first, passed = [], []
        for t in range(n):
            for j, chip in enumerate(chips):
                mine = outs[t].at[me, half(t, c)]
                cp = _remote(mine, mine, send.at[6 * t + j], recv.at[6 * t + j], (*chip, c))
                cp.start()
                first.append(cp)
        for t in range(n):
            for j, (px, py) in enumerate(chips):
                got = outs[t].at[2 * px + py, half(t, c)]
                _remote(got, got, send.at[6 * t + j], recv.at[6 * t + j], (px, py, c)).wait_recv()
                cp = _remote(got, got, send.at[6 * t + 3 + j], recv.at[6 * t + 3 + j], sib)
                cp.start()
                passed.append(cp)
        for t in range(n):
            for j, (px, py) in enumerate(chips):
                got = outs[t].at[2 * px + py, half(t, 1 - c)]
                _remote(got, got, send.at[6 * t + 3 + j], recv.at[6 * t + 3 + j], sib).wait_recv()
        for cp in first + passed:
            cp.wait_send()

    return pl.pallas_call(
        body, in_specs=[ANY] * n, out_specs=[ANY] * n,
        out_shape=[jax.ShapeDtypeStruct(p.shape, p.dtype) for p in placed],
        scratch_shapes=[pltpu.SemaphoreType.DMA((6 * n,)), pltpu.SemaphoreType.DMA((6 * n,))],
        input_output_aliases={t: t for t in range(n)},
        name="weight_all_gather")(*placed)


def _in_place(bufs):
    return [jax.ShapeDtypeStruct(b.shape, b.dtype) for b in bufs], {t: t for t in range(len(bufs))}


def _gather_ici(bufs):
    def copies(srcs, dsts, send, recv):
        x, y, c, chips = _place()
        out = []
        for t, buf in enumerate(dsts):
            rh = buf.shape[1] // 2
            mine = buf.at[2 * x + y, pl.ds(c * rh, rh)]
            out += [_remote(mine, mine, send.at[3 * t + j], recv.at[3 * t + j], (*chip, c))
                    for j, chip in enumerate(chips)]
        return out

    return _Comm(bufs, *_in_place(bufs), copies, 3 * len(bufs))


def _gather_pass(bufs):
    def copies(srcs, dsts, send, recv):
        x, y, c, chips = _place()
        out = []
        for t, buf in enumerate(dsts):
            rh = buf.shape[1] // 2
            for j, (px, py) in enumerate(chips):
                got = buf.at[2 * px + py, pl.ds(c * rh, rh)]
                out.append(_remote(got, got, send.at[3 * t + j], recv.at[3 * t + j], (x, y, 1 - c)))
        return out

    return _Comm(bufs, *_in_place(bufs), copies, 3 * len(bufs))


def _reduce_pair(grads):
    def copies(srcs, dsts, send, recv):
        x, y, c, _ = _place()
        out = []
        for t, g in enumerate(srcs):
            rh = g.shape[1] // 2
            out.append(_remote(g.at[:, pl.ds((1 - c) * rh, rh)], dsts[t], send.at[t], recv.at[t], (x, y, 1 - c)))
        return out

    return _Comm(grads, [jax.ShapeDtypeStruct((N_CHIPS, g.shape[1] // 2, g.shape[2]), g.dtype) for g in grads], {},
                 copies, len(grads))


def _reduce_chips(parts):
    def copies(srcs, dsts, send, recv):
        x, y, c, chips = _place()
        return [_remote(p.at[2 * px + py], dsts[t].at[j], send.at[3 * t + j], recv.at[3 * t + j], (px, py, c))
                for t, p in enumerate(srcs) for j, (px, py) in enumerate(chips)]

    return _Comm(parts, [jax.ShapeDtypeStruct((N_CHIPS - 1, *p.shape[1:]), p.dtype) for p in parts], {}, copies,
                 3 * len(parts))


def _reduce_share(totals, spans):
    def copies(srcs, dsts, send, recv):
        x, y, c, _ = _place()
        out = []
        for t, (buf, (start, rh)) in enumerate(zip(dsts, spans)):
            mine = buf.at[pl.ds(start + c * rh, rh)]
            out.append(_remote(mine, mine, send.at[t], recv.at[t], (x, y, 1 - c)))
        return out

    return _Comm(totals, *_in_place(totals), copies, len(totals))


def _exchange(name, comm):
    ns, nd = len(comm.srcs), len(comm.dsts)

    def body(*refs):
        send, recv = refs[ns + nd:]
        cps = comm.copies(refs[:ns], refs[ns:ns + nd], send, recv)
        for cp in cps:
            cp.start()
        for cp in cps:
            cp.wait()

    comm.out = pl.pallas_call(
        body, in_specs=[ANY] * ns, out_specs=[ANY] * nd, out_shape=comm.dsts,
        scratch_shapes=[pltpu.SemaphoreType.DMA((comm.n,)), pltpu.SemaphoreType.DMA((comm.n,))],
        input_output_aliases=comm.alias, name=name)(*comm.srcs)
    return comm.out


def _all_reduce_small(name, pack):
    rows, cols = pack.shape

    def body(p_ref, o_ref, slots, send, recv):
        x, y, c, _ = _place()
        me = 4 * x + 2 * y + c
        slots[me] = p_ref[...]
        cps = []
        for r in range(1, N_DEV):
            to = ((1 - x) if r & 4 else x, (1 - y) if r & 2 else y, (1 - c) if r & 1 else c)
            cp = _remote(p_ref, slots.at[me], send.at[r - 1], recv.at[r - 1], to)
            cp.start()
            cps.append(cp)
        for cp in cps:
            cp.wait()
        acc = slots[0]
        for d in range(1, N_DEV):
            acc = acc + slots[d]
        o_ref[...] = acc

    return pl.pallas_call(
        body, in_specs=[VMEM_SPEC], out_specs=VMEM_SPEC, out_shape=jax.ShapeDtypeStruct((rows, cols), F32),
        scratch_shapes=[pltpu.VMEM((N_DEV, rows, cols), F32), pltpu.SemaphoreType.DMA((N_DEV - 1,)),
                        pltpu.SemaphoreType.DMA((N_DEV - 1,))],
        name=name)(pack)


BIG = ("ffn_w_in", "ffn_w_out", "mla_w_down", "mla_w_uq", "mla_w_ukv", "mla_w_o", "sb_w_qkv", "sb_w_o", "ca_w_qkv",
       "ca_w_o")
COL_SHARDED = {"ffn_w_in": True, "ffn_w_out": False, "mla_w_down": False, "mla_w_uq": True, "mla_w_ukv": True,
               "mla_w_o": False, "sb_w_qkv": True, "sb_w_o": False, "ca_w_qkv": True, "ca_w_o": False}
WEIGHTS = ("ln_mix_g", "ln_mix_b", "ln_ffn_g", "ln_ffn_b", "ffn_w_in", "ffn_w_out", "mla_w_down", "mla_q_norm_g",
           "mla_w_uq", "mla_kv_norm_g", "mla_w_ukv", "mla_w_o", "sb_w_qkv", "sb_w_o", "ca_w_qkv", "ca_rel_bias",
           "ca_w_o")
HEADS_PER_CHIP = HEADS // N_CHIPS


def _mxu_shards(w):
    down = w["mla_w_down"]
    uq = w["mla_w_uq"]
    n, ql = uq.shape[:2]
    lane_pad = 128 - MLA_ROPE
    shaped = dict(w)
    shaped["mla_w_down"] = jnp.pad(down, ((0, 0), (0, 0), (0, lane_pad)))
    shaped["mla_w_uq"] = jnp.pad(uq.reshape(n, ql, HEADS_PER_CHIP, MLA_QK_DIM),
                                 ((0, 0), (0, 0), (0, 0), (0, HEAD_PAD - MLA_QK_DIM))).reshape(n, ql, -1)
    return {k: shaped[k].reshape(-1, shaped[k].shape[-1]) for k in BIG}


def _unpad_grad(name, g, like):
    if name == "mla_w_down":
        g = g[:, :like.shape[-1]]
    elif name == "mla_w_uq":
        g = g.reshape(g.shape[0], HEADS_PER_CHIP, HEAD_PAD)[:, :, :MLA_QK_DIM]
    return g.reshape(like.shape)


def _mixer_keys(i):
    kind, slot = i % 3, i // 3
    if kind == 0:
        return [(k, slot) for k in ("mla_w_down", "mla_w_uq", "mla_w_ukv", "mla_w_o")]
    pre = "sb" if kind == 1 else "ca"
    return [(f"{pre}_w_qkv", slot), (f"{pre}_w_o", slot)]


def _ffn_keys(i):
    return [("ffn_w_in", i), ("ffn_w_out", i)]


def _step(x, target, wl, dims, p, tabs, place):
    dist = place is not None
    wl = dict(wl)

    def carried(make, keys):
        return make([wl[k] for k in keys]) if dist and keys else None

    def landed(keys, comm):
        if comm is not None:
            wl.update(zip(keys, comm.out))

    if dist:
        first = _mixer_keys(0)
        wl.update(zip(first, _all_gather([wl[k] for k in first])))
    saved = []
    h = x
    for i in range(DEPTH):
        kind, slot = i % 3, i // 3
        sv = {"h0": h}
        ffn, nxt = _ffn_keys(i), (_mixer_keys(i + 1) if i + 1 < DEPTH else [])
        ici = carried(_gather_ici, ffn)
        if kind == 0:
            down = _fwd_row(f"mla_down_{i}", h, wl["mla_w_down", slot], F32)
            gq, gkv = p["mla_q_norm_g"][slot][None], p["mla_kv_norm_g"][slot][None]
            cq, ckv, kr = _mla_mid_fwd(down, gq, gkv, tabs)
            q = _fwd_col(f"mla_uq_{i}", cq, wl["mla_w_uq", slot], F32)
            kv = _fwd_col(f"mla_ukv_{i}", ckv, wl["mla_w_ukv", slot], MXU_DTYPE)
            o, lse = _mla_attn_fwd(f"mla_attn_fwd_{i}", q, kv, kr, tabs, comm=ici)
            landed(ffn, ici)
            fwd = carried(_gather_pass, ffn)
            m = _fwd_row(f"mla_o_{i}", o, wl["mla_w_o", slot], F32, comm=fwd)
            sv.update(down=down, gq=gq, gkv=gkv, cq=cq, ckv=ckv, kr=kr, q=q, kv=kv, o=o, lse=lse)
        else:
            pre = "sb" if kind == 1 else "ca"
            qkv = _fwd_col(f"{pre}_qkv_{i}", h, wl[f"{pre}_w_qkv", slot], MXU_DTYPE)
            if kind == 1:
                o = _sb_attn_fwd(f"sb_attn_fwd_{i}", qkv, comm=ici)
            else:
                tiles = _ca_bias_tiles(p["ca_rel_bias"][slot])
                o, lse = _ca_attn_fwd(f"ca_attn_fwd_{i}", qkv, tiles, comm=ici)
                sv.update(tiles=tiles, lse=lse)
            landed(ffn, ici)
            fwd = carried(_gather_pass, ffn)
            m = _fwd_row(f"{pre}_o_{i}", o, wl[f"{pre}_w_o", slot], F32, comm=fwd)
            sv.update(qkv=qkv, o=o)
        landed(ffn, fwd)
        h1, sv["xh1"], sv["r1"] = _ln_fwd(f"ln_mix_{i}", h, m, p["ln_mix_g"][i][None], p["ln_mix_b"][i][None])
        ici = carried(_gather_ici, nxt)
        u, a = _fwd_col(f"ffn_in_{i}", h1, wl["ffn_w_in", i], None, epilogue=_epi_relu2, dtypes=(F32, MXU_DTYPE),
                        comm=ici)
        landed(nxt, ici)
        fwd = carried(_gather_pass, nxt)
        y = _fwd_row(f"ffn_out_{i}", a, wl["ffn_w_out", i], F32, comm=fwd)
        landed(nxt, fwd)
        h2, sv["xh2"], sv["r2"] = _ln_fwd(f"ln_ffn_{i}", h1, y, p["ln_ffn_g"][i][None], p["ln_ffn_b"][i][None])
        sv.update(h1=h1, u=u, a=a)
        saved.append(sv)
        h = h2

    part, pair, total = {}, {}, {}

    def dw(key, xin, dyin):
        _, rows, cols = dims[key[0]]
        part[key] = _dw(f"dw_{key[0]}_{key[1]}", xin, dyin, rows, cols, COL_SHARDED[key[0]])

    def to_sibling(keys):
        return _reduce_pair([part[k] for k in keys]) if dist and keys else None

    def pair_sums(keys, comm):
        for k, r1 in zip(keys, comm.out if comm is not None else ()):
            pair[k] = _pair_sum(f"pair_sum_{k[0]}_{k[1]}", part[k], r1, place[1:])

    def to_chips(keys):
        return _reduce_chips([pair[k] for k in keys]) if dist and keys else None

    def chip_sums(keys, comm):
        for (name, l), r2 in zip(keys, comm.out if comm is not None else ()):
            total[name] = _chip_sum(f"chip_sum_{name}_{l}", pair[name, l], r2, place, total.get(name), l,
                                    dims[name][0])

    def share(keys):
        spans = [(l * dims[name][1], dims[name][1] // 2) for name, l in keys]
        return _reduce_share([total[name] for name, _ in keys], spans) if dist and keys else None

    def shared(keys, comm):
        if comm is not None:
            total.update(zip([name for name, _ in keys], comm.out))

    loss, dy = _loss_head(h, target)
    small = {k: [None] * DEPTH for k in ("ln_mix_g", "ln_mix_b", "ln_ffn_g", "ln_ffn_b")}
    n_mla = p["mla_q_norm_g"].shape[0]
    small["mla_q_norm_g"], small["mla_kv_norm_g"] = [None] * n_mla, [None] * n_mla
    for i in reversed(range(DEPTH)):
        kind, slot = i % 3, i // 3
        sv = saved[i]
        later = _mixer_keys(i + 1) if i + 1 < DEPTH else []
        dz, small["ln_ffn_g"][i], small["ln_ffn_b"][i] = _ln_bwd(f"ln_ffn_bwd_{i}", dy, sv["xh2"], sv["r2"],
                                                                 p["ln_ffn_g"][i][None])
        dw(("ffn_w_out", i), sv["a"], dz)
        early = [("ffn_w_out", i)] + later
        sib = to_sibling(early)
        du = _dx_row(f"ffn_du_{i}", dz, wl["ffn_w_out", i], MXU_DTYPE, epilogue=_epi_drelu2, extra=sv["u"], comm=sib)
        pair_sums(early, sib)
        dw(("ffn_w_in", i), sv["h1"], du)
        sib = to_sibling([("ffn_w_in", i)])
        dy = _dx_col(f"ffn_dh_{i}", du, wl["ffn_w_in", i], F32, epilogue=_epi_residual, extra=dz, comm=sib)
        pair_sums([("ffn_w_in", i)], sib)
        ready = early + [("ffn_w_in", i)]
        dz, small["ln_mix_g"][i], small["ln_mix_b"][i] = _ln_bwd(f"ln_mix_bwd_{i}", dy, sv["xh1"], sv["r1"],
                                                                 p["ln_mix_g"][i][None])
        ici = to_chips(ready)
        if kind == 0:
            dw(("mla_w_o", slot), sv["o"], dz)
            do = _dx_row(f"mla_do_{i}", dz, wl["mla_w_o", slot], F32)
            dq, dkv, dkr = _mla_attn_bwd(f"mla_attn_bwd_{i}", sv["q"], sv["kv"], sv["kr"], tabs, do, sv["o"],
                                         sv["lse"], comm=ici)
            chip_sums(ready, ici)
            dw(("mla_w_uq", slot), sv["cq"], dq)
            dcq = _dx_col(f"mla_dcq_{i}", dq, wl["mla_w_uq", slot], F32)
            dw(("mla_w_ukv", slot), sv["ckv"], dkv)
            dckv = _dx_col(f"mla_dckv_{i}", dkv, wl["mla_w_ukv", slot], F32)
            ddown, small["mla_q_norm_g"][slot], small["mla_kv_norm_g"][slot] = _mla_mid_bwd(
                sv["down"], dcq, dckv, dkr, sv["gq"], sv["gkv"], tabs)
            dw(("mla_w_down", slot), sv["h0"], ddown)
            both = share(ready)
            dy = _dx_row(f"mla_dh_{i}", ddown, wl["mla_w_down", slot], F32, epilogue=_epi_residual, extra=dz,
                         comm=both)
        else:
            pre = "sb" if kind == 1 else "ca"
            dw((f"{pre}_w_o", slot), sv["o"], dz)
            do = _dx_row(f"{pre}_do_{i}", dz, wl[f"{pre}_w_o", slot], F32)
            if kind == 1:
                dq, dk, dv = _sb_attn_bwd(f"sb_attn_bwd_{i}", sv["qkv"], do, comm=ici)
            else:
                dq, dk, dv, dtiles = _ca_attn_bwd(f"ca_attn_bwd_{i}", sv["qkv"], sv["tiles"], do, sv["o"], sv["lse"],
                                                  comm=ici)
                small["ca_rel_bias"] = [jnp.transpose(_ca_table_grad(dtiles)[:, 0, :REL_TABLE])]
            chip_sums(ready, ici)
            dqkv = jnp.concatenate([dq, dk, dv], 1)
            dw((f"{pre}_w_qkv", slot), sv["h0"], dqkv)
            both = share(ready)
            dy = _dx_col(f"{pre}_dh_{i}", dqkv, wl[f"{pre}_w_qkv", slot], F32, epilogue=_epi_residual, extra=dz,
                         comm=both)
        shared(ready, both)
    small = {k: jnp.stack([g.reshape(g.shape[-2:]) if k == "ca_rel_bias" else g[0] for g in v]) for k, v in small.items()}
    if not dist:
        return loss, dy, part, small
    last = _mixer_keys(0)
    sib = to_sibling(last)
    _exchange("grad_pair_last", sib)
    pair_sums(last, sib)
    ici = to_chips(last)
    _exchange("grad_chips_last", ici)
    chip_sums(last, ici)
    both = share(last)
    _exchange("grad_share_last", both)
    shared(last, both)
    return loss, dy, total, small


SMALL = ("ln_mix_g", "ln_mix_b", "ln_ffn_g", "ln_ffn_b", "mla_q_norm_g", "mla_kv_norm_g", "ca_rel_bias")


def _pack_small(parts, width):
    flat = jnp.concatenate([parts[k].reshape(-1) for k in SMALL])
    rows = -(-flat.shape[0] // width)
    rows += -rows % 8
    return jnp.pad(flat, (0, rows * width - flat.shape[0])).reshape(rows, width)


def _unpack_small(pack, like):
    flat, out, at = pack.reshape(-1), {}, 0
    for k in SMALL:
        n = int(np.prod(like[k].shape))
        out[k] = flat[at:at + n].reshape(like[k].shape)
        at += n
    return out


def kernel(x, ln_mix_g, ln_mix_b, ln_ffn_g, ln_ffn_b, ffn_w_in, ffn_w_out, mla_w_down, mla_q_norm_g, mla_w_uq, mla_kv_norm_g, mla_w_ukv, mla_w_o, sb_w_qkv, sb_w_o, ca_w_qkv, ca_rel_bias, ca_w_o, loss_target, m_ln_mix_g, m_ln_mix_b, m_ln_ffn_g, m_ln_ffn_b, m_ffn_w_in, m_ffn_w_out, m_mla_w_down, m_mla_q_norm_g, m_mla_w_uq, m_mla_kv_norm_g, m_mla_w_ukv, m_mla_w_o, m_sb_w_qkv, m_sb_w_o, m_ca_w_qkv, m_ca_rel_bias, m_ca_w_o, v_ln_mix_g, v_ln_mix_b, v_ln_ffn_g, v_ln_ffn_b, v_ffn_w_in, v_ffn_w_out, v_mla_w_down, v_mla_q_norm_g, v_mla_w_uq, v_mla_kv_norm_g, v_mla_w_ukv, v_mla_w_o, v_sb_w_qkv, v_sb_w_o, v_ca_w_qkv, v_ca_rel_bias, v_ca_w_o):
    w = dict(zip(WEIGHTS, (ln_mix_g, ln_mix_b, ln_ffn_g, ln_ffn_b, ffn_w_in, ffn_w_out, mla_w_down, mla_q_norm_g,
                           mla_w_uq, mla_kv_norm_g, mla_w_ukv, mla_w_o, sb_w_qkv, sb_w_o, ca_w_qkv, ca_rel_bias,
                           ca_w_o)))
    mom1 = dict(zip(WEIGHTS, (m_ln_mix_g, m_ln_mix_b, m_ln_ffn_g, m_ln_ffn_b, m_ffn_w_in, m_ffn_w_out, m_mla_w_down,
                              m_mla_q_norm_g, m_mla_w_uq, m_mla_kv_norm_g, m_mla_w_ukv, m_mla_w_o, m_sb_w_qkv,
                              m_sb_w_o, m_ca_w_qkv, m_ca_rel_bias, m_ca_w_o)))
    mom2 = dict(zip(WEIGHTS, (v_ln_mix_g, v_ln_mix_b, v_ln_ffn_g, v_ln_ffn_b, v_ffn_w_in, v_ffn_w_out, v_mla_w_down,
                              v_mla_q_norm_g, v_mla_w_uq, v_mla_kv_norm_g, v_mla_w_ukv, v_mla_w_o, v_sb_w_qkv,
                              v_sb_w_o, v_ca_w_qkv, v_ca_rel_bias, v_ca_w_o)))
    xi, yi, ci = lax.axis_index("x"), lax.axis_index("y"), lax.axis_index("c")
    chip = 2 * xi + yi
    d_model = x.shape[-1]

    shards = _mxu_shards(w)
    layers = {k: w[k].shape[0] for k in BIG}
    dims = {k: (layers[k], shards[k].shape[0] // layers[k], shards[k].shape[1]) for k in BIG}
    chip1 = jnp.reshape(chip, (1,)).astype(jnp.int32)
    wl = {(k, l): _cast_place(f"cast_{k}_{l}", shards[k], chip1, l, dims[k][1]) for k in BIG for l in range(layers[k])}
    gains = jnp.stack([w["mla_q_norm_g"], w["mla_kv_norm_g"]])
    gains = jnp.where(ci == 0, gains, 0.0)
    placed = lax.dynamic_update_slice_in_dim(jnp.zeros((*gains.shape[:2], N_CHIPS, gains.shape[2]), F32),
                                             gains[:, :, None], chip, 2)
    full_gains = _all_reduce_small("norm_gain_gather", placed.reshape(2 * gains.shape[1], -1))
    full_gains = full_gains.reshape(2, gains.shape[1], -1)
    p = {"ln_mix_g": ln_mix_g, "ln_mix_b": ln_mix_b, "ln_ffn_g": ln_ffn_g, "ln_ffn_b": ln_ffn_b,
         "mla_q_norm_g": full_gains[0], "mla_kv_norm_g": full_gains[1], "ca_rel_bias": ca_rel_bias}

    place = jnp.stack([chip, ci]).astype(jnp.int32)
    loss, grad_x, total, small = _step(x[0], loss_target[0], wl, dims, p, _rope_tables(x.shape[1]), place)
    loss = lax.psum(loss[0, 0], ("x", "y", "c"))

    small = _unpack_small(_all_reduce_small("small_grad_all_reduce", _pack_small(small, d_model)), small)
    grad = {k: small[k] for k in ("ln_mix_g", "ln_mix_b", "ln_ffn_g", "ln_ffn_b", "ca_rel_bias")}
    for k in ("mla_q_norm_g", "mla_kv_norm_g"):
        g = small[k].reshape(small[k].shape[0], N_CHIPS, -1)
        grad[k] = lax.dynamic_index_in_dim(g, chip, 1, keepdims=False)

    for k in BIG:
        grad[k] = _unpad_grad(k, total[k], w[k])

    delta, new_m, new_v = {}, {}, {}
    for k in WEIGHTS:
        flat = lambda a: a.reshape(-1, a.shape[-1])
        dl, m2, v2 = _adamw(f"adamw_{k}", flat(w[k]), flat(grad[k]), flat(mom1[k]), flat(mom2[k]))
        delta[k], new_m[k], new_v[k] = dl.reshape(w[k].shape), m2.reshape(w[k].shape), v2.reshape(w[k].shape)
    return (loss, grad_x[None], *[grad[k] for k in WEIGHTS], *[delta[k] for k in WEIGHTS],
            *[new_m[k] for k in WEIGHTS], *[new_v[k] for k in WEIGHTS])
```

```python
import functools

import numpy as np
import jax
import jax.numpy as jnp
from jax import lax
from jax.experimental import pallas as pl
from jax.experimental.pallas import tpu as pltpu

F32, BF16 = jnp.float32, jnp.bfloat16
MXU_DTYPE = BF16
WIRE_DTYPE = BF16

DEPTH = 4
HEADS = 16
HEAD_DIM = 128
CHUNK_SHIFT = 6
TQ = 256
MLA_ROPE = 64
MLA_QK_DIM = 192
HEAD_PAD = 256
CA_LEFT_CHUNKS = 8
CA_LEFT_BLOCKS = (CA_LEFT_CHUNKS << CHUNK_SHIFT) // TQ
REL_CLIP_LEFT = 128
REL_TABLE = 192
ROPE_THETA = 10000.0
LN_EPS = 1e-5
RMS_EPS = 1e-6
ALPHA = (2.0 * DEPTH) ** 0.25
NEG = -1e30
ADAM_LR, ADAM_B1, ADAM_B2, ADAM_EPS, ADAM_WD, ADAM_STEP = 0.001, 0.9, 0.999, 1e-08, 0.01, 10
N_CHIPS = 4
N_DEV = 8
VMEM_LIMIT = 48 << 20
MESH = pl.DeviceIdType.MESH
ANY = pl.BlockSpec(memory_space=pl.ANY)
VMEM_SPEC = pl.BlockSpec(memory_space=pltpu.VMEM)

NN = (((1,), (0,)), ((), ()))
NT = (((1,), (1,)), ((), ()))
TN = (((0,), (0,)), ((), ()))


def _dot(a, b, dims=NN):
    return lax.dot_general(a, b, dims, preferred_element_type=F32)


def _exact_dot(x, u):
    hi = x.astype(BF16)
    r1 = x - hi.astype(F32)
    mid = r1.astype(BF16)
    lo = (r1 - mid.astype(F32)).astype(BF16)
    return _dot(hi, u) + _dot(mid, u) + _dot(lo, u)


def _params(*sem):
    return pltpu.CompilerParams(dimension_semantics=sem, vmem_limit_bytes=VMEM_LIMIT)


TILE_K = 2048


def _tile(n, pref):
    for t in (2048, 1536, 1152, 1024, 768, 512, 384, 256, 128):
        if t <= pref and n % t == 0:
            return t
    return n


class _Comm:
    def __init__(self, srcs, dsts, alias, copies, n):
        self.srcs, self.dsts, self.alias, self.copies, self.n = list(srcs), list(dsts), dict(alias), copies, n
        self.out = None

    def deliver(self, out):
        self.out = list(out)


class _SemView:
    def __init__(self, sems, base):
        self.sems, self.base, self.at = sems, base, self

    def __getitem__(self, i):
        return self.sems.at[self.base + i]


class _Merged(_Comm):
    def __init__(self, parts):
        srcs, dsts, alias, n, self.spans = [], [], {}, 0, []
        for c in parts:
            self.spans.append((c, len(srcs), len(dsts), n))
            alias.update({len(srcs) + s: len(dsts) + d for s, d in c.alias.items()})
            srcs, dsts, n = srcs + c.srcs, dsts + c.dsts, n + c.n

        def copies(src_refs, dst_refs, send, recv):
            out = []
            for c, s0, d0, n0 in self.spans:
                out += c.copies(src_refs[s0:s0 + len(c.srcs)], dst_refs[d0:d0 + len(c.dsts)], _SemView(send, n0),
                                _SemView(recv, n0))
            return out

        super().__init__(srcs, dsts, alias, copies, n)

    def deliver(self, out):
        self.out = list(out)
        for c, _, d0, _ in self.spans:
            c.deliver(self.out[d0:d0 + len(c.dsts)])


def _merged(*comms):
    comms = [c for c in comms if c is not None]
    return None if not comms else comms[0] if len(comms) == 1 else _Merged(comms)


def _call(body, args, *, name, grid, in_specs, out_specs, out_shape, scratch_shapes=(), semantics, comm=None):
    in_specs, out_specs, out_shape = list(in_specs), list(out_specs), list(out_shape)
    scratch_shapes = list(scratch_shapes)
    if comm is None:
        return pl.pallas_call(body, grid=grid, in_specs=in_specs, out_specs=out_specs, out_shape=out_shape,
                              scratch_shapes=scratch_shapes, name=name, compiler_params=_params(*semantics))(*args)
    n_in, n_out, n_scr, ns, nd = len(in_specs), len(out_specs), len(scratch_shapes), len(comm.srcs), len(comm.dsts)

    def carrier(*refs):
        ins, refs = refs[:n_in], refs[n_in:]
        srcs, refs = refs[:ns], refs[ns:]
        outs, refs = refs[:n_out], refs[n_out:]
        dsts, refs = refs[:nd], refs[nd:]
        scratch, (send, recv) = refs[:n_scr], refs[n_scr:]
        ids = [pl.program_id(a) for a in range(len(grid))]
        first = functools.reduce(jnp.logical_and, [i == 0 for i in ids])
        last = functools.reduce(jnp.logical_and, [i == g - 1 for i, g in zip(ids, grid)])

        @pl.when(first)
        def _():
            for cp in comm.copies(srcs, dsts, send, recv):
                cp.start()

        body(*ins, *outs, *scratch)

        @pl.when(last)
        def _():
            for cp in comm.copies(srcs, dsts, send, recv):
                cp.wait()

    res = pl.pallas_call(
        carrier, grid=grid, in_specs=in_specs + [ANY] * ns, out_specs=out_specs + [ANY] * nd,
        out_shape=out_shape + comm.dsts,
        scratch_shapes=scratch_shapes + [pltpu.SemaphoreType.DMA((comm.n,)), pltpu.SemaphoreType.DMA((comm.n,))],
        input_output_aliases={n_in + s: n_out + d for s, d in comm.alias.items()}, name=name,
        compiler_params=_params(*["arbitrary"] * len(grid)))(*args, *comm.srcs)
    comm.deliver(res[n_out:])
    return res[:n_out]


def _mm(name, a, b, extras, *, grid, a_spec, b_spec, extra_specs, out_specs, out_shape, dims,
        epilogue, acc_shape, comm=None):
    nk = grid[2]
    n_ex = len(extras)

    def product(a_ref, b_ref):
        return lax.dot_general(a_ref[...].astype(MXU_DTYPE), b_ref[...].astype(MXU_DTYPE), dims,
                               preferred_element_type=F32)

    def whole(*refs):
        epilogue(product(*refs[:2]), refs[2:2 + n_ex], refs[2 + n_ex:])

    def stepped(*refs):
        a_ref, b_ref = refs[:2]
        ex = refs[2:2 + n_ex]
        outs = refs[2 + n_ex:-1]
        acc = refs[-1]
        k = pl.program_id(2)

        @pl.when(k == 0)
        def _():
            acc[...] = product(a_ref, b_ref)

        @pl.when(k > 0)
        def _():
            acc[...] += product(a_ref, b_ref)

        @pl.when(k == nk - 1)
        def _():
            epilogue(acc[...], ex, outs)

    return _call(whole if nk == 1 else stepped, (a, b, *extras), name=name, grid=grid,
                 in_specs=[a_spec, b_spec, *extra_specs], out_specs=out_specs, out_shape=out_shape,
                 scratch_shapes=[] if nk == 1 else [pltpu.VMEM(acc_shape, F32)],
                 semantics=("parallel", "parallel", "arbitrary"), comm=comm)


def _epi_store(acc, ex, outs):
    outs[0][...] = acc.astype(outs[0].dtype)


def _epi_relu2(acc, ex, outs):
    outs[0][...] = acc
    r = jnp.maximum(acc, 0.0)
    outs[1][...] = (r * r).astype(outs[1].dtype)


def _epi_drelu2(acc, ex, outs):
    outs[0][...] = (acc * (2.0 * jnp.maximum(ex[0][...], 0.0))).astype(outs[0].dtype)


def _epi_residual(acc, ex, outs):
    outs[0][...] = acc + ALPHA * ex[0][...]


def _fwd_col(name, x, wg, dtype, epilogue=_epi_store, dtypes=None, comm=None):
    m, (_, rows, cols) = x.shape[0], wg.shape
    tm, tn, tk = _tile(m, 1024), _tile(cols, 1024), _tile(rows, TILE_K)
    nps = cols // tn
    dtypes = dtypes or (dtype,)
    out = pl.BlockSpec((tm, tn), lambda i, j, k: (i, j))
    res = _mm(name, x, wg, (), grid=(m // tm, N_CHIPS * nps, rows // tk),
              a_spec=pl.BlockSpec((tm, tk), lambda i, j, k: (i, k)),
              b_spec=pl.BlockSpec((None, tk, tn), lambda i, j, k: (j // nps, k, j % nps)),
              extra_specs=(), out_specs=[out] * len(dtypes),
              out_shape=[jax.ShapeDtypeStruct((m, N_CHIPS * cols), d) for d in dtypes],
              dims=NN, epilogue=epilogue, acc_shape=(tm, tn), comm=comm)
    return res if len(dtypes) > 1 else res[0]


def _fwd_row(name, x, wg, dtype, comm=None):
    m, (_, rows, cols) = x.shape[0], wg.shape
    tm, tn, tk = _tile(m, 1024), _tile(cols, 1024), _tile(rows, TILE_K)
    kps = rows // tk
    return _mm(name, x, wg, (), grid=(m // tm, cols // tn, N_CHIPS * kps),
               a_spec=pl.BlockSpec((tm, tk), lambda i, j, k: (i, k)),
               b_spec=pl.BlockSpec((None, tk, tn), lambda i, j, k: (k // kps, k % kps, j)),
               extra_specs=(), out_specs=[pl.BlockSpec((tm, tn), lambda i, j, k: (i, j))],
               out_shape=[jax.ShapeDtypeStruct((m, cols), dtype)],
               dims=NN, epilogue=_epi_store, acc_shape=(tm, tn), comm=comm)[0]


def _dx_col(name, dy, wg, dtype, epilogue=_epi_store, extra=None, comm=None):
    m, (_, rows, cols) = dy.shape[0], wg.shape
    tm, tn, tk = _tile(m, 1024), _tile(rows, 1024), _tile(cols, TILE_K)
    kps = cols // tk
    tile = pl.BlockSpec((tm, tn), lambda i, j, k: (i, j))
    return _mm(name, dy, wg, () if extra is None else (extra,), grid=(m // tm, rows // tn, N_CHIPS * kps),
               a_spec=pl.BlockSpec((tm, tk), lambda i, j, k: (i, k)),
               b_spec=pl.BlockSpec((None, tn, tk), lambda i, j, k: (k // kps, j, k % kps)),
               extra_specs=() if extra is None else (tile,), out_specs=[tile],
               out_shape=[jax.ShapeDtypeStruct((m, rows), dtype)],
               dims=NT, epilogue=epilogue, acc_shape=(tm, tn), comm=comm)[0]


def _dx_row(name, dy, wg, dtype, epilogue=_epi_store, extra=None, comm=None):
    m, (_, rows, cols) = dy.shape[0], wg.shape
    tm, tn, tk = _tile(m, 1024), _tile(rows, 1024), _tile(cols, TILE_K)
    nps = rows // tn
    tile = pl.BlockSpec((tm, tn), lambda i, j, k: (i, j))
    return _mm(name, dy, wg, () if extra is None else (extra,), grid=(m // tm, N_CHIPS * nps, cols // tk),
               a_spec=pl.BlockSpec((tm, tk), lambda i, j, k: (i, k)),
               b_spec=pl.BlockSpec((None, tn, tk), lambda i, j, k: (j // nps, j % nps, k)),
               extra_specs=() if extra is None else (tile,), out_specs=[tile],
               out_shape=[jax.ShapeDtypeStruct((m, N_CHIPS * rows), dtype)],
               dims=NT, epilogue=epilogue, acc_shape=(tm, tn), comm=comm)[0]


def _dw(name, x, dy, rows, cols, col_sharded, comm=None):
    s_tok = x.shape[0]
    tm, tn, tk = _tile(rows, 1024), _tile(cols, 1024), _tile(s_tok, TILE_K)
    mt, nps = rows // tm, cols // tn
    if col_sharded:
        grid = (mt, N_CHIPS * nps, s_tok // tk)
        out = pl.BlockSpec((None, tm, tn), lambda i, j, k: (j // nps, i, j % nps))
    else:
        grid = (N_CHIPS * mt, nps, s_tok // tk)
        out = pl.BlockSpec((None, tm, tn), lambda i, j, k: (i // mt, i % mt, j))
    return _mm(name, x, dy, (), grid=grid,
               a_spec=pl.BlockSpec((tk, tm), lambda i, j, k: (k, i)),
               b_spec=pl.BlockSpec((tk, tn), lambda i, j, k: (k, j)),
               extra_specs=(), out_specs=[out],
               out_shape=[jax.ShapeDtypeStruct((N_CHIPS, rows, cols), WIRE_DTYPE)],
               dims=TN, epilogue=_epi_store, acc_shape=(tm, tn), comm=comm)[0]


def _ln_fwd(name, h, m, g, b):
    s, d = h.shape
    tm = _tile(s, 256)
    row = pl.BlockSpec((tm, d), lambda i: (i, 0))
    vec = pl.BlockSpec((1, d), lambda i: (0, 0))

    def body(h_ref, m_ref, g_ref, b_ref, y_ref, ymx_ref, xh_ref, r_ref):
        z = ALPHA * h_ref[...] + m_ref[...]
        mu = jnp.mean(z, -1, keepdims=True)
        zc = z - mu
        r = lax.rsqrt(jnp.mean(zc * zc, -1, keepdims=True) + LN_EPS)
        xh = zc * r
        xh_ref[...] = xh
        r_ref[...] = r
        y = xh * g_ref[...] + b_ref[...]
        y_ref[...] = y
        ymx_ref[...] = y.astype(ymx_ref.dtype)

    return pl.pallas_call(
        body, grid=(s // tm,), in_specs=[row, row, vec, vec],
        out_specs=[row, row, row, pl.BlockSpec((tm, 1), lambda i: (i, 0))],
        out_shape=[jax.ShapeDtypeStruct((s, d), F32), jax.ShapeDtypeStruct((s, d), MXU_DTYPE),
                   jax.ShapeDtypeStruct((s, d), F32), jax.ShapeDtypeStruct((s, 1), F32)],
        name=name, compiler_params=_params("parallel"))(h, m, g, b)


def _ln_bwd(name, dy, xh, r, g):
    s, d = dy.shape
    tm = _tile(s, 256)
    row = pl.BlockSpec((tm, d), lambda i: (i, 0))
    vec = pl.BlockSpec((1, d), lambda i: (0, 0))

    def body(dy_ref, xh_ref, r_ref, g_ref, dz_ref, dzmx_ref, dg_ref, db_ref):
        i = pl.program_id(0)
        dy_, xh_ = dy_ref[...], xh_ref[...]
        dyg = dy_ * g_ref[...]
        m1 = jnp.mean(dyg, -1, keepdims=True)
        m2 = jnp.mean(dyg * xh_, -1, keepdims=True)
        dz = r_ref[...] * (dyg - m1 - xh_ * m2)
        dz_ref[...] = dz
        dzmx_ref[...] = dz.astype(dzmx_ref.dtype)
        pg = jnp.sum(dy_ * xh_, 0, keepdims=True)
        pb = jnp.sum(dy_, 0, keepdims=True)

        @pl.when(i == 0)
        def _():
            dg_ref[...] = pg
            db_ref[...] = pb

        @pl.when(i > 0)
        def _():
            dg_ref[...] += pg
            db_ref[...] += pb

    return pl.pallas_call(
        body, grid=(s // tm,), in_specs=[row, row, pl.BlockSpec((tm, 1), lambda i: (i, 0)), vec],
        out_specs=[row, row, vec, vec],
        out_shape=[jax.ShapeDtypeStruct((s, d), F32), jax.ShapeDtypeStruct((s, d), MXU_DTYPE),
                   jax.ShapeDtypeStruct((1, d), F32), jax.ShapeDtypeStruct((1, d), F32)],
        name=name, compiler_params=_params("arbitrary"))(dy, xh, r, g)


def _loss_head(y, t):
    s, d = y.shape
    tm = _tile(s, 256)
    row = pl.BlockSpec((tm, d), lambda i: (i, 0))

    def body(y_ref, t_ref, l_ref, dy_ref):
        i = pl.program_id(0)
        e = y_ref[...] - t_ref[...]
        dy_ref[...] = e * (1.0 / d)
        part = 0.5 * jnp.sum(jnp.mean(e * e, -1, keepdims=True), 0, keepdims=True)

        @pl.when(i == 0)
        def _():
            l_ref[...] = part

        @pl.when(i > 0)
        def _():
            l_ref[...] += part

    return pl.pallas_call(
        body, grid=(s // tm,), in_specs=[row, row],
        out_specs=[pl.BlockSpec((1, 1), lambda i: (0, 0)), row],
        out_shape=[jax.ShapeDtypeStruct((1, 1), F32), jax.ShapeDtypeStruct((s, d), F32)],
        name="loss_head", compiler_params=_params("arbitrary"))(y, t)


def _rope_tables(s):
    half = MLA_ROPE // 2
    inv = ROPE_THETA ** (-jnp.arange(half, dtype=F32) / half)
    ang = jnp.arange(s).astype(F32)[:, None] * inv[None, :]
    cos, sin = jnp.cos(ang), jnp.sin(ang)
    c = jnp.concatenate([cos, cos, jnp.ones((s, 128 - MLA_ROPE), F32)], 1)
    s1 = jnp.concatenate([-sin, jnp.zeros((s, 128 - half), F32)], 1)
    s2 = jnp.concatenate([jnp.zeros((s, half), F32), sin, jnp.zeros((s, 128 - MLA_ROPE), F32)], 1)
    return c, s1, s2


def _rope(x, c, s1, s2):
    half = MLA_ROPE // 2
    return x * c + pltpu.roll(x, 128 - half, 1) * s1 + pltpu.roll(x, half, 1) * s2


def _rope_t(dy, c, s1, s2):
    half = MLA_ROPE // 2
    return dy * c + pltpu.roll(dy * s1, half, 1) + pltpu.roll(dy * s2, 128 - half, 1)


def _mla_mid_fwd(down, gq, gkv, tabs):
    s, w = down.shape
    ql, kvl = gq.shape[1], gkv.shape[1]
    tm = _tile(s, 256)

    def body(d_ref, gq_ref, gkv_ref, c_ref, s1_ref, s2_ref, cq_ref, ckv_ref, kr_ref):
        cq = d_ref[:, :ql]
        ckv = d_ref[:, ql:ql + kvl]
        cq_ref[...] = (cq * lax.rsqrt(jnp.mean(cq * cq, -1, keepdims=True) + RMS_EPS)
                       * gq_ref[...]).astype(cq_ref.dtype)
        ckv_ref[...] = (ckv * lax.rsqrt(jnp.mean(ckv * ckv, -1, keepdims=True) + RMS_EPS)
                        * gkv_ref[...]).astype(ckv_ref.dtype)
        kr_ref[...] = _rope(d_ref[:, ql + kvl:], c_ref[...], s1_ref[...], s2_ref[...]).astype(kr_ref.dtype)

    tab = pl.BlockSpec((tm, 128), lambda i: (i, 0))
    return pl.pallas_call(
        body, grid=(s // tm,),
        in_specs=[pl.BlockSpec((tm, w), lambda i: (i, 0)), pl.BlockSpec((1, ql), lambda i: (0, 0)),
                  pl.BlockSpec((1, kvl), lambda i: (0, 0)), tab, tab, tab],
        out_specs=[pl.BlockSpec((tm, ql), lambda i: (i, 0)), pl.BlockSpec((tm, kvl), lambda i: (i, 0)), tab],
        out_shape=[jax.ShapeDtypeStruct((s, ql), MXU_DTYPE), jax.ShapeDtypeStruct((s, kvl), MXU_DTYPE),
                   jax.ShapeDtypeStruct((s, 128), MXU_DTYPE)],
        name="mla_mid_fwd", compiler_params=_params("parallel"))(down, gq, gkv, *tabs)


def _mla_mid_bwd(down, dcq, dckv, dkr, gq, gkv, tabs):
    s, w = down.shape
    ql, kvl = gq.shape[1], gkv.shape[1]
    tm = _tile(s, 256)

    def rms_bwd(x, dy, g):
        n = x.shape[1]
        r = lax.rsqrt(jnp.mean(x * x, -1, keepdims=True) + RMS_EPS)
        dyg = dy * g
        dx = r * dyg - x * (r * r * r * (1.0 / n)) * jnp.sum(dyg * x, -1, keepdims=True)
        return dx, jnp.sum(dy * x * r, 0, keepdims=True)

    def body(d_ref, dcq_ref, dckv_ref, dkr_ref, gq_ref, gkv_ref, c_ref, s1_ref, s2_ref, o_ref, dgq_ref, dgkv_ref):
        i = pl.program_id(0)
        dxq, pq = rms_bwd(d_ref[:, :ql], dcq_ref[...], gq_ref[...])
        dxkv, pkv = rms_bwd(d_ref[:, ql:ql + kvl], dckv_ref[...], gkv_ref[...])
        o_ref[:, :ql] = dxq.astype(o_ref.dtype)
        o_ref[:, ql:ql + kvl] = dxkv.astype(o_ref.dtype)
        o_ref[:, ql + kvl:] = _rope_t(dkr_ref[...], c_ref[...], s1_ref[...], s2_ref[...]).astype(o_ref.dtype)

        @pl.when(i == 0)
        def _():
            dgq_ref[...] = pq
            dgkv_ref[...] = pkv

        @pl.when(i > 0)
        def _():
            dgq_ref[...] += pq
            dgkv_ref[...] += pkv

    tab = pl.BlockSpec((tm, 128), lambda i: (i, 0))
    vq = pl.BlockSpec((1, ql), lambda i: (0, 0))
    vkv = pl.BlockSpec((1, kvl), lambda i: (0, 0))
    full = pl.BlockSpec((tm, w), lambda i: (i, 0))
    return pl.pallas_call(
        body, grid=(s // tm,),
        in_specs=[full, pl.BlockSpec((tm, ql), lambda i: (i, 0)), pl.BlockSpec((tm, kvl), lambda i: (i, 0)), tab,
                  vq, vkv, tab, tab, tab],
        out_specs=[full, vq, vkv],
        out_shape=[jax.ShapeDtypeStruct((s, w), MXU_DTYPE), jax.ShapeDtypeStruct((1, ql), F32),
                   jax.ShapeDtypeStruct((1, kvl), F32)],
        name="mla_mid_bwd", compiler_params=_params("arbitrary"))(down, dcq, dckv, dkr, gq, gkv, *tabs)


def _iota2():
    return (lax.broadcasted_iota(jnp.int32, (TQ, TQ), 0), lax.broadcasted_iota(jnp.int32, (TQ, TQ), 1))


def _mla_attn_fwd(name, q, kv, kr, tabs, comm=None):
    s = q.shape[0]
    nq = s // TQ
    scale = MLA_QK_DIM ** -0.5

    def body(q_ref, kn_ref, v_ref, kr_ref, c_ref, s1_ref, s2_ref, o_ref, omx_ref, lse_ref):
        i = pl.program_id(1)
        row, col = _iota2()
        qn = q_ref[:, :HEAD_DIM].astype(MXU_DTYPE)
        qr = _rope(q_ref[:, HEAD_DIM:], c_ref[...], s1_ref[...], s2_ref[...]).astype(MXU_DTYPE)
        qc = jnp.right_shift(i * TQ + row, CHUNK_SHIFT)

        def step(kb, carry):
            m, l, acc = carry
            ks = pl.multiple_of(kb * TQ, TQ)
            sc = (_dot(qn, kn_ref[pl.ds(ks, TQ), :].astype(MXU_DTYPE), NT)
                  + _dot(qr, kr_ref[pl.ds(ks, TQ), :].astype(MXU_DTYPE), NT)) * scale
            sc = jnp.where(jnp.right_shift(ks + col, CHUNK_SHIFT) <= qc, sc, NEG)
            m_new = jnp.maximum(m, jnp.max(sc, -1, keepdims=True))
            p = jnp.exp(sc - m_new)
            corr = jnp.exp(m - m_new)
            l = corr * l + jnp.sum(p, -1, keepdims=True)
            acc = corr * acc + _dot(p.astype(MXU_DTYPE), v_ref[pl.ds(ks, TQ), :].astype(MXU_DTYPE))
            return m_new, l, acc

        m, l, acc = lax.fori_loop(0, i + 1, step, (jnp.full((TQ, 1), NEG, F32), jnp.zeros((TQ, 1), F32),
                                                   jnp.zeros((TQ, HEAD_DIM), F32)))
        o = acc / l
        o_ref[...] = o
        omx_ref[...] = o.astype(omx_ref.dtype)
        lse_ref[...] = m + jnp.log(l)

    tab = pl.BlockSpec((TQ, 128), lambda h, i: (i, 0))
    oblk = pl.BlockSpec((TQ, HEAD_DIM), lambda h, i: (i, h))
    return _call(
        body, (q, kv, kv, kr, *tabs), grid=(HEADS, nq),
        in_specs=[pl.BlockSpec((TQ, HEAD_PAD), lambda h, i: (i, h)),
                  pl.BlockSpec((s, HEAD_DIM), lambda h, i: (0, 2 * h)),
                  pl.BlockSpec((s, HEAD_DIM), lambda h, i: (0, 2 * h + 1)),
                  pl.BlockSpec((s, 128), lambda h, i: (0, 0)), tab, tab, tab],
        out_specs=[oblk, oblk, pl.BlockSpec((None, TQ, 1), lambda h, i: (h, i, 0))],
        out_shape=[jax.ShapeDtypeStruct((s, HEADS * HEAD_DIM), F32),
                   jax.ShapeDtypeStruct((s, HEADS * HEAD_DIM), MXU_DTYPE), jax.ShapeDtypeStruct((HEADS, s, 1), F32)],
        name=name, semantics=("parallel", "parallel"), comm=comm)


def _mla_attn_bwd(name, q, kv, kr, tabs, do, o, lse, comm=None):
    s = q.shape[0]
    nq = s // TQ
    scale = MLA_QK_DIM ** -0.5

    def body(q_ref, kn_ref, v_ref, kr_ref, c_ref, s1_ref, s2_ref, do_ref, o_ref, lse_ref,
             dq_ref, dkv_ref, dkr_ref, dkv_acc, dkr_acc):
        h, i = pl.program_id(0), pl.program_id(1)
        row, col = _iota2()

        @pl.when(i == 0)
        def _():
            dkv_acc[...] = jnp.zeros_like(dkv_acc)

        @pl.when((h == 0) & (i == 0))
        def _():
            dkr_acc[...] = jnp.zeros_like(dkr_acc)

        tabs_i = (c_ref[...], s1_ref[...], s2_ref[...])
        qn = q_ref[:, :HEAD_DIM].astype(MXU_DTYPE)
        qr = _rope(q_ref[:, HEAD_DIM:], *tabs_i).astype(MXU_DTYPE)
        qc = jnp.right_shift(i * TQ + row, CHUNK_SHIFT)
        do_ = do_ref[...]
        delta = jnp.sum(do_ * o_ref[...], -1, keepdims=True)
        lse_ = lse_ref[...]
        dob = do_.astype(MXU_DTYPE)

        def step(kb, carry):
            dqn, dqr = carry
            ks = pl.multiple_of(kb * TQ, TQ)
            kn = kn_ref[pl.ds(ks, TQ), :].astype(MXU_DTYPE)
            krb = kr_ref[pl.ds(ks, TQ), :].astype(MXU_DTYPE)
            v = v_ref[pl.ds(ks, TQ), :].astype(MXU_DTYPE)
            sc = (_dot(qn, kn, NT) + _dot(qr, krb, NT)) * scale
            sc = jnp.where(jnp.right_shift(ks + col, CHUNK_SHIFT) <= qc, sc, NEG)
            p = jnp.exp(sc - lse_)
            ds = (p * (_dot(dob, v, NT) - delta) * scale).astype(MXU_DTYPE)
            dkv_acc[pl.ds(ks, TQ), :HEAD_DIM] += _dot(ds, qn, TN)
            dkv_acc[pl.ds(ks, TQ), HEAD_DIM:] += _dot(p.astype(MXU_DTYPE), dob, TN)
            dkr_acc[pl.ds(ks, TQ), :] += _dot(ds, qr, TN)
            return dqn + _dot(ds, kn), dqr + _dot(ds, krb)

        dqn, dqr = lax.fori_loop(0, i + 1, step, (jnp.zeros((TQ, HEAD_DIM), F32), jnp.zeros((TQ, 128), F32)))
        dq_ref[:, :HEAD_DIM] = dqn.astype(dq_ref.dtype)
        dq_ref[:, HEAD_DIM:] = _rope_t(dqr, *tabs_i).astype(dq_ref.dtype)

        @pl.when(i == nq - 1)
        def _():
            dkv_ref[...] = dkv_acc[...].astype(dkv_ref.dtype)

        @pl.when((h == HEADS - 1) & (i == nq - 1))
        def _():
            dkr_ref[...] = dkr_acc[...]

    tab = pl.BlockSpec((TQ, 128), lambda h, i: (i, 0))
    qblk = pl.BlockSpec((TQ, HEAD_PAD), lambda h, i: (i, h))
    oblk = pl.BlockSpec((TQ, HEAD_DIM), lambda h, i: (i, h))
    return _call(
        body, (q, kv, kv, kr, *tabs, do, o, lse), grid=(HEADS, nq),
        in_specs=[qblk, pl.BlockSpec((s, HEAD_DIM), lambda h, i: (0, 2 * h)),
                  pl.BlockSpec((s, HEAD_DIM), lambda h, i: (0, 2 * h + 1)),
                  pl.BlockSpec((s, 128), lambda h, i: (0, 0)), tab, tab, tab, oblk, oblk,
                  pl.BlockSpec((None, TQ, 1), lambda h, i: (h, i, 0))],
        out_specs=[qblk, pl.BlockSpec((s, HEAD_PAD), lambda h, i: (0, h)),
                   pl.BlockSpec((s, 128), lambda h, i: (0, 0))],
        out_shape=[jax.ShapeDtypeStruct((s, HEADS * HEAD_PAD), MXU_DTYPE), jax.ShapeDtypeStruct((s, HEADS * HEAD_PAD), MXU_DTYPE),
                   jax.ShapeDtypeStruct((s, 128), F32)],
        scratch_shapes=[pltpu.VMEM((s, HEAD_PAD), F32), pltpu.VMEM((s, 128), F32)],
        name=name, semantics=("arbitrary", "arbitrary"), comm=comm)


def _qkv_specs(s):
    return [pl.BlockSpec((TQ, HEAD_DIM), lambda h, i: (i, h)),
            pl.BlockSpec((s, HEAD_DIM), lambda h, i: (0, HEADS + h)),
            pl.BlockSpec((s, HEAD_DIM), lambda h, i: (0, 2 * HEADS + h))]


def _sb_terms(z):
    sp = jnp.log(1.0 + jnp.exp(-jnp.abs(z)))
    return jnp.minimum(z, 0.0) - sp, jnp.minimum(-z, 0.0) - sp


def _sb_attn_fwd(name, qkv, comm=None):
    s = qkv.shape[0]
    nq = s // TQ
    scale = HEAD_DIM ** -0.5

    def body(q_ref, k_ref, v_ref, o_ref):
        i = pl.program_id(1)
        row, col = _iota2()
        after = (row > col).astype(BF16)
        q = q_ref[...].astype(MXU_DTYPE)
        qpos = i * TQ + row

        def step(n, carry):
            tail, acc = carry
            ks = pl.multiple_of((i - n) * TQ, TQ)
            z = _dot(q, k_ref[pl.ds(ks, TQ), :].astype(MXU_DTYPE), NT) * scale
            strict = (ks + col) < qpos
            lb, l1 = _sb_terms(z)
            l1 = jnp.where(strict, l1, 0.0)
            a = jnp.where(strict, jnp.exp(lb + tail + _exact_dot(l1, after)), 0.0)
            acc = acc + _dot(a.astype(MXU_DTYPE), v_ref[pl.ds(ks, TQ), :].astype(MXU_DTYPE))
            return tail + jnp.sum(l1, -1, keepdims=True), acc

        _, acc = lax.fori_loop(0, i + 1, step, (jnp.zeros((TQ, 1), F32), jnp.zeros((TQ, HEAD_DIM), F32)))
        o_ref[...] = acc.astype(o_ref.dtype)

    return _call(
        body, (qkv, qkv, qkv), grid=(HEADS, nq), in_specs=_qkv_specs(s),
        out_specs=[pl.BlockSpec((TQ, HEAD_DIM), lambda h, i: (i, h))],
        out_shape=[jax.ShapeDtypeStruct((s, HEADS * HEAD_DIM), MXU_DTYPE)],
        name=name, semantics=("parallel", "parallel"), comm=comm)[0]


def _sb_attn_bwd(name, qkv, do, comm=None):
    s = qkv.shape[0]
    nq = s // TQ
    scale = HEAD_DIM ** -0.5

    def body(q_ref, k_ref, v_ref, do_ref, dq_ref, dk_ref, dv_ref, a_buf, dk_acc, dv_acc):
        i = pl.program_id(1)
        row, col = _iota2()
        after = (row > col).astype(BF16)
        before = (row < col).astype(BF16)

        @pl.when(i == 0)
        def _():
            dk_acc[...] = jnp.zeros_like(dk_acc)
            dv_acc[...] = jnp.zeros_like(dv_acc)

        q = q_ref[...].astype(MXU_DTYPE)
        dob = do_ref[...].astype(MXU_DTYPE)
        qpos = i * TQ + row

        def weights(n, tail):
            kb = i - n
            ks = pl.multiple_of(kb * TQ, TQ)
            z = _dot(q, k_ref[pl.ds(ks, TQ), :].astype(MXU_DTYPE), NT) * scale
            strict = (ks + col) < qpos
            lb, l1 = _sb_terms(z)
            l1 = jnp.where(strict, l1, 0.0)
            a = jnp.where(strict, jnp.exp(lb + tail + _exact_dot(l1, after)), 0.0)
            a_buf[kb] = a
            dv_acc[pl.ds(ks, TQ), :] += _dot(a.astype(MXU_DTYPE), dob, TN)
            return tail + jnp.sum(l1, -1, keepdims=True)

        lax.fori_loop(0, i + 1, weights, jnp.zeros((TQ, 1), F32))

        def grads(kb, carry):
            head, dq = carry
            ks = pl.multiple_of(kb * TQ, TQ)
            k = k_ref[pl.ds(ks, TQ), :].astype(MXU_DTYPE)
            z = _dot(q, k, NT) * scale
            strict = (ks + col) < qpos
            e = jnp.exp(-jnp.abs(z))
            beta = jnp.where(z >= 0.0, 1.0, e) / (1.0 + e)
            w = _dot(dob, v_ref[pl.ds(ks, TQ), :].astype(MXU_DTYPE), NT) * a_buf[kb]
            dz = jnp.where(strict, w * (1.0 - beta) - beta * (head + _exact_dot(w, before)), 0.0) * scale
            dzb = dz.astype(MXU_DTYPE)
            dk_acc[pl.ds(ks, TQ), :] += _dot(dzb, q, TN)
            return head + jnp.sum(w, -1, keepdims=True), dq + _dot(dzb, k)

        _, dq = lax.fori_loop(0, i + 1, grads, (jnp.zeros((TQ, 1), F32), jnp.zeros((TQ, HEAD_DIM), F32)))
        dq_ref[...] = dq.astype(dq_ref.dtype)

        @pl.when(i == nq - 1)
        def _():
            dk_ref[...] = dk_acc[...].astype(dk_ref.dtype)
            dv_ref[...] = dv_acc[...].astype(dv_ref.dtype)

    blk = pl.BlockSpec((TQ, HEAD_DIM), lambda h, i: (i, h))
    col_h = pl.BlockSpec((s, HEAD_DIM), lambda h, i: (0, h))
    shp = jax.ShapeDtypeStruct((s, HEADS * HEAD_DIM), MXU_DTYPE)
    return _call(
        body, (qkv, qkv, qkv, do), grid=(HEADS, nq), in_specs=_qkv_specs(s) + [blk],
        out_specs=[blk, col_h, col_h], out_shape=[shp, shp, shp],
        scratch_shapes=[pltpu.VMEM((nq, TQ, TQ), F32), pltpu.VMEM((s, HEAD_DIM), F32), pltpu.VMEM((s, HEAD_DIM), F32)],
        name=name, semantics=("arbitrary", "arbitrary"), comm=comm)


LANES = 128
LAST_REL = (1 << CHUNK_SHIFT) - 1


def _ca_subtiles():
    nb = TQ // LANES
    return [(d, a, b, -d * TQ + (b - a) * LANES) for d in range(3) for a in range(nb) for b in range(nb)]


def _ca_bias_tiles(rel_bias):
    table = jnp.pad(jnp.transpose(rel_bias), ((0, 0), (0, 2 * LANES - REL_TABLE)))[:, None]

    def unskew(x, row):
        for b in range(7):
            x = jnp.where((jnp.right_shift(row, b) & 1) == 1, pltpu.roll(x, 1 << b, 1), x)
        return x

    def body(t_ref, o_ref):
        row = lax.broadcasted_iota(jnp.int32, (LANES, LANES), 0)
        col = lax.broadcasted_iota(jnp.int32, (LANES, LANES), 1)
        lane = col[:1]
        lo, hi = t_ref[:, :LANES], t_ref[:, LANES:]
        first = jnp.sum(jnp.where(lane == 0, lo, 0.0), -1, keepdims=True)
        last = jnp.sum(jnp.where(lane == LAST_REL, hi, 0.0), -1, keepdims=True)
        hi = jnp.where(lane <= LAST_REL, hi, last)
        r_lo = unskew(jnp.broadcast_to(lo, (LANES, LANES)), row)
        r_hi = unskew(jnp.broadcast_to(hi, (LANES, LANES)), row)
        upper = col >= row
        for d, a, b, o in _ca_subtiles():
            if o >= LANES:
                piece = jnp.broadcast_to(last, (LANES, LANES))
            elif o == 0:
                piece = jnp.where(upper, r_hi, r_lo)
            elif o == -LANES:
                piece = jnp.where(upper, r_lo, first)
            else:
                piece = jnp.broadcast_to(first, (LANES, LANES))
            o_ref[d, a * LANES:(a + 1) * LANES, b * LANES:(b + 1) * LANES] = piece

    return pl.pallas_call(
        body, grid=(HEADS,), in_specs=[pl.BlockSpec((None, 1, 2 * LANES), lambda h: (h, 0, 0))],
        out_specs=pl.BlockSpec((None, 3, TQ, TQ), lambda h: (h, 0, 0, 0)),
        out_shape=jax.ShapeDtypeStruct((HEADS, 3, TQ, TQ), F32),
        name="ca_bias_tiles", compiler_params=_params("parallel"))(table)


def _ca_mask(i, ks, row, col):
    qc = jnp.right_shift(i * TQ + row, CHUNK_SHIFT)
    kc = jnp.right_shift(ks + col, CHUNK_SHIFT)
    return (kc <= qc) & (kc >= qc - CA_LEFT_CHUNKS)


def _ca_attn_fwd(name, qkv, tiles, comm=None):
    s = qkv.shape[0]
    nq = s // TQ
    scale = HEAD_DIM ** -0.5

    def body(q_ref, k_ref, v_ref, bt_ref, o_ref, omx_ref, lse_ref):
        i = pl.program_id(1)
        row, col = _iota2()
        q = q_ref[...].astype(MXU_DTYPE)

        def step(kb, carry):
            m, l, acc = carry
            ks = pl.multiple_of(kb * TQ, TQ)
            sc = _dot(q, k_ref[pl.ds(ks, TQ), :].astype(MXU_DTYPE), NT) * scale + bt_ref[jnp.minimum(i - kb, 2)]
            sc = jnp.where(_ca_mask(i, ks, row, col), sc, NEG)
            m_new = jnp.maximum(m, jnp.max(sc, -1, keepdims=True))
            p = jnp.exp(sc - m_new)
            corr = jnp.exp(m - m_new)
            l = corr * l + jnp.sum(p, -1, keepdims=True)
            acc = corr * acc + _dot(p.astype(MXU_DTYPE), v_ref[pl.ds(ks, TQ), :].astype(MXU_DTYPE))
            return m_new, l, acc

        m, l, acc = lax.fori_loop(jnp.maximum(i - CA_LEFT_BLOCKS, 0), i + 1, step,
                                  (jnp.full((TQ, 1), NEG, F32), jnp.zeros((TQ, 1), F32),
                                   jnp.zeros((TQ, HEAD_DIM), F32)))
        o = acc / l
        o_ref[...] = o
        omx_ref[...] = o.astype(omx_ref.dtype)
        lse_ref[...] = m + jnp.log(l)

    oblk = pl.BlockSpec((TQ, HEAD_DIM), lambda h, i: (i, h))
    return _call(
        body, (qkv, qkv, qkv, tiles), grid=(HEADS, nq),
        in_specs=_qkv_specs(s) + [pl.BlockSpec((None, 3, TQ, TQ), lambda h, i: (h, 0, 0, 0))],
        out_specs=[oblk, oblk, pl.BlockSpec((None, TQ, 1), lambda h, i: (h, i, 0))],
        out_shape=[jax.ShapeDtypeStruct((s, HEADS * HEAD_DIM), F32),
                   jax.ShapeDtypeStruct((s, HEADS * HEAD_DIM), MXU_DTYPE), jax.ShapeDtypeStruct((HEADS, s, 1), F32)],
        name=name, semantics=("parallel", "parallel"), comm=comm)


def _ca_attn_bwd(name, qkv, tiles, do, o, lse, comm=None):
    s = qkv.shape[0]
    nq = s // TQ
    scale = HEAD_DIM ** -0.5

    def body(q_ref, k_ref, v_ref, bt_ref, do_ref, o_ref, lse_ref, dq_ref, dk_ref, dv_ref, dbt_ref, dk_acc, dv_acc):
        i = pl.program_id(1)
        row, col = _iota2()

        @pl.when(i == 0)
        def _():
            dk_acc[...] = jnp.zeros_like(dk_acc)
            dv_acc[...] = jnp.zeros_like(dv_acc)
            dbt_ref[...] = jnp.zeros_like(dbt_ref)

        q = q_ref[...].astype(MXU_DTYPE)
        do_ = do_ref[...]
        delta = jnp.sum(do_ * o_ref[...], -1, keepdims=True)
        lse_ = lse_ref[...]
        dob = do_.astype(MXU_DTYPE)

        def step(kb, dq):
            ks = pl.multiple_of(kb * TQ, TQ)
            k = k_ref[pl.ds(ks, TQ), :].astype(MXU_DTYPE)
            v = v_ref[pl.ds(ks, TQ), :].astype(MXU_DTYPE)
            t = jnp.minimum(i - kb, 2)
            sc = _dot(q, k, NT) * scale + bt_ref[t]
            sc = jnp.where(_ca_mask(i, ks, row, col), sc, NEG)
            p = jnp.exp(sc - lse_)
            dsc = p * (_dot(dob, v, NT) - delta)
            dbt_ref[t] += dsc
            ds = (dsc * scale).astype(MXU_DTYPE)
            dk_acc[pl.ds(ks, TQ), :] += _dot(ds, q, TN)
            dv_acc[pl.ds(ks, TQ), :] += _dot(p.astype(MXU_DTYPE), dob, TN)
            return dq + _dot(ds, k)

        dq = lax.fori_loop(jnp.maximum(i - CA_LEFT_BLOCKS, 0), i + 1, step, jnp.zeros((TQ, HEAD_DIM), F32))
        dq_ref[...] = dq.astype(dq_ref.dtype)

        @pl.when(i == nq - 1)
        def _():
            dk_ref[...] = dk_acc[...].astype(dk_ref.dtype)
            dv_ref[...] = dv_acc[...].astype(dv_ref.dtype)

    blk = pl.BlockSpec((TQ, HEAD_DIM), lambda h, i: (i, h))
    col_h = pl.BlockSpec((s, HEAD_DIM), lambda h, i: (0, h))
    tile = pl.BlockSpec((None, 3, TQ, TQ), lambda h, i: (h, 0, 0, 0))
    shp = jax.ShapeDtypeStruct((s, HEADS * HEAD_DIM), MXU_DTYPE)
    return _call(
        body, (qkv, qkv, qkv, tiles, do, o, lse), grid=(HEADS, nq),
        in_specs=_qkv_specs(s) + [tile, blk, blk, pl.BlockSpec((None, TQ, 1), lambda h, i: (h, i, 0))],
        out_specs=[blk, col_h, col_h, tile],
        out_shape=[shp, shp, shp, jax.ShapeDtypeStruct((HEADS, 3, TQ, TQ), F32)],
        scratch_shapes=[pltpu.VMEM((s, HEAD_DIM), F32), pltpu.VMEM((s, HEAD_DIM), F32)],
        name=name, semantics=("arbitrary", "arbitrary"), comm=comm)


def _ca_table_grad(dtiles):
    def skew(x, row):
        for b in range(7):
            x = jnp.where((jnp.right_shift(row, b) & 1) == 1, pltpu.roll(x, LANES - (1 << b), 1), x)
        return x

    def body(t_ref, o_ref):
        row = lax.broadcasted_iota(jnp.int32, (LANES, LANES), 0)
        col = lax.broadcasted_iota(jnp.int32, (LANES, LANES), 1)
        wrapped = (row + col) >= LANES
        lane = col[:1]
        total = lambda x: jnp.sum(jnp.sum(x, 0, keepdims=True), -1, keepdims=True)
        lo = jnp.zeros((1, LANES), F32)
        hi = jnp.zeros((1, LANES), F32)
        for d, a, b, o in _ca_subtiles():
            x = t_ref[d, a * LANES:(a + 1) * LANES, b * LANES:(b + 1) * LANES]
            if o >= LANES:
                hi = hi + jnp.where(lane == LAST_REL, total(x), 0.0)
            elif o <= -2 * LANES:
                lo = lo + jnp.where(lane == 0, total(x), 0.0)
            else:
                y = skew(x, row)
                pos = jnp.sum(jnp.where(wrapped, 0.0, y), 0, keepdims=True)
                neg = jnp.sum(jnp.where(wrapped, y, 0.0), 0, keepdims=True)
                if o == 0:
                    clipped = jnp.sum(jnp.where(lane > LAST_REL, pos, 0.0), -1, keepdims=True)
                    hi = hi + jnp.where(lane <= LAST_REL, pos, 0.0) + jnp.where(lane == LAST_REL, clipped, 0.0)
                    lo = lo + neg
                else:
                    lo = lo + pos + jnp.where(lane == 0, jnp.sum(neg, -1, keepdims=True), 0.0)
        o_ref[:, :LANES] = lo
        o_ref[:, LANES:] = hi

    return pl.pallas_call(
        body, grid=(HEADS,), in_specs=[pl.BlockSpec((None, 3, TQ, TQ), lambda h: (h, 0, 0, 0))],
        out_specs=pl.BlockSpec((None, 1, 2 * LANES), lambda h: (h, 0, 0)),
        out_shape=jax.ShapeDtypeStruct((HEADS, 1, 2 * LANES), F32),
        name="ca_table_grad", compiler_params=_params("parallel"))(dtiles)


def _row_tile(rows, cols):
    if rows % 128:
        return rows
    tr = 128
    while rows % (2 * tr) == 0 and 2 * tr * cols * 4 <= (1 << 20):
        tr *= 2
    return tr


def _adamw(name, w, g, m, v):
    rows, cols = w.shape
    tr = _row_tile(rows, cols)
    blk = pl.BlockSpec((tr, cols), lambda i: (i, 0))

    def body(w_ref, g_ref, m_ref, v_ref, d_ref, m2_ref, v2_ref):
        g_ = g_ref[...]
        m2 = ADAM_B1 * m_ref[...] + (1.0 - ADAM_B1) * g_
        v2 = ADAM_B2 * v_ref[...] + (1.0 - ADAM_B2) * jnp.square(g_)
        m_hat = m2 / (1.0 - ADAM_B1 ** ADAM_STEP)
        v_hat = v2 / (1.0 - ADAM_B2 ** ADAM_STEP)
        d_ref[...] = -ADAM_LR * (m_hat / (jnp.sqrt(v_hat) + ADAM_EPS) + ADAM_WD * w_ref[...])
        m2_ref[...] = m2
        v2_ref[...] = v2

    shp = jax.ShapeDtypeStruct((rows, cols), F32)
    return pl.pallas_call(body, grid=(rows // tr,), in_specs=[blk] * 4, out_specs=[blk] * 3, out_shape=[shp] * 3,
                          name=name, compiler_params=_params("parallel"))(w, g, m, v)


def _pair_sum(name, g, r1, c):
    _, rh, cols = r1.shape
    tr = _row_tile(rh, cols)
    nb = rh // tr

    def body(c_ref, g_ref, r_ref, o_ref):
        o_ref[...] = (g_ref[...].astype(F32) + r_ref[...].astype(F32)).astype(o_ref.dtype)

    return pl.pallas_call(
        body, grid_spec=pltpu.PrefetchScalarGridSpec(
            num_scalar_prefetch=1, grid=(N_CHIPS, nb),
            in_specs=[pl.BlockSpec((None, tr, cols), lambda k, i, c_ref: (k, c_ref[0] * nb + i, 0)),
                      pl.BlockSpec((None, tr, cols), lambda k, i, c_ref: (k, i, 0))],
            out_specs=pl.BlockSpec((None, tr, cols), lambda k, i, c_ref: (k, i, 0))),
        out_shape=jax.ShapeDtypeStruct(r1.shape, WIRE_DTYPE), name=name,
        compiler_params=_params("parallel", "parallel"))(c, g, r1)


def _chip_sum(name, a1, r2, place, total, l, layers):
    _, rh, cols = a1.shape
    tr = _row_tile(rh, cols)
    nb = rh // tr

    def body(p_ref, a_ref, r_ref, *rest):
        rest[-1][...] = ((a_ref[...].astype(F32) + r_ref[0].astype(F32)) + r_ref[1].astype(F32)) + r_ref[2].astype(F32)

    return pl.pallas_call(
        body, grid_spec=pltpu.PrefetchScalarGridSpec(
            num_scalar_prefetch=1, grid=(nb,),
            in_specs=[pl.BlockSpec((None, tr, cols), lambda i, p_ref: (p_ref[0], i, 0)),
                      pl.BlockSpec((N_CHIPS - 1, tr, cols), lambda i, p_ref: (0, i, 0))] + ([] if total is None else [ANY]),
            out_specs=pl.BlockSpec((tr, cols), lambda i, p_ref: ((2 * l + p_ref[1]) * nb + i, 0))),
        out_shape=jax.ShapeDtypeStruct((layers * 2 * rh, cols), F32), name=name,
        input_output_aliases={} if total is None else {3: 0},
        compiler_params=_params("parallel"))(place, a1, r2, *([] if total is None else [total]))


def _cast_place(name, w, chip, l, rows):
    cols = w.shape[1]
    tr = _row_tile(rows, cols)
    nb = rows // tr

    def body(c_ref, w_ref, o_ref):
        o_ref[...] = w_ref[...].astype(o_ref.dtype)

    return pl.pallas_call(
        body, grid_spec=pltpu.PrefetchScalarGridSpec(
            num_scalar_prefetch=1, grid=(nb,),
            in_specs=[pl.BlockSpec((tr, cols), lambda i, c_ref: (l * nb + i, 0))],
            out_specs=pl.BlockSpec((None, tr, cols), lambda i, c_ref: (c_ref[0], i, 0))),
        out_shape=jax.ShapeDtypeStruct((N_CHIPS, rows, cols), MXU_DTYPE), name=name,
        compiler_params=_params("parallel"))(chip, w)


def _place():
    x, y, c = lax.axis_index("x"), lax.axis_index("y"), lax.axis_index("c")
    chips = [(1 - x, y), (x, 1 - y), (1 - x, 1 - y)]
    return x, y, c, chips


def _remote(src, dst, send_sem, recv_sem, to):
    return pltpu.make_async_remote_copy(src_ref=src, dst_ref=dst, send_sem=send_sem, recv_sem=recv_sem,
                                        device_id=to, device_id_type=MESH)


def _all_gather(placed):
    n = len(placed)

    def body(*refs):
        outs = refs[n:2 * n]
        send, recv = refs[2 * n:]
        x, y, c, chips = _place()
        me, sib = 2 * x + y, (x, y, 1 - c)

        def half(t, which):
            return pl.ds(which * (outs[t].shape[1] // 2), outs[t].shape[1] // 2)

        first, passed = [], []
        for t in range(n):
            for j, chip in enumerate(chips):
                mine = outs[t].at[me, half(t, c)]
                cp = _remote(mine, mine, send.at[6 * t + j], recv.at[6 * t + j], (*chip, c))
                cp.start()
                first.append(cp)
        for t in range(n):
            for j, (px, py) in enumerate(chips):
                got = outs[t].at[2 * px + py, half(t, c)]
                _remote(got, got, send.at[6 * t + j], recv.at[6 * t + j], (px, py, c)).wait_recv()
                cp = _remote(got, got, send.at[6 * t + 3 + j], recv.at[6 * t + 3 + j], sib)
                cp.start()
                passed.append(cp)
        for t in range(n):
            for j, (px, py) in enumerate(chips):
                got = outs[t].at[2 * px + py, half(t, 1 - c)]
                _remote(got, got, send.at[6 * t + 3 + j], recv.at[6 * t + 3 + j], sib).wait_recv()
        for cp in first + passed:
            cp.wait_send()

    return pl.pallas_call(
        body, in_specs=[ANY] * n, out_specs=[ANY] * n,
        out_shape=[jax.ShapeDtypeStruct(p.shape, p.dtype) for p in placed],
        scratch_shapes=[pltpu.SemaphoreType.DMA((6 * n,)), pltpu.SemaphoreType.DMA((6 * n,))],
        input_output_aliases={t: t for t in range(n)},
        name="weight_all_gather")(*placed)


def _in_place(bufs):
    return [jax.ShapeDtypeStruct(b.shape, b.dtype) for b in bufs], {t: t for t in range(len(bufs))}


def _gather_ici(bufs):
    def copies(srcs, dsts, send, recv):
        x, y, c, chips = _place()
        out = []
        for t, buf in enumerate(dsts):
            rh = buf.shape[1] // 2
            mine = buf.at[2 * x + y, pl.ds(c * rh, rh)]
            out += [_remote(mine, mine, send.at[3 * t + j], recv.at[3 * t + j], (*chip, c))
                    for j, chip in enumerate(chips)]
        return out

    return _Comm(bufs, *_in_place(bufs), copies, 3 * len(bufs))


def _gather_pass(bufs):
    def copies(srcs, dsts, send, recv):
        x, y, c, chips = _place()
        out = []
        for t, buf in enumerate(dsts):
            rh = buf.shape[1] // 2
            for j, (px, py) in enumerate(chips):
                got = buf.at[2 * px + py, pl.ds(c * rh, rh)]
                out.append(_remote(got, got, send.at[3 * t + j], recv.at[3 * t + j], (x, y, 1 - c)))
        return out

    return _Comm(bufs, *_in_place(bufs), copies, 3 * len(bufs))


def _reduce_pair(grads):
    def copies(srcs, dsts, send, recv):
        x, y, c, _ = _place()
        out = []
        for t, g in enumerate(srcs):
            rh = g.shape[1] // 2
            out.append(_remote(g.at[:, pl.ds((1 - c) * rh, rh)], dsts[t], send.at[t], recv.at[t], (x, y, 1 - c)))
        return out

    return _Comm(grads, [jax.ShapeDtypeStruct((N_CHIPS, g.shape[1] // 2, g.shape[2]), g.dtype) for g in grads], {},
                 copies, len(grads))


def _reduce_chips(parts):
    def copies(srcs, dsts, send, recv):
        x, y, c, chips = _place()
        return [_remote(p.at[2 * px + py], dsts[t].at[j], send.at[3 * t + j], recv.at[3 * t + j], (px, py, c))
                for t, p in enumerate(srcs) for j, (px, py) in enumerate(chips)]

    return _Comm(parts, [jax.ShapeDtypeStruct((N_CHIPS - 1, *p.shape[1:]), p.dtype) for p in parts], {}, copies,
                 3 * len(parts))


def _reduce_share(totals, spans):
    def copies(srcs, dsts, send, recv):
        x, y, c, _ = _place()
        out = []
        for t, (buf, (start, rh)) in enumerate(zip(dsts, spans)):
            mine = buf.at[pl.ds(start + c * rh, rh)]
            out.append(_remote(mine, mine, send.at[t], recv.at[t], (x, y, 1 - c)))
        return out

    return _Comm(totals, *_in_place(totals), copies, len(totals))


def _exchange(name, comm):
    ns, nd = len(comm.srcs), len(comm.dsts)

    def body(*refs):
        send, recv = refs[ns + nd:]
        cps = comm.copies(refs[:ns], refs[ns:ns + nd], send, recv)
        for cp in cps:
            cp.start()
        for cp in cps:
            cp.wait()

    comm.deliver(pl.pallas_call(
        body, in_specs=[ANY] * ns, out_specs=[ANY] * nd, out_shape=comm.dsts,
        scratch_shapes=[pltpu.SemaphoreType.DMA((comm.n,)), pltpu.SemaphoreType.DMA((comm.n,))],
        input_output_aliases=comm.alias, name=name)(*comm.srcs))
    return comm.out


def _all_reduce_small(name, pack):
    rows, cols = pack.shape

    def body(p_ref, o_ref, slots, send, recv):
        x, y, c, _ = _place()
        me = 4 * x + 2 * y + c
        slots[me] = p_ref[...]
        cps = []
        for r in range(1, N_DEV):
            to = ((1 - x) if r & 4 else x, (1 - y) if r & 2 else y, (1 - c) if r & 1 else c)
            cp = _remote(p_ref, slots.at[me], send.at[r - 1], recv.at[r - 1], to)
            cp.start()
            cps.append(cp)
        for cp in cps:
            cp.wait()
        acc = slots[0]
        for d in range(1, N_DEV):
            acc = acc + slots[d]
        o_ref[...] = acc

    return pl.pallas_call(
        body, in_specs=[VMEM_SPEC], out_specs=VMEM_SPEC, out_shape=jax.ShapeDtypeStruct((rows, cols), F32),
        scratch_shapes=[pltpu.VMEM((N_DEV, rows, cols), F32), pltpu.SemaphoreType.DMA((N_DEV - 1,)),
                        pltpu.SemaphoreType.DMA((N_DEV - 1,))],
        name=name)(pack)


BIG = ("ffn_w_in", "ffn_w_out", "mla_w_down", "mla_w_uq", "mla_w_ukv", "mla_w_o", "sb_w_qkv", "sb_w_o", "ca_w_qkv",
       "ca_w_o")
COL_SHARDED = {"ffn_w_in": True, "ffn_w_out": False, "mla_w_down": False, "mla_w_uq": True, "mla_w_ukv": True,
               "mla_w_o": False, "sb_w_qkv": True, "sb_w_o": False, "ca_w_qkv": True, "ca_w_o": False}
WEIGHTS = ("ln_mix_g", "ln_mix_b", "ln_ffn_g", "ln_ffn_b", "ffn_w_in", "ffn_w_out", "mla_w_down", "mla_q_norm_g",
           "mla_w_uq", "mla_kv_norm_g", "mla_w_ukv", "mla_w_o", "sb_w_qkv", "sb_w_o", "ca_w_qkv", "ca_rel_bias",
           "ca_w_o")
HEADS_PER_CHIP = HEADS // N_CHIPS


def _mxu_shards(w):
    down = w["mla_w_down"]
    uq = w["mla_w_uq"]
    n, ql = uq.shape[:2]
    lane_pad = 128 - MLA_ROPE
    shaped = dict(w)
    shaped["mla_w_down"] = jnp.pad(down, ((0, 0), (0, 0), (0, lane_pad)))
    shaped["mla_w_uq"] = jnp.pad(uq.reshape(n, ql, HEADS_PER_CHIP, MLA_QK_DIM),
                                 ((0, 0), (0, 0), (0, 0), (0, HEAD_PAD - MLA_QK_DIM))).reshape(n, ql, -1)
    return {k: shaped[k].reshape(-1, shaped[k].shape[-1]) for k in BIG}


def _unpad_grad(name, g, like):
    if name == "mla_w_down":
        g = g[:, :like.shape[-1]]
    elif name == "mla_w_uq":
        g = g.reshape(g.shape[0], HEADS_PER_CHIP, HEAD_PAD)[:, :, :MLA_QK_DIM]
    return g.reshape(like.shape)


def _mixer_keys(i):
    kind, slot = i % 3, i // 3
    if kind == 0:
        return [(k, slot) for k in ("mla_w_down", "mla_w_uq", "mla_w_ukv", "mla_w_o")]
    pre = "sb" if kind == 1 else "ca"
    return [(f"{pre}_w_qkv", slot), (f"{pre}_w_o", slot)]


def _ffn_keys(i):
    return [("ffn_w_in", i), ("ffn_w_out", i)]


def _step(x, target, wl, dims, p, tabs, place):
    dist = place is not None
    wl = dict(wl)

    def carried(make, keys):
        return make([wl[k] for k in keys]) if dist and keys else None

    def landed(keys, comm):
        if comm is not None:
            wl.update(zip(keys, comm.out))

    if dist:
        first = _mixer_keys(0)
        wl.update(zip(first, _all_gather([wl[k] for k in first])))
    saved = []
    h = hm = x
    for i in range(DEPTH):
        kind, slot = i % 3, i // 3
        sv = {"h0m": hm}
        ffn, nxt = _ffn_keys(i), (_mixer_keys(i + 1) if i + 1 < DEPTH else [])
        ahead = ffn[:1] if kind == 2 else []
        rest = ffn[len(ahead):]
        ici0 = carried(_gather_ici, ahead)
        if kind == 0:
            ici, pass0 = carried(_gather_ici, rest), None
            down =_fwd_row(f"mla_down_{i}", hm, wl["mla_w_down", slot], F32)
            gq, gkv = p["mla_q_norm_g"][slot][None], p["mla_kv_norm_g"][slot][None]
            cq, ckv, kr = _mla_mid_fwd(down, gq, gkv, tabs)
            q = _fwd_col(f"mla_uq_{i}", cq, wl["mla_w_uq", slot], F32)
            kv = _fwd_col(f"mla_ukv_{i}", ckv, wl["mla_w_ukv", slot], MXU_DTYPE)
            o, om, lse = _mla_attn_fwd(f"mla_attn_fwd_{i}", q, kv, kr, tabs, comm=_merged(ici, pass0))
            sv.update(down=down, gq=gq, gkv=gkv, cq=cq, ckv=ckv, kr=kr, q=q, kv=kv, o=o, lse=lse)
        else:
            qkv = _fwd_col(f"{'sb' if kind == 1 else 'ca'}_qkv_{i}", hm, wl[_mixer_keys(i)[0]], MXU_DTYPE, comm=ici0)
            landed(ahead, ici0)
            ici, pass0 = carried(_gather_ici, rest), carried(_gather_pass, ahead)
            if kind == 1:
                om = _sb_attn_fwd(f"sb_attn_fwd_{i}", qkv, comm=_merged(ici, pass0))
            else:
                tiles = _ca_bias_tiles(p["ca_rel_bias"][slot])
                o, om, lse = _ca_attn_fwd(f"ca_attn_fwd_{i}", qkv, tiles, comm=_merged(ici, pass0))
                sv.update(tiles=tiles, lse=lse, o=o)
            sv.update(qkv=qkv)
        landed(rest, ici)
        landed(ahead, pass0)
        fwd = carried(_gather_pass, rest)
        m = _fwd_row(f"mixer_o_{i}", om, wl[_mixer_keys(i)[-1]], F32, comm=fwd)
        landed(rest, fwd)
        h1, h1m, sv["xh1"], sv["r1"] = _ln_fwd(f"ln_mix_{i}", h, m, p["ln_mix_g"][i][None], p["ln_mix_b"][i][None])
        ici = carried(_gather_ici, nxt)
        u, a = _fwd_col(f"ffn_in_{i}", h1m, wl["ffn_w_in", i], None, epilogue=_epi_relu2, dtypes=(F32, MXU_DTYPE),
                        comm=ici)
        landed(nxt, ici)
        fwd = carried(_gather_pass, nxt)
        y = _fwd_row(f"ffn_out_{i}", a, wl["ffn_w_out", i], F32, comm=fwd)
        landed(nxt, fwd)
        h, hm, sv["xh2"], sv["r2"] = _ln_fwd(f"ln_ffn_{i}", h1, y, p["ln_ffn_g"][i][None], p["ln_ffn_b"][i][None])
        sv.update(om=om, h1m=h1m, u=u, a=a)
        saved.append(sv)

    part, pair, total = {}, {}, {}

    def dw(key, xin, dyin):
        _, rows, cols = dims[key[0]]
        part[key] = _dw(f"dw_{key[0]}_{key[1]}", xin, dyin, rows, cols, COL_SHARDED[key[0]])

    def to_sibling(keys):
        return _reduce_pair([part[k] for k in keys]) if dist and keys else None

    def pair_sums(keys, comm):
        for k, r1 in zip(keys, comm.out if comm is not None else ()):
            pair[k] = _pair_sum(f"pair_sum_{k[0]}_{k[1]}", part[k], r1, place[1:])

    def to_chips(keys):
        return _reduce_chips([pair[k] for k in keys]) if dist and keys else None

    def chip_sums(keys, comm):
        for (name, l), r2 in zip(keys, comm.out if comm is not None else ()):
            total[name] = _chip_sum(f"chip_sum_{name}_{l}", pair[name, l], r2, place, total.get(name), l,
                                    dims[name][0])

    def share(keys):
        spans = [(l * dims[name][1], dims[name][1] // 2) for name, l in keys]
        return _reduce_share([total[name] for name, _ in keys], spans) if dist and keys else None

    def shared(keys, comm):
        if comm is not None:
            total.update(zip([name for name, _ in keys], comm.out))

    loss, dy = _loss_head(h, target)
    small = {k: [None] * DEPTH for k in ("ln_mix_g", "ln_mix_b", "ln_ffn_g", "ln_ffn_b")}
    n_mla = p["mla_q_norm_g"].shape[0]
    small["mla_q_norm_g"], small["mla_kv_norm_g"] = [None] * n_mla, [None] * n_mla
    for i in reversed(range(DEPTH)):
        kind, slot = i % 3, i // 3
        sv = saved[i]
        later = _mixer_keys(i + 1) if i + 1 < DEPTH else []
        dz, dzm, small["ln_ffn_g"][i], small["ln_ffn_b"][i] = _ln_bwd(f"ln_ffn_bwd_{i}", dy, sv["xh2"], sv["r2"],
                                                                      p["ln_ffn_g"][i][None])
        dw(("ffn_w_out", i), sv["a"], dzm)
        early = [("ffn_w_out", i)] + later
        sib = to_sibling(early)
        du = _dx_row(f"ffn_du_{i}", dzm, wl["ffn_w_out", i], MXU_DTYPE, epilogue=_epi_drelu2, extra=sv["u"], comm=sib)
        pair_sums(early, sib)
        dw(("ffn_w_in", i), sv["h1m"], du)
        sib = to_sibling([("ffn_w_in", i)])
        sooner = later if kind == 2 else []
        ici0 = to_chips(sooner)
        dy = _dx_col(f"ffn_dh_{i}", du, wl["ffn_w_in", i], F32, epilogue=_epi_residual, extra=dz,
                     comm=_merged(sib, ici0))
        pair_sums([("ffn_w_in", i)], sib)
        chip_sums(sooner, ici0)
        ready = early + [("ffn_w_in", i)]
        behind = [k for k in ready if k not in sooner]
        dz, dzm, small["ln_mix_g"][i], small["ln_mix_b"][i] = _ln_bwd(f"ln_mix_bwd_{i}", dy, sv["xh1"], sv["r1"],
                                                                      p["ln_mix_g"][i][None])
        ici = to_chips(behind)
        dw(_mixer_keys(i)[-1], sv["om"], dzm)
        do = _dx_row(f"mixer_do_{i}", dzm, wl[_mixer_keys(i)[-1]], F32)
        if kind == 0:
            dq, dkv, dkr = _mla_attn_bwd(f"mla_attn_bwd_{i}", sv["q"], sv["kv"], sv["kr"], tabs, do, sv["o"],
                                         sv["lse"], comm=ici)
            chip_sums(behind, ici)
            dw(("mla_w_uq", slot), sv["cq"], dq)
            dcq = _dx_col(f"mla_dcq_{i}", dq, wl["mla_w_uq", slot], F32)
            dw(("mla_w_ukv", slot), sv["ckv"], dkv)
            dckv = _dx_col(f"mla_dckv_{i}", dkv, wl["mla_w_ukv", slot], F32)
            ddown, small["mla_q_norm_g"][slot], small["mla_kv_norm_g"][slot] = _mla_mid_bwd(
                sv["down"], dcq, dckv, dkr, sv["gq"], sv["gkv"], tabs)
            dw(("mla_w_down", slot), sv["h0m"], ddown)
            both = share(ready)
            dy = _dx_row(f"mla_dh_{i}", ddown, wl["mla_w_down", slot], F32, epilogue=_epi_residual, extra=dz,
                         comm=both)
        else:
            pre = "sb" if kind == 1 else "ca"
            if kind == 1:
                dq, dk, dv = _sb_attn_bwd(f"sb_attn_bwd_{i}", sv["qkv"], do, comm=ici)
            else:
                dq, dk, dv, dtiles = _ca_attn_bwd(f"ca_attn_bwd_{i}", sv["qkv"], sv["tiles"], do, sv["o"], sv["lse"],
                                                  comm=ici)
                small["ca_rel_bias"] = [jnp.transpose(_ca_table_grad(dtiles)[:, 0, :REL_TABLE])]
            chip_sums(behind, ici)
            dqkv = jnp.concatenate([dq, dk, dv], 1)
            dw((f"{pre}_w_qkv", slot), sv["h0m"], dqkv)
            both = share(ready)
            dy = _dx_col(f"{pre}_dh_{i}", dqkv, wl[f"{pre}_w_qkv", slot], F32, epilogue=_epi_residual, extra=dz,
                         comm=both)
        shared(ready, both)
    small = {k: jnp.stack([g.reshape(g.shape[-2:]) if k == "ca_rel_bias" else g[0] for g in v]) for k, v in small.items()}
    if not dist:
        return loss, dy, part, small
    last = _mixer_keys(0)
    sib = to_sibling(last)
    _exchange("grad_pair_last", sib)
    pair_sums(last, sib)
    ici = to_chips(last)
    _exchange("grad_chips_last", ici)
    chip_sums(last, ici)
    both = share(last)
    _exchange("grad_share_last", both)
    shared(last, both)
    return loss, dy, total, small


SMALL = ("ln_mix_g", "ln_mix_b", "ln_ffn_g", "ln_ffn_b", "mla_q_norm_g", "mla_kv_norm_g", "ca_rel_bias")


def _pack_small(parts, width):
    flat = jnp.concatenate([parts[k].reshape(-1) for k in SMALL])
    rows = -(-flat.shape[0] // width)
    rows += -rows % 8
    return jnp.pad(flat, (0, rows * width - flat.shape[0])).reshape(rows, width)


def _unpack_small(pack, like):
    flat, out, at = pack.reshape(-1), {}, 0
    for k in SMALL:
        n = int(np.prod(like[k].shape))
        out[k] = flat[at:at + n].reshape(like[k].shape)
        at += n
    return out


def kernel(x, ln_mix_g, ln_mix_b, ln_ffn_g, ln_ffn_b, ffn_w_in, ffn_w_out, mla_w_down, mla_q_norm_g, mla_w_uq, mla_kv_norm_g, mla_w_ukv, mla_w_o, sb_w_qkv, sb_w_o, ca_w_qkv, ca_rel_bias, ca_w_o, loss_target, m_ln_mix_g, m_ln_mix_b, m_ln_ffn_g, m_ln_ffn_b, m_ffn_w_in, m_ffn_w_out, m_mla_w_down, m_mla_q_norm_g, m_mla_w_uq, m_mla_kv_norm_g, m_mla_w_ukv, m_mla_w_o, m_sb_w_qkv, m_sb_w_o, m_ca_w_qkv, m_ca_rel_bias, m_ca_w_o, v_ln_mix_g, v_ln_mix_b, v_ln_ffn_g, v_ln_ffn_b, v_ffn_w_in, v_ffn_w_out, v_mla_w_down, v_mla_q_norm_g, v_mla_w_uq, v_mla_kv_norm_g, v_mla_w_ukv, v_mla_w_o, v_sb_w_qkv, v_sb_w_o, v_ca_w_qkv, v_ca_rel_bias, v_ca_w_o):
    w = dict(zip(WEIGHTS, (ln_mix_g, ln_mix_b, ln_ffn_g, ln_ffn_b, ffn_w_in, ffn_w_out, mla_w_down, mla_q_norm_g,
                           mla_w_uq, mla_kv_norm_g, mla_w_ukv, mla_w_o, sb_w_qkv, sb_w_o, ca_w_qkv, ca_rel_bias,
                           ca_w_o)))
    mom1 = dict(zip(WEIGHTS, (m_ln_mix_g, m_ln_mix_b, m_ln_ffn_g, m_ln_ffn_b, m_ffn_w_in, m_ffn_w_out, m_mla_w_down,
                              m_mla_q_norm_g, m_mla_w_uq, m_mla_kv_norm_g, m_mla_w_ukv, m_mla_w_o, m_sb_w_qkv,
                              m_sb_w_o, m_ca_w_qkv, m_ca_rel_bias, m_ca_w_o)))
    mom2 = dict(zip(WEIGHTS, (v_ln_mix_g, v_ln_mix_b, v_ln_ffn_g, v_ln_ffn_b, v_ffn_w_in, v_ffn_w_out, v_mla_w_down,
                              v_mla_q_norm_g, v_mla_w_uq, v_mla_kv_norm_g, v_mla_w_ukv, v_mla_w_o, v_sb_w_qkv,
                              v_sb_w_o, v_ca_w_qkv, v_ca_rel_bias, v_ca_w_o)))
    xi, yi, ci = lax.axis_index("x"), lax.axis_index("y"), lax.axis_index("c")
    chip = 2 * xi + yi
    d_model = x.shape[-1]

    shards = _mxu_shards(w)
    layers = {k: w[k].shape[0] for k in BIG}
    dims = {k: (layers[k], shards[k].shape[0] // layers[k], shards[k].shape[1]) for k in BIG}
    chip1 = jnp.reshape(chip, (1,)).astype(jnp.int32)
    wl = {(k, l): _cast_place(f"cast_{k}_{l}", shards[k], chip1, l, dims[k][1]) for k in BIG for l in range(layers[k])}
    gains = jnp.stack([w["mla_q_norm_g"], w["mla_kv_norm_g"]])
    gains = jnp.where(ci == 0, gains, 0.0)
    placed = lax.dynamic_update_slice_in_dim(jnp.zeros((*gains.shape[:2], N_CHIPS, gains.shape[2]), F32),
                                             gains[:, :, None], chip, 2)
    full_gains = _all_reduce_small("norm_gain_gather", placed.reshape(2 * gains.shape[1], -1))
    full_gains = full_gains.reshape(2, gains.shape[1], -1)
    p = {"ln_mix_g": ln_mix_g, "ln_mix_b": ln_mix_b, "ln_ffn_g": ln_ffn_g, "ln_ffn_b": ln_ffn_b,
         "mla_q_norm_g": full_gains[0], "mla_kv_norm_g": full_gains[1], "ca_rel_bias": ca_rel_bias}

    place = jnp.stack([chip, ci]).astype(jnp.int32)
    loss, grad_x, total, small = _step(x[0], loss_target[0], wl, dims, p, _rope_tables(x.shape[1]), place)
    loss = lax.psum(loss[0, 0], ("x", "y", "c"))

    small = _unpack_small(_all_reduce_small("small_grad_all_reduce", _pack_small(small, d_model)), small)
    grad = {k: small[k] for k in ("ln_mix_g", "ln_mix_b", "ln_ffn_g", "ln_ffn_b", "ca_rel_bias")}
    for k in ("mla_q_norm_g", "mla_kv_norm_g"):
        g = small[k].reshape(small[k].shape[0], N_CHIPS, -1)
        grad[k] = lax.dynamic_index_in_dim(g, chip, 1, keepdims=False)

    for k in BIG:
        grad[k] = _unpad_grad(k, total[k], w[k])

    delta, new_m, new_v = {}, {}, {}
    for k in WEIGHTS:
        flat = lambda a: a.reshape(-1, a.shape[-1])
        dl, m2, v2 = _adamw(f"adamw_{k}", flat(w[k]), flat(grad[k]), flat(mom1[k]), flat(mom2[k]))
        delta[k], new_m[k], new_v[k] = dl.reshape(w[k].shape), m2.reshape(w[k].shape), v2.reshape(w[k].shape)
    return (loss, grad_x[None], *[grad[k] for k in WEIGHTS], *[delta[k] for k in WEIGHTS],
            *[new_m[k] for k in WEIGHTS], *[new_v[k] for k in WEIGHTS])
```

```python
import functools

import numpy as np
import jax
import jax.numpy as jnp
from jax import lax
from jax.experimental import pallas as pl
from jax.experimental.pallas import tpu as pltpu

F32, BF16 = jnp.float32, jnp.bfloat16
MXU_DTYPE = BF16
WIRE_DTYPE = BF16

DEPTH = 4
HEADS = 16
HEAD_DIM = 128
CHUNK_SHIFT = 6
TQ = 512
MLA_ROPE = 64
MLA_QK_DIM = 192
HEAD_PAD = 256
CA_LEFT_CHUNKS = 8
CA_LEFT_BLOCKS = (CA_LEFT_CHUNKS << CHUNK_SHIFT) // TQ
CA_TILES = min(CA_LEFT_BLOCKS, 2) + 1
REL_CLIP_LEFT = 128
REL_TABLE = 192
ROPE_THETA = 10000.0
LN_EPS = 1e-5
RMS_EPS = 1e-6
ALPHA = (2.0 * DEPTH) ** 0.25
NEG = -1e30
ADAM_LR, ADAM_B1, ADAM_B2, ADAM_EPS, ADAM_WD, ADAM_STEP = 0.001, 0.9, 0.999, 1e-08, 0.01, 10
N_CHIPS = 4
N_DEV = 8
VMEM_LIMIT = 48 << 20
MESH = pl.DeviceIdType.MESH
ANY = pl.BlockSpec(memory_space=pl.ANY)
VMEM_SPEC = pl.BlockSpec(memory_space=pltpu.VMEM)

NN = (((1,), (0,)), ((), ()))
NT = (((1,), (1,)), ((), ()))
TN = (((0,), (0,)), ((), ()))


def _dot(a, b, dims=NN):
    return lax.dot_general(a, b, dims, preferred_element_type=F32)


def _exact_dot(x, u):
    hi = x.astype(BF16)
    r1 = x - hi.astype(F32)
    mid = r1.astype(BF16)
    lo = (r1 - mid.astype(F32)).astype(BF16)
    return _dot(hi, u) + _dot(mid, u) + _dot(lo, u)


def _params(*sem):
    return pltpu.CompilerParams(dimension_semantics=sem, vmem_limit_bytes=VMEM_LIMIT)


TILE_K = 2048


def _tile(n, pref):
    for t in (2048, 1536, 1152, 1024, 768, 512, 384, 256, 128):
        if t <= pref and n % t == 0:
            return t
    return n


class _Comm:
    def __init__(self, srcs, dsts, alias, copies, n):
        self.srcs, self.dsts, self.alias, self.copies, self.n = list(srcs), list(dsts), dict(alias), copies, n
        self.out = None

    def deliver(self, out):
        self.out = list(out)


class _SemView:
    def __init__(self, sems, base):
        self.sems, self.base, self.at = sems, base, self

    def __getitem__(self, i):
        return self.sems.at[self.base + i]


class _Merged(_Comm):
    def __init__(self, parts):
        srcs, dsts, alias, n, self.spans = [], [], {}, 0, []
        for c in parts:
            self.spans.append((c, len(srcs), len(dsts), n))
            alias.update({len(srcs) + s: len(dsts) + d for s, d in c.alias.items()})
            srcs, dsts, n = srcs + c.srcs, dsts + c.dsts, n + c.n

        def copies(src_refs, dst_refs, send, recv):
            out = []
            for c, s0, d0, n0 in self.spans:
                out += c.copies(src_refs[s0:s0 + len(c.srcs)], dst_refs[d0:d0 + len(c.dsts)], _SemView(send, n0),
                                _SemView(recv, n0))
            return out

        super().__init__(srcs, dsts, alias, copies, n)

    def deliver(self, out):
        self.out = list(out)
        for c, _, d0, _ in self.spans:
            c.deliver(self.out[d0:d0 + len(c.dsts)])


def _merged(*comms):
    comms = [c for c in comms if c is not None]
    return None if not comms else comms[0] if len(comms) == 1 else _Merged(comms)


def _call(body, args, *, name, grid, in_specs, out_specs, out_shape, scratch_shapes=(), semantics, comm=None):
    in_specs, out_specs, out_shape = list(in_specs), list(out_specs), list(out_shape)
    scratch_shapes = list(scratch_shapes)
    if comm is None:
        return pl.pallas_call(body, grid=grid, in_specs=in_specs, out_specs=out_specs, out_shape=out_shape,
                              scratch_shapes=scratch_shapes, name=name, compiler_params=_params(*semantics))(*args)
    n_in, n_out, n_scr, ns, nd = len(in_specs), len(out_specs), len(scratch_shapes), len(comm.srcs), len(comm.dsts)

    def carrier(*refs):
        ins, refs = refs[:n_in], refs[n_in:]
        srcs, refs = refs[:ns], refs[ns:]
        outs, refs = refs[:n_out], refs[n_out:]
        dsts, refs = refs[:nd], refs[nd:]
        scratch, (send, recv) = refs[:n_scr], refs[n_scr:]
        ids = [pl.program_id(a) for a in range(len(grid))]
        first = functools.reduce(jnp.logical_and, [i == 0 for i in ids])
        last = functools.reduce(jnp.logical_and, [i == g - 1 for i, g in zip(ids, grid)])

        @pl.when(first)
        def _():
            for cp in comm.copies(srcs, dsts, send, recv):
                cp.start()

        body(*ins, *outs, *scratch)

        @pl.when(last)
        def _():
            for cp in comm.copies(srcs, dsts, send, recv):
                cp.wait()

    res = pl.pallas_call(
        carrier, grid=grid, in_specs=in_specs + [ANY] * ns, out_specs=out_specs + [ANY] * nd,
        out_shape=out_shape + comm.dsts,
        scratch_shapes=scratch_shapes + [pltpu.SemaphoreType.DMA((comm.n,)), pltpu.SemaphoreType.DMA((comm.n,))],
        input_output_aliases={n_in + s: n_out + d for s, d in comm.alias.items()}, name=name,
        compiler_params=_params(*["arbitrary"] * len(grid)))(*args, *comm.srcs)
    comm.deliver(res[n_out:])
    return res[:n_out]


def _mm(name, a, b, extras, *, grid, a_spec, b_spec, extra_specs, out_specs, out_shape, dims,
        epilogue, acc_shape, comm=None):
    nk = grid[2]
    n_ex = len(extras)

    def product(a_ref, b_ref):
        return lax.dot_general(a_ref[...].astype(MXU_DTYPE), b_ref[...].astype(MXU_DTYPE), dims,
                               preferred_element_type=F32)

    def whole(*refs):
        epilogue(product(*refs[:2]), refs[2:2 + n_ex], refs[2 + n_ex:])

    def stepped(*refs):
        a_ref, b_ref = refs[:2]
        ex = refs[2:2 + n_ex]
        outs = refs[2 + n_ex:-1]
        acc = refs[-1]
        k = pl.program_id(2)

        @pl.when(k == 0)
        def _():
            acc[...] = product(a_ref, b_ref)

        @pl.when(k > 0)
        def _():
            acc[...] += product(a_ref, b_ref)

        @pl.when(k == nk - 1)
        def _():
            epilogue(acc[...], ex, outs)

    return _call(whole if nk == 1 else stepped, (a, b, *extras), name=name, grid=grid,
                 in_specs=[a_spec, b_spec, *extra_specs], out_specs=out_specs, out_shape=out_shape,
                 scratch_shapes=[] if nk == 1 else [pltpu.VMEM(acc_shape, F32)],
                 semantics=("parallel", "parallel", "arbitrary"), comm=comm)


def _epi_store(acc, ex, outs):
    outs[0][...] = acc.astype(outs[0].dtype)


def _epi_relu2(acc, ex, outs):
    outs[0][...] = acc
    r = jnp.maximum(acc, 0.0)
    outs[1][...] = (r * r).astype(outs[1].dtype)


def _epi_drelu2(acc, ex, outs):
    outs[0][...] = (acc * (2.0 * jnp.maximum(ex[0][...], 0.0))).astype(outs[0].dtype)


def _epi_residual(acc, ex, outs):
    outs[0][...] = acc + ALPHA * ex[0][...]


def _fwd_col(name, x, wg, dtype, epilogue=_epi_store, dtypes=None, comm=None):
    m, (_, rows, cols) = x.shape[0], wg.shape
    tm, tn, tk = _tile(m, 1024), _tile(cols, 1024), _tile(rows, TILE_K)
    nps = cols // tn
    dtypes = dtypes or (dtype,)
    out = pl.BlockSpec((tm, tn), lambda i, j, k: (i, j))
    res = _mm(name, x, wg, (), grid=(m // tm, N_CHIPS * nps, rows // tk),
              a_spec=pl.BlockSpec((tm, tk), lambda i, j, k: (i, k)),
              b_spec=pl.BlockSpec((None, tk, tn), lambda i, j, k: (j // nps, k, j % nps)),
              extra_specs=(), out_specs=[out] * len(dtypes),
              out_shape=[jax.ShapeDtypeStruct((m, N_CHIPS * cols), d) for d in dtypes],
              dims=NN, epilogue=epilogue, acc_shape=(tm, tn), comm=comm)
    return res if len(dtypes) > 1 else res[0]


def _fwd_row(name, x, wg, dtype, comm=None):
    m, (_, rows, cols) = x.shape[0], wg.shape
    tm, tn, tk = _tile(m, 1024), _tile(cols, 1024), _tile(rows, TILE_K)
    kps = rows // tk
    return _mm(name, x, wg, (), grid=(m // tm, cols // tn, N_CHIPS * kps),
               a_spec=pl.BlockSpec((tm, tk), lambda i, j, k: (i, k)),
               b_spec=pl.BlockSpec((None, tk, tn), lambda i, j, k: (k // kps, k % kps, j)),
               extra_specs=(), out_specs=[pl.BlockSpec((tm, tn), lambda i, j, k: (i, j))],
               out_shape=[jax.ShapeDtypeStruct((m, cols), dtype)],
               dims=NN, epilogue=_epi_store, acc_shape=(tm, tn), comm=comm)[0]


def _dx_col(name, dy, wg, dtype, epilogue=_epi_store, extra=None, comm=None):
    m, (_, rows, cols) = dy.shape[0], wg.shape
    tm, tn, tk = _tile(m, 1024), _tile(rows, 1024), _tile(cols, TILE_K)
    kps = cols // tk
    tile = pl.BlockSpec((tm, tn), lambda i, j, k: (i, j))
    return _mm(name, dy, wg, () if extra is None else (extra,), grid=(m // tm, rows // tn, N_CHIPS * kps),
               a_spec=pl.BlockSpec((tm, tk), lambda i, j, k: (i, k)),
               b_spec=pl.BlockSpec((None, tn, tk), lambda i, j, k: (k // kps, j, k % kps)),
               extra_specs=() if extra is None else (tile,), out_specs=[tile],
               out_shape=[jax.ShapeDtypeStruct((m, rows), dtype)],
               dims=NT, epilogue=epilogue, acc_shape=(tm, tn), comm=comm)[0]


def _dx_row(name, dy, wg, dtype, epilogue=_epi_store, extra=None, comm=None):
    m, (_, rows, cols) = dy.shape[0], wg.shape
    tm, tn, tk = _tile(m, 1024), _tile(rows, 1024), _tile(cols, TILE_K)
    nps = rows // tn
    tile = pl.BlockSpec((tm, tn), lambda i, j, k: (i, j))
    return _mm(name, dy, wg, () if extra is None else (extra,), grid=(m // tm, N_CHIPS * nps, cols // tk),
               a_spec=pl.BlockSpec((tm, tk), lambda i, j, k: (i, k)),
               b_spec=pl.BlockSpec((None, tn, tk), lambda i, j, k: (j // nps, j % nps, k)),
               extra_specs=() if extra is None else (tile,), out_specs=[tile],
               out_shape=[jax.ShapeDtypeStruct((m, N_CHIPS * rows), dtype)],
               dims=NT, epilogue=epilogue, acc_shape=(tm, tn), comm=comm)[0]


def _dw(name, x, dy, rows, cols, col_sharded, comm=None):
    s_tok = x.shape[0]
    tm, tn, tk = _tile(rows, 1024), _tile(cols, 1024), _tile(s_tok, TILE_K)
    mt, nps = rows // tm, cols // tn
    if col_sharded:
        grid = (mt, N_CHIPS * nps, s_tok // tk)
        out = pl.BlockSpec((None, tm, tn), lambda i, j, k: (j // nps, i, j % nps))
    else:
        grid = (N_CHIPS * mt, nps, s_tok // tk)
        out = pl.BlockSpec((None, tm, tn), lambda i, j, k: (i // mt, i % mt, j))
    return _mm(name, x, dy, (), grid=grid,
               a_spec=pl.BlockSpec((tk, tm), lambda i, j, k: (k, i)),
               b_spec=pl.BlockSpec((tk, tn), lambda i, j, k: (k, j)),
               extra_specs=(), out_specs=[out],
               out_shape=[jax.ShapeDtypeStruct((N_CHIPS, rows, cols), WIRE_DTYPE)],
               dims=TN, epilogue=_epi_store, acc_shape=(tm, tn), comm=comm)[0]


def _ln_fwd(name, h, m, g, b):
    s, d = h.shape
    tm = _tile(s, 256)
    row = pl.BlockSpec((tm, d), lambda i: (i, 0))
    vec = pl.BlockSpec((1, d), lambda i: (0, 0))

    def body(h_ref, m_ref, g_ref, b_ref, y_ref, ymx_ref, xh_ref, r_ref):
        z = ALPHA * h_ref[...] + m_ref[...]
        mu = jnp.mean(z, -1, keepdims=True)
        zc = z - mu
        r = lax.rsqrt(jnp.mean(zc * zc, -1, keepdims=True) + LN_EPS)
        xh = zc * r
        xh_ref[...] = xh
        r_ref[...] = r
        y = xh * g_ref[...] + b_ref[...]
        y_ref[...] = y
        ymx_ref[...] = y.astype(ymx_ref.dtype)

    return pl.pallas_call(
        body, grid=(s // tm,), in_specs=[row, row, vec, vec],
        out_specs=[row, row, row, pl.BlockSpec((tm, 1), lambda i: (i, 0))],
        out_shape=[jax.ShapeDtypeStruct((s, d), F32), jax.ShapeDtypeStruct((s, d), MXU_DTYPE),
                   jax.ShapeDtypeStruct((s, d), F32), jax.ShapeDtypeStruct((s, 1), F32)],
        name=name, compiler_params=_params("parallel"))(h, m, g, b)


def _ln_bwd(name, dy, xh, r, g):
    s, d = dy.shape
    tm = _tile(s, 256)
    row = pl.BlockSpec((tm, d), lambda i: (i, 0))
    vec = pl.BlockSpec((1, d), lambda i: (0, 0))

    def body(dy_ref, xh_ref, r_ref, g_ref, dz_ref, dzmx_ref, dg_ref, db_ref):
        i = pl.program_id(0)
        dy_, xh_ = dy_ref[...], xh_ref[...]
        dyg = dy_ * g_ref[...]
        m1 = jnp.mean(dyg, -1, keepdims=True)
        m2 = jnp.mean(dyg * xh_, -1, keepdims=True)
        dz = r_ref[...] * (dyg - m1 - xh_ * m2)
        dz_ref[...] = dz
        dzmx_ref[...] = dz.astype(dzmx_ref.dtype)
        pg = jnp.sum(dy_ * xh_, 0, keepdims=True)
        pb = jnp.sum(dy_, 0, keepdims=True)

        @pl.when(i == 0)
        def _():
            dg_ref[...] = pg
            db_ref[...] = pb

        @pl.when(i > 0)
        def _():
            dg_ref[...] += pg
            db_ref[...] += pb

    return pl.pallas_call(
        body, grid=(s // tm,), in_specs=[row, row, pl.BlockSpec((tm, 1), lambda i: (i, 0)), vec],
        out_specs=[row, row, vec, vec],
        out_shape=[jax.ShapeDtypeStruct((s, d), F32), jax.ShapeDtypeStruct((s, d), MXU_DTYPE),
                   jax.ShapeDtypeStruct((1, d), F32), jax.ShapeDtypeStruct((1, d), F32)],
        name=name, compiler_params=_params("arbitrary"))(dy, xh, r, g)


def _loss_head(y, t):
    s, d = y.shape
    tm = _tile(s, 256)
    row = pl.BlockSpec((tm, d), lambda i: (i, 0))

    def body(y_ref, t_ref, l_ref, dy_ref):
        i = pl.program_id(0)
        e = y_ref[...] - t_ref[...]
        dy_ref[...] = e * (1.0 / d)
        part = 0.5 * jnp.sum(jnp.mean(e * e, -1, keepdims=True), 0, keepdims=True)

        @pl.when(i == 0)
        def _():
            l_ref[...] = part

        @pl.when(i > 0)
        def _():
            l_ref[...] += part

    return pl.pallas_call(
        body, grid=(s // tm,), in_specs=[row, row],
        out_specs=[pl.BlockSpec((1, 1), lambda i: (0, 0)), row],
        out_shape=[jax.ShapeDtypeStruct((1, 1), F32), jax.ShapeDtypeStruct((s, d), F32)],
        name="loss_head", compiler_params=_params("arbitrary"))(y, t)


def _rope_tables(s):
    half = MLA_ROPE // 2
    inv = ROPE_THETA ** (-jnp.arange(half, dtype=F32) / half)
    ang = jnp.arange(s).astype(F32)[:, None] * inv[None, :]
    cos, sin = jnp.cos(ang), jnp.sin(ang)
    c = jnp.concatenate([cos, cos, jnp.ones((s, 128 - MLA_ROPE), F32)], 1)
    s1 = jnp.concatenate([-sin, jnp.zeros((s, 128 - half), F32)], 1)
    s2 = jnp.concatenate([jnp.zeros((s, half), F32), sin, jnp.zeros((s, 128 - MLA_ROPE), F32)], 1)
    return c, s1, s2


def _rope(x, c, s1, s2):
    half = MLA_ROPE // 2
    return x * c + pltpu.roll(x, 128 - half, 1) * s1 + pltpu.roll(x, half, 1) * s2


def _rope_t(dy, c, s1, s2):
    half = MLA_ROPE // 2
    return dy * c + pltpu.roll(dy * s1, half, 1) + pltpu.roll(dy * s2, 128 - half, 1)


def _mla_mid_fwd(down, gq, gkv, tabs):
    s, w = down.shape
    ql, kvl = gq.shape[1], gkv.shape[1]
    tm = _tile(s, 256)

    def body(d_ref, gq_ref, gkv_ref, c_ref, s1_ref, s2_ref, cq_ref, ckv_ref, kr_ref):
        cq = d_ref[:, :ql]
        ckv = d_ref[:, ql:ql + kvl]
        cq_ref[...] = (cq * lax.rsqrt(jnp.mean(cq * cq, -1, keepdims=True) + RMS_EPS)
                       * gq_ref[...]).astype(cq_ref.dtype)
        ckv_ref[...] = (ckv * lax.rsqrt(jnp.mean(ckv * ckv, -1, keepdims=True) + RMS_EPS)
                        * gkv_ref[...]).astype(ckv_ref.dtype)
        kr_ref[...] = _rope(d_ref[:, ql + kvl:], c_ref[...], s1_ref[...], s2_ref[...]).astype(kr_ref.dtype)

    tab = pl.BlockSpec((tm, 128), lambda i: (i, 0))
    return pl.pallas_call(
        body, grid=(s // tm,),
        in_specs=[pl.BlockSpec((tm, w), lambda i: (i, 0)), pl.BlockSpec((1, ql), lambda i: (0, 0)),
                  pl.BlockSpec((1, kvl), lambda i: (0, 0)), tab, tab, tab],
        out_specs=[pl.BlockSpec((tm, ql), lambda i: (i, 0)), pl.BlockSpec((tm, kvl), lambda i: (i, 0)), tab],
        out_shape=[jax.ShapeDtypeStruct((s, ql), MXU_DTYPE), jax.ShapeDtypeStruct((s, kvl), MXU_DTYPE),
                   jax.ShapeDtypeStruct((s, 128), MXU_DTYPE)],
        name="mla_mid_fwd", compiler_params=_params("parallel"))(down, gq, gkv, *tabs)


def _mla_mid_bwd(down, dcq, dckv, dkr, gq, gkv, tabs):
    s, w = down.shape
    ql, kvl = gq.shape[1], gkv.shape[1]
    tm = _tile(s, 256)

    def rms_bwd(x, dy, g):
        n = x.shape[1]
        r = lax.rsqrt(jnp.mean(x * x, -1, keepdims=True) + RMS_EPS)
        dyg = dy * g
        dx = r * dyg - x * (r * r * r * (1.0 / n)) * jnp.sum(dyg * x, -1, keepdims=True)
        return dx, jnp.sum(dy * x * r, 0, keepdims=True)

    def body(d_ref, dcq_ref, dckv_ref, dkr_ref, gq_ref, gkv_ref, c_ref, s1_ref, s2_ref, o_ref, dgq_ref, dgkv_ref):
        i = pl.program_id(0)
        dxq, pq = rms_bwd(d_ref[:, :ql], dcq_ref[...], gq_ref[...])
        dxkv, pkv = rms_bwd(d_ref[:, ql:ql + kvl], dckv_ref[...], gkv_ref[...])
        o_ref[:, :ql] = dxq.astype(o_ref.dtype)
        o_ref[:, ql:ql + kvl] = dxkv.astype(o_ref.dtype)
        o_ref[:, ql + kvl:] = _rope_t(dkr_ref[...], c_ref[...], s1_ref[...], s2_ref[...]).astype(o_ref.dtype)

        @pl.when(i == 0)
        def _():
            dgq_ref[...] = pq
            dgkv_ref[...] = pkv

        @pl.when(i > 0)
        def _():
            dgq_ref[...] += pq
            dgkv_ref[...] += pkv

    tab = pl.BlockSpec((tm, 128), lambda i: (i, 0))
    vq = pl.BlockSpec((1, ql), lambda i: (0, 0))
    vkv = pl.BlockSpec((1, kvl), lambda i: (0, 0))
    full = pl.BlockSpec((tm, w), lambda i: (i, 0))
    return pl.pallas_call(
        body, grid=(s // tm,),
        in_specs=[full, pl.BlockSpec((tm, ql), lambda i: (i, 0)), pl.BlockSpec((tm, kvl), lambda i: (i, 0)), tab,
                  vq, vkv, tab, tab, tab],
        out_specs=[full, vq, vkv],
        out_shape=[jax.ShapeDtypeStruct((s, w), MXU_DTYPE), jax.ShapeDtypeStruct((1, ql), F32),
                   jax.ShapeDtypeStruct((1, kvl), F32)],
        name="mla_mid_bwd", compiler_params=_params("arbitrary"))(down, dcq, dckv, dkr, gq, gkv, *tabs)


def _iota2():
    return (lax.broadcasted_iota(jnp.int32, (TQ, TQ), 0), lax.broadcasted_iota(jnp.int32, (TQ, TQ), 1))


def _mla_attn_fwd(name, q, kv, kr, tabs, comm=None):
    s = q.shape[0]
    nq = s // TQ
    scale = MLA_QK_DIM ** -0.5

    def body(q_ref, kv_ref, kr_ref, c_ref, s1_ref, s2_ref, o_ref, omx_ref, lse_ref):
        i = pl.program_id(1)
        row, col = _iota2()
        tabs_i = (c_ref[...], s1_ref[...], s2_ref[...])
        qn = [q_ref[:, lo].astype(MXU_DTYPE) for lo, _ in PAD_COLS]
        qr = [_rope(q_ref[:, hi], *tabs_i).astype(MXU_DTYPE) for _, hi in PAD_COLS]
        qc = jnp.right_shift(i * TQ + row, CHUNK_SHIFT)

        def step(kb, carry):
            ks = pl.multiple_of(kb * TQ, TQ)
            krb = kr_ref[pl.ds(ks, TQ), :].astype(MXU_DTYPE)
            mask = jnp.right_shift(ks + col, CHUNK_SHIFT) <= qc
            out = []
            for j, ((lo, hi), (m, l, acc)) in enumerate(zip(PAD_COLS, carry)):
                sc = (_dot(qn[j], kv_ref[pl.ds(ks, TQ), lo].astype(MXU_DTYPE), NT) + _dot(qr[j], krb, NT)) * scale
                sc = jnp.where(mask, sc, NEG)
                m_new = jnp.maximum(m, jnp.max(sc, -1, keepdims=True))
                p = jnp.exp(sc - m_new)
                corr = jnp.exp(m - m_new)
                l = corr * l + jnp.sum(p, -1, keepdims=True)
                acc = corr * acc + _dot(p.astype(MXU_DTYPE), kv_ref[pl.ds(ks, TQ), hi].astype(MXU_DTYPE))
                out.append((m_new, l, acc))
            return tuple(out)

        res = lax.fori_loop(0, i + 1, step,
                            tuple((jnp.full((TQ, 1), NEG, F32), jnp.zeros((TQ, 1), F32),
                                   jnp.zeros((TQ, HEAD_DIM), F32)) for _ in HEAD_COLS))
        for j, (hd, (m, l, acc)) in enumerate(zip(HEAD_COLS, res)):
            o = acc / l
            o_ref[:, hd] = o
            omx_ref[:, hd] = o.astype(omx_ref.dtype)
            lse_ref[j] = m + jnp.log(l)

    tab = pl.BlockSpec((TQ, 128), lambda h, i: (i, 0))
    oblk = pl.BlockSpec((TQ, HEADS_PER_STEP * HEAD_DIM), lambda h, i: (i, h))
    return _call(
        body, (q, kv, kr, *tabs), grid=(HEADS // HEADS_PER_STEP, nq),
        in_specs=[pl.BlockSpec((TQ, HEADS_PER_STEP * HEAD_PAD), lambda h, i: (i, h)),
                  pl.BlockSpec((s, HEADS_PER_STEP * HEAD_PAD), lambda h, i: (0, h)),
                  pl.BlockSpec((s, 128), lambda h, i: (0, 0)), tab, tab, tab],
        out_specs=[oblk, oblk, pl.BlockSpec((HEADS_PER_STEP, TQ, 1), lambda h, i: (h, i, 0))],
        out_shape=[jax.ShapeDtypeStruct((s, HEADS * HEAD_DIM), F32),
                   jax.ShapeDtypeStruct((s, HEADS * HEAD_DIM), MXU_DTYPE), jax.ShapeDtypeStruct((HEADS, s, 1), F32)],
        name=name, semantics=("parallel", "parallel"), comm=comm)


def _mla_attn_bwd(name, q, kv, kr, tabs, do, o, lse, comm=None):
    s = q.shape[0]
    nq = s // TQ
    scale = MLA_QK_DIM ** -0.5

    def body(q_ref, kv_ref, kr_ref, c_ref, s1_ref, s2_ref, do_ref, o_ref, lse_ref,
             dq_ref, dkv_ref, dkr_ref, dkv_acc, dkr_acc):
        h, i = pl.program_id(0), pl.program_id(1)
        row, col = _iota2()

        @pl.when(i == 0)
        def _():
            dkv_acc[...] = jnp.zeros_like(dkv_acc)

        @pl.when((h == 0) & (i == 0))
        def _():
            dkr_acc[...] = jnp.zeros_like(dkr_acc)

        tabs_i = (c_ref[...], s1_ref[...], s2_ref[...])
        qn = [q_ref[:, lo].astype(MXU_DTYPE) for lo, _ in PAD_COLS]
        qr = [_rope(q_ref[:, hi], *tabs_i).astype(MXU_DTYPE) for _, hi in PAD_COLS]
        qc = jnp.right_shift(i * TQ + row, CHUNK_SHIFT)
        delta = [jnp.sum(do_ref[:, hd] * o_ref[:, hd], -1, keepdims=True) for hd in HEAD_COLS]
        lse_ = [lse_ref[j] for j in range(HEADS_PER_STEP)]
        dob = [do_ref[:, hd].astype(MXU_DTYPE) for hd in HEAD_COLS]

        def step(kb, carry):
            ks = pl.multiple_of(kb * TQ, TQ)
            krb = kr_ref[pl.ds(ks, TQ), :].astype(MXU_DTYPE)
            mask = jnp.right_shift(ks + col, CHUNK_SHIFT) <= qc
            out = []
            for j, ((lo, hi), (dqn, dqr)) in enumerate(zip(PAD_COLS, carry)):
                kn = kv_ref[pl.ds(ks, TQ), lo].astype(MXU_DTYPE)
                v = kv_ref[pl.ds(ks, TQ), hi].astype(MXU_DTYPE)
                sc = (_dot(qn[j], kn, NT) + _dot(qr[j], krb, NT)) * scale
                sc = jnp.where(mask, sc, NEG)
                p = jnp.exp(sc - lse_[j])
                ds = (p * (_dot(dob[j], v, NT) - delta[j]) * scale).astype(MXU_DTYPE)
                dkv_acc[pl.ds(ks, TQ), lo] += _dot(ds, qn[j], TN)
                dkv_acc[pl.ds(ks, TQ), hi] += _dot(p.astype(MXU_DTYPE), dob[j], TN)
                dkr_acc[pl.ds(ks, TQ), :] += _dot(ds, qr[j], TN)
                out.append((dqn + _dot(ds, kn), dqr + _dot(ds, krb)))
            return tuple(out)

        res = lax.fori_loop(0, i + 1, step,
                            tuple((jnp.zeros((TQ, HEAD_DIM), F32), jnp.zeros((TQ, 128), F32)) for _ in HEAD_COLS))
        for (lo, hi), (dqn, dqr) in zip(PAD_COLS, res):
            dq_ref[:, lo] = dqn.astype(dq_ref.dtype)
            dq_ref[:, hi] = _rope_t(dqr, *tabs_i).astype(dq_ref.dtype)

        @pl.when(i == nq - 1)
        def _():
            dkv_ref[...] = dkv_acc[...].astype(dkv_ref.dtype)

        @pl.when((h == HEADS // HEADS_PER_STEP - 1) & (i == nq - 1))
        def _():
            dkr_ref[...] = dkr_acc[...]

    tab = pl.BlockSpec((TQ, 128), lambda h, i: (i, 0))
    qblk = pl.BlockSpec((TQ, HEADS_PER_STEP * HEAD_PAD), lambda h, i: (i, h))
    oblk = pl.BlockSpec((TQ, HEADS_PER_STEP * HEAD_DIM), lambda h, i: (i, h))
    kvblk = pl.BlockSpec((s, HEADS_PER_STEP * HEAD_PAD), lambda h, i: (0, h))
    return _call(
        body, (q, kv, kr, *tabs, do, o, lse), grid=(HEADS // HEADS_PER_STEP, nq),
        in_specs=[qblk, kvblk, pl.BlockSpec((s, 128), lambda h, i: (0, 0)), tab, tab, tab, oblk, oblk,
                  pl.BlockSpec((HEADS_PER_STEP, TQ, 1), lambda h, i: (h, i, 0))],
        out_specs=[qblk, kvblk, pl.BlockSpec((s, 128), lambda h, i: (0, 0))],
        out_shape=[jax.ShapeDtypeStruct((s, HEADS * HEAD_PAD), MXU_DTYPE), jax.ShapeDtypeStruct((s, HEADS * HEAD_PAD), MXU_DTYPE),
                   jax.ShapeDtypeStruct((s, 128), F32)],
        scratch_shapes=[pltpu.VMEM((s, HEADS_PER_STEP * HEAD_PAD), F32), pltpu.VMEM((s, 128), F32)],
        name=name, semantics=("arbitrary", "arbitrary"), comm=comm)


HEADS_PER_STEP = 2
HEAD_COLS = [slice(j * HEAD_DIM, (j + 1) * HEAD_DIM) for j in range(HEADS_PER_STEP)]
PAD_COLS = [(slice(j * HEAD_PAD, j * HEAD_PAD + HEAD_DIM), slice(j * HEAD_PAD + HEAD_DIM, (j + 1) * HEAD_PAD))
            for j in range(HEADS_PER_STEP)]


def _qkv_specs(s):
    groups, width = HEADS // HEADS_PER_STEP, HEADS_PER_STEP * HEAD_DIM
    return [pl.BlockSpec((TQ, width), lambda h, i: (i, h)),
            pl.BlockSpec((s, width), lambda h, i: (0, groups + h)),
            pl.BlockSpec((s, width), lambda h, i: (0, 2 * groups + h))]


def _sb_terms(z):
    sp = jnp.log(1.0 + jnp.exp(-jnp.abs(z)))
    return jnp.minimum(z, 0.0) - sp, jnp.minimum(-z, 0.0) - sp


def _sb_attn_fwd(name, qkv, comm=None):
    s = qkv.shape[0]
    nq = s // TQ
    scale = HEAD_DIM ** -0.5

    def body(q_ref, k_ref, v_ref, o_ref):
        i = pl.program_id(1)
        row, col = _iota2()
        after = (row > col).astype(BF16)
        q = [q_ref[:, hd].astype(MXU_DTYPE) for hd in HEAD_COLS]
        qpos = i * TQ + row

        def step(n, carry):
            ks = pl.multiple_of((i - n) * TQ, TQ)
            strict = (ks + col) < qpos
            out = []
            for hd, qh, (tail, acc) in zip(HEAD_COLS, q, carry):
                z = _dot(qh, k_ref[pl.ds(ks, TQ), hd].astype(MXU_DTYPE), NT) * scale
                lb, l1 = _sb_terms(z)
                l1 = jnp.where(strict, l1, 0.0)
                a = jnp.where(strict, jnp.exp(lb + tail + _exact_dot(l1, after)), 0.0)
                acc = acc + _dot(a.astype(MXU_DTYPE), v_ref[pl.ds(ks, TQ), hd].astype(MXU_DTYPE))
                out.append((tail + jnp.sum(l1, -1, keepdims=True), acc))
            return tuple(out)

        res = lax.fori_loop(0, i + 1, step,
                            tuple((jnp.zeros((TQ, 1), F32), jnp.zeros((TQ, HEAD_DIM), F32)) for _ in HEAD_COLS))
        for hd, (_, acc) in zip(HEAD_COLS, res):
            o_ref[:, hd] = acc.astype(o_ref.dtype)

    return _call(
        body, (qkv, qkv, qkv), grid=(HEADS // HEADS_PER_STEP, nq), in_specs=_qkv_specs(s),
        out_specs=[pl.BlockSpec((TQ, HEADS_PER_STEP * HEAD_DIM), lambda h, i: (i, h))],
        out_shape=[jax.ShapeDtypeStruct((s, HEADS * HEAD_DIM), MXU_DTYPE)],
        name=name, semantics=("parallel", "parallel"), comm=comm)[0]


def _sb_attn_bwd(name, qkv, do, comm=None):
    s = qkv.shape[0]
    nq = s // TQ
    scale = HEAD_DIM ** -0.5

    def body(q_ref, k_ref, v_ref, do_ref, dq_ref, dk_ref, dv_ref, a_buf, dk_acc, dv_acc):
        i = pl.program_id(1)
        row, col = _iota2()
        after = (row > col).astype(BF16)
        before = (row < col).astype(BF16)

        @pl.when(i == 0)
        def _():
            dk_acc[...] = jnp.zeros_like(dk_acc)
            dv_acc[...] = jnp.zeros_like(dv_acc)

        q = [q_ref[:, hd].astype(MXU_DTYPE) for hd in HEAD_COLS]
        dob = [do_ref[:, hd].astype(MXU_DTYPE) for hd in HEAD_COLS]
        qpos = i * TQ + row

        def weights(n, tails):
            kb = i - n
            ks = pl.multiple_of(kb * TQ, TQ)
            strict = (ks + col) < qpos
            out = []
            for j, (hd, tail) in enumerate(zip(HEAD_COLS, tails)):
                z = _dot(q[j], k_ref[pl.ds(ks, TQ), hd].astype(MXU_DTYPE), NT) * scale
                lb, l1 = _sb_terms(z)
                l1 = jnp.where(strict, l1, 0.0)
                a = jnp.where(strict, jnp.exp(lb + tail + _exact_dot(l1, after)), 0.0)
                a_buf[j, kb] = a
                dv_acc[pl.ds(ks, TQ), hd] += _dot(a.astype(MXU_DTYPE), dob[j], TN)
                out.append(tail + jnp.sum(l1, -1, keepdims=True))
            return tuple(out)

        lax.fori_loop(0, i + 1, weights, tuple(jnp.zeros((TQ, 1), F32) for _ in HEAD_COLS))

        def grads(kb, carry):
            ks = pl.multiple_of(kb * TQ, TQ)
            strict = (ks + col) < qpos
            out = []
            for j, (hd, (head, dq)) in enumerate(zip(HEAD_COLS, carry)):
                k = k_ref[pl.ds(ks, TQ), hd].astype(MXU_DTYPE)
                z = _dot(q[j], k, NT) * scale
                e = jnp.exp(-jnp.abs(z))
                beta = jnp.where(z >= 0.0, 1.0, e) / (1.0 + e)
                w = _dot(dob[j], v_ref[pl.ds(ks, TQ), hd].astype(MXU_DTYPE), NT) * a_buf[j, kb]
                dz = jnp.where(strict, w * (1.0 - beta) - beta * (head + _exact_dot(w, before)), 0.0) * scale
                dzb = dz.astype(MXU_DTYPE)
                dk_acc[pl.ds(ks, TQ), hd] += _dot(dzb, q[j], TN)
                out.append((head + jnp.sum(w, -1, keepdims=True), dq + _dot(dzb, k)))
            return tuple(out)

        res = lax.fori_loop(0, i + 1, grads,
                            tuple((jnp.zeros((TQ, 1), F32), jnp.zeros((TQ, HEAD_DIM), F32)) for _ in HEAD_COLS))
        for hd, (_, dq) in zip(HEAD_COLS, res):
            dq_ref[:, hd] = dq.astype(dq_ref.dtype)

        @pl.when(i == nq - 1)
        def _():
            dk_ref[...] = dk_acc[...].astype(dk_ref.dtype)
            dv_ref[...] = dv_acc[...].astype(dv_ref.dtype)

    width = HEADS_PER_STEP * HEAD_DIM
    blk = pl.BlockSpec((TQ, width), lambda h, i: (i, h))
    col_h = pl.BlockSpec((s, width), lambda h, i: (0, h))
    shp = jax.ShapeDtypeStruct((s, HEADS * HEAD_DIM), MXU_DTYPE)
    return _call(
        body, (qkv, qkv, qkv, do), grid=(HEADS // HEADS_PER_STEP, nq), in_specs=_qkv_specs(s) + [blk],
        out_specs=[blk, col_h, col_h], out_shape=[shp, shp, shp],
        scratch_shapes=[pltpu.VMEM((HEADS_PER_STEP, nq, TQ, TQ), F32), pltpu.VMEM((s, width), F32),
                        pltpu.VMEM((s, width), F32)],
        name=name, semantics=("arbitrary", "arbitrary"), comm=comm)


LANES = 128
LAST_REL = (1 << CHUNK_SHIFT) - 1


def _ca_subtiles():
    nb = TQ // LANES
    return [(d, a, b, -d * TQ + (b - a) * LANES) for d in range(CA_TILES) for a in range(nb) for b in range(nb)]


def _ca_bias_tiles(rel_bias):
    table = jnp.pad(jnp.transpose(rel_bias), ((0, 0), (0, 2 * LANES - REL_TABLE)))[:, None]

    def unskew(x, row):
        for b in range(7):
            x = jnp.where((jnp.right_shift(row, b) & 1) == 1, pltpu.roll(x, 1 << b, 1), x)
        return x

    def body(t_ref, o_ref):
        row = lax.broadcasted_iota(jnp.int32, (LANES, LANES), 0)
        col = lax.broadcasted_iota(jnp.int32, (LANES, LANES), 1)
        lane = col[:1]
        lo, hi = t_ref[:, :LANES], t_ref[:, LANES:]
        first = jnp.sum(jnp.where(lane == 0, lo, 0.0), -1, keepdims=True)
        last = jnp.sum(jnp.where(lane == LAST_REL, hi, 0.0), -1, keepdims=True)
        hi = jnp.where(lane <= LAST_REL, hi, last)
        r_lo = unskew(jnp.broadcast_to(lo, (LANES, LANES)), row)
        r_hi = unskew(jnp.broadcast_to(hi, (LANES, LANES)), row)
        upper = col >= row
        for d, a, b, o in _ca_subtiles():
            if o >= LANES:
                piece = jnp.broadcast_to(last, (LANES, LANES))
            elif o == 0:
                piece = jnp.where(upper, r_hi, r_lo)
            elif o == -LANES:
                piece = jnp.where(upper, r_lo, first)
            else:
                piece = jnp.broadcast_to(first, (LANES, LANES))
            o_ref[d, a * LANES:(a + 1) * LANES, b * LANES:(b + 1) * LANES] = piece

    return pl.pallas_call(
        body, grid=(HEADS,), in_specs=[pl.BlockSpec((None, 1, 2 * LANES), lambda h: (h, 0, 0))],
        out_specs=pl.BlockSpec((None, CA_TILES, TQ, TQ), lambda h: (h, 0, 0, 0)),
        out_shape=jax.ShapeDtypeStruct((HEADS, CA_TILES, TQ, TQ), F32),
        name="ca_bias_tiles", compiler_params=_params("parallel"))(table)


def _ca_mask(i, ks, row, col):
    qc = jnp.right_shift(i * TQ + row, CHUNK_SHIFT)
    kc = jnp.right_shift(ks + col, CHUNK_SHIFT)
    return (kc <= qc) & (kc >= qc - CA_LEFT_CHUNKS)


def _ca_attn_fwd(name, qkv, tiles, comm=None):
    s = qkv.shape[0]
    nq = s // TQ
    scale = HEAD_DIM ** -0.5

    def body(q_ref, k_ref, v_ref, bt_ref, o_ref, omx_ref, lse_ref):
        i = pl.program_id(1)
        row, col = _iota2()
        q = [q_ref[:, hd].astype(MXU_DTYPE) for hd in HEAD_COLS]

        def step(kb, carry):
            ks = pl.multiple_of(kb * TQ, TQ)
            mask = _ca_mask(i, ks, row, col)
            t = jnp.minimum(i - kb, CA_TILES - 1)
            out = []
            for j, (hd, (m, l, acc)) in enumerate(zip(HEAD_COLS, carry)):
                sc = _dot(q[j], k_ref[pl.ds(ks, TQ), hd].astype(MXU_DTYPE), NT) * scale + bt_ref[j, t]
                sc = jnp.where(mask, sc, NEG)
                m_new = jnp.maximum(m, jnp.max(sc, -1, keepdims=True))
                p = jnp.exp(sc - m_new)
                corr = jnp.exp(m - m_new)
                l = corr * l + jnp.sum(p, -1, keepdims=True)
                acc = corr * acc + _dot(p.astype(MXU_DTYPE), v_ref[pl.ds(ks, TQ), hd].astype(MXU_DTYPE))
                out.append((m_new, l, acc))
            return tuple(out)

        res = lax.fori_loop(jnp.maximum(i - CA_LEFT_BLOCKS, 0), i + 1, step,
                            tuple((jnp.full((TQ, 1), NEG, F32), jnp.zeros((TQ, 1), F32),
                                   jnp.zeros((TQ, HEAD_DIM), F32)) for _ in HEAD_COLS))
        for j, (hd, (m, l, acc)) in enumerate(zip(HEAD_COLS, res)):
            o = acc / l
            o_ref[:, hd] = o
            omx_ref[:, hd] = o.astype(omx_ref.dtype)
            lse_ref[j] = m + jnp.log(l)

    oblk = pl.BlockSpec((TQ, HEADS_PER_STEP * HEAD_DIM), lambda h, i: (i, h))
    return _call(
        body, (qkv, qkv, qkv, tiles), grid=(HEADS // HEADS_PER_STEP, nq),
        in_specs=_qkv_specs(s) + [pl.BlockSpec((HEADS_PER_STEP, CA_TILES, TQ, TQ), lambda h, i: (h, 0, 0, 0))],
        out_specs=[oblk, oblk, pl.BlockSpec((HEADS_PER_STEP, TQ, 1), lambda h, i: (h, i, 0))],
        out_shape=[jax.ShapeDtypeStruct((s, HEADS * HEAD_DIM), F32),
                   jax.ShapeDtypeStruct((s, HEADS * HEAD_DIM), MXU_DTYPE), jax.ShapeDtypeStruct((HEADS, s, 1), F32)],
        name=name, semantics=("parallel", "parallel"), comm=comm)


def _ca_attn_bwd(name, qkv, tiles, do, o, lse, comm=None):
    s = qkv.shape[0]
    nq = s // TQ
    scale = HEAD_DIM ** -0.5

    def body(q_ref, k_ref, v_ref, bt_ref, do_ref, o_ref, lse_ref, dq_ref, dk_ref, dv_ref, dbt_ref, dk_acc, dv_acc):
        i = pl.program_id(1)
        row, col = _iota2()

        @pl.when(i == 0)
        def _():
            dk_acc[...] = jnp.zeros_like(dk_acc)
            dv_acc[...] = jnp.zeros_like(dv_acc)
            dbt_ref[...] = jnp.zeros_like(dbt_ref)

        q = [q_ref[:, hd].astype(MXU_DTYPE) for hd in HEAD_COLS]
        delta = [jnp.sum(do_ref[:, hd] * o_ref[:, hd], -1, keepdims=True) for hd in HEAD_COLS]
        lse_ = [lse_ref[j] for j in range(HEADS_PER_STEP)]
        dob = [do_ref[:, hd].astype(MXU_DTYPE) for hd in HEAD_COLS]

        def step(kb, dqs):
            ks = pl.multiple_of(kb * TQ, TQ)
            mask = _ca_mask(i, ks, row, col)
            t = jnp.minimum(i - kb, CA_TILES - 1)
            out = []
            for j, (hd, dq) in enumerate(zip(HEAD_COLS, dqs)):
                k = k_ref[pl.ds(ks, TQ), hd].astype(MXU_DTYPE)
                v = v_ref[pl.ds(ks, TQ), hd].astype(MXU_DTYPE)
                sc = _dot(q[j], k, NT) * scale + bt_ref[j, t]
                sc = jnp.where(mask, sc, NEG)
                p = jnp.exp(sc - lse_[j])
                dsc = p * (_dot(dob[j], v, NT) - delta[j])
                dbt_ref[j, t] += dsc
                ds = (dsc * scale).astype(MXU_DTYPE)
                dk_acc[pl.ds(ks, TQ), hd] += _dot(ds, q[j], TN)
                dv_acc[pl.ds(ks, TQ), hd] += _dot(p.astype(MXU_DTYPE), dob[j], TN)
                out.append(dq + _dot(ds, k))
            return tuple(out)

        dqs = lax.fori_loop(jnp.maximum(i - CA_LEFT_BLOCKS, 0), i + 1, step,
                            tuple(jnp.zeros((TQ, HEAD_DIM), F32) for _ in HEAD_COLS))
        for hd, dq in zip(HEAD_COLS, dqs):
            dq_ref[:, hd] = dq.astype(dq_ref.dtype)

        @pl.when(i == nq - 1)
        def _():
            dk_ref[...] = dk_acc[...].astype(dk_ref.dtype)
            dv_ref[...] = dv_acc[...].astype(dv_ref.dtype)

    width = HEADS_PER_STEP * HEAD_DIM
    blk = pl.BlockSpec((TQ, width), lambda h, i: (i, h))
    col_h = pl.BlockSpec((s, width), lambda h, i: (0, h))
    tile = pl.BlockSpec((HEADS_PER_STEP, CA_TILES, TQ, TQ), lambda h, i: (h, 0, 0, 0))
    shp = jax.ShapeDtypeStruct((s, HEADS * HEAD_DIM), MXU_DTYPE)
    return _call(
        body, (qkv, qkv, qkv, tiles, do, o, lse), grid=(HEADS // HEADS_PER_STEP, nq),
        in_specs=_qkv_specs(s) + [tile, blk, blk, pl.BlockSpec((HEADS_PER_STEP, TQ, 1), lambda h, i: (h, i, 0))],
        out_specs=[blk, col_h, col_h, tile],
        out_shape=[shp, shp, shp, jax.ShapeDtypeStruct((HEADS, CA_TILES, TQ, TQ), F32)],
        scratch_shapes=[pltpu.VMEM((s, width), F32), pltpu.VMEM((s, width), F32)],
        name=name, semantics=("arbitrary", "arbitrary"), comm=comm)


def _ca_table_grad(dtiles):
    def skew(x, row):
        for b in range(7):
            x = jnp.where((jnp.right_shift(row, b) & 1) == 1, pltpu.roll(x, LANES - (1 << b), 1), x)
        return x

    def body(t_ref, o_ref):
        row = lax.broadcasted_iota(jnp.int32, (LANES, LANES), 0)
        col = lax.broadcasted_iota(jnp.int32, (LANES, LANES), 1)
        wrapped = (row + col) >= LANES
        lane = col[:1]
        total = lambda x: jnp.sum(jnp.sum(x, 0, keepdims=True), -1, keepdims=True)
        lo = jnp.zeros((1, LANES), F32)
        hi = jnp.zeros((1, LANES), F32)
        for d, a, b, o in _ca_subtiles():
            x = t_ref[d, a * LANES:(a + 1) * LANES, b * LANES:(b + 1) * LANES]
            if o >= LANES:
                hi = hi + jnp.where(lane == LAST_REL, total(x), 0.0)
            elif o <= -2 * LANES:
                lo = lo + jnp.where(lane == 0, total(x), 0.0)
            else:
                y = skew(x, row)
                pos = jnp.sum(jnp.where(wrapped, 0.0, y), 0, keepdims=True)
                neg = jnp.sum(jnp.where(wrapped, y, 0.0), 0, keepdims=True)
                if o == 0:
                    clipped = jnp.sum(jnp.where(lane > LAST_REL, pos, 0.0), -1, keepdims=True)
                    hi = hi + jnp.where(lane <= LAST_REL, pos, 0.0) + jnp.where(lane == LAST_REL, clipped, 0.0)
                    lo = lo + neg
                else:
                    lo = lo + pos + jnp.where(lane == 0, jnp.sum(neg, -1, keepdims=True), 0.0)
        o_ref[:, :LANES] = lo
        o_ref[:, LANES:] = hi

    return pl.pallas_call(
        body, grid=(HEADS,), in_specs=[pl.BlockSpec((None, CA_TILES, TQ, TQ), lambda h: (h, 0, 0, 0))],
        out_specs=pl.BlockSpec((None, 1, 2 * LANES), lambda h: (h, 0, 0)),
        out_shape=jax.ShapeDtypeStruct((HEADS, 1, 2 * LANES), F32),
        name="ca_table_grad", compiler_params=_params("parallel"))(dtiles)


def _row_tile(rows, cols):
    if rows % 128:
        return rows
    tr = 128
    while rows % (2 * tr) == 0 and 2 * tr * cols * 4 <= (1 << 20):
        tr *= 2
    return tr


def _adamw(name, w, g, m, v):
    rows, cols = w.shape
    tr = _row_tile(rows, cols)
    blk = pl.BlockSpec((tr, cols), lambda i: (i, 0))

    def body(w_ref, g_ref, m_ref, v_ref, d_ref, m2_ref, v2_ref):
        g_ = g_ref[...]
        m2 = ADAM_B1 * m_ref[...] + (1.0 - ADAM_B1) * g_
        v2 = ADAM_B2 * v_ref[...] + (1.0 - ADAM_B2) * jnp.square(g_)
        m_hat = m2 / (1.0 - ADAM_B1 ** ADAM_STEP)
        v_hat = v2 / (1.0 - ADAM_B2 ** ADAM_STEP)
        d_ref[...] = -ADAM_LR * (m_hat / (jnp.sqrt(v_hat) + ADAM_EPS) + ADAM_WD * w_ref[...])
        m2_ref[...] = m2
        v2_ref[...] = v2

    shp = jax.ShapeDtypeStruct((rows, cols), F32)
    return pl.pallas_call(body, grid=(rows // tr,), in_specs=[blk] * 4, out_specs=[blk] * 3, out_shape=[shp] * 3,
                          name=name, compiler_params=_params("parallel"))(w, g, m, v)


def _pair_sum(name, g, r1, c):
    _, rh, cols = r1.shape
    tr = _row_tile(rh, cols)
    nb = rh // tr

    def body(c_ref, g_ref, r_ref, o_ref):
        o_ref[...] = (g_ref[...].astype(F32) + r_ref[...].astype(F32)).astype(o_ref.dtype)

    return pl.pallas_call(
        body, grid_spec=pltpu.PrefetchScalarGridSpec(
            num_scalar_prefetch=1, grid=(N_CHIPS, nb),
            in_specs=[pl.BlockSpec((None, tr, cols), lambda k, i, c_ref: (k, c_ref[0] * nb + i, 0)),
                      pl.BlockSpec((None, tr, cols), lambda k, i, c_ref: (k, i, 0))],
            out_specs=pl.BlockSpec((None, tr, cols), lambda k, i, c_ref: (k, i, 0))),
        out_shape=jax.ShapeDtypeStruct(r1.shape, WIRE_DTYPE), name=name,
        compiler_params=_params("parallel", "parallel"))(c, g, r1)


def _chip_sum(name, a1, r2, place, total, l, layers):
    _, rh, cols = a1.shape
    tr = _row_tile(rh, cols)
    nb = rh // tr

    def body(p_ref, a_ref, r_ref, *rest):
        rest[-1][...] = ((a_ref[...].astype(F32) + r_ref[0].astype(F32)) + r_ref[1].astype(F32)) + r_ref[2].astype(F32)

    return pl.pallas_call(
        body, grid_spec=pltpu.PrefetchScalarGridSpec(
            num_scalar_prefetch=1, grid=(nb,),
            in_specs=[pl.BlockSpec((None, tr, cols), lambda i, p_ref: (p_ref[0], i, 0)),
                      pl.BlockSpec((N_CHIPS - 1, tr, cols), lambda i, p_ref: (0, i, 0))] + ([] if total is None else [ANY]),
            out_specs=pl.BlockSpec((tr, cols), lambda i, p_ref: ((2 * l + p_ref[1]) * nb + i, 0))),
        out_shape=jax.ShapeDtypeStruct((layers * 2 * rh, cols), F32), name=name,
        input_output_aliases={} if total is None else {3: 0},
        compiler_params=_params("parallel"))(place, a1, r2, *([] if total is None else [total]))


def _cast_place(name, w, chip, l, rows):
    cols = w.shape[1]
    tr = _row_tile(rows, cols)
    nb = rows // tr

    def body(c_ref, w_ref, o_ref):
        o_ref[...] = w_ref[...].astype(o_ref.dtype)

    return pl.pallas_call(
        body, grid_spec=pltpu.PrefetchScalarGridSpec(
            num_scalar_prefetch=1, grid=(nb,),
            in_specs=[pl.BlockSpec((tr, cols), lambda i, c_ref: (l * nb + i, 0))],
            out_specs=pl.BlockSpec((None, tr, cols), lambda i, c_ref: (c_ref[0], i, 0))),
        out_shape=jax.ShapeDtypeStruct((N_CHIPS, rows, cols), MXU_DTYPE), name=name,
        compiler_params=_params("parallel"))(chip, w)


def _place():
    x, y, c = lax.axis_index("x"), lax.axis_index("y"), lax.axis_index("c")
    chips = [(1 - x, y), (x, 1 - y), (1 - x, 1 - y)]
    return x, y, c, chips


def _remote(src, dst, send_sem, recv_sem, to):
    return pltpu.make_async_remote_copy(src_ref=src, dst_ref=dst, send_sem=send_sem, recv_sem=recv_sem,
                                        device_id=to, device_id_type=MESH)


def _all_gather(placed):
    n = len(placed)

    def body(*refs):
        outs = refs[n:2 * n]
        send, recv = refs[2 * n:]
        x, y, c, chips = _place()
        me, sib = 2 * x + y, (x, y, 1 - c)

        def half(t, which):
            return pl.ds(which * (outs[t].shape[1] // 2), outs[t].shape[1] // 2)

        first, passed = [], []
        for t in range(n):
            for j, chip in enumerate(chips):
                mine = outs[t].at[me, half(t, c)]
                cp = _remote(mine, mine, send.at[6 * t + j], recv.at[6 * t + j], (*chip, c))
                cp.start()
                first.append(cp)
        for t in range(n):
            for j, (px, py) in enumerate(chips):
                got = outs[t].at[2 * px + py, half(t, c)]
                _remote(got, got, send.at[6 * t + j], recv.at[6 * t + j], (px, py, c)).wait_recv()
                cp = _remote(got, got, send.at[6 * t + 3 + j], recv.at[6 * t + 3 + j], sib)
                cp.start()
                passed.append(cp)
        for t in range(n):
            for j, (px, py) in enumerate(chips):
                got = outs[t].at[2 * px + py, half(t, 1 - c)]
                _remote(got, got, send.at[6 * t + 3 + j], recv.at[6 * t + 3 + j], sib).wait_recv()
        for cp in first + passed:
            cp.wait_send()

    return pl.pallas_call(
        body, in_specs=[ANY] * n, out_specs=[ANY] * n,
        out_shape=[jax.ShapeDtypeStruct(p.shape, p.dtype) for p in placed],
        scratch_shapes=[pltpu.SemaphoreType.DMA((6 * n,)), pltpu.SemaphoreType.DMA((6 * n,))],
        input_output_aliases={t: t for t in range(n)},
        name="weight_all_gather")(*placed)


def _in_place(bufs):
    return [jax.ShapeDtypeStruct(b.shape, b.dtype) for b in bufs], {t: t for t in range(len(bufs))}


def _gather_ici(bufs):
    def copies(srcs, dsts, send, recv):
        x, y, c, chips = _place()
        out = []
        for t, buf in enumerate(dsts):
            rh = buf.shape[1] // 2
            mine = buf.at[2 * x + y, pl.ds(c * rh, rh)]
            out += [_remote(mine, mine, send.at[3 * t + j], recv.at[3 * t + j], (*chip, c))
                    for j, chip in enumerate(chips)]
        return out

    return _Comm(bufs, *_in_place(bufs), copies, 3 * len(bufs))


def _gather_pass(bufs):
    def copies(srcs, dsts, send, recv):
        x, y, c, chips = _place()
        out = []
        for t, buf in enumerate(dsts):
            rh = buf.shape[1] // 2
            for j, (px, py) in enumerate(chips):
                got = buf.at[2 * px + py, pl.ds(c * rh, rh)]
                out.append(_remote(got, got, send.at[3 * t + j], recv.at[3 * t + j], (x, y, 1 - c)))
        return out

    return _Comm(bufs, *_in_place(bufs), copies, 3 * len(bufs))


def _reduce_pair(grads):
    def copies(srcs, dsts, send, recv):
        x, y, c, _ = _place()
        out = []
        for t, g in enumerate(srcs):
            rh = g.shape[1] // 2
            out.append(_remote(g.at[:, pl.ds((1 - c) * rh, rh)], dsts[t], send.at[t], recv.at[t], (x, y, 1 - c)))
        return out

    return _Comm(grads, [jax.ShapeDtypeStruct((N_CHIPS, g.shape[1] // 2, g.shape[2]), g.dtype) for g in grads], {},
                 copies, len(grads))


def _reduce_chips(parts):
    def copies(srcs, dsts, send, recv):
        x, y, c, chips = _place()
        return [_remote(p.at[2 * px + py], dsts[t].at[j], send.at[3 * t + j], recv.at[3 * t + j], (px, py, c))
                for t, p in enumerate(srcs) for j, (px, py) in enumerate(chips)]

    return _Comm(parts, [jax.ShapeDtypeStruct((N_CHIPS - 1, *p.shape[1:]), p.dtype) for p in parts], {}, copies,
                 3 * len(parts))


def _reduce_share(totals, spans):
    def copies(srcs, dsts, send, recv):
        x, y, c, _ = _place()
        out = []
        for t, (buf, (start, rh)) in enumerate(zip(dsts, spans)):
            mine = buf.at[pl.ds(start + c * rh, rh)]
            out.append(_remote(mine, mine, send.at[t], recv.at[t], (x, y, 1 - c)))
        return out

    return _Comm(totals, *_in_place(totals), copies, len(totals))


def _exchange(name, comm):
    ns, nd = len(comm.srcs), len(comm.dsts)

    def body(*refs):
        send, recv = refs[ns + nd:]
        cps = comm.copies(refs[:ns], refs[ns:ns + nd], send, recv)
        for cp in cps:
            cp.start()
        for cp in cps:
            cp.wait()

    comm.deliver(pl.pallas_call(
        body, in_specs=[ANY] * ns, out_specs=[ANY] * nd, out_shape=comm.dsts,
        scratch_shapes=[pltpu.SemaphoreType.DMA((comm.n,)), pltpu.SemaphoreType.DMA((comm.n,))],
        input_output_aliases=comm.alias, name=name)(*comm.srcs))
    return comm.out


def _all_reduce_small(name, pack):
    rows, cols = pack.shape

    def body(p_ref, o_ref, slots, send, recv):
        x, y, c, _ = _place()
        me = 4 * x + 2 * y + c
        slots[me] = p_ref[...]
        cps = []
        for r in range(1, N_DEV):
            to = ((1 - x) if r & 4 else x, (1 - y) if r & 2 else y, (1 - c) if r & 1 else c)
            cp = _remote(p_ref, slots.at[me], send.at[r - 1], recv.at[r - 1], to)
            cp.start()
            cps.append(cp)
        for cp in cps:
            cp.wait()
        acc = slots[0]
        for d in range(1, N_DEV):
            acc = acc + slots[d]
        o_ref[...] = acc

    return pl.pallas_call(
        body, in_specs=[VMEM_SPEC], out_specs=VMEM_SPEC, out_shape=jax.ShapeDtypeStruct((rows, cols), F32),
        scratch_shapes=[pltpu.VMEM((N_DEV, rows, cols), F32), pltpu.SemaphoreType.DMA((N_DEV - 1,)),
                        pltpu.SemaphoreType.DMA((N_DEV - 1,))],
        name=name)(pack)


BIG = ("ffn_w_in", "ffn_w_out", "mla_w_down", "mla_w_uq", "mla_w_ukv", "mla_w_o", "sb_w_qkv", "sb_w_o", "ca_w_qkv",
       "ca_w_o")
COL_SHARDED = {"ffn_w_in": True, "ffn_w_out": False, "mla_w_down": False, "mla_w_uq": True, "mla_w_ukv": True,
               "mla_w_o": False, "sb_w_qkv": True, "sb_w_o": False, "ca_w_qkv": True, "ca_w_o": False}
WEIGHTS = ("ln_mix_g", "ln_mix_b", "ln_ffn_g", "ln_ffn_b", "ffn_w_in", "ffn_w_out", "mla_w_down", "mla_q_norm_g",
           "mla_w_uq", "mla_kv_norm_g", "mla_w_ukv", "mla_w_o", "sb_w_qkv", "sb_w_o", "ca_w_qkv", "ca_rel_bias",
           "ca_w_o")
HEADS_PER_CHIP = HEADS // N_CHIPS


def _mxu_shards(w):
    down = w["mla_w_down"]
    uq = w["mla_w_uq"]
    n, ql = uq.shape[:2]
    lane_pad = 128 - MLA_ROPE
    shaped = dict(w)
    shaped["mla_w_down"] = jnp.pad(down, ((0, 0), (0, 0), (0, lane_pad)))
    shaped["mla_w_uq"] = jnp.pad(uq.reshape(n, ql, HEADS_PER_CHIP, MLA_QK_DIM),
                                 ((0, 0), (0, 0), (0, 0), (0, HEAD_PAD - MLA_QK_DIM))).reshape(n, ql, -1)
    return {k: shaped[k].reshape(-1, shaped[k].shape[-1]) for k in BIG}


def _unpad_grad(name, g, like):
    if name == "mla_w_down":
        g = g[:, :like.shape[-1]]
    elif name == "mla_w_uq":
        g = g.reshape(g.shape[0], HEADS_PER_CHIP, HEAD_PAD)[:, :, :MLA_QK_DIM]
    return g.reshape(like.shape)


def _mixer_keys(i):
    kind, slot = i % 3, i // 3
    if kind == 0:
        return [(k, slot) for k in ("mla_w_down", "mla_w_uq", "mla_w_ukv", "mla_w_o")]
    pre = "sb" if kind == 1 else "ca"
    return [(f"{pre}_w_qkv", slot), (f"{pre}_w_o", slot)]


def _ffn_keys(i):
    return [("ffn_w_in", i), ("ffn_w_out", i)]


def _step(x, target, wl, dims, p, tabs, place):
    dist = place is not None
    wl = dict(wl)

    def carried(make, keys):
        return make([wl[k] for k in keys]) if dist and keys else None

    def landed(keys, comm):
        if comm is not None:
            wl.update(zip(keys, comm.out))

    if dist:
        first = _mixer_keys(0)
        wl.update(zip(first, _all_gather([wl[k] for k in first])))
    saved = []
    h = hm = x
    for i in range(DEPTH):
        kind, slot = i % 3, i // 3
        sv = {"h0m": hm}
        ffn, nxt = _ffn_keys(i), (_mixer_keys(i + 1) if i + 1 < DEPTH else [])
        ahead = ffn[:1] if kind == 2 else []
        rest = ffn[len(ahead):]
        ici0 = carried(_gather_ici, ahead)
        if kind == 0:
            ici, pass0 = carried(_gather_ici, rest), None
            down =_fwd_row(f"mla_down_{i}", hm, wl["mla_w_down", slot], F32)
            gq, gkv = p["mla_q_norm_g"][slot][None], p["mla_kv_norm_g"][slot][None]
            cq, ckv, kr = _mla_mid_fwd(down, gq, gkv, tabs)
            q = _fwd_col(f"mla_uq_{i}", cq, wl["mla_w_uq", slot], F32)
            kv = _fwd_col(f"mla_ukv_{i}", ckv, wl["mla_w_ukv", slot], MXU_DTYPE)
            o, om, lse = _mla_attn_fwd(f"mla_attn_fwd_{i}", q, kv, kr, tabs, comm=_merged(ici, pass0))
            sv.update(down=down, gq=gq, gkv=gkv, cq=cq, ckv=ckv, kr=kr, q=q, kv=kv, o=o, lse=lse)
        else:
            qkv = _fwd_col(f"{'sb' if kind == 1 else 'ca'}_qkv_{i}", hm, wl[_mixer_keys(i)[0]], MXU_DTYPE, comm=ici0)
            landed(ahead, ici0)
            ici, pass0 = carried(_gather_ici, rest), carried(_gather_pass, ahead)
            if kind == 1:
                om = _sb_attn_fwd(f"sb_attn_fwd_{i}", qkv, comm=_merged(ici, pass0))
            else:
                tiles = _ca_bias_tiles(p["ca_rel_bias"][slot])
                o, om, lse = _ca_attn_fwd(f"ca_attn_fwd_{i}", qkv, tiles, comm=_merged(ici, pass0))
                sv.update(tiles=tiles, lse=lse, o=o)
            sv.update(qkv=qkv)
        landed(rest, ici)
        landed(ahead, pass0)
        fwd = carried(_gather_pass, rest)
        m = _fwd_row(f"mixer_o_{i}", om, wl[_mixer_keys(i)[-1]], F32, comm=fwd)
        landed(rest, fwd)
        h1, h1m, sv["xh1"], sv["r1"] = _ln_fwd(f"ln_mix_{i}", h, m, p["ln_mix_g"][i][None], p["ln_mix_b"][i][None])
        ici = carried(_gather_ici, nxt)
        u, a = _fwd_col(f"ffn_in_{i}", h1m, wl["ffn_w_in", i], None, epilogue=_epi_relu2, dtypes=(F32, MXU_DTYPE),
                        comm=ici)
        landed(nxt, ici)
        fwd = carried(_gather_pass, nxt)
        y = _fwd_row(f"ffn_out_{i}", a, wl["ffn_w_out", i], F32, comm=fwd)
        landed(nxt, fwd)
        h, hm, sv["xh2"], sv["r2"] = _ln_fwd(f"ln_ffn_{i}", h1, y, p["ln_ffn_g"][i][None], p["ln_ffn_b"][i][None])
        sv.update(om=om, h1m=h1m, u=u, a=a)
        saved.append(sv)

    part, pair, total = {}, {}, {}

    def dw(key, xin, dyin):
        _, rows, cols = dims[key[0]]
        part[key] = _dw(f"dw_{key[0]}_{key[1]}", xin, dyin, rows, cols, COL_SHARDED[key[0]])

    def to_sibling(keys):
        return _reduce_pair([part[k] for k in keys]) if dist and keys else None

    def pair_sums(keys, comm):
        for k, r1 in zip(keys, comm.out if comm is not None else ()):
            pair[k] = _pair_sum(f"pair_sum_{k[0]}_{k[1]}", part[k], r1, place[1:])

    def to_chips(keys):
        return _reduce_chips([pair[k] for k in keys]) if dist and keys else None

    def chip_sums(keys, comm):
        for (name, l), r2 in zip(keys, comm.out if comm is not None else ()):
            total[name] = _chip_sum(f"chip_sum_{name}_{l}", pair[name, l], r2, place, total.get(name), l,
                                    dims[name][0])

    def share(keys):
        spans = [(l * dims[name][1], dims[name][1] // 2) for name, l in keys]
        return _reduce_share([total[name] for name, _ in keys], spans) if dist and keys else None

    def shared(keys, comm):
        if comm is not None:
            total.update(zip([name for name, _ in keys], comm.out))

    loss, dy = _loss_head(h, target)
    small = {k: [None] * DEPTH for k in ("ln_mix_g", "ln_mix_b", "ln_ffn_g", "ln_ffn_b")}
    n_mla = p["mla_q_norm_g"].shape[0]
    small["mla_q_norm_g"], small["mla_kv_norm_g"] = [None] * n_mla, [None] * n_mla
    for i in reversed(range(DEPTH)):
        kind, slot = i % 3, i // 3
        sv = saved[i]
        later = _mixer_keys(i + 1) if i + 1 < DEPTH else []
        dz, dzm, small["ln_ffn_g"][i], small["ln_ffn_b"][i] = _ln_bwd(f"ln_ffn_bwd_{i}", dy, sv["xh2"], sv["r2"],
                                                                      p["ln_ffn_g"][i][None])
        dw(("ffn_w_out", i), sv["a"], dzm)
        early = [("ffn_w_out", i)] + later
        sib = to_sibling(early)
        du = _dx_row(f"ffn_du_{i}", dzm, wl["ffn_w_out", i], MXU_DTYPE, epilogue=_epi_drelu2, extra=sv["u"], comm=sib)
        pair_sums(early, sib)
        dw(("ffn_w_in", i), sv["h1m"], du)
        sib = to_sibling([("ffn_w_in", i)])
        sooner = later if kind == 2 else []
        ici0 = to_chips(sooner)
        dy = _dx_col(f"ffn_dh_{i}", du, wl["ffn_w_in", i], F32, epilogue=_epi_residual, extra=dz,
                     comm=_merged(sib, ici0))
        pair_sums([("ffn_w_in", i)], sib)
        chip_sums(sooner, ici0)
        ready = early + [("ffn_w_in", i)]
        behind = [k for k in ready if k not in sooner]
        dz, dzm, small["ln_mix_g"][i], small["ln_mix_b"][i] = _ln_bwd(f"ln_mix_bwd_{i}", dy, sv["xh1"], sv["r1"],
                                                                      p["ln_mix_g"][i][None])
        ici = to_chips(behind)
        dw(_mixer_keys(i)[-1], sv["om"], dzm)
        do = _dx_row(f"mixer_do_{i}", dzm, wl[_mixer_keys(i)[-1]], F32)
        if kind == 0:
            dq, dkv, dkr = _mla_attn_bwd(f"mla_attn_bwd_{i}", sv["q"], sv["kv"], sv["kr"], tabs, do, sv["o"],
                                         sv["lse"], comm=ici)
            chip_sums(behind, ici)
            dw(("mla_w_uq", slot), sv["cq"], dq)
            dcq = _dx_col(f"mla_dcq_{i}", dq, wl["mla_w_uq", slot], F32)
            dw(("mla_w_ukv", slot), sv["ckv"], dkv)
            dckv = _dx_col(f"mla_dckv_{i}", dkv, wl["mla_w_ukv", slot], F32)
            ddown, small["mla_q_norm_g"][slot], small["mla_kv_norm_g"][slot] = _mla_mid_bwd(
                sv["down"], dcq, dckv, dkr, sv["gq"], sv["gkv"], tabs)
            dw(("mla_w_down", slot), sv["h0m"], ddown)
            both = share(ready)
            dy = _dx_row(f"mla_dh_{i}", ddown, wl["mla_w_down", slot], F32, epilogue=_epi_residual, extra=dz,
                         comm=both)
        else:
            pre = "sb" if kind == 1 else "ca"
            if kind == 1:
                dq, dk, dv = _sb_attn_bwd(f"sb_attn_bwd_{i}", sv["qkv"], do, comm=ici)
            else:
                dq, dk, dv, dtiles = _ca_attn_bwd(f"ca_attn_bwd_{i}", sv["qkv"], sv["tiles"], do, sv["o"], sv["lse"],
                                                  comm=ici)
                small["ca_rel_bias"] = [jnp.transpose(_ca_table_grad(dtiles)[:, 0, :REL_TABLE])]
            chip_sums(behind, ici)
            dqkv = jnp.concatenate([dq, dk, dv], 1)
            dw((f"{pre}_w_qkv", slot), sv["h0m"], dqkv)
            both = share(ready)
            dy = _dx_col(f"{pre}_dh_{i}", dqkv, wl[f"{pre}_w_qkv", slot], F32, epilogue=_epi_residual, extra=dz,
                         comm=both)
        shared(ready, both)
    small = {k: jnp.stack([g.reshape(g.shape[-2:]) if k == "ca_rel_bias" else g[0] for g in v]) for k, v in small.items()}
    if not dist:
        return loss, dy, part, small
    last = _mixer_keys(0)
    sib = to_sibling(last)
    _exchange("grad_pair_last", sib)
    pair_sums(last, sib)
    ici = to_chips(last)
    _exchange("grad_chips_last", ici)
    chip_sums(last, ici)
    both = share(last)
    _exchange("grad_share_last", both)
    shared(last, both)
    return loss, dy, total, small


SMALL = ("ln_mix_g", "ln_mix_b", "ln_ffn_g", "ln_ffn_b", "mla_q_norm_g", "mla_kv_norm_g", "ca_rel_bias")


def _pack_small(parts, width):
    flat = jnp.concatenate([parts[k].reshape(-1) for k in SMALL])
    rows = -(-flat.shape[0] // width)
    rows += -rows % 8
    return jnp.pad(flat, (0, rows * width - flat.shape[0])).reshape(rows, width)


def _unpack_small(pack, like):
    flat, out, at = pack.reshape(-1), {}, 0
    for k in SMALL:
        n = int(np.prod(like[k].shape))
        out[k] = flat[at:at + n].reshape(like[k].shape)
        at += n
    return out


def kernel(x, ln_mix_g, ln_mix_b, ln_ffn_g, ln_ffn_b, ffn_w_in, ffn_w_out, mla_w_down, mla_q_norm_g, mla_w_uq, mla_kv_norm_g, mla_w_ukv, mla_w_o, sb_w_qkv, sb_w_o, ca_w_qkv, ca_rel_bias, ca_w_o, loss_target, m_ln_mix_g, m_ln_mix_b, m_ln_ffn_g, m_ln_ffn_b, m_ffn_w_in, m_ffn_w_out, m_mla_w_down, m_mla_q_norm_g, m_mla_w_uq, m_mla_kv_norm_g, m_mla_w_ukv, m_mla_w_o, m_sb_w_qkv, m_sb_w_o, m_ca_w_qkv, m_ca_rel_bias, m_ca_w_o, v_ln_mix_g, v_ln_mix_b, v_ln_ffn_g, v_ln_ffn_b, v_ffn_w_in, v_ffn_w_out, v_mla_w_down, v_mla_q_norm_g, v_mla_w_uq, v_mla_kv_norm_g, v_mla_w_ukv, v_mla_w_o, v_sb_w_qkv, v_sb_w_o, v_ca_w_qkv, v_ca_rel_bias, v_ca_w_o):
    w = dict(zip(WEIGHTS, (ln_mix_g, ln_mix_b, ln_ffn_g, ln_ffn_b, ffn_w_in, ffn_w_out, mla_w_down, mla_q_norm_g,
                           mla_w_uq, mla_kv_norm_g, mla_w_ukv, mla_w_o, sb_w_qkv, sb_w_o, ca_w_qkv, ca_rel_bias,
                           ca_w_o)))
    mom1 = dict(zip(WEIGHTS, (m_ln_mix_g, m_ln_mix_b, m_ln_ffn_g, m_ln_ffn_b, m_ffn_w_in, m_ffn_w_out, m_mla_w_down,
                              m_mla_q_norm_g, m_mla_w_uq, m_mla_kv_norm_g, m_mla_w_ukv, m_mla_w_o, m_sb_w_qkv,
                              m_sb_w_o, m_ca_w_qkv, m_ca_rel_bias, m_ca_w_o)))
    mom2 = dict(zip(WEIGHTS, (v_ln_mix_g, v_ln_mix_b, v_ln_ffn_g, v_ln_ffn_b, v_ffn_w_in, v_ffn_w_out, v_mla_w_down,
                              v_mla_q_norm_g, v_mla_w_uq, v_mla_kv_norm_g, v_mla_w_ukv, v_mla_w_o, v_sb_w_qkv,
                              v_sb_w_o, v_ca_w_qkv, v_ca_rel_bias, v_ca_w_o)))
    xi, yi, ci = lax.axis_index("x"), lax.axis_index("y"), lax.axis_index("c")
    chip = 2 * xi + yi
    d_model = x.shape[-1]

    shards = _mxu_shards(w)
    layers = {k: w[k].shape[0] for k in BIG}
    dims = {k: (layers[k], shards[k].shape[0] // layers[k], shards[k].shape[1]) for k in BIG}
    chip1 = jnp.reshape(chip, (1,)).astype(jnp.int32)
    wl = {(k, l): _cast_place(f"cast_{k}_{l}", shards[k], chip1, l, dims[k][1]) for k in BIG for l in range(layers[k])}
    gains = jnp.stack([w["mla_q_norm_g"], w["mla_kv_norm_g"]])
    gains = jnp.where(ci == 0, gains, 0.0)
    placed = lax.dynamic_update_slice_in_dim(jnp.zeros((*gains.shape[:2], N_CHIPS, gains.shape[2]), F32),
                                             gains[:, :, None], chip, 2)
    full_gains = _all_reduce_small("norm_gain_gather", placed.reshape(2 * gains.shape[1], -1))
    full_gains = full_gains.reshape(2, gains.shape[1], -1)
    p = {"ln_mix_g": ln_mix_g, "ln_mix_b": ln_mix_b, "ln_ffn_g": ln_ffn_g, "ln_ffn_b": ln_ffn_b,
         "mla_q_norm_g": full_gains[0], "mla_kv_norm_g": full_gains[1], "ca_rel_bias": ca_rel_bias}

    place = jnp.stack([chip, ci]).astype(jnp.int32)
    loss, grad_x, total, small = _step(x[0], loss_target[0], wl, dims, p, _rope_tables(x.shape[1]), place)
    loss = lax.psum(loss[0, 0], ("x", "y", "c"))

    small = _unpack_small(_all_reduce_small("small_grad_all_reduce", _pack_small(small, d_model)), small)
    grad = {k: small[k] for k in ("ln_mix_g", "ln_mix_b", "ln_ffn_g", "ln_ffn_b", "ca_rel_bias")}
    for k in ("mla_q_norm_g", "mla_kv_norm_g"):
        g = small[k].reshape(small[k].shape[0], N_CHIPS, -1)
        grad[k] = lax.dynamic_index_in_dim(g, chip, 1, keepdims=False)

    for k in BIG:
        grad[k] = _unpad_grad(k, total[k], w[k])

    delta, new_m, new_v = {}, {}, {}
    for k in WEIGHTS:
        flat = lambda a: a.reshape(-1, a.shape[-1])
        dl, m2, v2 = _adamw(f"adamw_{k}", flat(w[k]), flat(grad[k]), flat(mom1[k]), flat(mom2[k]))
        delta[k], new_m[k], new_v[k] = dl.reshape(w[k].shape), m2.reshape(w[k].shape), v2.reshape(w[k].shape)
    return (loss, grad_x[None], *[grad[k] for k in WEIGHTS], *[delta[k] for k in WEIGHTS],
            *[new_m[k] for k in WEIGHTS], *[new_v[k] for k in WEIGHTS])
```

```python
import functools

import numpy as np
import jax
import jax.numpy as jnp
from jax import lax
from jax.experimental import pallas as pl
from jax.experimental.pallas import tpu as pltpu

F32, BF16 = jnp.float32, jnp.bfloat16
MXU_DTYPE = BF16
WIRE_DTYPE = BF16

DEPTH = 4
HEADS = 16
HEAD_DIM = 128
CHUNK_SHIFT = 6
TQ = 512
MLA_ROPE = 64
MLA_QK_DIM = 192
HEAD_PAD = 256
CA_LEFT_CHUNKS = 8
CA_LEFT_BLOCKS = (CA_LEFT_CHUNKS << CHUNK_SHIFT) // TQ
CA_TILES = min(CA_LEFT_BLOCKS, 2) + 1
REL_CLIP_LEFT = 128
REL_TABLE = 192
ROPE_THETA = 10000.0
LN_EPS = 1e-5
RMS_EPS = 1e-6
ALPHA = (2.0 * DEPTH) ** 0.25
NEG = -1e30
ADAM_LR, ADAM_B1, ADAM_B2, ADAM_EPS, ADAM_WD, ADAM_STEP = 0.001, 0.9, 0.999, 1e-08, 0.01, 10
N_CHIPS = 4
N_DEV = 8
VMEM_LIMIT = 48 << 20
MXU_FLOPS_PER_US = 0.9e9
ICI_BYTES_PER_US = 69.9e3
CHUNK_US = 45.0
LINK_SHARE = 1.2
ATTN_US = {"mla_attn_fwd": 155.0, "mla_attn_bwd": 295.0, "sb_attn_fwd": 300.0, "sb_attn_bwd": 610.0,
           "ca_attn_fwd": 100.0, "ca_attn_bwd": 175.0}
MESH = pl.DeviceIdType.MESH
ANY = pl.BlockSpec(memory_space=pl.ANY)
VMEM_SPEC = pl.BlockSpec(memory_space=pltpu.VMEM)

NN = (((1,), (0,)), ((), ()))
NT = (((1,), (1,)), ((), ()))
TN = (((0,), (0,)), ((), ()))


def _dot(a, b, dims=NN):
    return lax.dot_general(a, b, dims, preferred_element_type=F32)


def _exact_dot(x, u):
    hi = x.astype(BF16)
    r1 = x - hi.astype(F32)
    mid = r1.astype(BF16)
    lo = (r1 - mid.astype(F32)).astype(BF16)
    return _dot(hi, u) + _dot(mid, u) + _dot(lo, u)


def _params(*sem):
    return pltpu.CompilerParams(dimension_semantics=sem, vmem_limit_bytes=VMEM_LIMIT)


TILE_K = 2048


def _tile(n, pref):
    for t in (2048, 1536, 1152, 1024, 768, 512, 384, 256, 128):
        if t <= pref and n % t == 0:
            return t
    return n


class _Comm:
    def __init__(self, srcs, dsts, alias, copies, n):
        self.srcs, self.dsts, self.alias, self.copies, self.n = list(srcs), list(dsts), dict(alias), copies, n
        self.out, self.then = None, None

    def deliver(self, out):
        self.out = list(out)
        if self.then is not None:
            self.then(self.out)


class _SemView:
    def __init__(self, sems, base):
        self.sems, self.base, self.at = sems, base, self

    def __getitem__(self, i):
        return self.sems.at[self.base + i]


class _Merged(_Comm):
    def __init__(self, parts):
        srcs, dsts, alias, n, self.spans = [], [], {}, 0, []
        for c in parts:
            self.spans.append((c, len(srcs), len(dsts), n))
            alias.update({len(srcs) + s: len(dsts) + d for s, d in c.alias.items()})
            srcs, dsts, n = srcs + c.srcs, dsts + c.dsts, n + c.n

        def copies(src_refs, dst_refs, send, recv):
            out = []
            for c, s0, d0, n0 in self.spans:
                out += c.copies(src_refs[s0:s0 + len(c.srcs)], dst_refs[d0:d0 + len(c.dsts)], _SemView(send, n0),
                                _SemView(recv, n0))
            return out

        super().__init__(srcs, dsts, alias, copies, n)

    def deliver(self, out):
        self.out = list(out)
        for c, _, d0, _ in self.spans:
            c.deliver(self.out[d0:d0 + len(c.dsts)])


def _merged(*comms):
    comms = [c for c in comms if c is not None]
    return None if not comms else comms[0] if len(comms) == 1 else _Merged(comms)


def _call(body, args, *, name, grid, in_specs, out_specs, out_shape, scratch_shapes=(), semantics, comm=None, us=0.0):
    in_specs, out_specs, out_shape = list(in_specs), list(out_specs), list(out_shape)
    scratch_shapes = list(scratch_shapes)
    if callable(comm):
        comm = comm(us or ATTN_US.get(name.rsplit("_", 1)[0], 0.0))
    if comm is None:
        return pl.pallas_call(body, grid=grid, in_specs=in_specs, out_specs=out_specs, out_shape=out_shape,
                              scratch_shapes=scratch_shapes, name=name, compiler_params=_params(*semantics))(*args)
    n_in, n_out, n_scr, ns, nd = len(in_specs), len(out_specs), len(scratch_shapes), len(comm.srcs), len(comm.dsts)

    def carrier(*refs):
        ins, refs = refs[:n_in], refs[n_in:]
        srcs, refs = refs[:ns], refs[ns:]
        outs, refs = refs[:n_out], refs[n_out:]
        dsts, refs = refs[:nd], refs[nd:]
        scratch, (send, recv) = refs[:n_scr], refs[n_scr:]
        ids = [pl.program_id(a) for a in range(len(grid))]
        first = functools.reduce(jnp.logical_and, [i == 0 for i in ids])
        last = functools.reduce(jnp.logical_and, [i == g - 1 for i, g in zip(ids, grid)])

        @pl.when(first)
        def _():
            for cp in comm.copies(srcs, dsts, send, recv):
                cp.start()

        body(*ins, *outs, *scratch)

        @pl.when(last)
        def _():
            for cp in comm.copies(srcs, dsts, send, recv):
                cp.wait()

    res = pl.pallas_call(
        carrier, grid=grid, in_specs=in_specs + [ANY] * ns, out_specs=out_specs + [ANY] * nd,
        out_shape=out_shape + comm.dsts,
        scratch_shapes=scratch_shapes + [pltpu.SemaphoreType.DMA((comm.n,)), pltpu.SemaphoreType.DMA((comm.n,))],
        input_output_aliases={n_in + s: n_out + d for s, d in comm.alias.items()}, name=name,
        compiler_params=_params(*["arbitrary"] * len(grid)))(*args, *comm.srcs)
    comm.deliver(res[n_out:])
    return res[:n_out]


def _mm(name, a, b, extras, *, grid, a_spec, b_spec, extra_specs, out_specs, out_shape, dims,
        epilogue, acc_shape, comm=None):
    nk = grid[2]
    n_ex = len(extras)

    def product(a_ref, b_ref):
        return lax.dot_general(a_ref[...].astype(MXU_DTYPE), b_ref[...].astype(MXU_DTYPE), dims,
                               preferred_element_type=F32)

    def whole(*refs):
        epilogue(product(*refs[:2]), refs[2:2 + n_ex], refs[2 + n_ex:])

    def stepped(*refs):
        a_ref, b_ref = refs[:2]
        ex = refs[2:2 + n_ex]
        outs = refs[2 + n_ex:-1]
        acc = refs[-1]
        k = pl.program_id(2)

        @pl.when(k == 0)
        def _():
            acc[...] = product(a_ref, b_ref)

        @pl.when(k > 0)
        def _():
            acc[...] += product(a_ref, b_ref)

        @pl.when(k == nk - 1)
        def _():
            epilogue(acc[...], ex, outs)

    flops = 2.0 * grid[0] * grid[1] * grid[2] * acc_shape[1] * a_spec.block_shape[-1] * a_spec.block_shape[-2]
    return _call(whole if nk == 1 else stepped, (a, b, *extras), name=name, grid=grid,
                 in_specs=[a_spec, b_spec, *extra_specs], out_specs=out_specs, out_shape=out_shape,
                 scratch_shapes=[] if nk == 1 else [pltpu.VMEM(acc_shape, F32)],
                 semantics=("parallel", "parallel", "arbitrary"), comm=comm, us=flops / MXU_FLOPS_PER_US)


def _epi_store(acc, ex, outs):
    outs[0][...] = acc.astype(outs[0].dtype)


def _epi_relu2(acc, ex, outs):
    outs[0][...] = acc
    r = jnp.maximum(acc, 0.0)
    outs[1][...] = (r * r).astype(outs[1].dtype)


def _epi_drelu2(acc, ex, outs):
    outs[0][...] = (acc * (2.0 * jnp.maximum(ex[0][...], 0.0))).astype(outs[0].dtype)


def _epi_residual(acc, ex, outs):
    outs[0][...] = acc + ALPHA * ex[0][...]


def _fwd_col(name, x, wg, dtype, epilogue=_epi_store, dtypes=None, comm=None):
    m, (_, rows, cols) = x.shape[0], wg.shape
    tm, tn, tk = _tile(m, 1024), _tile(cols, 1024), _tile(rows, TILE_K)
    nps = cols // tn
    dtypes = dtypes or (dtype,)
    out = pl.BlockSpec((tm, tn), lambda i, j, k: (i, j))
    res = _mm(name, x, wg, (), grid=(m // tm, N_CHIPS * nps, rows // tk),
              a_spec=pl.BlockSpec((tm, tk), lambda i, j, k: (i, k)),
              b_spec=pl.BlockSpec((None, tk, tn), lambda i, j, k: (j // nps, k, j % nps)),
              extra_specs=(), out_specs=[out] * len(dtypes),
              out_shape=[jax.ShapeDtypeStruct((m, N_CHIPS * cols), d) for d in dtypes],
              dims=NN, epilogue=epilogue, acc_shape=(tm, tn), comm=comm)
    return res if len(dtypes) > 1 else res[0]


def _fwd_row(name, x, wg, dtype, comm=None):
    m, (_, rows, cols) = x.shape[0], wg.shape
    tm, tn, tk = _tile(m, 1024), _tile(cols, 1024), _tile(rows, TILE_K)
    kps = rows // tk
    return _mm(name, x, wg, (), grid=(m // tm, cols // tn, N_CHIPS * kps),
               a_spec=pl.BlockSpec((tm, tk), lambda i, j, k: (i, k)),
               b_spec=pl.BlockSpec((None, tk, tn), lambda i, j, k: (k // kps, k % kps, j)),
               extra_specs=(), out_specs=[pl.BlockSpec((tm, tn), lambda i, j, k: (i, j))],
               out_shape=[jax.ShapeDtypeStruct((m, cols), dtype)],
               dims=NN, epilogue=_epi_store, acc_shape=(tm, tn), comm=comm)[0]


def _dx_col(name, dy, wg, dtype, epilogue=_epi_store, extra=None, comm=None):
    m, (_, rows, cols) = dy.shape[0], wg.shape
    tm, tn, tk = _tile(m, 1024), _tile(rows, 1024), _tile(cols, TILE_K)
    kps = cols // tk
    tile = pl.BlockSpec((tm, tn), lambda i, j, k: (i, j))
    return _mm(name, dy, wg, () if extra is None else (extra,), grid=(m // tm, rows // tn, N_CHIPS * kps),
               a_spec=pl.BlockSpec((tm, tk), lambda i, j, k: (i, k)),
               b_spec=pl.BlockSpec((None, tn, tk), lambda i, j, k: (k // kps, j, k % kps)),
               extra_specs=() if extra is None else (tile,), out_specs=[tile],
               out_shape=[jax.ShapeDtypeStruct((m, rows), dtype)],
               dims=NT, epilogue=epilogue, acc_shape=(tm, tn), comm=comm)[0]


def _dx_row(name, dy, wg, dtype, epilogue=_epi_store, extra=None, comm=None):
    m, (_, rows, cols) = dy.shape[0], wg.shape
    tm, tn, tk = _tile(m, 1024), _tile(rows, 1024), _tile(cols, TILE_K)
    nps = rows // tn
    tile = pl.BlockSpec((tm, tn), lambda i, j, k: (i, j))
    return _mm(name, dy, wg, () if extra is None else (extra,), grid=(m // tm, N_CHIPS * nps, cols // tk),
               a_spec=pl.BlockSpec((tm, tk), lambda i, j, k: (i, k)),
               b_spec=pl.BlockSpec((None, tn, tk), lambda i, j, k: (j // nps, j % nps, k)),
               extra_specs=() if extra is None else (tile,), out_specs=[tile],
               out_shape=[jax.ShapeDtypeStruct((m, N_CHIPS * rows), dtype)],
               dims=NT, epilogue=epilogue, acc_shape=(tm, tn), comm=comm)[0]


def _dw(name, x, dy, rows, cols, col_sharded, comm=None):
    s_tok = x.shape[0]
    tm, tn, tk = _tile(rows, 1024), _tile(cols, 1024), _tile(s_tok, TILE_K)
    mt, nps = rows // tm, cols // tn
    if col_sharded:
        grid = (mt, N_CHIPS * nps, s_tok // tk)
        out = pl.BlockSpec((None, tm, tn), lambda i, j, k: (j // nps, i, j % nps))
    else:
        grid = (N_CHIPS * mt, nps, s_tok // tk)
        out = pl.BlockSpec((None, tm, tn), lambda i, j, k: (i // mt, i % mt, j))
    return _mm(name, x, dy, (), grid=grid,
               a_spec=pl.BlockSpec((tk, tm), lambda i, j, k: (k, i)),
               b_spec=pl.BlockSpec((tk, tn), lambda i, j, k: (k, j)),
               extra_specs=(), out_specs=[out],
               out_shape=[jax.ShapeDtypeStruct((N_CHIPS, rows, cols), WIRE_DTYPE)],
               dims=TN, epilogue=_epi_store, acc_shape=(tm, tn), comm=comm)[0]


def _ln_fwd(name, h, m, g, b):
    s, d = h.shape
    tm = _tile(s, 256)
    row = pl.BlockSpec((tm, d), lambda i: (i, 0))
    vec = pl.BlockSpec((1, d), lambda i: (0, 0))

    def body(h_ref, m_ref, g_ref, b_ref, y_ref, ymx_ref, xh_ref, r_ref):
        z = ALPHA * h_ref[...] + m_ref[...]
        mu = jnp.mean(z, -1, keepdims=True)
        zc = z - mu
        r = lax.rsqrt(jnp.mean(zc * zc, -1, keepdims=True) + LN_EPS)
        xh = zc * r
        xh_ref[...] = xh
        r_ref[...] = r
        y = xh * g_ref[...] + b_ref[...]
        y_ref[...] = y
        ymx_ref[...] = y.astype(ymx_ref.dtype)

    return pl.pallas_call(
        body, grid=(s // tm,), in_specs=[row, row, vec, vec],
        out_specs=[row, row, row, pl.BlockSpec((tm, 1), lambda i: (i, 0))],
        out_shape=[jax.ShapeDtypeStruct((s, d), F32), jax.ShapeDtypeStruct((s, d), MXU_DTYPE),
                   jax.ShapeDtypeStruct((s, d), F32), jax.ShapeDtypeStruct((s, 1), F32)],
        name=name, compiler_params=_params("parallel"))(h, m, g, b)


def _ln_bwd(name, dy, xh, r, g):
    s, d = dy.shape
    tm = _tile(s, 256)
    row = pl.BlockSpec((tm, d), lambda i: (i, 0))
    vec = pl.BlockSpec((1, d), lambda i: (0, 0))

    def body(dy_ref, xh_ref, r_ref, g_ref, dz_ref, dzmx_ref, dg_ref, db_ref):
        i = pl.program_id(0)
        dy_, xh_ = dy_ref[...], xh_ref[...]
        dyg = dy_ * g_ref[...]
        m1 = jnp.mean(dyg, -1, keepdims=True)
        m2 = jnp.mean(dyg * xh_, -1, keepdims=True)
        dz = r_ref[...] * (dyg - m1 - xh_ * m2)
        dz_ref[...] = dz
        dzmx_ref[...] = dz.astype(dzmx_ref.dtype)
        pg = jnp.sum(dy_ * xh_, 0, keepdims=True)
        pb = jnp.sum(dy_, 0, keepdims=True)

        @pl.when(i == 0)
        def _():
            dg_ref[...] = pg
            db_ref[...] = pb

        @pl.when(i > 0)
        def _():
            dg_ref[...] += pg
            db_ref[...] += pb

    return pl.pallas_call(
        body, grid=(s // tm,), in_specs=[row, row, pl.BlockSpec((tm, 1), lambda i: (i, 0)), vec],
        out_specs=[row, row, vec, vec],
        out_shape=[jax.ShapeDtypeStruct((s, d), F32), jax.ShapeDtypeStruct((s, d), MXU_DTYPE),
                   jax.ShapeDtypeStruct((1, d), F32), jax.ShapeDtypeStruct((1, d), F32)],
        name=name, compiler_params=_params("arbitrary"))(dy, xh, r, g)


def _loss_head(y, t):
    s, d = y.shape
    tm = _tile(s, 256)
    row = pl.BlockSpec((tm, d), lambda i: (i, 0))

    def body(y_ref, t_ref, l_ref, dy_ref):
        i = pl.program_id(0)
        e = y_ref[...] - t_ref[...]
        dy_ref[...] = e * (1.0 / d)
        part = 0.5 * jnp.sum(jnp.mean(e * e, -1, keepdims=True), 0, keepdims=True)

        @pl.when(i == 0)
        def _():
            l_ref[...] = part

        @pl.when(i > 0)
        def _():
            l_ref[...] += part

    return pl.pallas_call(
        body, grid=(s // tm,), in_specs=[row, row],
        out_specs=[pl.BlockSpec((1, 1), lambda i: (0, 0)), row],
        out_shape=[jax.ShapeDtypeStruct((1, 1), F32), jax.ShapeDtypeStruct((s, d), F32)],
        name="loss_head", compiler_params=_params("arbitrary"))(y, t)


def _rope_tables(s):
    half = MLA_ROPE // 2
    inv = ROPE_THETA ** (-jnp.arange(half, dtype=F32) / half)
    ang = jnp.arange(s).astype(F32)[:, None] * inv[None, :]
    cos, sin = jnp.cos(ang), jnp.sin(ang)
    c = jnp.concatenate([cos, cos, jnp.ones((s, 128 - MLA_ROPE), F32)], 1)
    s1 = jnp.concatenate([-sin, jnp.zeros((s, 128 - half), F32)], 1)
    s2 = jnp.concatenate([jnp.zeros((s, half), F32), sin, jnp.zeros((s, 128 - MLA_ROPE), F32)], 1)
    return c, s1, s2


def _rope(x, c, s1, s2):
    half = MLA_ROPE // 2
    return x * c + pltpu.roll(x, 128 - half, 1) * s1 + pltpu.roll(x, half, 1) * s2


def _rope_t(dy, c, s1, s2):
    half = MLA_ROPE // 2
    return dy * c + pltpu.roll(dy * s1, half, 1) + pltpu.roll(dy * s2, 128 - half, 1)


def _mla_mid_fwd(down, gq, gkv, tabs):
    s, w = down.shape
    ql, kvl = gq.shape[1], gkv.shape[1]
    tm = _tile(s, 256)

    def body(d_ref, gq_ref, gkv_ref, c_ref, s1_ref, s2_ref, cq_ref, ckv_ref, kr_ref):
        cq = d_ref[:, :ql]
        ckv = d_ref[:, ql:ql + kvl]
        cq_ref[...] = (cq * lax.rsqrt(jnp.mean(cq * cq, -1, keepdims=True) + RMS_EPS)
                       * gq_ref[...]).astype(cq_ref.dtype)
        ckv_ref[...] = (ckv * lax.rsqrt(jnp.mean(ckv * ckv, -1, keepdims=True) + RMS_EPS)
                        * gkv_ref[...]).astype(ckv_ref.dtype)
        kr_ref[...] = _rope(d_ref[:, ql + kvl:], c_ref[...], s1_ref[...], s2_ref[...]).astype(kr_ref.dtype)

    tab = pl.BlockSpec((tm, 128), lambda i: (i, 0))
    return pl.pallas_call(
        body, grid=(s // tm,),
        in_specs=[pl.BlockSpec((tm, w), lambda i: (i, 0)), pl.BlockSpec((1, ql), lambda i: (0, 0)),
                  pl.BlockSpec((1, kvl), lambda i: (0, 0)), tab, tab, tab],
        out_specs=[pl.BlockSpec((tm, ql), lambda i: (i, 0)), pl.BlockSpec((tm, kvl), lambda i: (i, 0)), tab],
        out_shape=[jax.ShapeDtypeStruct((s, ql), MXU_DTYPE), jax.ShapeDtypeStruct((s, kvl), MXU_DTYPE),
                   jax.ShapeDtypeStruct((s, 128), MXU_DTYPE)],
        name="mla_mid_fwd", compiler_params=_params("parallel"))(down, gq, gkv, *tabs)


def _mla_mid_bwd(down, dcq, dckv, dkr, gq, gkv, tabs):
    s, w = down.shape
    ql, kvl = gq.shape[1], gkv.shape[1]
    tm = _tile(s, 256)

    def rms_bwd(x, dy, g):
        n = x.shape[1]
        r = lax.rsqrt(jnp.mean(x * x, -1, keepdims=True) + RMS_EPS)
        dyg = dy * g
        dx = r * dyg - x * (r * r * r * (1.0 / n)) * jnp.sum(dyg * x, -1, keepdims=True)
        return dx, jnp.sum(dy * x * r, 0, keepdims=True)

    def body(d_ref, dcq_ref, dckv_ref, dkr_ref, gq_ref, gkv_ref, c_ref, s1_ref, s2_ref, o_ref, dgq_ref, dgkv_ref):
        i = pl.program_id(0)
        dxq, pq = rms_bwd(d_ref[:, :ql], dcq_ref[...], gq_ref[...])
        dxkv, pkv = rms_bwd(d_ref[:, ql:ql + kvl], dckv_ref[...], gkv_ref[...])
        o_ref[:, :ql] = dxq.astype(o_ref.dtype)
        o_ref[:, ql:ql + kvl] = dxkv.astype(o_ref.dtype)
        o_ref[:, ql + kvl:] = _rope_t(dkr_ref[...], c_ref[...], s1_ref[...], s2_ref[...]).astype(o_ref.dtype)

        @pl.when(i == 0)
        def _():
            dgq_ref[...] = pq
            dgkv_ref[...] = pkv

        @pl.when(i > 0)
        def _():
            dgq_ref[...] += pq
            dgkv_ref[...] += pkv

    tab = pl.BlockSpec((tm, 128), lambda i: (i, 0))
    vq = pl.BlockSpec((1, ql), lambda i: (0, 0))
    vkv = pl.BlockSpec((1, kvl), lambda i: (0, 0))
    full = pl.BlockSpec((tm, w), lambda i: (i, 0))
    return pl.pallas_call(
        body, grid=(s // tm,),
        in_specs=[full, pl.BlockSpec((tm, ql), lambda i: (i, 0)), pl.BlockSpec((tm, kvl), lambda i: (i, 0)), tab,
                  vq, vkv, tab, tab, tab],
        out_specs=[full, vq, vkv],
        out_shape=[jax.ShapeDtypeStruct((s, w), MXU_DTYPE), jax.ShapeDtypeStruct((1, ql), F32),
                   jax.ShapeDtypeStruct((1, kvl), F32)],
        name="mla_mid_bwd", compiler_params=_params("arbitrary"))(down, dcq, dckv, dkr, gq, gkv, *tabs)


def _iota2():
    return (lax.broadcasted_iota(jnp.int32, (TQ, TQ), 0), lax.broadcasted_iota(jnp.int32, (TQ, TQ), 1))


def _mla_attn_fwd(name, q, kv, kr, tabs, comm=None):
    s = q.shape[0]
    nq = s // TQ
    scale = MLA_QK_DIM ** -0.5

    def body(q_ref, kv_ref, kr_ref, c_ref, s1_ref, s2_ref, o_ref, omx_ref, lse_ref):
        i = pl.program_id(1)
        row, col = _iota2()
        tabs_i = (c_ref[...], s1_ref[...], s2_ref[...])
        qn = [q_ref[:, lo].astype(MXU_DTYPE) for lo, _ in PAD_COLS]
        qr = [_rope(q_ref[:, hi], *tabs_i).astype(MXU_DTYPE) for _, hi in PAD_COLS]
        qc = jnp.right_shift(i * TQ + row, CHUNK_SHIFT)

        def step(kb, carry):
            ks = pl.multiple_of(kb * TQ, TQ)
            krb = kr_ref[pl.ds(ks, TQ), :].astype(MXU_DTYPE)
            mask = jnp.right_shift(ks + col, CHUNK_SHIFT) <= qc
            out = []
            for j, ((lo, hi), (m, l, acc)) in enumerate(zip(PAD_COLS, carry)):
                sc = (_dot(qn[j], kv_ref[pl.ds(ks, TQ), lo].astype(MXU_DTYPE), NT) + _dot(qr[j], krb, NT)) * scale
                sc = jnp.where(mask, sc, NEG)
                m_new = jnp.maximum(m, jnp.max(sc, -1, keepdims=True))
                p = jnp.exp(sc - m_new)
                corr = jnp.exp(m - m_new)
                l = corr * l + jnp.sum(p, -1, keepdims=True)
                acc = corr * acc + _dot(p.astype(MXU_DTYPE), kv_ref[pl.ds(ks, TQ), hi].astype(MXU_DTYPE))
                out.append((m_new, l, acc))
            return tuple(out)

        res = lax.fori_loop(0, i + 1, step,
                            tuple((jnp.full((TQ, 1), NEG, F32), jnp.zeros((TQ, 1), F32),
                                   jnp.zeros((TQ, HEAD_DIM), F32)) for _ in HEAD_COLS))
        for j, (hd, (m, l, acc)) in enumerate(zip(HEAD_COLS, res)):
            o = acc / l
            o_ref[:, hd] = o
            omx_ref[:, hd] = o.astype(omx_ref.dtype)
            lse_ref[j] = m + jnp.log(l)

    tab = pl.BlockSpec((TQ, 128), lambda h, i: (i, 0))
    oblk = pl.BlockSpec((TQ, HEADS_PER_STEP * HEAD_DIM), lambda h, i: (i, h))
    return _call(
        body, (q, kv, kr, *tabs), grid=(HEADS // HEADS_PER_STEP, nq),
        in_specs=[pl.BlockSpec((TQ, HEADS_PER_STEP * HEAD_PAD), lambda h, i: (i, h)),
                  pl.BlockSpec((s, HEADS_PER_STEP * HEAD_PAD), lambda h, i: (0, h)),
                  pl.BlockSpec((s, 128), lambda h, i: (0, 0)), tab, tab, tab],
        out_specs=[oblk, oblk, pl.BlockSpec((HEADS_PER_STEP, TQ, 1), lambda h, i: (h, i, 0))],
        out_shape=[jax.ShapeDtypeStruct((s, HEADS * HEAD_DIM), F32),
                   jax.ShapeDtypeStruct((s, HEADS * HEAD_DIM), MXU_DTYPE), jax.ShapeDtypeStruct((HEADS, s, 1), F32)],
        name=name, semantics=("parallel", "parallel"), comm=comm)


def _mla_attn_bwd(name, q, kv, kr, tabs, do, o, lse, comm=None):
    s = q.shape[0]
    nq = s // TQ
    scale = MLA_QK_DIM ** -0.5

    def body(q_ref, kv_ref, kr_ref, c_ref, s1_ref, s2_ref, do_ref, o_ref, lse_ref,
             dq_ref, dkv_ref, dkr_ref, dkv_acc, dkr_acc):
        h, i = pl.program_id(0), pl.program_id(1)
        row, col = _iota2()

        @pl.when(i == 0)
        def _():
            dkv_acc[...] = jnp.zeros_like(dkv_acc)

        @pl.when((h == 0) & (i == 0))
        def _():
            dkr_acc[...] = jnp.zeros_like(dkr_acc)

        tabs_i = (c_ref[...], s1_ref[...], s2_ref[...])
        qn = [q_ref[:, lo].astype(MXU_DTYPE) for lo, _ in PAD_COLS]
        qr = [_rope(q_ref[:, hi], *tabs_i).astype(MXU_DTYPE) for _, hi in PAD_COLS]
        qc = jnp.right_shift(i * TQ + row, CHUNK_SHIFT)
        delta = [jnp.sum(do_ref[:, hd] * o_ref[:, hd], -1, keepdims=True) for hd in HEAD_COLS]
        lse_ = [lse_ref[j] for j in range(HEADS_PER_STEP)]
        dob = [do_ref[:, hd].astype(MXU_DTYPE) for hd in HEAD_COLS]

        def step(kb, carry):
            ks = pl.multiple_of(kb * TQ, TQ)
            krb = kr_ref[pl.ds(ks, TQ), :].astype(MXU_DTYPE)
            mask = jnp.right_shift(ks + col, CHUNK_SHIFT) <= qc
            out = []
            for j, ((lo, hi), (dqn, dqr)) in enumerate(zip(PAD_COLS, carry)):
                kn = kv_ref[pl.ds(ks, TQ), lo].astype(MXU_DTYPE)
                v = kv_ref[pl.ds(ks, TQ), hi].astype(MXU_DTYPE)
                sc = (_dot(qn[j], kn, NT) + _dot(qr[j], krb, NT)) * scale
                sc = jnp.where(mask, sc, NEG)
                p = jnp.exp(sc - lse_[j])
                ds = (p * (_dot(dob[j], v, NT) - delta[j]) * scale).astype(MXU_DTYPE)
                dkv_acc[pl.ds(ks, TQ), lo] += _dot(ds, qn[j], TN)
                dkv_acc[pl.ds(ks, TQ), hi] += _dot(p.astype(MXU_DTYPE), dob[j], TN)
                dkr_acc[pl.ds(ks, TQ), :] += _dot(ds, qr[j], TN)
                out.append((dqn + _dot(ds, kn), dqr + _dot(ds, krb)))
            return tuple(out)

        res = lax.fori_loop(0, i + 1, step,
                            tuple((jnp.zeros((TQ, HEAD_DIM), F32), jnp.zeros((TQ, 128), F32)) for _ in HEAD_COLS))
        for (lo, hi), (dqn, dqr) in zip(PAD_COLS, res):
            dq_ref[:, lo] = dqn.astype(dq_ref.dtype)
            dq_ref[:, hi] = _rope_t(dqr, *tabs_i).astype(dq_ref.dtype)

        @pl.when(i == nq - 1)
        def _():
            dkv_ref[...] = dkv_acc[...].astype(dkv_ref.dtype)

        @pl.when((h == HEADS // HEADS_PER_STEP - 1) & (i == nq - 1))
        def _():
            dkr_ref[...] = dkr_acc[...]

    tab = pl.BlockSpec((TQ, 128), lambda h, i: (i, 0))
    qblk = pl.BlockSpec((TQ, HEADS_PER_STEP * HEAD_PAD), lambda h, i: (i, h))
    oblk = pl.BlockSpec((TQ, HEADS_PER_STEP * HEAD_DIM), lambda h, i: (i, h))
    kvblk = pl.BlockSpec((s, HEADS_PER_STEP * HEAD_PAD), lambda h, i: (0, h))
    return _call(
        body, (q, kv, kr, *tabs, do, o, lse), grid=(HEADS // HEADS_PER_STEP, nq),
        in_specs=[qblk, kvblk, pl.BlockSpec((s, 128), lambda h, i: (0, 0)), tab, tab, tab, oblk, oblk,
                  pl.BlockSpec((HEADS_PER_STEP, TQ, 1), lambda h, i: (h, i, 0))],
        out_specs=[qblk, kvblk, pl.BlockSpec((s, 128), lambda h, i: (0, 0))],
        out_shape=[jax.ShapeDtypeStruct((s, HEADS * HEAD_PAD), MXU_DTYPE), jax.ShapeDtypeStruct((s, HEADS * HEAD_PAD), MXU_DTYPE),
                   jax.ShapeDtypeStruct((s, 128), F32)],
        scratch_shapes=[pltpu.VMEM((s, HEADS_PER_STEP * HEAD_PAD), F32), pltpu.VMEM((s, 128), F32)],
        name=name, semantics=("arbitrary", "arbitrary"), comm=comm)


HEADS_PER_STEP = 2
HEAD_COLS = [slice(j * HEAD_DIM, (j + 1) * HEAD_DIM) for j in range(HEADS_PER_STEP)]
PAD_COLS = [(slice(j * HEAD_PAD, j * HEAD_PAD + HEAD_DIM), slice(j * HEAD_PAD + HEAD_DIM, (j + 1) * HEAD_PAD))
            for j in range(HEADS_PER_STEP)]


def _qkv_specs(s):
    groups, width = HEADS // HEADS_PER_STEP, HEADS_PER_STEP * HEAD_DIM
    return [pl.BlockSpec((TQ, width), lambda h, i: (i, h)),
            pl.BlockSpec((s, width), lambda h, i: (0, groups + h)),
            pl.BlockSpec((s, width), lambda h, i: (0, 2 * groups + h))]


def _sb_terms(z):
    sp = jnp.log(1.0 + jnp.exp(-jnp.abs(z)))
    return jnp.minimum(z, 0.0) - sp, jnp.minimum(-z, 0.0) - sp


def _sb_attn_fwd(name, qkv, comm=None):
    s = qkv.shape[0]
    nq = s // TQ
    scale = HEAD_DIM ** -0.5

    def body(q_ref, k_ref, v_ref, o_ref):
        i = pl.program_id(1)
        row, col = _iota2()
        after = (row > col).astype(BF16)
        q = [q_ref[:, hd].astype(MXU_DTYPE) for hd in HEAD_COLS]
        qpos = i * TQ + row

        def step(n, carry):
            ks = pl.multiple_of((i - n) * TQ, TQ)
            strict = (ks + col) < qpos
            out = []
            for hd, qh, (tail, acc) in zip(HEAD_COLS, q, carry):
                z = _dot(qh, k_ref[pl.ds(ks, TQ), hd].astype(MXU_DTYPE), NT) * scale
                lb, l1 = _sb_terms(z)
                l1 = jnp.where(strict, l1, 0.0)
                a = jnp.where(strict, jnp.exp(lb + tail + _exact_dot(l1, after)), 0.0)
                acc = acc + _dot(a.astype(MXU_DTYPE), v_ref[pl.ds(ks, TQ), hd].astype(MXU_DTYPE))
                out.append((tail + jnp.sum(l1, -1, keepdims=True), acc))
            return tuple(out)

        res = lax.fori_loop(0, i + 1, step,
                            tuple((jnp.zeros((TQ, 1), F32), jnp.zeros((TQ, HEAD_DIM), F32)) for _ in HEAD_COLS))
        for hd, (_, acc) in zip(HEAD_COLS, res):
            o_ref[:, hd] = acc.astype(o_ref.dtype)

    return _call(
        body, (qkv, qkv, qkv), grid=(HEADS // HEADS_PER_STEP, nq), in_specs=_qkv_specs(s),
        out_specs=[pl.BlockSpec((TQ, HEADS_PER_STEP * HEAD_DIM), lambda h, i: (i, h))],
        out_shape=[jax.ShapeDtypeStruct((s, HEADS * HEAD_DIM), MXU_DTYPE)],
        name=name, semantics=("parallel", "parallel"), comm=comm)[0]


def _sb_attn_bwd(name, qkv, do, comm=None):
    s = qkv.shape[0]
    nq = s // TQ
    scale = HEAD_DIM ** -0.5

    def body(q_ref, k_ref, v_ref, do_ref, dq_ref, dk_ref, dv_ref, a_buf, dk_acc, dv_acc):
        i = pl.program_id(1)
        row, col = _iota2()
        after = (row > col).astype(BF16)
        before = (row < col).astype(BF16)

        @pl.when(i == 0)
        def _():
            dk_acc[...] = jnp.zeros_like(dk_acc)
            dv_acc[...] = jnp.zeros_like(dv_acc)

        q = [q_ref[:, hd].astype(MXU_DTYPE) for hd in HEAD_COLS]
        dob = [do_ref[:, hd].astype(MXU_DTYPE) for hd in HEAD_COLS]
        qpos = i * TQ + row

        def weights(n, tails):
            kb = i - n
            ks = pl.multiple_of(kb * TQ, TQ)
            strict = (ks + col) < qpos
            out = []
            for j, (hd, tail) in enumerate(zip(HEAD_COLS, tails)):
                z = _dot(q[j], k_ref[pl.ds(ks, TQ), hd].astype(MXU_DTYPE), NT) * scale
                lb, l1 = _sb_terms(z)
                l1 = jnp.where(strict, l1, 0.0)
                a = jnp.where(strict, jnp.exp(lb + tail + _exact_dot(l1, after)), 0.0)
                a_buf[j, kb] = a
                dv_acc[pl.ds(ks, TQ), hd] += _dot(a.astype(MXU_DTYPE), dob[j], TN)
                out.append(tail + jnp.sum(l1, -1, keepdims=True))
            return tuple(out)

        lax.fori_loop(0, i + 1, weights, tuple(jnp.zeros((TQ, 1), F32) for _ in HEAD_COLS))

        def grads(kb, carry):
            ks = pl.multiple_of(kb * TQ, TQ)
            strict = (ks + col) < qpos
            out = []
            for j, (hd, (head, dq)) in enumerate(zip(HEAD_COLS, carry)):
                k = k_ref[pl.ds(ks, TQ), hd].astype(MXU_DTYPE)
                z = _dot(q[j], k, NT) * scale
                e = jnp.exp(-jnp.abs(z))
                beta = jnp.where(z >= 0.0, 1.0, e) / (1.0 + e)
                w = _dot(dob[j], v_ref[pl.ds(ks, TQ), hd].astype(MXU_DTYPE), NT) * a_buf[j, kb]
                dz = jnp.where(strict, w * (1.0 - beta) - beta * (head + _exact_dot(w, before)), 0.0) * scale
                dzb = dz.astype(MXU_DTYPE)
                dk_acc[pl.ds(ks, TQ), hd] += _dot(dzb, q[j], TN)
                out.append((head + jnp.sum(w, -1, keepdims=True), dq + _dot(dzb, k)))
            return tuple(out)

        res = lax.fori_loop(0, i + 1, grads,
                            tuple((jnp.zeros((TQ, 1), F32), jnp.zeros((TQ, HEAD_DIM), F32)) for _ in HEAD_COLS))
        for hd, (_, dq) in zip(HEAD_COLS, res):
            dq_ref[:, hd] = dq.astype(dq_ref.dtype)

        @pl.when(i == nq - 1)
        def _():
            dk_ref[...] = dk_acc[...].astype(dk_ref.dtype)
            dv_ref[...] = dv_acc[...].astype(dv_ref.dtype)

    width = HEADS_PER_STEP * HEAD_DIM
    blk = pl.BlockSpec((TQ, width), lambda h, i: (i, h))
    col_h = pl.BlockSpec((s, width), lambda h, i: (0, h))
    shp = jax.ShapeDtypeStruct((s, HEADS * HEAD_DIM), MXU_DTYPE)
    return _call(
        body, (qkv, qkv, qkv, do), grid=(HEADS // HEADS_PER_STEP, nq), in_specs=_qkv_specs(s) + [blk],
        out_specs=[blk, col_h, col_h], out_shape=[shp, shp, shp],
        scratch_shapes=[pltpu.VMEM((HEADS_PER_STEP, nq, TQ, TQ), F32), pltpu.VMEM((s, width), F32),
                        pltpu.VMEM((s, width), F32)],
        name=name, semantics=("arbitrary", "arbitrary"), comm=comm)


LANES = 128
LAST_REL = (1 << CHUNK_SHIFT) - 1


def _ca_subtiles():
    nb = TQ // LANES
    return [(d, a, b, -d * TQ + (b - a) * LANES) for d in range(CA_TILES) for a in range(nb) for b in range(nb)]


def _ca_bias_tiles(rel_bias):
    table = jnp.pad(jnp.transpose(rel_bias), ((0, 0), (0, 2 * LANES - REL_TABLE)))[:, None]

    def unskew(x, row):
        for b in range(7):
            x = jnp.where((jnp.right_shift(row, b) & 1) == 1, pltpu.roll(x, 1 << b, 1), x)
        return x

    def body(t_ref, o_ref):
        row = lax.broadcasted_iota(jnp.int32, (LANES, LANES), 0)
        col = lax.broadcasted_iota(jnp.int32, (LANES, LANES), 1)
        lane = col[:1]
        lo, hi = t_ref[:, :LANES], t_ref[:, LANES:]
        first = jnp.sum(jnp.where(lane == 0, lo, 0.0), -1, keepdims=True)
        last = jnp.sum(jnp.where(lane == LAST_REL, hi, 0.0), -1, keepdims=True)
        hi = jnp.where(lane <= LAST_REL, hi, last)
        r_lo = unskew(jnp.broadcast_to(lo, (LANES, LANES)), row)
        r_hi = unskew(jnp.broadcast_to(hi, (LANES, LANES)), row)
        upper = col >= row
        for d, a, b, o in _ca_subtiles():
            if o >= LANES:
                piece = jnp.broadcast_to(last, (LANES, LANES))
            elif o == 0:
                piece = jnp.where(upper, r_hi, r_lo)
            elif o == -LANES:
                piece = jnp.where(upper, r_lo, first)
            else:
                piece = jnp.broadcast_to(first, (LANES, LANES))
            o_ref[d, a * LANES:(a + 1) * LANES, b * LANES:(b + 1) * LANES] = piece

    return pl.pallas_call(
        body, grid=(HEADS,), in_specs=[pl.BlockSpec((None, 1, 2 * LANES), lambda h: (h, 0, 0))],
        out_specs=pl.BlockSpec((None, CA_TILES, TQ, TQ), lambda h: (h, 0, 0, 0)),
        out_shape=jax.ShapeDtypeStruct((HEADS, CA_TILES, TQ, TQ), F32),
        name="ca_bias_tiles", compiler_params=_params("parallel"))(table)


def _ca_mask(i, ks, row, col):
    qc = jnp.right_shift(i * TQ + row, CHUNK_SHIFT)
    kc = jnp.right_shift(ks + col, CHUNK_SHIFT)
    return (kc <= qc) & (kc >= qc - CA_LEFT_CHUNKS)


def _ca_attn_fwd(name, qkv, tiles, comm=None):
    s = qkv.shape[0]
    nq = s // TQ
    scale = HEAD_DIM ** -0.5

    def body(q_ref, k_ref, v_ref, bt_ref, o_ref, omx_ref, lse_ref):
        i = pl.program_id(1)
        row, col = _iota2()
        q = [q_ref[:, hd].astype(MXU_DTYPE) for hd in HEAD_COLS]

        def step(kb, carry):
            ks = pl.multiple_of(kb * TQ, TQ)
            mask = _ca_mask(i, ks, row, col)
            t = jnp.minimum(i - kb, CA_TILES - 1)
            out = []
            for j, (hd, (m, l, acc)) in enumerate(zip(HEAD_COLS, carry)):
                sc = _dot(q[j], k_ref[pl.ds(ks, TQ), hd].astype(MXU_DTYPE), NT) * scale + bt_ref[j, t]
                sc = jnp.where(mask, sc, NEG)
                m_new = jnp.maximum(m, jnp.max(sc, -1, keepdims=True))
                p = jnp.exp(sc - m_new)
                corr = jnp.exp(m - m_new)
                l = corr * l + jnp.sum(p, -1, keepdims=True)
                acc = corr * acc + _dot(p.astype(MXU_DTYPE), v_ref[pl.ds(ks, TQ), hd].astype(MXU_DTYPE))
                out.append((m_new, l, acc))
            return tuple(out)

        res = lax.fori_loop(jnp.maximum(i - CA_LEFT_BLOCKS, 0), i + 1, step,
                            tuple((jnp.full((TQ, 1), NEG, F32), jnp.zeros((TQ, 1), F32),
                                   jnp.zeros((TQ, HEAD_DIM), F32)) for _ in HEAD_COLS))
        for j, (hd, (m, l, acc)) in enumerate(zip(HEAD_COLS, res)):
            o = acc / l
            o_ref[:, hd] = o
            omx_ref[:, hd] = o.astype(omx_ref.dtype)
            lse_ref[j] = m + jnp.log(l)

    oblk = pl.BlockSpec((TQ, HEADS_PER_STEP * HEAD_DIM), lambda h, i: (i, h))
    return _call(
        body, (qkv, qkv, qkv, tiles), grid=(HEADS // HEADS_PER_STEP, nq),
        in_specs=_qkv_specs(s) + [pl.BlockSpec((HEADS_PER_STEP, CA_TILES, TQ, TQ), lambda h, i: (h, 0, 0, 0))],
        out_specs=[oblk, oblk, pl.BlockSpec((HEADS_PER_STEP, TQ, 1), lambda h, i: (h, i, 0))],
        out_shape=[jax.ShapeDtypeStruct((s, HEADS * HEAD_DIM), F32),
                   jax.ShapeDtypeStruct((s, HEADS * HEAD_DIM), MXU_DTYPE), jax.ShapeDtypeStruct((HEADS, s, 1), F32)],
        name=name, semantics=("parallel", "parallel"), comm=comm)


def _ca_attn_bwd(name, qkv, tiles, do, o, lse, comm=None):
    s = qkv.shape[0]
    nq = s // TQ
    scale = HEAD_DIM ** -0.5

    def body(q_ref, k_ref, v_ref, bt_ref, do_ref, o_ref, lse_ref, dq_ref, dk_ref, dv_ref, dbt_ref, dk_acc, dv_acc):
        i = pl.program_id(1)
        row, col = _iota2()

        @pl.when(i == 0)
        def _():
            dk_acc[...] = jnp.zeros_like(dk_acc)
            dv_acc[...] = jnp.zeros_like(dv_acc)
            dbt_ref[...] = jnp.zeros_like(dbt_ref)

        q = [q_ref[:, hd].astype(MXU_DTYPE) for hd in HEAD_COLS]
        delta = [jnp.sum(do_ref[:, hd] * o_ref[:, hd], -1, keepdims=True) for hd in HEAD_COLS]
        lse_ = [lse_ref[j] for j in range(HEADS_PER_STEP)]
        dob = [do_ref[:, hd].astype(MXU_DTYPE) for hd in HEAD_COLS]

        def step(kb, dqs):
            ks = pl.multiple_of(kb * TQ, TQ)
            mask = _ca_mask(i, ks, row, col)
            t = jnp.minimum(i - kb, CA_TILES - 1)
            out = []
            for j, (hd, dq) in enumerate(zip(HEAD_COLS, dqs)):
                k = k_ref[pl.ds(ks, TQ), hd].astype(MXU_DTYPE)
                v = v_ref[pl.ds(ks, TQ), hd].astype(MXU_DTYPE)
                sc = _dot(q[j], k, NT) * scale + bt_ref[j, t]
                sc = jnp.where(mask, sc, NEG)
                p = jnp.exp(sc - lse_[j])
                dsc = p * (_dot(dob[j], v, NT) - delta[j])
                dbt_ref[j, t] += dsc
                ds = (dsc * scale).astype(MXU_DTYPE)
                dk_acc[pl.ds(ks, TQ), hd] += _dot(ds, q[j], TN)
                dv_acc[pl.ds(ks, TQ), hd] += _dot(p.astype(MXU_DTYPE), dob[j], TN)
                out.append(dq + _dot(ds, k))
            return tuple(out)

        dqs = lax.fori_loop(jnp.maximum(i - CA_LEFT_BLOCKS, 0), i + 1, step,
                            tuple(jnp.zeros((TQ, HEAD_DIM), F32) for _ in HEAD_COLS))
        for hd, dq in zip(HEAD_COLS, dqs):
            dq_ref[:, hd] = dq.astype(dq_ref.dtype)

        @pl.when(i == nq - 1)
        def _():
            dk_ref[...] = dk_acc[...].astype(dk_ref.dtype)
            dv_ref[...] = dv_acc[...].astype(dv_ref.dtype)

    width = HEADS_PER_STEP * HEAD_DIM
    blk = pl.BlockSpec((TQ, width), lambda h, i: (i, h))
    col_h = pl.BlockSpec((s, width), lambda h, i: (0, h))
    tile = pl.BlockSpec((HEADS_PER_STEP, CA_TILES, TQ, TQ), lambda h, i: (h, 0, 0, 0))
    shp = jax.ShapeDtypeStruct((s, HEADS * HEAD_DIM), MXU_DTYPE)
    return _call(
        body, (qkv, qkv, qkv, tiles, do, o, lse), grid=(HEADS // HEADS_PER_STEP, nq),
        in_specs=_qkv_specs(s) + [tile, blk, blk, pl.BlockSpec((HEADS_PER_STEP, TQ, 1), lambda h, i: (h, i, 0))],
        out_specs=[blk, col_h, col_h, tile],
        out_shape=[shp, shp, shp, jax.ShapeDtypeStruct((HEADS, CA_TILES, TQ, TQ), F32)],
        scratch_shapes=[pltpu.VMEM((s, width), F32), pltpu.VMEM((s, width), F32)],
        name=name, semantics=("arbitrary", "arbitrary"), comm=comm)


def _ca_table_grad(dtiles):
    def skew(x, row):
        for b in range(7):
            x = jnp.where((jnp.right_shift(row, b) & 1) == 1, pltpu.roll(x, LANES - (1 << b), 1), x)
        return x

    def body(t_ref, o_ref):
        row = lax.broadcasted_iota(jnp.int32, (LANES, LANES), 0)
        col = lax.broadcasted_iota(jnp.int32, (LANES, LANES), 1)
        wrapped = (row + col) >= LANES
        lane = col[:1]
        total = lambda x: jnp.sum(jnp.sum(x, 0, keepdims=True), -1, keepdims=True)
        lo = jnp.zeros((1, LANES), F32)
        hi = jnp.zeros((1, LANES), F32)
        for d, a, b, o in _ca_subtiles():
            x = t_ref[d, a * LANES:(a + 1) * LANES, b * LANES:(b + 1) * LANES]
            if o >= LANES:
                hi = hi + jnp.where(lane == LAST_REL, total(x), 0.0)
            elif o <= -2 * LANES:
                lo = lo + jnp.where(lane == 0, total(x), 0.0)
            else:
                y = skew(x, row)
                pos = jnp.sum(jnp.where(wrapped, 0.0, y), 0, keepdims=True)
                neg = jnp.sum(jnp.where(wrapped, y, 0.0), 0, keepdims=True)
                if o == 0:
                    clipped = jnp.sum(jnp.where(lane > LAST_REL, pos, 0.0), -1, keepdims=True)
                    hi = hi + jnp.where(lane <= LAST_REL, pos, 0.0) + jnp.where(lane == LAST_REL, clipped, 0.0)
                    lo = lo + neg
                else:
                    lo = lo + pos + jnp.where(lane == 0, jnp.sum(neg, -1, keepdims=True), 0.0)
        o_ref[:, :LANES] = lo
        o_ref[:, LANES:] = hi

    return pl.pallas_call(
        body, grid=(HEADS,), in_specs=[pl.BlockSpec((None, CA_TILES, TQ, TQ), lambda h: (h, 0, 0, 0))],
        out_specs=pl.BlockSpec((None, 1, 2 * LANES), lambda h: (h, 0, 0)),
        out_shape=jax.ShapeDtypeStruct((HEADS, 1, 2 * LANES), F32),
        name="ca_table_grad", compiler_params=_params("parallel"))(dtiles)


def _row_tile(rows, cols):
    if rows % 128:
        return rows
    tr = 128
    while rows % (2 * tr) == 0 and 2 * tr * cols * 4 <= (1 << 20):
        tr *= 2
    return tr


def _adamw(name, w, g, m, v):
    rows, cols = w.shape
    tr = _row_tile(rows, cols)
    blk = pl.BlockSpec((tr, cols), lambda i: (i, 0))

    def body(w_ref, g_ref, m_ref, v_ref, d_ref, m2_ref, v2_ref):
        g_ = g_ref[...]
        m2 = ADAM_B1 * m_ref[...] + (1.0 - ADAM_B1) * g_
        v2 = ADAM_B2 * v_ref[...] + (1.0 - ADAM_B2) * jnp.square(g_)
        m_hat = m2 / (1.0 - ADAM_B1 ** ADAM_STEP)
        v_hat = v2 / (1.0 - ADAM_B2 ** ADAM_STEP)
        d_ref[...] = -ADAM_LR * (m_hat / (jnp.sqrt(v_hat) + ADAM_EPS) + ADAM_WD * w_ref[...])
        m2_ref[...] = m2
        v2_ref[...] = v2

    shp = jax.ShapeDtypeStruct((rows, cols), F32)
    return pl.pallas_call(body, grid=(rows // tr,), in_specs=[blk] * 4, out_specs=[blk] * 3, out_shape=[shp] * 3,
                          name=name, compiler_params=_params("parallel"))(w, g, m, v)


def _pair_sum(name, g, r1, c, lo, n):
    _, rh, cols = r1.shape
    tr = _row_tile(n, cols)
    nb, off, half = n // tr, lo // tr, rh // tr

    def body(c_ref, g_ref, r_ref, o_ref):
        o_ref[...] = (g_ref[...].astype(F32) + r_ref[...].astype(F32)).astype(o_ref.dtype)

    return pl.pallas_call(
        body, grid_spec=pltpu.PrefetchScalarGridSpec(
            num_scalar_prefetch=1, grid=(N_CHIPS, nb),
            in_specs=[pl.BlockSpec((None, tr, cols), lambda k, i, c_ref: (k, c_ref[0] * half + off + i, 0)),
                      pl.BlockSpec((None, tr, cols), lambda k, i, c_ref: (k, off + i, 0))],
            out_specs=pl.BlockSpec((None, tr, cols), lambda k, i, c_ref: (k, i, 0))),
        out_shape=jax.ShapeDtypeStruct((N_CHIPS, n, cols), WIRE_DTYPE), name=name,
        compiler_params=_params("parallel", "parallel"))(c, g, r1)


def _chip_sum(name, a1, r2, place, total, start, rh, total_rows):
    _, n, cols = a1.shape
    tr = _row_tile(n, cols)
    nb, off, half = n // tr, start // tr, rh // tr

    def body(p_ref, a_ref, r_ref, *rest):
        rest[-1][...] = ((a_ref[...].astype(F32) + r_ref[0].astype(F32)) + r_ref[1].astype(F32)) + r_ref[2].astype(F32)

    return pl.pallas_call(
        body, grid_spec=pltpu.PrefetchScalarGridSpec(
            num_scalar_prefetch=1, grid=(nb,),
            in_specs=[pl.BlockSpec((None, tr, cols), lambda i, p_ref: (p_ref[0], i, 0)),
                      pl.BlockSpec((N_CHIPS - 1, tr, cols), lambda i, p_ref: (0, i, 0))] + ([] if total is None else [ANY]),
            out_specs=pl.BlockSpec((tr, cols), lambda i, p_ref: (off + p_ref[1] * half + i, 0))),
        out_shape=jax.ShapeDtypeStruct((total_rows, cols), F32), name=name,
        input_output_aliases={} if total is None else {3: 0},
        compiler_params=_params("parallel"))(place, a1, r2, *([] if total is None else [total]))


def _cast_place(name, w, chip, l, rows):
    cols = w.shape[1]
    tr = _row_tile(rows, cols)
    nb = rows // tr

    def body(c_ref, w_ref, o_ref):
        o_ref[...] = w_ref[...].astype(o_ref.dtype)

    return pl.pallas_call(
        body, grid_spec=pltpu.PrefetchScalarGridSpec(
            num_scalar_prefetch=1, grid=(nb,),
            in_specs=[pl.BlockSpec((tr, cols), lambda i, c_ref: (l * nb + i, 0))],
            out_specs=pl.BlockSpec((None, tr, cols), lambda i, c_ref: (c_ref[0], i, 0))),
        out_shape=jax.ShapeDtypeStruct((N_CHIPS, rows, cols), MXU_DTYPE), name=name,
        compiler_params=_params("parallel"))(chip, w)


def _place():
    x, y, c = lax.axis_index("x"), lax.axis_index("y"), lax.axis_index("c")
    chips = [(1 - x, y), (x, 1 - y), (1 - x, 1 - y)]
    return x, y, c, chips


def _remote(src, dst, send_sem, recv_sem, to):
    return pltpu.make_async_remote_copy(src_ref=src, dst_ref=dst, send_sem=send_sem, recv_sem=recv_sem,
                                        device_id=to, device_id_type=MESH)


def _in_place(bufs):
    return [jax.ShapeDtypeStruct(b.shape, b.dtype) for b in bufs], {t: t for t in range(len(bufs))}


def _gather(bufs, jobs):
    def copies(srcs, dsts, send, recv):
        x, y, c, chips = _place()
        out = []
        for buf, todo in zip(dsts, jobs):
            rh = buf.shape[1] // 2
            for phase, lo, n in todo:
                for px, py in chips:
                    piece = buf.at[2 * x + y if phase == 0 else 2 * px + py, pl.ds(c * rh + lo, n)]
                    to = (px, py, c) if phase == 0 else (x, y, 1 - c)
                    out.append(_remote(piece, piece, send.at[len(out)], recv.at[len(out)], to))
        return out

    return _Comm(bufs, *_in_place(bufs), copies, 3 * sum(len(todo) for todo in jobs))


def _pieces(rh, cols, itemsize):
    us = 3.0 * rh * cols * itemsize / ICI_BYTES_PER_US
    k = max(1, int(round(us / CHUNK_US)))
    while rh % (16 * k):
        k -= 1
        if k <= 1:
            k = 1
            break
    return [(j * (rh // k), rh // k) for j in range(k)], us / k


class _GatherStream:
    def __init__(self, weights, order):
        self.weights, self.order, self.state, self.us, self.credit = weights, order, {}, {}, 0.0
        for key in order:
            _, rows, cols = weights[key].shape
            cuts, self.us[key] = _pieces(rows // 2, cols, jnp.dtype(weights[key].dtype).itemsize)
            self.state[key] = [[lo, n, 0] for lo, n in cuts]

    def _comm(self, picks):
        if not picks:
            return None
        keys = list(dict.fromkeys(key for key, _ in picks))
        jobs = [[(self.state[k][j][2], *self.state[k][j][:2]) for k, j in picks if k == key] for key in keys]
        comm = _gather([self.weights[key] for key in keys], jobs)

        def then(out):
            self.weights.update(zip(keys, out))
            for k, j in picks:
                self.state[k][j][2] += 1

        comm.then = then
        return comm

    def carry(self, us):
        todo = [(key, j) for key in self.order for j, piece in enumerate(self.state[key]) if piece[2] < 2]
        picks = [p for p in todo if self.state[p[0]][p[1]][2] == 1]
        self.credit += LINK_SHARE * us
        for key, j in todo:
            if self.state[key][j][2] == 0:
                if self.credit < 0.5 * self.us[key]:
                    break
                picks.append((key, j))
                self.credit -= self.us[key]
        return self._comm(picks)

    def require(self, keys, tag):
        for step in range(2):
            picks = [(key, j) for key in keys for j, piece in enumerate(self.state[key]) if piece[2] < 2]
            if picks:
                _exchange(f"gather_{tag}_{step}", self._comm(picks))
                self.credit = 0.0


def _reduce_pair(grads):
    def copies(srcs, dsts, send, recv):
        x, y, c, _ = _place()
        out = []
        for t, g in enumerate(srcs):
            rh = g.shape[1] // 2
            out.append(_remote(g.at[:, pl.ds((1 - c) * rh, rh)], dsts[t], send.at[t], recv.at[t], (x, y, 1 - c)))
        return out

    return _Comm(grads, [jax.ShapeDtypeStruct((N_CHIPS, g.shape[1] // 2, g.shape[2]), g.dtype) for g in grads], {},
                 copies, len(grads))


def _reduce_chips(parts):
    def copies(srcs, dsts, send, recv):
        x, y, c, chips = _place()
        return [_remote(p.at[2 * px + py], dsts[t].at[j], send.at[3 * t + j], recv.at[3 * t + j], (px, py, c))
                for t, p in enumerate(srcs) for j, (px, py) in enumerate(chips)]

    return _Comm(parts, [jax.ShapeDtypeStruct((N_CHIPS - 1, *p.shape[1:]), p.dtype) for p in parts], {}, copies,
                 3 * len(parts))


def _reduce_share(totals, spans):
    def copies(srcs, dsts, send, recv):
        x, y, c, _ = _place()
        out = []
        for t, (buf, (start, rh)) in enumerate(zip(dsts, spans)):
            mine = buf.at[pl.ds(start + c * rh, rh)]
            out.append(_remote(mine, mine, send.at[t], recv.at[t], (x, y, 1 - c)))
        return out

    return _Comm(totals, *_in_place(totals), copies, len(totals))


def _exchange(name, comm):
    ns, nd = len(comm.srcs), len(comm.dsts)

    def body(*refs):
        send, recv = refs[ns + nd:]
        cps = comm.copies(refs[:ns], refs[ns:ns + nd], send, recv)
        for cp in cps:
            cp.start()
        for cp in cps:
            cp.wait()

    comm.deliver(pl.pallas_call(
        body, in_specs=[ANY] * ns, out_specs=[ANY] * nd, out_shape=comm.dsts,
        scratch_shapes=[pltpu.SemaphoreType.DMA((comm.n,)), pltpu.SemaphoreType.DMA((comm.n,))],
        input_output_aliases=comm.alias, name=name)(*comm.srcs))
    return comm.out


class _ReduceStream:
    def __init__(self, dims, place):
        self.dims, self.place = dims, place
        self.parts, self.pieces, self.left, self.ready, self.total = {}, [], {}, [], {}

    def add(self, key, part):
        self.parts[key] = part

    def _pair(self):
        keys = list(self.parts)
        if not keys:
            return None
        parts = [self.parts.pop(key) for key in keys]
        comm = _reduce_pair(parts)

        def then(out):
            for (name, l), g, r1 in zip(keys, parts, out):
                _, rows, cols = self.dims[name]
                cuts, us = _pieces(rows // 2, cols, jnp.dtype(WIRE_DTYPE).itemsize)
                self.left[name, l] = len(cuts)
                for lo, n in cuts:
                    piece = _pair_sum(f"pair_sum_{name}_{l}_{lo}", g, r1, self.place[1:], lo, n)
                    self.pieces.append(((name, l), lo, piece, us))

        comm.then = then
        return comm

    def _chips(self, us):
        picks, spent = [], 0.0
        while self.pieces and spent + 0.5 * self.pieces[0][3] <= us:
            picks.append(self.pieces.pop(0))
            spent += picks[-1][3]
        if not picks:
            return None
        comm = _reduce_chips([piece for _, _, piece, _ in picks])

        def then(out):
            for ((name, l), lo, piece, _), r2 in zip(picks, out):
                layers, rows, _ = self.dims[name]
                self.total[name] = _chip_sum(f"chip_sum_{name}_{l}_{lo}", piece, r2, self.place, self.total.get(name),
                                             l * rows + lo, rows // 2, layers * rows)
                self.left[name, l] -= 1
                if not self.left[name, l]:
                    self.ready.append((name, l))

        comm.then = then
        return comm

    def _share(self):
        keys = []
        for key in self.ready:
            if key[0] not in [name for name, _ in keys]:
                keys.append(key)
        if not keys:
            return None
        self.ready = [key for key in self.ready if key not in keys]
        spans = [(l * self.dims[name][1], self.dims[name][1] // 2) for name, l in keys]
        comm = _reduce_share([self.total[name] for name, _ in keys], spans)

        def then(out):
            self.total.update(zip([name for name, _ in keys], out))

        comm.then = then
        return comm

    def carry(self, us):
        return _merged(self._pair(), self._share(), self._chips(us))

    def finish(self):
        step = 0
        while self.parts or self.pieces or self.ready:
            _exchange(f"grad_reduce_tail_{step}", self.carry(float("inf")))
            step += 1
        return self.total


def _all_reduce_small(name, pack):
    rows, cols = pack.shape

    def body(p_ref, o_ref, slots, send, recv):
        x, y, c, _ = _place()
        me = 4 * x + 2 * y + c
        slots[me] = p_ref[...]
        cps = []
        for r in range(1, N_DEV):
            to = ((1 - x) if r & 4 else x, (1 - y) if r & 2 else y, (1 - c) if r & 1 else c)
            cp = _remote(p_ref, slots.at[me], send.at[r - 1], recv.at[r - 1], to)
            cp.start()
            cps.append(cp)
        for cp in cps:
            cp.wait()
        acc = slots[0]
        for d in range(1, N_DEV):
            acc = acc + slots[d]
        o_ref[...] = acc

    return pl.pallas_call(
        body, in_specs=[VMEM_SPEC], out_specs=VMEM_SPEC, out_shape=jax.ShapeDtypeStruct((rows, cols), F32),
        scratch_shapes=[pltpu.VMEM((N_DEV, rows, cols), F32), pltpu.SemaphoreType.DMA((N_DEV - 1,)),
                        pltpu.SemaphoreType.DMA((N_DEV - 1,))],
        name=name)(pack)


BIG = ("ffn_w_in", "ffn_w_out", "mla_w_down", "mla_w_uq", "mla_w_ukv", "mla_w_o", "sb_w_qkv", "sb_w_o", "ca_w_qkv",
       "ca_w_o")
COL_SHARDED = {"ffn_w_in": True, "ffn_w_out": False, "mla_w_down": False, "mla_w_uq": True, "mla_w_ukv": True,
               "mla_w_o": False, "sb_w_qkv": True, "sb_w_o": False, "ca_w_qkv": True, "ca_w_o": False}
WEIGHTS = ("ln_mix_g", "ln_mix_b", "ln_ffn_g", "ln_ffn_b", "ffn_w_in", "ffn_w_out", "mla_w_down", "mla_q_norm_g",
           "mla_w_uq", "mla_kv_norm_g", "mla_w_ukv", "mla_w_o", "sb_w_qkv", "sb_w_o", "ca_w_qkv", "ca_rel_bias",
           "ca_w_o")
HEADS_PER_CHIP = HEADS // N_CHIPS


def _mxu_shards(w):
    down = w["mla_w_down"]
    uq = w["mla_w_uq"]
    n, ql = uq.shape[:2]
    lane_pad = 128 - MLA_ROPE
    shaped = dict(w)
    shaped["mla_w_down"] = jnp.pad(down, ((0, 0), (0, 0), (0, lane_pad)))
    shaped["mla_w_uq"] = jnp.pad(uq.reshape(n, ql, HEADS_PER_CHIP, MLA_QK_DIM),
                                 ((0, 0), (0, 0), (0, 0), (0, HEAD_PAD - MLA_QK_DIM))).reshape(n, ql, -1)
    return {k: shaped[k].reshape(-1, shaped[k].shape[-1]) for k in BIG}


def _unpad_grad(name, g, like):
    if name == "mla_w_down":
        g = g[:, :like.shape[-1]]
    elif name == "mla_w_uq":
        g = g.reshape(g.shape[0], HEADS_PER_CHIP, HEAD_PAD)[:, :, :MLA_QK_DIM]
    return g.reshape(like.shape)


def _mixer_keys(i):
    kind, slot = i % 3, i // 3
    if kind == 0:
        return [(k, slot) for k in ("mla_w_down", "mla_w_uq", "mla_w_ukv", "mla_w_o")]
    pre = "sb" if kind == 1 else "ca"
    return [(f"{pre}_w_qkv", slot), (f"{pre}_w_o", slot)]


def _ffn_keys(i):
    return [("ffn_w_in", i), ("ffn_w_out", i)]


def _step(x, target, wl, dims, p, tabs, place):
    dist = place is not None
    wl = dict(wl)
    gather = _GatherStream(wl, [key for i in range(DEPTH) for key in _mixer_keys(i) + _ffn_keys(i)]) if dist else None
    fetch = gather.carry if dist else None

    def weight(key, user):
        if dist:
            gather.require([key], user)
        return wl[key]

    if dist:
        gather.require(_mixer_keys(0), "first")
    saved = []
    h = hm = x
    for i in range(DEPTH):
        kind, slot = i % 3, i // 3
        sv = {"h0m": hm}
        mix = _mixer_keys(i)
        if kind == 0:
            down = _fwd_row(f"mla_down_{i}", hm, weight(mix[0], f"mla_down_{i}"), F32, comm=fetch)
            gq, gkv = p["mla_q_norm_g"][slot][None], p["mla_kv_norm_g"][slot][None]
            cq, ckv, kr = _mla_mid_fwd(down, gq, gkv, tabs)
            q = _fwd_col(f"mla_uq_{i}", cq, weight(mix[1], f"mla_uq_{i}"), F32, comm=fetch)
            kv = _fwd_col(f"mla_ukv_{i}", ckv, weight(mix[2], f"mla_ukv_{i}"), MXU_DTYPE, comm=fetch)
            o, om, lse = _mla_attn_fwd(f"mla_attn_fwd_{i}", q, kv, kr, tabs, comm=fetch)
            sv.update(down=down, gq=gq, gkv=gkv, cq=cq, ckv=ckv, kr=kr, q=q, kv=kv, o=o, lse=lse)
        else:
            pre = "sb" if kind == 1 else "ca"
            qkv = _fwd_col(f"{pre}_qkv_{i}", hm, weight(mix[0], f"{pre}_qkv_{i}"), MXU_DTYPE, comm=fetch)
            if kind == 1:
                om = _sb_attn_fwd(f"sb_attn_fwd_{i}", qkv, comm=fetch)
            else:
                tiles = _ca_bias_tiles(p["ca_rel_bias"][slot])
                o, om, lse = _ca_attn_fwd(f"ca_attn_fwd_{i}", qkv, tiles, comm=fetch)
                sv.update(tiles=tiles, lse=lse, o=o)
            sv.update(qkv=qkv)
        m = _fwd_row(f"mixer_o_{i}", om, weight(mix[-1], f"mixer_o_{i}"), F32, comm=fetch)
        h1, h1m, sv["xh1"], sv["r1"] = _ln_fwd(f"ln_mix_{i}", h, m, p["ln_mix_g"][i][None], p["ln_mix_b"][i][None])
        u, a = _fwd_col(f"ffn_in_{i}", h1m, weight(("ffn_w_in", i), f"ffn_in_{i}"), None, epilogue=_epi_relu2,
                        dtypes=(F32, MXU_DTYPE), comm=fetch)
        y = _fwd_row(f"ffn_out_{i}", a, weight(("ffn_w_out", i), f"ffn_out_{i}"), F32, comm=fetch)
        h, hm, sv["xh2"], sv["r2"] = _ln_fwd(f"ln_ffn_{i}", h1, y, p["ln_ffn_g"][i][None], p["ln_ffn_b"][i][None])
        sv.update(om=om, h1m=h1m, u=u, a=a)
        saved.append(sv)

    reduction = _ReduceStream(dims, place) if dist else None
    carry = reduction.carry if dist else None
    part = {}

    def dw(key, xin, dyin):
        _, rows, cols = dims[key[0]]
        g = _dw(f"dw_{key[0]}_{key[1]}", xin, dyin, rows, cols, COL_SHARDED[key[0]], comm=carry)
        if dist:
            reduction.add(key, g)
        else:
            part[key] = g

    loss, dy = _loss_head(h, target)
    small = {k: [None] * DEPTH for k in ("ln_mix_g", "ln_mix_b", "ln_ffn_g", "ln_ffn_b")}
    n_mla = p["mla_q_norm_g"].shape[0]
    small["mla_q_norm_g"], small["mla_kv_norm_g"] = [None] * n_mla, [None] * n_mla
    for i in reversed(range(DEPTH)):
        kind, slot = i % 3, i // 3
        sv = saved[i]
        mix = _mixer_keys(i)
        dz, dzm, small["ln_ffn_g"][i], small["ln_ffn_b"][i] = _ln_bwd(f"ln_ffn_bwd_{i}", dy, sv["xh2"], sv["r2"],
                                                                      p["ln_ffn_g"][i][None])
        dw(("ffn_w_out", i), sv["a"], dzm)
        du = _dx_row(f"ffn_du_{i}", dzm, wl["ffn_w_out", i], MXU_DTYPE, epilogue=_epi_drelu2, extra=sv["u"], comm=carry)
        dw(("ffn_w_in", i), sv["h1m"], du)
        dy = _dx_col(f"ffn_dh_{i}", du, wl["ffn_w_in", i], F32, epilogue=_epi_residual, extra=dz, comm=carry)
        dz, dzm, small["ln_mix_g"][i], small["ln_mix_b"][i] = _ln_bwd(f"ln_mix_bwd_{i}", dy, sv["xh1"], sv["r1"],
                                                                      p["ln_mix_g"][i][None])
        dw(mix[-1], sv["om"], dzm)
        do = _dx_row(f"mixer_do_{i}", dzm, wl[mix[-1]], F32, comm=carry)
        last = carry if i else None
        if kind == 0:
            dq, dkv, dkr = _mla_attn_bwd(f"mla_attn_bwd_{i}", sv["q"], sv["kv"], sv["kr"], tabs, do, sv["o"],
                                         sv["lse"], comm=carry)
            dw(("mla_w_uq", slot), sv["cq"], dq)
            dcq = _dx_col(f"mla_dcq_{i}", dq, wl["mla_w_uq", slot], F32, comm=carry)
            dw(("mla_w_ukv", slot), sv["ckv"], dkv)
            dckv = _dx_col(f"mla_dckv_{i}", dkv, wl["mla_w_ukv", slot], F32, comm=carry)
            ddown, small["mla_q_norm_g"][slot], small["mla_kv_norm_g"][slot] = _mla_mid_bwd(
                sv["down"], dcq, dckv, dkr, sv["gq"], sv["gkv"], tabs)
            dw(("mla_w_down", slot), sv["h0m"], ddown)
            dy = _dx_row(f"mla_dh_{i}", ddown, wl["mla_w_down", slot], F32, epilogue=_epi_residual, extra=dz,
                         comm=last)
        else:
            pre = "sb" if kind == 1 else "ca"
            if kind == 1:
                dq, dk, dv = _sb_attn_bwd(f"sb_attn_bwd_{i}", sv["qkv"], do, comm=carry)
            else:
                dq, dk, dv, dtiles = _ca_attn_bwd(f"ca_attn_bwd_{i}", sv["qkv"], sv["tiles"], do, sv["o"], sv["lse"],
                                                  comm=carry)
                small["ca_rel_bias"] = [jnp.transpose(_ca_table_grad(dtiles)[:, 0, :REL_TABLE])]
            dqkv = jnp.concatenate([dq, dk, dv], 1)
            dw((f"{pre}_w_qkv", slot), sv["h0m"], dqkv)
            dy = _dx_col(f"{pre}_dh_{i}", dqkv, wl[f"{pre}_w_qkv", slot], F32, epilogue=_epi_residual, extra=dz,
                         comm=last)
    small = {k: jnp.stack([g.reshape(g.shape[-2:]) if k == "ca_rel_bias" else g[0] for g in v]) for k, v in small.items()}
    return loss, dy, (reduction.finish() if dist else part), small


SMALL = ("ln_mix_g", "ln_mix_b", "ln_ffn_g", "ln_ffn_b", "mla_q_norm_g", "mla_kv_norm_g", "ca_rel_bias")


def _pack_small(parts, width):
    flat = jnp.concatenate([parts[k].reshape(-1) for k in SMALL])
    rows = -(-flat.shape[0] // width)
    rows += -rows % 8
    return jnp.pad(flat, (0, rows * width - flat.shape[0])).reshape(rows, width)


def _unpack_small(pack, like):
    flat, out, at = pack.reshape(-1), {}, 0
    for k in SMALL:
        n = int(np.prod(like[k].shape))
        out[k] = flat[at:at + n].reshape(like[k].shape)
        at += n
    return out


def kernel(x, ln_mix_g, ln_mix_b, ln_ffn_g, ln_ffn_b, ffn_w_in, ffn_w_out, mla_w_down, mla_q_norm_g, mla_w_uq, mla_kv_norm_g, mla_w_ukv, mla_w_o, sb_w_qkv, sb_w_o, ca_w_qkv, ca_rel_bias, ca_w_o, loss_target, m_ln_mix_g, m_ln_mix_b, m_ln_ffn_g, m_ln_ffn_b, m_ffn_w_in, m_ffn_w_out, m_mla_w_down, m_mla_q_norm_g, m_mla_w_uq, m_mla_kv_norm_g, m_mla_w_ukv, m_mla_w_o, m_sb_w_qkv, m_sb_w_o, m_ca_w_qkv, m_ca_rel_bias, m_ca_w_o, v_ln_mix_g, v_ln_mix_b, v_ln_ffn_g, v_ln_ffn_b, v_ffn_w_in, v_ffn_w_out, v_mla_w_down, v_mla_q_norm_g, v_mla_w_uq, v_mla_kv_norm_g, v_mla_w_ukv, v_mla_w_o, v_sb_w_qkv, v_sb_w_o, v_ca_w_qkv, v_ca_rel_bias, v_ca_w_o):
    w = dict(zip(WEIGHTS, (ln_mix_g, ln_mix_b, ln_ffn_g, ln_ffn_b, ffn_w_in, ffn_w_out, mla_w_down, mla_q_norm_g,
                           mla_w_uq, mla_kv_norm_g, mla_w_ukv, mla_w_o, sb_w_qkv, sb_w_o, ca_w_qkv, ca_rel_bias,
                           ca_w_o)))
    mom1 = dict(zip(WEIGHTS, (m_ln_mix_g, m_ln_mix_b, m_ln_ffn_g, m_ln_ffn_b, m_ffn_w_in, m_ffn_w_out, m_mla_w_down,
                              m_mla_q_norm_g, m_mla_w_uq, m_mla_kv_norm_g, m_mla_w_ukv, m_mla_w_o, m_sb_w_qkv,
                              m_sb_w_o, m_ca_w_qkv, m_ca_rel_bias, m_ca_w_o)))
    mom2 = dict(zip(WEIGHTS, (v_ln_mix_g, v_ln_mix_b, v_ln_ffn_g, v_ln_ffn_b, v_ffn_w_in, v_ffn_w_out, v_mla_w_down,
                              v_mla_q_norm_g, v_mla_w_uq, v_mla_kv_norm_g, v_mla_w_ukv, v_mla_w_o, v_sb_w_qkv,
                              v_sb_w_o, v_ca_w_qkv, v_ca_rel_bias, v_ca_w_o)))
    xi, yi, ci = lax.axis_index("x"), lax.axis_index("y"), lax.axis_index("c")
    chip = 2 * xi + yi
    d_model = x.shape[-1]

    shards = _mxu_shards(w)
    layers = {k: w[k].shape[0] for k in BIG}
    dims = {k: (layers[k], shards[k].shape[0] // layers[k], shards[k].shape[1]) for k in BIG}
    chip1 = jnp.reshape(chip, (1,)).astype(jnp.int32)
    wl = {(k, l): _cast_place(f"cast_{k}_{l}", shards[k], chip1, l, dims[k][1]) for k in BIG for l in range(layers[k])}
    gains = jnp.stack([w["mla_q_norm_g"], w["mla_kv_norm_g"]])
    gains = jnp.where(ci == 0, gains, 0.0)
    placed = lax.dynamic_update_slice_in_dim(jnp.zeros((*gains.shape[:2], N_CHIPS, gains.shape[2]), F32),
                                             gains[:, :, None], chip, 2)
    full_gains = _all_reduce_small("norm_gain_gather", placed.reshape(2 * gains.shape[1], -1))
    full_gains = full_gains.reshape(2, gains.shape[1], -1)
    p = {"ln_mix_g": ln_mix_g, "ln_mix_b": ln_mix_b, "ln_ffn_g": ln_ffn_g, "ln_ffn_b": ln_ffn_b,
         "mla_q_norm_g": full_gains[0], "mla_kv_norm_g": full_gains[1], "ca_rel_bias": ca_rel_bias}

    place = jnp.stack([chip, ci]).astype(jnp.int32)
    loss, grad_x, total, small = _step(x[0], loss_target[0], wl, dims, p, _rope_tables(x.shape[1]), place)
    loss = lax.psum(loss[0, 0], ("x", "y", "c"))

    small = _unpack_small(_all_reduce_small("small_grad_all_reduce", _pack_small(small, d_model)), small)
    grad = {k: small[k] for k in ("ln_mix_g", "ln_mix_b", "ln_ffn_g", "ln_ffn_b", "ca_rel_bias")}
    for k in ("mla_q_norm_g", "mla_kv_norm_g"):
        g = small[k].reshape(small[k].shape[0], N_CHIPS, -1)
        grad[k] = lax.dynamic_index_in_dim(g, chip, 1, keepdims=False)

    for k in BIG:
        grad[k] = _unpad_grad(k, total[k], w[k])

    delta, new_m, new_v = {}, {}, {}
    for k in WEIGHTS:
        flat = lambda a: a.reshape(-1, a.shape[-1])
        dl, m2, v2 = _adamw(f"adamw_{k}", flat(w[k]), flat(grad[k]), flat(mom1[k]), flat(mom2[k]))
        delta[k], new_m[k], new_v[k] = dl.reshape(w[k].shape), m2.reshape(w[k].shape), v2.reshape(w[k].shape)
    return (loss, grad_x[None], *[grad[k] for k in WEIGHTS], *[delta[k] for k in WEIGHTS],
            *[new_m[k] for k in WEIGHTS], *[new_v[k] for k in WEIGHTS])
```

```python
import functools

import numpy as np
import jax
import jax.numpy as jnp
from jax import lax
from jax.experimental import pallas as pl
from jax.experimental.pallas import tpu as pltpu

F32, BF16 = jnp.float32, jnp.bfloat16
MXU_DTYPE = BF16
WIRE_DTYPE = BF16

DEPTH = 4
HEADS = 16
HEAD_DIM = 128
CHUNK_SHIFT = 6
TQ = 512
MLA_ROPE = 64
MLA_QK_DIM = 192
HEAD_PAD = 256
CA_LEFT_CHUNKS = 8
CA_LEFT_BLOCKS = (CA_LEFT_CHUNKS << CHUNK_SHIFT) // TQ
CA_TILES = min(CA_LEFT_BLOCKS, 2) + 1
REL_CLIP_LEFT = 128
REL_TABLE = 192
ROPE_THETA = 10000.0
LN_EPS = 1e-5
RMS_EPS = 1e-6
ALPHA = (2.0 * DEPTH) ** 0.25
NEG = -1e30
ADAM_LR, ADAM_B1, ADAM_B2, ADAM_EPS, ADAM_WD, ADAM_STEP = 0.001, 0.9, 0.999, 1e-08, 0.01, 10
N_CHIPS = 4
N_DEV = 8
VMEM_LIMIT = 48 << 20
MXU_FLOPS_PER_US = 0.9e9
ICI_BYTES_PER_US = 69.9e3
CHUNK_US = 45.0
LINK_SHARE = 1.2
ATTN_US = {"mla_attn_fwd": 155.0, "mla_attn_bwd": 295.0, "sb_attn_fwd": 300.0, "sb_attn_bwd": 610.0,
           "ca_attn_fwd": 100.0, "ca_attn_bwd": 175.0}
MESH = pl.DeviceIdType.MESH
ANY = pl.BlockSpec(memory_space=pl.ANY)
VMEM_SPEC = pl.BlockSpec(memory_space=pltpu.VMEM)

NN = (((1,), (0,)), ((), ()))
NT = (((1,), (1,)), ((), ()))
TN = (((0,), (0,)), ((), ()))


def _dot(a, b, dims=NN):
    return lax.dot_general(a, b, dims, preferred_element_type=F32)


def _exact_dot(x, u):
    hi = x.astype(BF16)
    r1 = x - hi.astype(F32)
    mid = r1.astype(BF16)
    lo = (r1 - mid.astype(F32)).astype(BF16)
    return _dot(hi, u) + _dot(mid, u) + _dot(lo, u)


def _params(*sem):
    return pltpu.CompilerParams(dimension_semantics=sem, vmem_limit_bytes=VMEM_LIMIT)


TILE_K = 2048


def _tile(n, pref):
    for t in (2048, 1536, 1152, 1024, 768, 512, 384, 256, 128):
        if t <= pref and n % t == 0:
            return t
    return n


class _Comm:
    def __init__(self, srcs, dsts, alias, copies, n):
        self.srcs, self.dsts, self.alias, self.copies, self.n = list(srcs), list(dsts), dict(alias), copies, n
        self.out, self.then = None, None

    def deliver(self, out):
        self.out = list(out)
        if self.then is not None:
            self.then(self.out)


class _SemView:
    def __init__(self, sems, base):
        self.sems, self.base, self.at = sems, base, self

    def __getitem__(self, i):
        return self.sems.at[self.base + i]


class _Merged(_Comm):
    def __init__(self, parts):
        srcs, dsts, alias, n, self.spans = [], [], {}, 0, []
        for c in parts:
            self.spans.append((c, len(srcs), len(dsts), n))
            alias.update({len(srcs) + s: len(dsts) + d for s, d in c.alias.items()})
            srcs, dsts, n = srcs + c.srcs, dsts + c.dsts, n + c.n

        def copies(src_refs, dst_refs, send, recv):
            out = []
            for c, s0, d0, n0 in self.spans:
                out += c.copies(src_refs[s0:s0 + len(c.srcs)], dst_refs[d0:d0 + len(c.dsts)], _SemView(send, n0),
                                _SemView(recv, n0))
            return out

        super().__init__(srcs, dsts, alias, copies, n)

    def deliver(self, out):
        self.out = list(out)
        for c, _, d0, _ in self.spans:
            c.deliver(self.out[d0:d0 + len(c.dsts)])


def _merged(*comms):
    comms = [c for c in comms if c is not None]
    return None if not comms else comms[0] if len(comms) == 1 else _Merged(comms)


def _call(body, args, *, name, grid, in_specs, out_specs, out_shape, scratch_shapes=(), semantics, comm=None, us=0.0):
    in_specs, out_specs, out_shape = list(in_specs), list(out_specs), list(out_shape)
    scratch_shapes = list(scratch_shapes)
    if callable(comm):
        comm = comm(us or ATTN_US.get(name.rsplit("_", 1)[0], 0.0))
    if comm is None:
        return pl.pallas_call(body, grid=grid, in_specs=in_specs, out_specs=out_specs, out_shape=out_shape,
                              scratch_shapes=scratch_shapes, name=name, compiler_params=_params(*semantics))(*args)
    n_in, n_out, n_scr, ns, nd = len(in_specs), len(out_specs), len(scratch_shapes), len(comm.srcs), len(comm.dsts)

    def carrier(*refs):
        ins, refs = refs[:n_in], refs[n_in:]
        srcs, refs = refs[:ns], refs[ns:]
        outs, refs = refs[:n_out], refs[n_out:]
        dsts, refs = refs[:nd], refs[nd:]
        scratch, (send, recv) = refs[:n_scr], refs[n_scr:]
        ids = [pl.program_id(a) for a in range(len(grid))]
        first = functools.reduce(jnp.logical_and, [i == 0 for i in ids])
        last = functools.reduce(jnp.logical_and, [i == g - 1 for i, g in zip(ids, grid)])

        @pl.when(first)
        def _():
            for cp in comm.copies(srcs, dsts, send, recv):
                cp.start()

        body(*ins, *outs, *scratch)

        @pl.when(last)
        def _():
            for cp in comm.copies(srcs, dsts, send, recv):
                cp.wait()

    res = pl.pallas_call(
        carrier, grid=grid, in_specs=in_specs + [ANY] * ns, out_specs=out_specs + [ANY] * nd,
        out_shape=out_shape + comm.dsts,
        scratch_shapes=scratch_shapes + [pltpu.SemaphoreType.DMA((comm.n,)), pltpu.SemaphoreType.DMA((comm.n,))],
        input_output_aliases={n_in + s: n_out + d for s, d in comm.alias.items()}, name=name,
        compiler_params=_params(*["arbitrary"] * len(grid)))(*args, *comm.srcs)
    comm.deliver(res[n_out:])
    return res[:n_out]


def _mm(name, a, b, extras, *, grid, a_spec, b_spec, extra_specs, out_specs, out_shape, dims,
        epilogue, acc_shape, comm=None):
    nk = grid[2]
    n_ex = len(extras)

    def product(a_ref, b_ref):
        return lax.dot_general(a_ref[...].astype(MXU_DTYPE), b_ref[...].astype(MXU_DTYPE), dims,
                               preferred_element_type=F32)

    def whole(*refs):
        epilogue(product(*refs[:2]), refs[2:2 + n_ex], refs[2 + n_ex:])

    def stepped(*refs):
        a_ref, b_ref = refs[:2]
        ex = refs[2:2 + n_ex]
        outs = refs[2 + n_ex:-1]
        acc = refs[-1]
        k = pl.program_id(2)

        @pl.when(k == 0)
        def _():
            acc[...] = product(a_ref, b_ref)

        @pl.when(k > 0)
        def _():
            acc[...] += product(a_ref, b_ref)

        @pl.when(k == nk - 1)
        def _():
            epilogue(acc[...], ex, outs)

    flops = 2.0 * grid[0] * grid[1] * grid[2] * acc_shape[1] * a_spec.block_shape[-1] * a_spec.block_shape[-2]
    return _call(whole if nk == 1 else stepped, (a, b, *extras), name=name, grid=grid,
                 in_specs=[a_spec, b_spec, *extra_specs], out_specs=out_specs, out_shape=out_shape,
                 scratch_shapes=[] if nk == 1 else [pltpu.VMEM(acc_shape, F32)],
                 semantics=("parallel", "parallel", "arbitrary"), comm=comm, us=flops / MXU_FLOPS_PER_US)


def _epi_store(acc, ex, outs):
    outs[0][...] = acc.astype(outs[0].dtype)


def _epi_relu2(acc, ex, outs):
    outs[0][...] = acc
    r = jnp.maximum(acc, 0.0)
    outs[1][...] = (r * r).astype(outs[1].dtype)


def _epi_drelu2(acc, ex, outs):
    outs[0][...] = (acc * (2.0 * jnp.maximum(ex[0][...], 0.0))).astype(outs[0].dtype)


def _epi_residual(acc, ex, outs):
    outs[0][...] = acc + ALPHA * ex[0][...]


def _fwd_col(name, x, wg, dtype, epilogue=_epi_store, dtypes=None, comm=None):
    m, (_, rows, cols) = x.shape[0], wg.shape
    tm, tn, tk = _tile(m, 1024), _tile(cols, 1024), _tile(rows, TILE_K)
    nps = cols // tn
    dtypes = dtypes or (dtype,)
    out = pl.BlockSpec((tm, tn), lambda i, j, k: (i, j))
    res = _mm(name, x, wg, (), grid=(m // tm, N_CHIPS * nps, rows // tk),
              a_spec=pl.BlockSpec((tm, tk), lambda i, j, k: (i, k)),
              b_spec=pl.BlockSpec((None, tk, tn), lambda i, j, k: (j // nps, k, j % nps)),
              extra_specs=(), out_specs=[out] * len(dtypes),
              out_shape=[jax.ShapeDtypeStruct((m, N_CHIPS * cols), d) for d in dtypes],
              dims=NN, epilogue=epilogue, acc_shape=(tm, tn), comm=comm)
    return res if len(dtypes) > 1 else res[0]


def _fwd_row(name, x, wg, dtype, comm=None):
    m, (_, rows, cols) = x.shape[0], wg.shape
    tm, tn, tk = _tile(m, 1024), _tile(cols, 1024), _tile(rows, TILE_K)
    kps = rows // tk
    return _mm(name, x, wg, (), grid=(m // tm, cols // tn, N_CHIPS * kps),
               a_spec=pl.BlockSpec((tm, tk), lambda i, j, k: (i, k)),
               b_spec=pl.BlockSpec((None, tk, tn), lambda i, j, k: (k // kps, k % kps, j)),
               extra_specs=(), out_specs=[pl.BlockSpec((tm, tn), lambda i, j, k: (i, j))],
               out_shape=[jax.ShapeDtypeStruct((m, cols), dtype)],
               dims=NN, epilogue=_epi_store, acc_shape=(tm, tn), comm=comm)[0]


def _dx_col(name, dy, wg, dtype, epilogue=_epi_store, extra=None, comm=None):
    m, (_, rows, cols) = dy.shape[0], wg.shape
    tm, tn, tk = _tile(m, 1024), _tile(rows, 1024), _tile(cols, TILE_K)
    kps = cols // tk
    tile = pl.BlockSpec((tm, tn), lambda i, j, k: (i, j))
    return _mm(name, dy, wg, () if extra is None else (extra,), grid=(m // tm, rows // tn, N_CHIPS * kps),
               a_spec=pl.BlockSpec((tm, tk), lambda i, j, k: (i, k)),
               b_spec=pl.BlockSpec((None, tn, tk), lambda i, j, k: (k // kps, j, k % kps)),
               extra_specs=() if extra is None else (tile,), out_specs=[tile],
               out_shape=[jax.ShapeDtypeStruct((m, rows), dtype)],
               dims=NT, epilogue=epilogue, acc_shape=(tm, tn), comm=comm)[0]


def _dx_row(name, dy, wg, dtype, epilogue=_epi_store, extra=None, comm=None):
    m, (_, rows, cols) = dy.shape[0], wg.shape
    tm, tn, tk = _tile(m, 1024), _tile(rows, 1024), _tile(cols, TILE_K)
    nps = rows // tn
    tile = pl.BlockSpec((tm, tn), lambda i, j, k: (i, j))
    return _mm(name, dy, wg, () if extra is None else (extra,), grid=(m // tm, N_CHIPS * nps, cols // tk),
               a_spec=pl.BlockSpec((tm, tk), lambda i, j, k: (i, k)),
               b_spec=pl.BlockSpec((None, tn, tk), lambda i, j, k: (j // nps, j % nps, k)),
               extra_specs=() if extra is None else (tile,), out_specs=[tile],
               out_shape=[jax.ShapeDtypeStruct((m, N_CHIPS * rows), dtype)],
               dims=NT, epilogue=epilogue, acc_shape=(tm, tn), comm=comm)[0]


def _dw(name, x, dy, rows, cols, col_sharded, comm=None):
    s_tok = x.shape[0]
    tm, tn, tk = _tile(rows, 1024), _tile(cols, 1024), _tile(s_tok, TILE_K)
    mt, nps = rows // tm, cols // tn
    if col_sharded:
        grid = (mt, N_CHIPS * nps, s_tok // tk)
        out = pl.BlockSpec((None, tm, tn), lambda i, j, k: (j // nps, i, j % nps))
    else:
        grid = (N_CHIPS * mt, nps, s_tok // tk)
        out = pl.BlockSpec((None, tm, tn), lambda i, j, k: (i // mt, i % mt, j))
    return _mm(name, x, dy, (), grid=grid,
               a_spec=pl.BlockSpec((tk, tm), lambda i, j, k: (k, i)),
               b_spec=pl.BlockSpec((tk, tn), lambda i, j, k: (k, j)),
               extra_specs=(), out_specs=[out],
               out_shape=[jax.ShapeDtypeStruct((N_CHIPS, rows, cols), WIRE_DTYPE)],
               dims=TN, epilogue=_epi_store, acc_shape=(tm, tn), comm=comm)[0]


def _ln_fwd(name, h, m, g, b):
    s, d = h.shape
    tm = _tile(s, 256)
    row = pl.BlockSpec((tm, d), lambda i: (i, 0))
    vec = pl.BlockSpec((1, d), lambda i: (0, 0))

    def body(h_ref, m_ref, g_ref, b_ref, y_ref, ymx_ref, xh_ref, r_ref):
        z = ALPHA * h_ref[...] + m_ref[...]
        mu = jnp.mean(z, -1, keepdims=True)
        zc = z - mu
        r = lax.rsqrt(jnp.mean(zc * zc, -1, keepdims=True) + LN_EPS)
        xh = zc * r
        xh_ref[...] = xh
        r_ref[...] = r
        y = xh * g_ref[...] + b_ref[...]
        y_ref[...] = y
        ymx_ref[...] = y.astype(ymx_ref.dtype)

    return pl.pallas_call(
        body, grid=(s // tm,), in_specs=[row, row, vec, vec],
        out_specs=[row, row, row, pl.BlockSpec((tm, 1), lambda i: (i, 0))],
        out_shape=[jax.ShapeDtypeStruct((s, d), F32), jax.ShapeDtypeStruct((s, d), MXU_DTYPE),
                   jax.ShapeDtypeStruct((s, d), F32), jax.ShapeDtypeStruct((s, 1), F32)],
        name=name, compiler_params=_params("parallel"))(h, m, g, b)


def _ln_bwd(name, dy, xh, r, g):
    s, d = dy.shape
    tm = _tile(s, 256)
    row = pl.BlockSpec((tm, d), lambda i: (i, 0))
    vec = pl.BlockSpec((1, d), lambda i: (0, 0))

    def body(dy_ref, xh_ref, r_ref, g_ref, dz_ref, dzmx_ref, dg_ref, db_ref):
        i = pl.program_id(0)
        dy_, xh_ = dy_ref[...], xh_ref[...]
        dyg = dy_ * g_ref[...]
        m1 = jnp.mean(dyg, -1, keepdims=True)
        m2 = jnp.mean(dyg * xh_, -1, keepdims=True)
        dz = r_ref[...] * (dyg - m1 - xh_ * m2)
        dz_ref[...] = dz
        dzmx_ref[...] = dz.astype(dzmx_ref.dtype)
        pg = jnp.sum(dy_ * xh_, 0, keepdims=True)
        pb = jnp.sum(dy_, 0, keepdims=True)

        @pl.when(i == 0)
        def _():
            dg_ref[...] = pg
            db_ref[...] = pb

        @pl.when(i > 0)
        def _():
            dg_ref[...] += pg
            db_ref[...] += pb

    return pl.pallas_call(
        body, grid=(s // tm,), in_specs=[row, row, pl.BlockSpec((tm, 1), lambda i: (i, 0)), vec],
        out_specs=[row, row, vec, vec],
        out_shape=[jax.ShapeDtypeStruct((s, d), F32), jax.ShapeDtypeStruct((s, d), MXU_DTYPE),
                   jax.ShapeDtypeStruct((1, d), F32), jax.ShapeDtypeStruct((1, d), F32)],
        name=name, compiler_params=_params("arbitrary"))(dy, xh, r, g)


def _loss_head(y, t):
    s, d = y.shape
    tm = _tile(s, 256)
    row = pl.BlockSpec((tm, d), lambda i: (i, 0))

    def body(y_ref, t_ref, l_ref, dy_ref):
        i = pl.program_id(0)
        e = y_ref[...] - t_ref[...]
        dy_ref[...] = e * (1.0 / d)
        part = 0.5 * jnp.sum(jnp.mean(e * e, -1, keepdims=True), 0, keepdims=True)

        @pl.when(i == 0)
        def _():
            l_ref[...] = part

        @pl.when(i > 0)
        def _():
            l_ref[...] += part

    return pl.pallas_call(
        body, grid=(s // tm,), in_specs=[row, row],
        out_specs=[pl.BlockSpec((1, 1), lambda i: (0, 0)), row],
        out_shape=[jax.ShapeDtypeStruct((1, 1), F32), jax.ShapeDtypeStruct((s, d), F32)],
        name="loss_head", compiler_params=_params("arbitrary"))(y, t)


def _rope_tables(s):
    half = MLA_ROPE // 2
    inv = ROPE_THETA ** (-jnp.arange(half, dtype=F32) / half)
    ang = jnp.arange(s).astype(F32)[:, None] * inv[None, :]
    cos, sin = jnp.cos(ang), jnp.sin(ang)
    c = jnp.concatenate([cos, cos, jnp.ones((s, 128 - MLA_ROPE), F32)], 1)
    s1 = jnp.concatenate([-sin, jnp.zeros((s, 128 - half), F32)], 1)
    s2 = jnp.concatenate([jnp.zeros((s, half), F32), sin, jnp.zeros((s, 128 - MLA_ROPE), F32)], 1)
    return c, s1, s2


def _rope(x, c, s1, s2):
    half = MLA_ROPE // 2
    return x * c + pltpu.roll(x, 128 - half, 1) * s1 + pltpu.roll(x, half, 1) * s2


def _rope_t(dy, c, s1, s2):
    half = MLA_ROPE // 2
    return dy * c + pltpu.roll(dy * s1, half, 1) + pltpu.roll(dy * s2, 128 - half, 1)


def _mla_mid_fwd(down, gq, gkv, tabs):
    s, w = down.shape
    ql, kvl = gq.shape[1], gkv.shape[1]
    tm = _tile(s, 256)

    def body(d_ref, gq_ref, gkv_ref, c_ref, s1_ref, s2_ref, cq_ref, ckv_ref, kr_ref):
        cq = d_ref[:, :ql]
        ckv = d_ref[:, ql:ql + kvl]
        cq_ref[...] = (cq * lax.rsqrt(jnp.mean(cq * cq, -1, keepdims=True) + RMS_EPS)
                       * gq_ref[...]).astype(cq_ref.dtype)
        ckv_ref[...] = (ckv * lax.rsqrt(jnp.mean(ckv * ckv, -1, keepdims=True) + RMS_EPS)
                        * gkv_ref[...]).astype(ckv_ref.dtype)
        kr_ref[...] = _rope(d_ref[:, ql + kvl:], c_ref[...], s1_ref[...], s2_ref[...]).astype(kr_ref.dtype)

    tab = pl.BlockSpec((tm, 128), lambda i: (i, 0))
    return pl.pallas_call(
        body, grid=(s // tm,),
        in_specs=[pl.BlockSpec((tm, w), lambda i: (i, 0)), pl.BlockSpec((1, ql), lambda i: (0, 0)),
                  pl.BlockSpec((1, kvl), lambda i: (0, 0)), tab, tab, tab],
        out_specs=[pl.BlockSpec((tm, ql), lambda i: (i, 0)), pl.BlockSpec((tm, kvl), lambda i: (i, 0)), tab],
        out_shape=[jax.ShapeDtypeStruct((s, ql), MXU_DTYPE), jax.ShapeDtypeStruct((s, kvl), MXU_DTYPE),
                   jax.ShapeDtypeStruct((s, 128), MXU_DTYPE)],
        name="mla_mid_fwd", compiler_params=_params("parallel"))(down, gq, gkv, *tabs)


def _mla_mid_bwd(down, dcq, dckv, dkr, gq, gkv, tabs):
    s, w = down.shape
    ql, kvl = gq.shape[1], gkv.shape[1]
    tm = _tile(s, 256)

    def rms_bwd(x, dy, g):
        n = x.shape[1]
        r = lax.rsqrt(jnp.mean(x * x, -1, keepdims=True) + RMS_EPS)
        dyg = dy * g
        dx = r * dyg - x * (r * r * r * (1.0 / n)) * jnp.sum(dyg * x, -1, keepdims=True)
        return dx, jnp.sum(dy * x * r, 0, keepdims=True)

    def body(d_ref, dcq_ref, dckv_ref, dkr_ref, gq_ref, gkv_ref, c_ref, s1_ref, s2_ref, o_ref, dgq_ref, dgkv_ref):
        i = pl.program_id(0)
        dxq, pq = rms_bwd(d_ref[:, :ql], dcq_ref[...], gq_ref[...])
        dxkv, pkv = rms_bwd(d_ref[:, ql:ql + kvl], dckv_ref[...], gkv_ref[...])
        o_ref[:, :ql] = dxq.astype(o_ref.dtype)
        o_ref[:, ql:ql + kvl] = dxkv.astype(o_ref.dtype)
        o_ref[:, ql + kvl:] = _rope_t(dkr_ref[...], c_ref[...], s1_ref[...], s2_ref[...]).astype(o_ref.dtype)

        @pl.when(i == 0)
        def _():
            dgq_ref[...] = pq
            dgkv_ref[...] = pkv

        @pl.when(i > 0)
        def _():
            dgq_ref[...] += pq
            dgkv_ref[...] += pkv

    tab = pl.BlockSpec((tm, 128), lambda i: (i, 0))
    vq = pl.BlockSpec((1, ql), lambda i: (0, 0))
    vkv = pl.BlockSpec((1, kvl), lambda i: (0, 0))
    full = pl.BlockSpec((tm, w), lambda i: (i, 0))
    return pl.pallas_call(
        body, grid=(s // tm,),
        in_specs=[full, pl.BlockSpec((tm, ql), lambda i: (i, 0)), pl.BlockSpec((tm, kvl), lambda i: (i, 0)), tab,
                  vq, vkv, tab, tab, tab],
        out_specs=[full, vq, vkv],
        out_shape=[jax.ShapeDtypeStruct((s, w), MXU_DTYPE), jax.ShapeDtypeStruct((1, ql), F32),
                   jax.ShapeDtypeStruct((1, kvl), F32)],
        name="mla_mid_bwd", compiler_params=_params("arbitrary"))(down, dcq, dckv, dkr, gq, gkv, *tabs)


def _iota2():
    return (lax.broadcasted_iota(jnp.int32, (TQ, TQ), 0), lax.broadcasted_iota(jnp.int32, (TQ, TQ), 1))


def _mla_attn_fwd(name, q, kv, kr, tabs, comm=None):
    s = q.shape[0]
    nq = s // TQ
    scale = MLA_QK_DIM ** -0.5

    def body(q_ref, kv_ref, kr_ref, c_ref, s1_ref, s2_ref, o_ref, omx_ref, lse_ref):
        i = pl.program_id(1)
        row, col = _iota2()
        tabs_i = (c_ref[...], s1_ref[...], s2_ref[...])
        qn = [q_ref[:, lo].astype(MXU_DTYPE) for lo, _ in PAD_COLS]
        qr = [_rope(q_ref[:, hi], *tabs_i).astype(MXU_DTYPE) for _, hi in PAD_COLS]
        qc = jnp.right_shift(i * TQ + row, CHUNK_SHIFT)

        def step(kb, carry):
            ks = pl.multiple_of(kb * TQ, TQ)
            krb = kr_ref[pl.ds(ks, TQ), :].astype(MXU_DTYPE)
            mask = jnp.right_shift(ks + col, CHUNK_SHIFT) <= qc
            out = []
            for j, ((lo, hi), (m, l, acc)) in enumerate(zip(PAD_COLS, carry)):
                sc = (_dot(qn[j], kv_ref[pl.ds(ks, TQ), lo].astype(MXU_DTYPE), NT) + _dot(qr[j], krb, NT)) * scale
                sc = jnp.where(mask, sc, NEG)
                m_new = jnp.maximum(m, jnp.max(sc, -1, keepdims=True))
                p = jnp.exp(sc - m_new)
                corr = jnp.exp(m - m_new)
                l = corr * l + jnp.sum(p, -1, keepdims=True)
                acc = corr * acc + _dot(p.astype(MXU_DTYPE), kv_ref[pl.ds(ks, TQ), hi].astype(MXU_DTYPE))
                out.append((m_new, l, acc))
            return tuple(out)

        res = lax.fori_loop(0, i + 1, step,
                            tuple((jnp.full((TQ, 1), NEG, F32), jnp.zeros((TQ, 1), F32),
                                   jnp.zeros((TQ, HEAD_DIM), F32)) for _ in HEAD_COLS))
        for j, (hd, (m, l, acc)) in enumerate(zip(HEAD_COLS, res)):
            o = acc / l
            o_ref[:, hd] = o
            omx_ref[:, hd] = o.astype(omx_ref.dtype)
            lse_ref[j] = m + jnp.log(l)

    tab = pl.BlockSpec((TQ, 128), lambda h, i: (i, 0))
    oblk = pl.BlockSpec((TQ, HEADS_PER_STEP * HEAD_DIM), lambda h, i: (i, h))
    return _call(
        body, (q, kv, kr, *tabs), grid=(HEADS // HEADS_PER_STEP, nq),
        in_specs=[pl.BlockSpec((TQ, HEADS_PER_STEP * HEAD_PAD), lambda h, i: (i, h)),
                  pl.BlockSpec((s, HEADS_PER_STEP * HEAD_PAD), lambda h, i: (0, h)),
                  pl.BlockSpec((s, 128), lambda h, i: (0, 0)), tab, tab, tab],
        out_specs=[oblk, oblk, pl.BlockSpec((HEADS_PER_STEP, TQ, 1), lambda h, i: (h, i, 0))],
        out_shape=[jax.ShapeDtypeStruct((s, HEADS * HEAD_DIM), F32),
                   jax.ShapeDtypeStruct((s, HEADS * HEAD_DIM), MXU_DTYPE), jax.ShapeDtypeStruct((HEADS, s, 1), F32)],
        name=name, semantics=("parallel", "parallel"), comm=comm)


def _mla_attn_bwd(name, q, kv, kr, tabs, do, o, lse, comm=None):
    s = q.shape[0]
    nq = s // TQ
    scale = MLA_QK_DIM ** -0.5

    def body(q_ref, kv_ref, kr_ref, c_ref, s1_ref, s2_ref, do_ref, o_ref, lse_ref,
             dq_ref, dkv_ref, dkr_ref, dkv_acc, dkr_acc):
        h, i = pl.program_id(0), pl.program_id(1)
        row, col = _iota2()

        @pl.when(i == 0)
        def _():
            dkv_acc[...] = jnp.zeros_like(dkv_acc)

        @pl.when((h == 0) & (i == 0))
        def _():
            dkr_acc[...] = jnp.zeros_like(dkr_acc)

        tabs_i = (c_ref[...], s1_ref[...], s2_ref[...])
        qn = [q_ref[:, lo].astype(MXU_DTYPE) for lo, _ in PAD_COLS]
        qr = [_rope(q_ref[:, hi], *tabs_i).astype(MXU_DTYPE) for _, hi in PAD_COLS]
        qc = jnp.right_shift(i * TQ + row, CHUNK_SHIFT)
        delta = [jnp.sum(do_ref[:, hd] * o_ref[:, hd], -1, keepdims=True) for hd in HEAD_COLS]
        lse_ = [lse_ref[j] for j in range(HEADS_PER_STEP)]
        dob = [do_ref[:, hd].astype(MXU_DTYPE) for hd in HEAD_COLS]

        def step(kb, carry):
            ks = pl.multiple_of(kb * TQ, TQ)
            krb = kr_ref[pl.ds(ks, TQ), :].astype(MXU_DTYPE)
            mask = jnp.right_shift(ks + col, CHUNK_SHIFT) <= qc
            out = []
            for j, ((lo, hi), (dqn, dqr)) in enumerate(zip(PAD_COLS, carry)):
                kn = kv_ref[pl.ds(ks, TQ), lo].astype(MXU_DTYPE)
                v = kv_ref[pl.ds(ks, TQ), hi].astype(MXU_DTYPE)
                sc = (_dot(qn[j], kn, NT) + _dot(qr[j], krb, NT)) * scale
                sc = jnp.where(mask, sc, NEG)
                p = jnp.exp(sc - lse_[j])
                ds = (p * (_dot(dob[j], v, NT) - delta[j]) * scale).astype(MXU_DTYPE)
                dkv_acc[pl.ds(ks, TQ), lo] += _dot(ds, qn[j], TN)
                dkv_acc[pl.ds(ks, TQ), hi] += _dot(p.astype(MXU_DTYPE), dob[j], TN)
                dkr_acc[pl.ds(ks, TQ), :] += _dot(ds, qr[j], TN)
                out.append((dqn + _dot(ds, kn), dqr + _dot(ds, krb)))
            return tuple(out)

        res = lax.fori_loop(0, i + 1, step,
                            tuple((jnp.zeros((TQ, HEAD_DIM), F32), jnp.zeros((TQ, 128), F32)) for _ in HEAD_COLS))
        for (lo, hi), (dqn, dqr) in zip(PAD_COLS, res):
            dq_ref[:, lo] = dqn.astype(dq_ref.dtype)
            dq_ref[:, hi] = _rope_t(dqr, *tabs_i).astype(dq_ref.dtype)

        @pl.when(i == nq - 1)
        def _():
            dkv_ref[...] = dkv_acc[...].astype(dkv_ref.dtype)

        @pl.when((h == HEADS // HEADS_PER_STEP - 1) & (i == nq - 1))
        def _():
            dkr_ref[...] = dkr_acc[...]

    tab = pl.BlockSpec((TQ, 128), lambda h, i: (i, 0))
    qblk = pl.BlockSpec((TQ, HEADS_PER_STEP * HEAD_PAD), lambda h, i: (i, h))
    oblk = pl.BlockSpec((TQ, HEADS_PER_STEP * HEAD_DIM), lambda h, i: (i, h))
    kvblk = pl.BlockSpec((s, HEADS_PER_STEP * HEAD_PAD), lambda h, i: (0, h))
    return _call(
        body, (q, kv, kr, *tabs, do, o, lse), grid=(HEADS // HEADS_PER_STEP, nq),
        in_specs=[qblk, kvblk, pl.BlockSpec((s, 128), lambda h, i: (0, 0)), tab, tab, tab, oblk, oblk,
                  pl.BlockSpec((HEADS_PER_STEP, TQ, 1), lambda h, i: (h, i, 0))],
        out_specs=[qblk, kvblk, pl.BlockSpec((s, 128), lambda h, i: (0, 0))],
        out_shape=[jax.ShapeDtypeStruct((s, HEADS * HEAD_PAD), MXU_DTYPE), jax.ShapeDtypeStruct((s, HEADS * HEAD_PAD), MXU_DTYPE),
                   jax.ShapeDtypeStruct((s, 128), F32)],
        scratch_shapes=[pltpu.VMEM((s, HEADS_PER_STEP * HEAD_PAD), F32), pltpu.VMEM((s, 128), F32)],
        name=name, semantics=("arbitrary", "arbitrary"), comm=comm)


HEADS_PER_STEP = 2
HEAD_COLS = [slice(j * HEAD_DIM, (j + 1) * HEAD_DIM) for j in range(HEADS_PER_STEP)]
PAD_COLS = [(slice(j * HEAD_PAD, j * HEAD_PAD + HEAD_DIM), slice(j * HEAD_PAD + HEAD_DIM, (j + 1) * HEAD_PAD))
            for j in range(HEADS_PER_STEP)]


def _qkv_specs(s):
    groups, width = HEADS // HEADS_PER_STEP, HEADS_PER_STEP * HEAD_DIM
    return [pl.BlockSpec((TQ, width), lambda h, i: (i, h)),
            pl.BlockSpec((s, width), lambda h, i: (0, groups + h)),
            pl.BlockSpec((s, width), lambda h, i: (0, 2 * groups + h))]


def _sb_terms(z):
    sp = jnp.log(1.0 + jnp.exp(-jnp.abs(z)))
    return jnp.minimum(z, 0.0) - sp, jnp.minimum(-z, 0.0) - sp


def _sb_attn_fwd(name, qkv, comm=None):
    s = qkv.shape[0]
    nq = s // TQ
    scale = HEAD_DIM ** -0.5

    def body(q_ref, k_ref, v_ref, o_ref):
        i = pl.program_id(1)
        row, col = _iota2()
        after = (row > col).astype(BF16)
        q = [q_ref[:, hd].astype(MXU_DTYPE) for hd in HEAD_COLS]
        qpos = i * TQ + row

        def step(n, carry):
            ks = pl.multiple_of((i - n) * TQ, TQ)
            strict = (ks + col) < qpos
            out = []
            for hd, qh, (tail, acc) in zip(HEAD_COLS, q, carry):
                z = _dot(qh, k_ref[pl.ds(ks, TQ), hd].astype(MXU_DTYPE), NT) * scale
                lb, l1 = _sb_terms(z)
                l1 = jnp.where(strict, l1, 0.0)
                a = jnp.where(strict, jnp.exp(lb + tail + _exact_dot(l1, after)), 0.0)
                acc = acc + _dot(a.astype(MXU_DTYPE), v_ref[pl.ds(ks, TQ), hd].astype(MXU_DTYPE))
                out.append((tail + jnp.sum(l1, -1, keepdims=True), acc))
            return tuple(out)

        res = lax.fori_loop(0, i + 1, step,
                            tuple((jnp.zeros((TQ, 1), F32), jnp.zeros((TQ, HEAD_DIM), F32)) for _ in HEAD_COLS))
        for hd, (_, acc) in zip(HEAD_COLS, res):
            o_ref[:, hd] = acc.astype(o_ref.dtype)

    return _call(
        body, (qkv, qkv, qkv), grid=(HEADS // HEADS_PER_STEP, nq), in_specs=_qkv_specs(s),
        out_specs=[pl.BlockSpec((TQ, HEADS_PER_STEP * HEAD_DIM), lambda h, i: (i, h))],
        out_shape=[jax.ShapeDtypeStruct((s, HEADS * HEAD_DIM), MXU_DTYPE)],
        name=name, semantics=("parallel", "parallel"), comm=comm)[0]


def _sb_attn_bwd(name, qkv, do, comm=None):
    s = qkv.shape[0]
    nq = s // TQ
    scale = HEAD_DIM ** -0.5

    def body(q_ref, k_ref, v_ref, do_ref, dq_ref, dk_ref, dv_ref, a_buf, dk_acc, dv_acc):
        i = pl.program_id(1)
        row, col = _iota2()
        after = (row > col).astype(BF16)
        before = (row < col).astype(BF16)

        @pl.when(i == 0)
        def _():
            dk_acc[...] = jnp.zeros_like(dk_acc)
            dv_acc[...] = jnp.zeros_like(dv_acc)

        q = [q_ref[:, hd].astype(MXU_DTYPE) for hd in HEAD_COLS]
        dob = [do_ref[:, hd].astype(MXU_DTYPE) for hd in HEAD_COLS]
        qpos = i * TQ + row

        def weights(n, tails):
            kb = i - n
            ks = pl.multiple_of(kb * TQ, TQ)
            strict = (ks + col) < qpos
            out = []
            for j, (hd, tail) in enumerate(zip(HEAD_COLS, tails)):
                z = _dot(q[j], k_ref[pl.ds(ks, TQ), hd].astype(MXU_DTYPE), NT) * scale
                lb, l1 = _sb_terms(z)
                l1 = jnp.where(strict, l1, 0.0)
                a = jnp.where(strict, jnp.exp(lb + tail + _exact_dot(l1, after)), 0.0)
                a_buf[j, kb] = a
                dv_acc[pl.ds(ks, TQ), hd] += _dot(a.astype(MXU_DTYPE), dob[j], TN)
                out.append(tail + jnp.sum(l1, -1, keepdims=True))
            return tuple(out)

        lax.fori_loop(0, i + 1, weights, tuple(jnp.zeros((TQ, 1), F32) for _ in HEAD_COLS))

        def grads(kb, carry):
            ks = pl.multiple_of(kb * TQ, TQ)
            strict = (ks + col) < qpos
            out = []
            for j, (hd, (head, dq)) in enumerate(zip(HEAD_COLS, carry)):
                k = k_ref[pl.ds(ks, TQ), hd].astype(MXU_DTYPE)
                z = _dot(q[j], k, NT) * scale
                e = jnp.exp(-jnp.abs(z))
                beta = jnp.where(z >= 0.0, 1.0, e) / (1.0 + e)
                w = _dot(dob[j], v_ref[pl.ds(ks, TQ), hd].astype(MXU_DTYPE), NT) * a_buf[j, kb]
                dz = jnp.where(strict, w * (1.0 - beta) - beta * (head + _exact_dot(w, before)), 0.0) * scale
                dzb = dz.astype(MXU_DTYPE)
                dk_acc[pl.ds(ks, TQ), hd] += _dot(dzb, q[j], TN)
                out.append((head + jnp.sum(w, -1, keepdims=True), dq + _dot(dzb, k)))
            return tuple(out)

        res = lax.fori_loop(0, i + 1, grads,
                            tuple((jnp.zeros((TQ, 1), F32), jnp.zeros((TQ, HEAD_DIM), F32)) for _ in HEAD_COLS))
        for hd, (_, dq) in zip(HEAD_COLS, res):
            dq_ref[:, hd] = dq.astype(dq_ref.dtype)

        @pl.when(i == nq - 1)
        def _():
            dk_ref[...] = dk_acc[...].astype(dk_ref.dtype)
            dv_ref[...] = dv_acc[...].astype(dv_ref.dtype)

    width = HEADS_PER_STEP * HEAD_DIM
    blk = pl.BlockSpec((TQ, width), lambda h, i: (i, h))
    col_h = pl.BlockSpec((s, width), lambda h, i: (0, h))
    shp = jax.ShapeDtypeStruct((s, HEADS * HEAD_DIM), MXU_DTYPE)
    return _call(
        body, (qkv, qkv, qkv, do), grid=(HEADS // HEADS_PER_STEP, nq), in_specs=_qkv_specs(s) + [blk],
        out_specs=[blk, col_h, col_h], out_shape=[shp, shp, shp],
        scratch_shapes=[pltpu.VMEM((HEADS_PER_STEP, nq, TQ, TQ), F32), pltpu.VMEM((s, width), F32),
                        pltpu.VMEM((s, width), F32)],
        name=name, semantics=("arbitrary", "arbitrary"), comm=comm)


LANES = 128
LAST_REL = (1 << CHUNK_SHIFT) - 1


def _ca_subtiles():
    nb = TQ // LANES
    return [(d, a, b, -d * TQ + (b - a) * LANES) for d in range(CA_TILES) for a in range(nb) for b in range(nb)]


def _ca_bias_tiles(rel_bias):
    table = jnp.pad(jnp.transpose(rel_bias), ((0, 0), (0, 2 * LANES - REL_TABLE)))[:, None]

    def unskew(x, row):
        for b in range(7):
            x = jnp.where((jnp.right_shift(row, b) & 1) == 1, pltpu.roll(x, 1 << b, 1), x)
        return x

    def body(t_ref, o_ref):
        row = lax.broadcasted_iota(jnp.int32, (LANES, LANES), 0)
        col = lax.broadcasted_iota(jnp.int32, (LANES, LANES), 1)
        lane = col[:1]
        lo, hi = t_ref[:, :LANES], t_ref[:, LANES:]
        first = jnp.sum(jnp.where(lane == 0, lo, 0.0), -1, keepdims=True)
        last = jnp.sum(jnp.where(lane == LAST_REL, hi, 0.0), -1, keepdims=True)
        hi = jnp.where(lane <= LAST_REL, hi, last)
        r_lo = unskew(jnp.broadcast_to(lo, (LANES, LANES)), row)
        r_hi = unskew(jnp.broadcast_to(hi, (LANES, LANES)), row)
        upper = col >= row
        for d, a, b, o in _ca_subtiles():
            if o >= LANES:
                piece = jnp.broadcast_to(last, (LANES, LANES))
            elif o == 0:
                piece = jnp.where(upper, r_hi, r_lo)
            elif o == -LANES:
                piece = jnp.where(upper, r_lo, first)
            else:
                piece = jnp.broadcast_to(first, (LANES, LANES))
            o_ref[d, a * LANES:(a + 1) * LANES, b * LANES:(b + 1) * LANES] = piece

    return pl.pallas_call(
        body, grid=(HEADS,), in_specs=[pl.BlockSpec((None, 1, 2 * LANES), lambda h: (h, 0, 0))],
        out_specs=pl.BlockSpec((None, CA_TILES, TQ, TQ), lambda h: (h, 0, 0, 0)),
        out_shape=jax.ShapeDtypeStruct((HEADS, CA_TILES, TQ, TQ), F32),
        name="ca_bias_tiles", compiler_params=_params("parallel"))(table)


def _ca_mask(i, ks, row, col):
    qc = jnp.right_shift(i * TQ + row, CHUNK_SHIFT)
    kc = jnp.right_shift(ks + col, CHUNK_SHIFT)
    return (kc <= qc) & (kc >= qc - CA_LEFT_CHUNKS)


def _ca_attn_fwd(name, qkv, tiles, comm=None):
    s = qkv.shape[0]
    nq = s // TQ
    scale = HEAD_DIM ** -0.5

    def body(q_ref, k_ref, v_ref, bt_ref, o_ref, omx_ref, lse_ref):
        i = pl.program_id(1)
        row, col = _iota2()
        q = [q_ref[:, hd].astype(MXU_DTYPE) for hd in HEAD_COLS]

        def step(kb, carry):
            ks = pl.multiple_of(kb * TQ, TQ)
            mask = _ca_mask(i, ks, row, col)
            t = jnp.minimum(i - kb, CA_TILES - 1)
            out = []
            for j, (hd, (m, l, acc)) in enumerate(zip(HEAD_COLS, carry)):
                sc = _dot(q[j], k_ref[pl.ds(ks, TQ), hd].astype(MXU_DTYPE), NT) * scale + bt_ref[j, t]
                sc = jnp.where(mask, sc, NEG)
                m_new = jnp.maximum(m, jnp.max(sc, -1, keepdims=True))
                p = jnp.exp(sc - m_new)
                corr = jnp.exp(m - m_new)
                l = corr * l + jnp.sum(p, -1, keepdims=True)
                acc = corr * acc + _dot(p.astype(MXU_DTYPE), v_ref[pl.ds(ks, TQ), hd].astype(MXU_DTYPE))
                out.append((m_new, l, acc))
            return tuple(out)

        res = lax.fori_loop(jnp.maximum(i - CA_LEFT_BLOCKS, 0), i + 1, step,
                            tuple((jnp.full((TQ, 1), NEG, F32), jnp.zeros((TQ, 1), F32),
                                   jnp.zeros((TQ, HEAD_DIM), F32)) for _ in HEAD_COLS))
        for j, (hd, (m, l, acc)) in enumerate(zip(HEAD_COLS, res)):
            o = acc / l
            o_ref[:, hd] = o
            omx_ref[:, hd] = o.astype(omx_ref.dtype)
            lse_ref[j] = m + jnp.log(l)

    oblk = pl.BlockSpec((TQ, HEADS_PER_STEP * HEAD_DIM), lambda h, i: (i, h))
    return _call(
        body, (qkv, qkv, qkv, tiles), grid=(HEADS // HEADS_PER_STEP, nq),
        in_specs=_qkv_specs(s) + [pl.BlockSpec((HEADS_PER_STEP, CA_TILES, TQ, TQ), lambda h, i: (h, 0, 0, 0))],
        out_specs=[oblk, oblk, pl.BlockSpec((HEADS_PER_STEP, TQ, 1), lambda h, i: (h, i, 0))],
        out_shape=[jax.ShapeDtypeStruct((s, HEADS * HEAD_DIM), F32),
                   jax.ShapeDtypeStruct((s, HEADS * HEAD_DIM), MXU_DTYPE), jax.ShapeDtypeStruct((HEADS, s, 1), F32)],
        name=name, semantics=("parallel", "parallel"), comm=comm)


def _ca_attn_bwd(name, qkv, tiles, do, o, lse, comm=None):
    s = qkv.shape[0]
    nq = s // TQ
    scale = HEAD_DIM ** -0.5

    def body(q_ref, k_ref, v_ref, bt_ref, do_ref, o_ref, lse_ref, dq_ref, dk_ref, dv_ref, dbt_ref, dk_acc, dv_acc):
        i = pl.program_id(1)
        row, col = _iota2()

        @pl.when(i == 0)
        def _():
            dk_acc[...] = jnp.zeros_like(dk_acc)
            dv_acc[...] = jnp.zeros_like(dv_acc)
            dbt_ref[...] = jnp.zeros_like(dbt_ref)

        q = [q_ref[:, hd].astype(MXU_DTYPE) for hd in HEAD_COLS]
        delta = [jnp.sum(do_ref[:, hd] * o_ref[:, hd], -1, keepdims=True) for hd in HEAD_COLS]
        lse_ = [lse_ref[j] for j in range(HEADS_PER_STEP)]
        dob = [do_ref[:, hd].astype(MXU_DTYPE) for hd in HEAD_COLS]

        def step(kb, dqs):
            ks = pl.multiple_of(kb * TQ, TQ)
            mask = _ca_mask(i, ks, row, col)
            t = jnp.minimum(i - kb, CA_TILES - 1)
            out = []
            for j, (hd, dq) in enumerate(zip(HEAD_COLS, dqs)):
                k = k_ref[pl.ds(ks, TQ), hd].astype(MXU_DTYPE)
                v = v_ref[pl.ds(ks, TQ), hd].astype(MXU_DTYPE)
                sc = _dot(q[j], k, NT) * scale + bt_ref[j, t]
                sc = jnp.where(mask, sc, NEG)
                p = jnp.exp(sc - lse_[j])
                dsc = p * (_dot(dob[j], v, NT) - delta[j])
                dbt_ref[j, t] += dsc
                ds = (dsc * scale).astype(MXU_DTYPE)
                dk_acc[pl.ds(ks, TQ), hd] += _dot(ds, q[j], TN)
                dv_acc[pl.ds(ks, TQ), hd] += _dot(p.astype(MXU_DTYPE), dob[j], TN)
                out.append(dq + _dot(ds, k))
            return tuple(out)

        dqs = lax.fori_loop(jnp.maximum(i - CA_LEFT_BLOCKS, 0), i + 1, step,
                            tuple(jnp.zeros((TQ, HEAD_DIM), F32) for _ in HEAD_COLS))
        for hd, dq in zip(HEAD_COLS, dqs):
            dq_ref[:, hd] = dq.astype(dq_ref.dtype)

        @pl.when(i == nq - 1)
        def _():
            dk_ref[...] = dk_acc[...].astype(dk_ref.dtype)
            dv_ref[...] = dv_acc[...].astype(dv_ref.dtype)

    width = HEADS_PER_STEP * HEAD_DIM
    blk = pl.BlockSpec((TQ, width), lambda h, i: (i, h))
    col_h = pl.BlockSpec((s, width), lambda h, i: (0, h))
    tile = pl.BlockSpec((HEADS_PER_STEP, CA_TILES, TQ, TQ), lambda h, i: (h, 0, 0, 0))
    shp = jax.ShapeDtypeStruct((s, HEADS * HEAD_DIM), MXU_DTYPE)
    return _call(
        body, (qkv, qkv, qkv, tiles, do, o, lse), grid=(HEADS // HEADS_PER_STEP, nq),
        in_specs=_qkv_specs(s) + [tile, blk, blk, pl.BlockSpec((HEADS_PER_STEP, TQ, 1), lambda h, i: (h, i, 0))],
        out_specs=[blk, col_h, col_h, tile],
        out_shape=[shp, shp, shp, jax.ShapeDtypeStruct((HEADS, CA_TILES, TQ, TQ), F32)],
        scratch_shapes=[pltpu.VMEM((s, width), F32), pltpu.VMEM((s, width), F32)],
        name=name, semantics=("arbitrary", "arbitrary"), comm=comm)


def _ca_table_grad(dtiles):
    def skew(x, row):
        for b in range(7):
            x = jnp.where((jnp.right_shift(row, b) & 1) == 1, pltpu.roll(x, LANES - (1 << b), 1), x)
        return x

    def body(t_ref, o_ref):
        row = lax.broadcasted_iota(jnp.int32, (LANES, LANES), 0)
        col = lax.broadcasted_iota(jnp.int32, (LANES, LANES), 1)
        wrapped = (row + col) >= LANES
        lane = col[:1]
        total = lambda x: jnp.sum(jnp.sum(x, 0, keepdims=True), -1, keepdims=True)
        lo = jnp.zeros((1, LANES), F32)
        hi = jnp.zeros((1, LANES), F32)
        for d, a, b, o in _ca_subtiles():
            x = t_ref[d, a * LANES:(a + 1) * LANES, b * LANES:(b + 1) * LANES]
            if o >= LANES:
                hi = hi + jnp.where(lane == LAST_REL, total(x), 0.0)
            elif o <= -2 * LANES:
                lo = lo + jnp.where(lane == 0, total(x), 0.0)
            else:
                y = skew(x, row)
                pos = jnp.sum(jnp.where(wrapped, 0.0, y), 0, keepdims=True)
                neg = jnp.sum(jnp.where(wrapped, y, 0.0), 0, keepdims=True)
                if o == 0:
                    clipped = jnp.sum(jnp.where(lane > LAST_REL, pos, 0.0), -1, keepdims=True)
                    hi = hi + jnp.where(lane <= LAST_REL, pos, 0.0) + jnp.where(lane == LAST_REL, clipped, 0.0)
                    lo = lo + neg
                else:
                    lo = lo + pos + jnp.where(lane == 0, jnp.sum(neg, -1, keepdims=True), 0.0)
        o_ref[:, :LANES] = lo
        o_ref[:, LANES:] = hi

    return pl.pallas_call(
        body, grid=(HEADS,), in_specs=[pl.BlockSpec((None, CA_TILES, TQ, TQ), lambda h: (h, 0, 0, 0))],
        out_specs=pl.BlockSpec((None, 1, 2 * LANES), lambda h: (h, 0, 0)),
        out_shape=jax.ShapeDtypeStruct((HEADS, 1, 2 * LANES), F32),
        name="ca_table_grad", compiler_params=_params("parallel"))(dtiles)


def _row_tile(rows, cols):
    if rows % 128:
        return rows
    tr = 128
    while rows % (2 * tr) == 0 and 2 * tr * cols * 4 <= (1 << 20):
        tr *= 2
    return tr


def _adamw(name, w, g, m, v):
    rows, cols = w.shape
    tr = _row_tile(rows, cols)
    blk = pl.BlockSpec((tr, cols), lambda i: (i, 0))

    def body(w_ref, g_ref, m_ref, v_ref, d_ref, m2_ref, v2_ref):
        g_ = g_ref[...]
        m2 = ADAM_B1 * m_ref[...] + (1.0 - ADAM_B1) * g_
        v2 = ADAM_B2 * v_ref[...] + (1.0 - ADAM_B2) * jnp.square(g_)
        m_hat = m2 / (1.0 - ADAM_B1 ** ADAM_STEP)
        v_hat = v2 / (1.0 - ADAM_B2 ** ADAM_STEP)
        d_ref[...] = -ADAM_LR * (m_hat / (jnp.sqrt(v_hat) + ADAM_EPS) + ADAM_WD * w_ref[...])
        m2_ref[...] = m2
        v2_ref[...] = v2

    shp = jax.ShapeDtypeStruct((rows, cols), F32)
    return pl.pallas_call(body, grid=(rows // tr,), in_specs=[blk] * 4, out_specs=[blk] * 3, out_shape=[shp] * 3,
                          name=name, compiler_params=_params("parallel"))(w, g, m, v)


def _pair_sum(name, g, r1, c, lo, n):
    _, rh, cols = r1.shape
    tr = _row_tile(n, cols)
    nb, off, half = n // tr, lo // tr, rh // tr

    def body(c_ref, g_ref, r_ref, o_ref):
        o_ref[...] = (g_ref[...].astype(F32) + r_ref[...].astype(F32)).astype(o_ref.dtype)

    return pl.pallas_call(
        body, grid_spec=pltpu.PrefetchScalarGridSpec(
            num_scalar_prefetch=1, grid=(N_CHIPS, nb),
            in_specs=[pl.BlockSpec((None, tr, cols), lambda k, i, c_ref: (k, c_ref[0] * half + off + i, 0)),
                      pl.BlockSpec((None, tr, cols), lambda k, i, c_ref: (k, off + i, 0))],
            out_specs=pl.BlockSpec((None, tr, cols), lambda k, i, c_ref: (k, i, 0))),
        out_shape=jax.ShapeDtypeStruct((N_CHIPS, n, cols), WIRE_DTYPE), name=name,
        compiler_params=_params("parallel", "parallel"))(c, g, r1)


def _chip_sum(name, a1, r2, place, total, start, rh, total_rows):
    _, n, cols = a1.shape
    tr = _row_tile(n, cols)
    nb, off, half = n // tr, start // tr, rh // tr

    def body(p_ref, a_ref, r_ref, *rest):
        rest[-1][...] = ((a_ref[...].astype(F32) + r_ref[0].astype(F32)) + r_ref[1].astype(F32)) + r_ref[2].astype(F32)

    return pl.pallas_call(
        body, grid_spec=pltpu.PrefetchScalarGridSpec(
            num_scalar_prefetch=1, grid=(nb,),
            in_specs=[pl.BlockSpec((None, tr, cols), lambda i, p_ref: (p_ref[0], i, 0)),
                      pl.BlockSpec((N_CHIPS - 1, tr, cols), lambda i, p_ref: (0, i, 0))] + ([] if total is None else [ANY]),
            out_specs=pl.BlockSpec((tr, cols), lambda i, p_ref: (off + p_ref[1] * half + i, 0))),
        out_shape=jax.ShapeDtypeStruct((total_rows, cols), F32), name=name,
        input_output_aliases={} if total is None else {3: 0},
        compiler_params=_params("parallel"))(place, a1, r2, *([] if total is None else [total]))


def _cast_place(name, w, chip, l, rows):
    cols = w.shape[1]
    tr = _row_tile(rows, cols)
    nb = rows // tr

    def body(c_ref, w_ref, o_ref):
        o_ref[...] = w_ref[...].astype(o_ref.dtype)

    return pl.pallas_call(
        body, grid_spec=pltpu.PrefetchScalarGridSpec(
            num_scalar_prefetch=1, grid=(nb,),
            in_specs=[pl.BlockSpec((tr, cols), lambda i, c_ref: (l * nb + i, 0))],
            out_specs=pl.BlockSpec((None, tr, cols), lambda i, c_ref: (c_ref[0], i, 0))),
        out_shape=jax.ShapeDtypeStruct((N_CHIPS, rows, cols), MXU_DTYPE), name=name,
        compiler_params=_params("parallel"))(chip, w)


def _place():
    x, y, c = lax.axis_index("x"), lax.axis_index("y"), lax.axis_index("c")
    chips = [(1 - x, y), (x, 1 - y), (1 - x, 1 - y)]
    return x, y, c, chips


def _remote(src, dst, send_sem, recv_sem, to):
    return pltpu.make_async_remote_copy(src_ref=src, dst_ref=dst, send_sem=send_sem, recv_sem=recv_sem,
                                        device_id=to, device_id_type=MESH)


def _in_place(bufs):
    return [jax.ShapeDtypeStruct(b.shape, b.dtype) for b in bufs], {t: t for t in range(len(bufs))}


def _gather(bufs, jobs):
    def copies(srcs, dsts, send, recv):
        x, y, c, chips = _place()
        out = []

        def push(piece, to):
            out.append(_remote(piece, piece, send.at[len(out)], recv.at[len(out)], to))

        for buf, todo in zip(dsts, jobs):
            rh = buf.shape[1] // 2
            for phase, lo, n in todo:
                if phase == 0:
                    for px, py in chips[:2]:
                        push(buf.at[2 * x + y, pl.ds(c * rh + lo, n)], (px, py, c))
                elif phase == 1:
                    push(buf.at[2 * x + (1 - y), pl.ds(c * rh + lo, n // 2)], (1 - x, y, c))
                    push(buf.at[2 * (1 - x) + y, pl.ds(c * rh + lo + n // 2, n // 2)], (x, 1 - y, c))
                else:
                    for px, py in chips:
                        push(buf.at[2 * px + py, pl.ds(c * rh + lo, n)], (x, y, 1 - c))
        return out

    return _Comm(bufs, *_in_place(bufs), copies, sum((2, 2, 3)[phase] for todo in jobs for phase, _, _ in todo))


def _pieces(rh, cols, itemsize):
    us = 3.0 * rh * cols * itemsize / ICI_BYTES_PER_US
    k = max(1, int(round(us / CHUNK_US)))
    while rh % (32 * k):
        k -= 1
        if k <= 1:
            k = 1
            break
    return [(j * (rh // k), rh // k) for j in range(k)], us / k


class _GatherStream:
    def __init__(self, weights, order):
        self.weights, self.order, self.state, self.us, self.credit = weights, order, {}, {}, 0.0
        for key in order:
            _, rows, cols = weights[key].shape
            cuts, self.us[key] = _pieces(rows // 2, cols, jnp.dtype(weights[key].dtype).itemsize)
            self.state[key] = [[lo, n, 0] for lo, n in cuts]

    def _comm(self, picks):
        if not picks:
            return None
        keys = list(dict.fromkeys(key for key, _ in picks))
        jobs = [[(self.state[k][j][2], *self.state[k][j][:2]) for k, j in picks if k == key] for key in keys]
        comm = _gather([self.weights[key] for key in keys], jobs)

        def then(out):
            self.weights.update(zip(keys, out))
            for k, j in picks:
                self.state[k][j][2] += 1

        comm.then = then
        return comm

    def carry(self, us):
        todo = [(key, j) for key in self.order for j, piece in enumerate(self.state[key]) if piece[2] < 3]
        picks = [p for p in todo if self.state[p[0]][p[1]][2] > 0]
        self.credit += LINK_SHARE * us - sum(self.us[key] / 4 for key, j in picks if self.state[key][j][2] == 1)
        for key, j in todo:
            if self.state[key][j][2] == 0:
                if self.credit < self.us[key] / 4:
                    break
                picks.append((key, j))
                self.credit -= self.us[key] / 2
        return self._comm(picks)

    def require(self, keys, tag):
        for step in range(3):
            picks = [(key, j) for key in keys for j, piece in enumerate(self.state[key]) if piece[2] < 3]
            if picks:
                _exchange(f"gather_{tag}_{step}", self._comm(picks))
                self.credit = 0.0


def _reduce_pair(grads):
    def copies(srcs, dsts, send, recv):
        x, y, c, _ = _place()
        out = []
        for t, g in enumerate(srcs):
            rh = g.shape[1] // 2
            out.append(_remote(g.at[:, pl.ds((1 - c) * rh, rh)], dsts[t], send.at[t], recv.at[t], (x, y, 1 - c)))
        return out

    return _Comm(grads, [jax.ShapeDtypeStruct((N_CHIPS, g.shape[1] // 2, g.shape[2]), g.dtype) for g in grads], {},
                 copies, len(grads))


def _reduce_chips(parts):
    def copies(srcs, dsts, send, recv):
        x, y, c, chips = _place()
        return [_remote(p.at[2 * px + py], dsts[t].at[j], send.at[3 * t + j], recv.at[3 * t + j], (px, py, c))
                for t, p in enumerate(srcs) for j, (px, py) in enumerate(chips)]

    return _Comm(parts, [jax.ShapeDtypeStruct((N_CHIPS - 1, *p.shape[1:]), p.dtype) for p in parts], {}, copies,
                 3 * len(parts))


def _reduce_share(totals, spans):
    def copies(srcs, dsts, send, recv):
        x, y, c, _ = _place()
        out = []
        for t, (buf, (start, rh)) in enumerate(zip(dsts, spans)):
            mine = buf.at[pl.ds(start + c * rh, rh)]
            out.append(_remote(mine, mine, send.at[t], recv.at[t], (x, y, 1 - c)))
        return out

    return _Comm(totals, *_in_place(totals), copies, len(totals))


def _exchange(name, comm):
    ns, nd = len(comm.srcs), len(comm.dsts)

    def body(*refs):
        send, recv = refs[ns + nd:]
        cps = comm.copies(refs[:ns], refs[ns:ns + nd], send, recv)
        for cp in cps:
            cp.start()
        for cp in cps:
            cp.wait()

    comm.deliver(pl.pallas_call(
        body, in_specs=[ANY] * ns, out_specs=[ANY] * nd, out_shape=comm.dsts,
        scratch_shapes=[pltpu.SemaphoreType.DMA((comm.n,)), pltpu.SemaphoreType.DMA((comm.n,))],
        input_output_aliases=comm.alias, name=name)(*comm.srcs))
    return comm.out


class _ReduceStream:
    def __init__(self, dims, place):
        self.dims, self.place = dims, place
        self.parts, self.pieces, self.left, self.ready, self.total = {}, [], {}, [], {}

    def add(self, key, part):
        self.parts[key] = part

    def _pair(self):
        keys = list(self.parts)
        if not keys:
            return None
        parts = [self.parts.pop(key) for key in keys]
        comm = _reduce_pair(parts)

        def then(out):
            for (name, l), g, r1 in zip(keys, parts, out):
                _, rows, cols = self.dims[name]
                cuts, us = _pieces(rows // 2, cols, jnp.dtype(WIRE_DTYPE).itemsize)
                self.left[name, l] = len(cuts)
                for lo, n in cuts:
                    piece = _pair_sum(f"pair_sum_{name}_{l}_{lo}", g, r1, self.place[1:], lo, n)
                    self.pieces.append(((name, l), lo, piece, us))

        comm.then = then
        return comm

    def _chips(self, us):
        picks, spent = [], 0.0
        while self.pieces and spent + 0.5 * self.pieces[0][3] <= us:
            picks.append(self.pieces.pop(0))
            spent += picks[-1][3]
        if not picks:
            return None
        comm = _reduce_chips([piece for _, _, piece, _ in picks])

        def then(out):
            for ((name, l), lo, piece, _), r2 in zip(picks, out):
                layers, rows, _ = self.dims[name]
                self.total[name] = _chip_sum(f"chip_sum_{name}_{l}_{lo}", piece, r2, self.place, self.total.get(name),
                                             l * rows + lo, rows // 2, layers * rows)
                self.left[name, l] -= 1
                if not self.left[name, l]:
                    self.ready.append((name, l))

        comm.then = then
        return comm

    def _share(self):
        keys = []
        for key in self.ready:
            if key[0] not in [name for name, _ in keys]:
                keys.append(key)
        if not keys:
            return None
        self.ready = [key for key in self.ready if key not in keys]
        spans = [(l * self.dims[name][1], self.dims[name][1] // 2) for name, l in keys]
        comm = _reduce_share([self.total[name] for name, _ in keys], spans)

        def then(out):
            self.total.update(zip([name for name, _ in keys], out))

        comm.then = then
        return comm

    def carry(self, us):
        return _merged(self._pair(), self._share(), self._chips(us))

    def finish(self):
        step = 0
        while self.parts or self.pieces or self.ready:
            _exchange(f"grad_reduce_tail_{step}", self.carry(float("inf")))
            step += 1
        return self.total


def _all_reduce_small(name, pack):
    rows, cols = pack.shape

    def body(p_ref, o_ref, slots, send, recv):
        x, y, c, _ = _place()
        me = 4 * x + 2 * y + c
        slots[me] = p_ref[...]
        cps = []
        for r in range(1, N_DEV):
            to = ((1 - x) if r & 4 else x, (1 - y) if r & 2 else y, (1 - c) if r & 1 else c)
            cp = _remote(p_ref, slots.at[me], send.at[r - 1], recv.at[r - 1], to)
            cp.start()
            cps.append(cp)
        for cp in cps:
            cp.wait()
        acc = slots[0]
        for d in range(1, N_DEV):
            acc = acc + slots[d]
        o_ref[...] = acc

    return pl.pallas_call(
        body, in_specs=[VMEM_SPEC], out_specs=VMEM_SPEC, out_shape=jax.ShapeDtypeStruct((rows, cols), F32),
        scratch_shapes=[pltpu.VMEM((N_DEV, rows, cols), F32), pltpu.SemaphoreType.DMA((N_DEV - 1,)),
                        pltpu.SemaphoreType.DMA((N_DEV - 1,))],
        name=name)(pack)


BIG = ("ffn_w_in", "ffn_w_out", "mla_w_down", "mla_w_uq", "mla_w_ukv", "mla_w_o", "sb_w_qkv", "sb_w_o", "ca_w_qkv",
       "ca_w_o")
COL_SHARDED = {"ffn_w_in": True, "ffn_w_out": False, "mla_w_down": False, "mla_w_uq": True, "mla_w_ukv": True,
               "mla_w_o": False, "sb_w_qkv": True, "sb_w_o": False, "ca_w_qkv": True, "ca_w_o": False}
WEIGHTS = ("ln_mix_g", "ln_mix_b", "ln_ffn_g", "ln_ffn_b", "ffn_w_in", "ffn_w_out", "mla_w_down", "mla_q_norm_g",
           "mla_w_uq", "mla_kv_norm_g", "mla_w_ukv", "mla_w_o", "sb_w_qkv", "sb_w_o", "ca_w_qkv", "ca_rel_bias",
           "ca_w_o")
HEADS_PER_CHIP = HEADS // N_CHIPS


def _mxu_shards(w):
    down = w["mla_w_down"]
    uq = w["mla_w_uq"]
    n, ql = uq.shape[:2]
    lane_pad = 128 - MLA_ROPE
    shaped = dict(w)
    shaped["mla_w_down"] = jnp.pad(down, ((0, 0), (0, 0), (0, lane_pad)))
    shaped["mla_w_uq"] = jnp.pad(uq.reshape(n, ql, HEADS_PER_CHIP, MLA_QK_DIM),
                                 ((0, 0), (0, 0), (0, 0), (0, HEAD_PAD - MLA_QK_DIM))).reshape(n, ql, -1)
    return {k: shaped[k].reshape(-1, shaped[k].shape[-1]) for k in BIG}


def _unpad_grad(name, g, like):
    if name == "mla_w_down":
        g = g[:, :like.shape[-1]]
    elif name == "mla_w_uq":
        g = g.reshape(g.shape[0], HEADS_PER_CHIP, HEAD_PAD)[:, :, :MLA_QK_DIM]
    return g.reshape(like.shape)


def _mixer_keys(i):
    kind, slot = i % 3, i // 3
    if kind == 0:
        return [(k, slot) for k in ("mla_w_down", "mla_w_uq", "mla_w_ukv", "mla_w_o")]
    pre = "sb" if kind == 1 else "ca"
    return [(f"{pre}_w_qkv", slot), (f"{pre}_w_o", slot)]


def _ffn_keys(i):
    return [("ffn_w_in", i), ("ffn_w_out", i)]


def _step(x, target, wl, dims, p, tabs, place):
    dist = place is not None
    wl = dict(wl)
    gather = _GatherStream(wl, [key for i in range(DEPTH) for key in _mixer_keys(i) + _ffn_keys(i)]) if dist else None
    fetch = gather.carry if dist else None

    def weight(key, user):
        if dist:
            gather.require([key], user)
        return wl[key]

    if dist:
        gather.require(_mixer_keys(0), "first")
    saved = []
    h = hm = x
    for i in range(DEPTH):
        kind, slot = i % 3, i // 3
        sv = {"h0m": hm}
        mix = _mixer_keys(i)
        if kind == 0:
            down = _fwd_row(f"mla_down_{i}", hm, weight(mix[0], f"mla_down_{i}"), F32, comm=fetch)
            gq, gkv = p["mla_q_norm_g"][slot][None], p["mla_kv_norm_g"][slot][None]
            cq, ckv, kr = _mla_mid_fwd(down, gq, gkv, tabs)
            q = _fwd_col(f"mla_uq_{i}", cq, weight(mix[1], f"mla_uq_{i}"), F32, comm=fetch)
            kv = _fwd_col(f"mla_ukv_{i}", ckv, weight(mix[2], f"mla_ukv_{i}"), MXU_DTYPE, comm=fetch)
            o, om, lse = _mla_attn_fwd(f"mla_attn_fwd_{i}", q, kv, kr, tabs, comm=fetch)
            sv.update(down=down, gq=gq, gkv=gkv, cq=cq, ckv=ckv, kr=kr, q=q, kv=kv, o=o, lse=lse)
        else:
            pre = "sb" if kind == 1 else "ca"
            qkv = _fwd_col(f"{pre}_qkv_{i}", hm, weight(mix[0], f"{pre}_qkv_{i}"), MXU_DTYPE, comm=fetch)
            if kind == 1:
                om = _sb_attn_fwd(f"sb_attn_fwd_{i}", qkv, comm=fetch)
            else:
                tiles = _ca_bias_tiles(p["ca_rel_bias"][slot])
                o, om, lse = _ca_attn_fwd(f"ca_attn_fwd_{i}", qkv, tiles, comm=fetch)
                sv.update(tiles=tiles, lse=lse, o=o)
            sv.update(qkv=qkv)
        m = _fwd_row(f"mixer_o_{i}", om, weight(mix[-1], f"mixer_o_{i}"), F32, comm=fetch)
        h1, h1m, sv["xh1"], sv["r1"] = _ln_fwd(f"ln_mix_{i}", h, m, p["ln_mix_g"][i][None], p["ln_mix_b"][i][None])
        u, a = _fwd_col(f"ffn_in_{i}", h1m, weight(("ffn_w_in", i), f"ffn_in_{i}"), None, epilogue=_epi_relu2,
                        dtypes=(F32, MXU_DTYPE), comm=fetch)
        y = _fwd_row(f"ffn_out_{i}", a, weight(("ffn_w_out", i), f"ffn_out_{i}"), F32, comm=fetch)
        h, hm, sv["xh2"], sv["r2"] = _ln_fwd(f"ln_ffn_{i}", h1, y, p["ln_ffn_g"][i][None], p["ln_ffn_b"][i][None])
        sv.update(om=om, h1m=h1m, u=u, a=a)
        saved.append(sv)

    reduction = _ReduceStream(dims, place) if dist else None
    carry = reduction.carry if dist else None
    part = {}

    def dw(key, xin, dyin):
        _, rows, cols = dims[key[0]]
        g = _dw(f"dw_{key[0]}_{key[1]}", xin, dyin, rows, cols, COL_SHARDED[key[0]], comm=carry)
        if dist:
            reduction.add(key, g)
        else:
            part[key] = g

    loss, dy = _loss_head(h, target)
    small = {k: [None] * DEPTH for k in ("ln_mix_g", "ln_mix_b", "ln_ffn_g", "ln_ffn_b")}
    n_mla = p["mla_q_norm_g"].shape[0]
    small["mla_q_norm_g"], small["mla_kv_norm_g"] = [None] * n_mla, [None] * n_mla
    for i in reversed(range(DEPTH)):
        kind, slot = i % 3, i // 3
        sv = saved[i]
        mix = _mixer_keys(i)
        dz, dzm, small["ln_ffn_g"][i], small["ln_ffn_b"][i] = _ln_bwd(f"ln_ffn_bwd_{i}", dy, sv["xh2"], sv["r2"],
                                                                      p["ln_ffn_g"][i][None])
        dw(("ffn_w_out", i), sv["a"], dzm)
        du = _dx_row(f"ffn_du_{i}", dzm, wl["ffn_w_out", i], MXU_DTYPE, epilogue=_epi_drelu2, extra=sv["u"], comm=carry)
        dw(("ffn_w_in", i), sv["h1m"], du)
        dy = _dx_col(f"ffn_dh_{i}", du, wl["ffn_w_in", i], F32, epilogue=_epi_residual, extra=dz, comm=carry)
        dz, dzm, small["ln_mix_g"][i], small["ln_mix_b"][i] = _ln_bwd(f"ln_mix_bwd_{i}", dy, sv["xh1"], sv["r1"],
                                                                      p["ln_mix_g"][i][None])
        dw(mix[-1], sv["om"], dzm)
        do = _dx_row(f"mixer_do_{i}", dzm, wl[mix[-1]], F32, comm=carry)
        last = carry if i else None
        if kind == 0:
            dq, dkv, dkr = _mla_attn_bwd(f"mla_attn_bwd_{i}", sv["q"], sv["kv"], sv["kr"], tabs, do, sv["o"],
                                         sv["lse"], comm=carry)
            dw(("mla_w_uq", slot), sv["cq"], dq)
            dcq = _dx_col(f"mla_dcq_{i}", dq, wl["mla_w_uq", slot], F32, comm=carry)
            dw(("mla_w_ukv", slot), sv["ckv"], dkv)
            dckv = _dx_col(f"mla_dckv_{i}", dkv, wl["mla_w_ukv", slot], F32, comm=carry)
            ddown, small["mla_q_norm_g"][slot], small["mla_kv_norm_g"][slot] = _mla_mid_bwd(
                sv["down"], dcq, dckv, dkr, sv["gq"], sv["gkv"], tabs)
            dw(("mla_w_down", slot), sv["h0m"], ddown)
            dy = _dx_row(f"mla_dh_{i}", ddown, wl["mla_w_down", slot], F32, epilogue=_epi_residual, extra=dz,
                         comm=last)
        else:
            pre = "sb" if kind == 1 else "ca"
            if kind == 1:
                dq, dk, dv = _sb_attn_bwd(f"sb_attn_bwd_{i}", sv["qkv"], do, comm=carry)
            else:
                dq, dk, dv, dtiles = _ca_attn_bwd(f"ca_attn_bwd_{i}", sv["qkv"], sv["tiles"], do, sv["o"], sv["lse"],
                                                  comm=carry)
                small["ca_rel_bias"] = [jnp.transpose(_ca_table_grad(dtiles)[:, 0, :REL_TABLE])]
            dqkv = jnp.concatenate([dq, dk, dv], 1)
            dw((f"{pre}_w_qkv", slot), sv["h0m"], dqkv)
            dy = _dx_col(f"{pre}_dh_{i}", dqkv, wl[f"{pre}_w_qkv", slot], F32, epilogue=_epi_residual, extra=dz,
                         comm=last)
    small = {k: jnp.stack([g.reshape(g.shape[-2:]) if k == "ca_rel_bias" else g[0] for g in v]) for k, v in small.items()}
    return loss, dy, (reduction.finish() if dist else part), small


SMALL = ("ln_mix_g", "ln_mix_b", "ln_ffn_g", "ln_ffn_b", "mla_q_norm_g", "mla_kv_norm_g", "ca_rel_bias")


def _pack_small(parts, width):
    flat = jnp.concatenate([parts[k].reshape(-1) for k in SMALL])
    rows = -(-flat.shape[0] // width)
    rows += -rows % 8
    return jnp.pad(flat, (0, rows * width - flat.shape[0])).reshape(rows, width)


def _unpack_small(pack, like):
    flat, out, at = pack.reshape(-1), {}, 0
    for k in SMALL:
        n = int(np.prod(like[k].shape))
        out[k] = flat[at:at + n].reshape(like[k].shape)
        at += n
    return out


def kernel(x, ln_mix_g, ln_mix_b, ln_ffn_g, ln_ffn_b, ffn_w_in, ffn_w_out, mla_w_down, mla_q_norm_g, mla_w_uq, mla_kv_norm_g, mla_w_ukv, mla_w_o, sb_w_qkv, sb_w_o, ca_w_qkv, ca_rel_bias, ca_w_o, loss_target, m_ln_mix_g, m_ln_mix_b, m_ln_ffn_g, m_ln_ffn_b, m_ffn_w_in, m_ffn_w_out, m_mla_w_down, m_mla_q_norm_g, m_mla_w_uq, m_mla_kv_norm_g, m_mla_w_ukv, m_mla_w_o, m_sb_w_qkv, m_sb_w_o, m_ca_w_qkv, m_ca_rel_bias, m_ca_w_o, v_ln_mix_g, v_ln_mix_b, v_ln_ffn_g, v_ln_ffn_b, v_ffn_w_in, v_ffn_w_out, v_mla_w_down, v_mla_q_norm_g, v_mla_w_uq, v_mla_kv_norm_g, v_mla_w_ukv, v_mla_w_o, v_sb_w_qkv, v_sb_w_o, v_ca_w_qkv, v_ca_rel_bias, v_ca_w_o):
    w = dict(zip(WEIGHTS, (ln_mix_g, ln_mix_b, ln_ffn_g, ln_ffn_b, ffn_w_in, ffn_w_out, mla_w_down, mla_q_norm_g,
                           mla_w_uq, mla_kv_norm_g, mla_w_ukv, mla_w_o, sb_w_qkv, sb_w_o, ca_w_qkv, ca_rel_bias,
                           ca_w_o)))
    mom1 = dict(zip(WEIGHTS, (m_ln_mix_g, m_ln_mix_b, m_ln_ffn_g, m_ln_ffn_b, m_ffn_w_in, m_ffn_w_out, m_mla_w_down,
                              m_mla_q_norm_g, m_mla_w_uq, m_mla_kv_norm_g, m_mla_w_ukv, m_mla_w_o, m_sb_w_qkv,
                              m_sb_w_o, m_ca_w_qkv, m_ca_rel_bias, m_ca_w_o)))
    mom2 = dict(zip(WEIGHTS, (v_ln_mix_g, v_ln_mix_b, v_ln_ffn_g, v_ln_ffn_b, v_ffn_w_in, v_ffn_w_out, v_mla_w_down,
                              v_mla_q_norm_g, v_mla_w_uq, v_mla_kv_norm_g, v_mla_w_ukv, v_mla_w_o, v_sb_w_qkv,
                              v_sb_w_o, v_ca_w_qkv, v_ca_rel_bias, v_ca_w_o)))
    xi, yi, ci = lax.axis_index("x"), lax.axis_index("y"), lax.axis_index("c")
    chip = 2 * xi + yi
    d_model = x.shape[-1]

    shards = _mxu_shards(w)
    layers = {k: w[k].shape[0] for k in BIG}
    dims = {k: (layers[k], shards[k].shape[0] // layers[k], shards[k].shape[1]) for k in BIG}
    chip1 = jnp.reshape(chip, (1,)).astype(jnp.int32)
    wl = {(k, l): _cast_place(f"cast_{k}_{l}", shards[k], chip1, l, dims[k][1]) for k in BIG for l in range(layers[k])}
    gains = jnp.stack([w["mla_q_norm_g"], w["mla_kv_norm_g"]])
    gains = jnp.where(ci == 0, gains, 0.0)
    placed = lax.dynamic_update_slice_in_dim(jnp.zeros((*gains.shape[:2], N_CHIPS, gains.shape[2]), F32),
                                             gains[:, :, None], chip, 2)
    full_gains = _all_reduce_small("norm_gain_gather", placed.reshape(2 * gains.shape[1], -1))
    full_gains = full_gains.reshape(2, gains.shape[1], -1)
    p = {"ln_mix_g": ln_mix_g, "ln_mix_b": ln_mix_b, "ln_ffn_g": ln_ffn_g, "ln_ffn_b": ln_ffn_b,
         "mla_q_norm_g": full_gains[0], "mla_kv_norm_g": full_gains[1], "ca_rel_bias": ca_rel_bias}

    place = jnp.stack([chip, ci]).astype(jnp.int32)
    loss, grad_x, total, small = _step(x[0], loss_target[0], wl, dims, p, _rope_tables(x.shape[1]), place)
    loss = lax.psum(loss[0, 0], ("x", "y", "c"))

    small = _unpack_small(_all_reduce_small("small_grad_all_reduce", _pack_small(small, d_model)), small)
    grad = {k: small[k] for k in ("ln_mix_g", "ln_mix_b", "ln_ffn_g", "ln_ffn_b", "ca_rel_bias")}
    for k in ("mla_q_norm_g", "mla_kv_norm_g"):
        g = small[k].reshape(small[k].shape[0], N_CHIPS, -1)
        grad[k] = lax.dynamic_index_in_dim(g, chip, 1, keepdims=False)

    for k in BIG:
        grad[k] = _unpad_grad(k, total[k], w[k])

    delta, new_m, new_v = {}, {}, {}
    for k in WEIGHTS:
        flat = lambda a: a.reshape(-1, a.shape[-1])
        dl, m2, v2 = _adamw(f"adamw_{k}", flat(w[k]), flat(grad[k]), flat(mom1[k]), flat(mom2[k]))
        delta[k], new_m[k], new_v[k] = dl.reshape(w[k].shape), m2.reshape(w[k].shape), v2.reshape(w[k].shape)
    return (loss, grad_x[None], *[grad[k] for k in WEIGHTS], *[delta[k] for k in WEIGHTS],
            *[new_m[k] for k in WEIGHTS], *[new_v[k] for k in WEIGHTS])
```

```python
import functools

import numpy as np
import jax
import jax.numpy as jnp
from jax import lax
from jax.experimental import pallas as pl
from jax.experimental.pallas import tpu as pltpu

F32, BF16 = jnp.float32, jnp.bfloat16
MXU_DTYPE = BF16
WIRE_DTYPE = BF16

DEPTH = 4
HEADS = 16
HEAD_DIM = 128
CHUNK_SHIFT = 6
TQ = 512
MLA_ROPE = 64
MLA_QK_DIM = 192
HEAD_PAD = 256
CA_LEFT_CHUNKS = 8
CA_LEFT_BLOCKS = (CA_LEFT_CHUNKS << CHUNK_SHIFT) // TQ
CA_TILES = min(CA_LEFT_BLOCKS, 2) + 1
REL_CLIP_LEFT = 128
REL_TABLE = 192
ROPE_THETA = 10000.0
LN_EPS = 1e-5
RMS_EPS = 1e-6
ALPHA = (2.0 * DEPTH) ** 0.25
NEG = -1e30
ADAM_LR, ADAM_B1, ADAM_B2, ADAM_EPS, ADAM_WD, ADAM_STEP = 0.001, 0.9, 0.999, 1e-08, 0.01, 10
N_CHIPS = 4
N_DEV = 8
VMEM_LIMIT = 48 << 20
MXU_FLOPS_PER_US = 0.9e9
ICI_BYTES_PER_US = 69.9e3
CHUNK_US = 45.0
LINK_SHARE = 1.2
ATTN_US = {"mla_attn_fwd": 155.0, "mla_attn_bwd": 295.0, "sb_attn_fwd": 300.0, "sb_attn_bwd": 610.0,
           "ca_attn_fwd": 100.0, "ca_attn_bwd": 175.0}
MESH = pl.DeviceIdType.MESH
ANY = pl.BlockSpec(memory_space=pl.ANY)
VMEM_SPEC = pl.BlockSpec(memory_space=pltpu.VMEM)

NN = (((1,), (0,)), ((), ()))
NT = (((1,), (1,)), ((), ()))
TN = (((0,), (0,)), ((), ()))


def _dot(a, b, dims=NN):
    return lax.dot_general(a, b, dims, preferred_element_type=F32)


def _exact_dot(x, u):
    hi = x.astype(BF16)
    r1 = x - hi.astype(F32)
    mid = r1.astype(BF16)
    lo = (r1 - mid.astype(F32)).astype(BF16)
    return _dot(hi, u) + _dot(mid, u) + _dot(lo, u)


def _params(*sem):
    return pltpu.CompilerParams(dimension_semantics=sem, vmem_limit_bytes=VMEM_LIMIT)


TILE_K = 2048


def _tile(n, pref):
    for t in (2048, 1536, 1152, 1024, 768, 512, 384, 256, 128):
        if t <= pref and n % t == 0:
            return t
    return n


class _Comm:
    def __init__(self, srcs, dsts, alias, copies, n):
        self.srcs, self.dsts, self.alias, self.copies, self.n = list(srcs), list(dsts), dict(alias), copies, n
        self.out, self.then = None, None

    def deliver(self, out):
        self.out = list(out)
        if self.then is not None:
            self.then(self.out)


class _SemView:
    def __init__(self, sems, base):
        self.sems, self.base, self.at = sems, base, self

    def __getitem__(self, i):
        return self.sems.at[self.base + i]


class _Merged(_Comm):
    def __init__(self, parts):
        srcs, dsts, alias, n, self.spans = [], [], {}, 0, []
        for c in parts:
            self.spans.append((c, len(srcs), len(dsts), n))
            alias.update({len(srcs) + s: len(dsts) + d for s, d in c.alias.items()})
            srcs, dsts, n = srcs + c.srcs, dsts + c.dsts, n + c.n

        def copies(src_refs, dst_refs, send, recv):
            out = []
            for c, s0, d0, n0 in self.spans:
                out += c.copies(src_refs[s0:s0 + len(c.srcs)], dst_refs[d0:d0 + len(c.dsts)], _SemView(send, n0),
                                _SemView(recv, n0))
            return out

        super().__init__(srcs, dsts, alias, copies, n)

    def deliver(self, out):
        self.out = list(out)
        for c, _, d0, _ in self.spans:
            c.deliver(self.out[d0:d0 + len(c.dsts)])


def _merged(*comms):
    comms = [c for c in comms if c is not None]
    return None if not comms else comms[0] if len(comms) == 1 else _Merged(comms)


def _call(body, args, *, name, grid, in_specs, out_specs, out_shape, scratch_shapes=(), semantics, comm=None, us=0.0):
    in_specs, out_specs, out_shape = list(in_specs), list(out_specs), list(out_shape)
    scratch_shapes = list(scratch_shapes)
    if callable(comm):
        comm = comm(us or ATTN_US.get(name.rsplit("_", 1)[0], 0.0))
    if comm is None:
        return pl.pallas_call(body, grid=grid, in_specs=in_specs, out_specs=out_specs, out_shape=out_shape,
                              scratch_shapes=scratch_shapes, name=name, compiler_params=_params(*semantics))(*args)
    n_in, n_out, n_scr, ns, nd = len(in_specs), len(out_specs), len(scratch_shapes), len(comm.srcs), len(comm.dsts)

    def carrier(*refs):
        ins, refs = refs[:n_in], refs[n_in:]
        srcs, refs = refs[:ns], refs[ns:]
        outs, refs = refs[:n_out], refs[n_out:]
        dsts, refs = refs[:nd], refs[nd:]
        scratch, (send, recv) = refs[:n_scr], refs[n_scr:]
        ids = [pl.program_id(a) for a in range(len(grid))]
        first = functools.reduce(jnp.logical_and, [i == 0 for i in ids])
        last = functools.reduce(jnp.logical_and, [i == g - 1 for i, g in zip(ids, grid)])

        @pl.when(first)
        def _():
            for cp in comm.copies(srcs, dsts, send, recv):
                cp.start()

        body(*ins, *outs, *scratch)

        @pl.when(last)
        def _():
            for cp in comm.copies(srcs, dsts, send, recv):
                cp.wait()

    res = pl.pallas_call(
        carrier, grid=grid, in_specs=in_specs + [ANY] * ns, out_specs=out_specs + [ANY] * nd,
        out_shape=out_shape + comm.dsts,
        scratch_shapes=scratch_shapes + [pltpu.SemaphoreType.DMA((comm.n,)), pltpu.SemaphoreType.DMA((comm.n,))],
        input_output_aliases={n_in + s: n_out + d for s, d in comm.alias.items()}, name=name,
        compiler_params=_params(*["arbitrary"] * len(grid)))(*args, *comm.srcs)
    comm.deliver(res[n_out:])
    return res[:n_out]


def _mm(name, a, b, extras, *, grid, a_spec, b_spec, extra_specs, out_specs, out_shape, dims,
        epilogue, acc_shape, comm=None):
    nk = grid[2]
    n_ex = len(extras)

    def product(a_ref, b_ref):
        return lax.dot_general(a_ref[...].astype(MXU_DTYPE), b_ref[...].astype(MXU_DTYPE), dims,
                               preferred_element_type=F32)

    def whole(*refs):
        epilogue(product(*refs[:2]), refs[2:2 + n_ex], refs[2 + n_ex:])

    def stepped(*refs):
        a_ref, b_ref = refs[:2]
        ex = refs[2:2 + n_ex]
        outs = refs[2 + n_ex:-1]
        acc = refs[-1]
        k = pl.program_id(2)

        @pl.when(k == 0)
        def _():
            acc[...] = product(a_ref, b_ref)

        @pl.when(k > 0)
        def _():
            acc[...] += product(a_ref, b_ref)

        @pl.when(k == nk - 1)
        def _():
            epilogue(acc[...], ex, outs)

    flops = 2.0 * grid[0] * grid[1] * grid[2] * acc_shape[1] * a_spec.block_shape[-1] * a_spec.block_shape[-2]
    return _call(whole if nk == 1 else stepped, (a, b, *extras), name=name, grid=grid,
                 in_specs=[a_spec, b_spec, *extra_specs], out_specs=out_specs, out_shape=out_shape,
                 scratch_shapes=[] if nk == 1 else [pltpu.VMEM(acc_shape, F32)],
                 semantics=("parallel", "parallel", "arbitrary"), comm=comm, us=flops / MXU_FLOPS_PER_US)


def _epi_store(acc, ex, outs):
    outs[0][...] = acc.astype(outs[0].dtype)


def _epi_relu2(acc, ex, outs):
    outs[0][...] = acc
    r = jnp.maximum(acc, 0.0)
    outs[1][...] = (r * r).astype(outs[1].dtype)


def _epi_drelu2(acc, ex, outs):
    outs[0][...] = (acc * (2.0 * jnp.maximum(ex[0][...], 0.0))).astype(outs[0].dtype)


def _epi_residual(acc, ex, outs):
    outs[0][...] = acc + ALPHA * ex[0][...]


def _fwd_col(name, x, wg, dtype, epilogue=_epi_store, dtypes=None, comm=None):
    m, (_, rows, cols) = x.shape[0], wg.shape
    tm, tn, tk = _tile(m, 1024), _tile(cols, 1024), _tile(rows, TILE_K)
    nps = cols // tn
    dtypes = dtypes or (dtype,)
    out = pl.BlockSpec((tm, tn), lambda i, j, k: (i, j))
    res = _mm(name, x, wg, (), grid=(m // tm, N_CHIPS * nps, rows // tk),
              a_spec=pl.BlockSpec((tm, tk), lambda i, j, k: (i, k)),
              b_spec=pl.BlockSpec((None, tk, tn), lambda i, j, k: (j // nps, k, j % nps)),
              extra_specs=(), out_specs=[out] * len(dtypes),
              out_shape=[jax.ShapeDtypeStruct((m, N_CHIPS * cols), d) for d in dtypes],
              dims=NN, epilogue=epilogue, acc_shape=(tm, tn), comm=comm)
    return res if len(dtypes) > 1 else res[0]


def _fwd_row(name, x, wg, dtype, comm=None):
    m, (_, rows, cols) = x.shape[0], wg.shape
    tm, tn, tk = _tile(m, 1024), _tile(cols, 1024), _tile(rows, TILE_K)
    kps = rows // tk
    return _mm(name, x, wg, (), grid=(m // tm, cols // tn, N_CHIPS * kps),
               a_spec=pl.BlockSpec((tm, tk), lambda i, j, k: (i, k)),
               b_spec=pl.BlockSpec((None, tk, tn), lambda i, j, k: (k // kps, k % kps, j)),
               extra_specs=(), out_specs=[pl.BlockSpec((tm, tn), lambda i, j, k: (i, j))],
               out_shape=[jax.ShapeDtypeStruct((m, cols), dtype)],
               dims=NN, epilogue=_epi_store, acc_shape=(tm, tn), comm=comm)[0]


def _dx_col(name, dy, wg, dtype, epilogue=_epi_store, extra=None, comm=None):
    m, (_, rows, cols) = dy.shape[0], wg.shape
    tm, tn, tk = _tile(m, 1024), _tile(rows, 1024), _tile(cols, TILE_K)
    kps = cols // tk
    tile = pl.BlockSpec((tm, tn), lambda i, j, k: (i, j))
    return _mm(name, dy, wg, () if extra is None else (extra,), grid=(m // tm, rows // tn, N_CHIPS * kps),
               a_spec=pl.BlockSpec((tm, tk), lambda i, j, k: (i, k)),
               b_spec=pl.BlockSpec((None, tn, tk), lambda i, j, k: (k // kps, j, k % kps)),
               extra_specs=() if extra is None else (tile,), out_specs=[tile],
               out_shape=[jax.ShapeDtypeStruct((m, rows), dtype)],
               dims=NT, epilogue=epilogue, acc_shape=(tm, tn), comm=comm)[0]


def _dx_row(name, dy, wg, dtype, epilogue=_epi_store, extra=None, comm=None):
    m, (_, rows, cols) = dy.shape[0], wg.shape
    tm, tn, tk = _tile(m, 1024), _tile(rows, 1024), _tile(cols, TILE_K)
    nps = rows // tn
    tile = pl.BlockSpec((tm, tn), lambda i, j, k: (i, j))
    return _mm(name, dy, wg, () if extra is None else (extra,), grid=(m // tm, N_CHIPS * nps, cols // tk),
               a_spec=pl.BlockSpec((tm, tk), lambda i, j, k: (i, k)),
               b_spec=pl.BlockSpec((None, tn, tk), lambda i, j, k: (j // nps, j % nps, k)),
               extra_specs=() if extra is None else (tile,), out_specs=[tile],
               out_shape=[jax.ShapeDtypeStruct((m, N_CHIPS * rows), dtype)],
               dims=NT, epilogue=epilogue, acc_shape=(tm, tn), comm=comm)[0]


def _dw(name, x, dy, rows, cols, col_sharded, comm=None):
    s_tok = x.shape[0]
    tm, tn, tk = _tile(rows, 1024), _tile(cols, 1024), _tile(s_tok, TILE_K)
    mt, nps = rows // tm, cols // tn
    if col_sharded:
        grid = (mt, N_CHIPS * nps, s_tok // tk)
        out = pl.BlockSpec((None, tm, tn), lambda i, j, k: (j // nps, i, j % nps))
    else:
        grid = (N_CHIPS * mt, nps, s_tok // tk)
        out = pl.BlockSpec((None, tm, tn), lambda i, j, k: (i // mt, i % mt, j))
    return _mm(name, x, dy, (), grid=grid,
               a_spec=pl.BlockSpec((tk, tm), lambda i, j, k: (k, i)),
               b_spec=pl.BlockSpec((tk, tn), lambda i, j, k: (k, j)),
               extra_specs=(), out_specs=[out],
               out_shape=[jax.ShapeDtypeStruct((N_CHIPS, rows, cols), WIRE_DTYPE)],
               dims=TN, epilogue=_epi_store, acc_shape=(tm, tn), comm=comm)[0]


def _ln_fwd(name, h, m, g, b):
    s, d = h.shape
    tm = _tile(s, 256)
    row = pl.BlockSpec((tm, d), lambda i: (i, 0))
    vec = pl.BlockSpec((1, d), lambda i: (0, 0))

    def body(h_ref, m_ref, g_ref, b_ref, y_ref, ymx_ref, xh_ref, r_ref):
        z = ALPHA * h_ref[...] + m_ref[...]
        mu = jnp.mean(z, -1, keepdims=True)
        zc = z - mu
        r = lax.rsqrt(jnp.mean(zc * zc, -1, keepdims=True) + LN_EPS)
        xh = zc * r
        xh_ref[...] = xh
        r_ref[...] = r
        y = xh * g_ref[...] + b_ref[...]
        y_ref[...] = y
        ymx_ref[...] = y.astype(ymx_ref.dtype)

    return pl.pallas_call(
        body, grid=(s // tm,), in_specs=[row, row, vec, vec],
        out_specs=[row, row, row, pl.BlockSpec((tm, 1), lambda i: (i, 0))],
        out_shape=[jax.ShapeDtypeStruct((s, d), F32), jax.ShapeDtypeStruct((s, d), MXU_DTYPE),
                   jax.ShapeDtypeStruct((s, d), F32), jax.ShapeDtypeStruct((s, 1), F32)],
        name=name, compiler_params=_params("parallel"))(h, m, g, b)


def _ln_bwd(name, dy, xh, r, g):
    s, d = dy.shape
    tm = _tile(s, 256)
    row = pl.BlockSpec((tm, d), lambda i: (i, 0))
    vec = pl.BlockSpec((1, d), lambda i: (0, 0))

    def body(dy_ref, xh_ref, r_ref, g_ref, dz_ref, dzmx_ref, dg_ref, db_ref):
        i = pl.program_id(0)
        dy_, xh_ = dy_ref[...], xh_ref[...]
        dyg = dy_ * g_ref[...]
        m1 = jnp.mean(dyg, -1, keepdims=True)
        m2 = jnp.mean(dyg * xh_, -1, keepdims=True)
        dz = r_ref[...] * (dyg - m1 - xh_ * m2)
        dz_ref[...] = dz
        dzmx_ref[...] = dz.astype(dzmx_ref.dtype)
        pg = jnp.sum(dy_ * xh_, 0, keepdims=True)
        pb = jnp.sum(dy_, 0, keepdims=True)

        @pl.when(i == 0)
        def _():
            dg_ref[...] = pg
            db_ref[...] = pb

        @pl.when(i > 0)
        def _():
            dg_ref[...] += pg
            db_ref[...] += pb

    return pl.pallas_call(
        body, grid=(s // tm,), in_specs=[row, row, pl.BlockSpec((tm, 1), lambda i: (i, 0)), vec],
        out_specs=[row, row, vec, vec],
        out_shape=[jax.ShapeDtypeStruct((s, d), F32), jax.ShapeDtypeStruct((s, d), MXU_DTYPE),
                   jax.ShapeDtypeStruct((1, d), F32), jax.ShapeDtypeStruct((1, d), F32)],
        name=name, compiler_params=_params("arbitrary"))(dy, xh, r, g)


def _loss_head(y, t):
    s, d = y.shape
    tm = _tile(s, 256)
    row = pl.BlockSpec((tm, d), lambda i: (i, 0))

    def body(y_ref, t_ref, l_ref, dy_ref):
        i = pl.program_id(0)
        e = y_ref[...] - t_ref[...]
        dy_ref[...] = e * (1.0 / d)
        part = 0.5 * jnp.sum(jnp.mean(e * e, -1, keepdims=True), 0, keepdims=True)

        @pl.when(i == 0)
        def _():
            l_ref[...] = part

        @pl.when(i > 0)
        def _():
            l_ref[...] += part

    return pl.pallas_call(
        body, grid=(s // tm,), in_specs=[row, row],
        out_specs=[pl.BlockSpec((1, 1), lambda i: (0, 0)), row],
        out_shape=[jax.ShapeDtypeStruct((1, 1), F32), jax.ShapeDtypeStruct((s, d), F32)],
        name="loss_head", compiler_params=_params("arbitrary"))(y, t)


def _rope_tables(s):
    half = MLA_ROPE // 2
    inv = ROPE_THETA ** (-jnp.arange(half, dtype=F32) / half)
    ang = jnp.arange(s).astype(F32)[:, None] * inv[None, :]
    cos, sin = jnp.cos(ang), jnp.sin(ang)
    c = jnp.concatenate([cos, cos, jnp.ones((s, 128 - MLA_ROPE), F32)], 1)
    s1 = jnp.concatenate([-sin, jnp.zeros((s, 128 - half), F32)], 1)
    s2 = jnp.concatenate([jnp.zeros((s, half), F32), sin, jnp.zeros((s, 128 - MLA_ROPE), F32)], 1)
    return c, s1, s2


def _rope(x, c, s1, s2):
    half = MLA_ROPE // 2
    return x * c + pltpu.roll(x, 128 - half, 1) * s1 + pltpu.roll(x, half, 1) * s2


def _rope_t(dy, c, s1, s2):
    half = MLA_ROPE // 2
    return dy * c + pltpu.roll(dy * s1, half, 1) + pltpu.roll(dy * s2, 128 - half, 1)


def _mla_mid_fwd(down, gq, gkv, tabs):
    s, w = down.shape
    ql, kvl = gq.shape[1], gkv.shape[1]
    tm = _tile(s, 256)

    def body(d_ref, gq_ref, gkv_ref, c_ref, s1_ref, s2_ref, cq_ref, ckv_ref, kr_ref):
        cq = d_ref[:, :ql]
        ckv = d_ref[:, ql:ql + kvl]
        cq_ref[...] = (cq * lax.rsqrt(jnp.mean(cq * cq, -1, keepdims=True) + RMS_EPS)
                       * gq_ref[...]).astype(cq_ref.dtype)
        ckv_ref[...] = (ckv * lax.rsqrt(jnp.mean(ckv * ckv, -1, keepdims=True) + RMS_EPS)
                        * gkv_ref[...]).astype(ckv_ref.dtype)
        kr_ref[...] = _rope(d_ref[:, ql + kvl:], c_ref[...], s1_ref[...], s2_ref[...]).astype(kr_ref.dtype)

    tab = pl.BlockSpec((tm, 128), lambda i: (i, 0))
    return pl.pallas_call(
        body, grid=(s // tm,),
        in_specs=[pl.BlockSpec((tm, w), lambda i: (i, 0)), pl.BlockSpec((1, ql), lambda i: (0, 0)),
                  pl.BlockSpec((1, kvl), lambda i: (0, 0)), tab, tab, tab],
        out_specs=[pl.BlockSpec((tm, ql), lambda i: (i, 0)), pl.BlockSpec((tm, kvl), lambda i: (i, 0)), tab],
        out_shape=[jax.ShapeDtypeStruct((s, ql), MXU_DTYPE), jax.ShapeDtypeStruct((s, kvl), MXU_DTYPE),
                   jax.ShapeDtypeStruct((s, 128), MXU_DTYPE)],
        name="mla_mid_fwd", compiler_params=_params("parallel"))(down, gq, gkv, *tabs)


def _mla_mid_bwd(down, dcq, dckv, dkr, gq, gkv, tabs):
    s, w = down.shape
    ql, kvl = gq.shape[1], gkv.shape[1]
    tm = _tile(s, 256)

    def rms_bwd(x, dy, g):
        n = x.shape[1]
        r = lax.rsqrt(jnp.mean(x * x, -1, keepdims=True) + RMS_EPS)
        dyg = dy * g
        dx = r * dyg - x * (r * r * r * (1.0 / n)) * jnp.sum(dyg * x, -1, keepdims=True)
        return dx, jnp.sum(dy * x * r, 0, keepdims=True)

    def body(d_ref, dcq_ref, dckv_ref, dkr_ref, gq_ref, gkv_ref, c_ref, s1_ref, s2_ref, o_ref, dgq_ref, dgkv_ref):
        i = pl.program_id(0)
        dxq, pq = rms_bwd(d_ref[:, :ql], dcq_ref[...], gq_ref[...])
        dxkv, pkv = rms_bwd(d_ref[:, ql:ql + kvl], dckv_ref[...], gkv_ref[...])
        o_ref[:, :ql] = dxq.astype(o_ref.dtype)
        o_ref[:, ql:ql + kvl] = dxkv.astype(o_ref.dtype)
        o_ref[:, ql + kvl:] = _rope_t(dkr_ref[...], c_ref[...], s1_ref[...], s2_ref[...]).astype(o_ref.dtype)

        @pl.when(i == 0)
        def _():
            dgq_ref[...] = pq
            dgkv_ref[...] = pkv

        @pl.when(i > 0)
        def _():
            dgq_ref[...] += pq
            dgkv_ref[...] += pkv

    tab = pl.BlockSpec((tm, 128), lambda i: (i, 0))
    vq = pl.BlockSpec((1, ql), lambda i: (0, 0))
    vkv = pl.BlockSpec((1, kvl), lambda i: (0, 0))
    full = pl.BlockSpec((tm, w), lambda i: (i, 0))
    return pl.pallas_call(
        body, grid=(s // tm,),
        in_specs=[full, pl.BlockSpec((tm, ql), lambda i: (i, 0)), pl.BlockSpec((tm, kvl), lambda i: (i, 0)), tab,
                  vq, vkv, tab, tab, tab],
        out_specs=[full, vq, vkv],
        out_shape=[jax.ShapeDtypeStruct((s, w), MXU_DTYPE), jax.ShapeDtypeStruct((1, ql), F32),
                   jax.ShapeDtypeStruct((1, kvl), F32)],
        name="mla_mid_bwd", compiler_params=_params("arbitrary"))(down, dcq, dckv, dkr, gq, gkv, *tabs)


def _iota2():
    return (lax.broadcasted_iota(jnp.int32, (TQ, TQ), 0), lax.broadcasted_iota(jnp.int32, (TQ, TQ), 1))


def _mla_attn_fwd(name, q, kv, kr, tabs, comm=None):
    s = q.shape[0]
    nq = s // TQ
    scale = MLA_QK_DIM ** -0.5

    def body(q_ref, kv_ref, kr_ref, c_ref, s1_ref, s2_ref, o_ref, omx_ref, lse_ref):
        i = pl.program_id(1)
        row, col = _iota2()
        tabs_i = (c_ref[...], s1_ref[...], s2_ref[...])
        qn = [q_ref[:, lo].astype(MXU_DTYPE) for lo, _ in PAD_COLS]
        qr = [_rope(q_ref[:, hi], *tabs_i).astype(MXU_DTYPE) for _, hi in PAD_COLS]
        qc = jnp.right_shift(i * TQ + row, CHUNK_SHIFT)

        def step(kb, carry):
            ks = pl.multiple_of(kb * TQ, TQ)
            krb = kr_ref[pl.ds(ks, TQ), :].astype(MXU_DTYPE)
            mask = jnp.right_shift(ks + col, CHUNK_SHIFT) <= qc
            out = []
            for j, ((lo, hi), (m, l, acc)) in enumerate(zip(PAD_COLS, carry)):
                sc = (_dot(qn[j], kv_ref[pl.ds(ks, TQ), lo].astype(MXU_DTYPE), NT) + _dot(qr[j], krb, NT)) * scale
                sc = jnp.where(mask, sc, NEG)
                m_new = jnp.maximum(m, jnp.max(sc, -1, keepdims=True))
                p = jnp.exp(sc - m_new)
                corr = jnp.exp(m - m_new)
                l = corr * l + jnp.sum(p, -1, keepdims=True)
                acc = corr * acc + _dot(p.astype(MXU_DTYPE), kv_ref[pl.ds(ks, TQ), hi].astype(MXU_DTYPE))
                out.append((m_new, l, acc))
            return tuple(out)

        res = lax.fori_loop(0, i + 1, step,
                            tuple((jnp.full((TQ, 1), NEG, F32), jnp.zeros((TQ, 1), F32),
                                   jnp.zeros((TQ, HEAD_DIM), F32)) for _ in HEAD_COLS))
        for j, (hd, (m, l, acc)) in enumerate(zip(HEAD_COLS, res)):
            o = acc / l
            o_ref[:, hd] = o
            omx_ref[:, hd] = o.astype(omx_ref.dtype)
            lse_ref[j] = m + jnp.log(l)

    tab = pl.BlockSpec((TQ, 128), lambda h, i: (i, 0))
    oblk = pl.BlockSpec((TQ, HEADS_PER_STEP * HEAD_DIM), lambda h, i: (i, h))
    return _call(
        body, (q, kv, kr, *tabs), grid=(HEADS // HEADS_PER_STEP, nq),
        in_specs=[pl.BlockSpec((TQ, HEADS_PER_STEP * HEAD_PAD), lambda h, i: (i, h)),
                  pl.BlockSpec((s, HEADS_PER_STEP * HEAD_PAD), lambda h, i: (0, h)),
                  pl.BlockSpec((s, 128), lambda h, i: (0, 0)), tab, tab, tab],
        out_specs=[oblk, oblk, pl.BlockSpec((HEADS_PER_STEP, TQ, 1), lambda h, i: (h, i, 0))],
        out_shape=[jax.ShapeDtypeStruct((s, HEADS * HEAD_DIM), F32),
                   jax.ShapeDtypeStruct((s, HEADS * HEAD_DIM), MXU_DTYPE), jax.ShapeDtypeStruct((HEADS, s, 1), F32)],
        name=name, semantics=("parallel", "parallel"), comm=comm)


def _mla_attn_bwd(name, q, kv, kr, tabs, do, o, lse, comm=None):
    s = q.shape[0]
    nq = s // TQ
    scale = MLA_QK_DIM ** -0.5

    def body(q_ref, kv_ref, kr_ref, c_ref, s1_ref, s2_ref, do_ref, o_ref, lse_ref,
             dq_ref, dkv_ref, dkr_ref, dkv_acc, dkr_acc):
        h, i = pl.program_id(0), pl.program_id(1)
        row, col = _iota2()

        @pl.when(i == 0)
        def _():
            dkv_acc[...] = jnp.zeros_like(dkv_acc)

        @pl.when((h == 0) & (i == 0))
        def _():
            dkr_acc[...] = jnp.zeros_like(dkr_acc)

        tabs_i = (c_ref[...], s1_ref[...], s2_ref[...])
        qn = [q_ref[:, lo].astype(MXU_DTYPE) for lo, _ in PAD_COLS]
        qr = [_rope(q_ref[:, hi], *tabs_i).astype(MXU_DTYPE) for _, hi in PAD_COLS]
        qc = jnp.right_shift(i * TQ + row, CHUNK_SHIFT)
        delta = [jnp.sum(do_ref[:, hd] * o_ref[:, hd], -1, keepdims=True) for hd in HEAD_COLS]
        lse_ = [lse_ref[j] for j in range(HEADS_PER_STEP)]
        dob = [do_ref[:, hd].astype(MXU_DTYPE) for hd in HEAD_COLS]

        def step(kb, carry):
            ks = pl.multiple_of(kb * TQ, TQ)
            krb = kr_ref[pl.ds(ks, TQ), :].astype(MXU_DTYPE)
            mask = jnp.right_shift(ks + col, CHUNK_SHIFT) <= qc
            out = []
            for j, ((lo, hi), (dqn, dqr)) in enumerate(zip(PAD_COLS, carry)):
                kn = kv_ref[pl.ds(ks, TQ), lo].astype(MXU_DTYPE)
                v = kv_ref[pl.ds(ks, TQ), hi].astype(MXU_DTYPE)
                sc = (_dot(qn[j], kn, NT) + _dot(qr[j], krb, NT)) * scale
                sc = jnp.where(mask, sc, NEG)
                p = jnp.exp(sc - lse_[j])
                ds = (p * (_dot(dob[j], v, NT) - delta[j]) * scale).astype(MXU_DTYPE)
                dkv_acc[pl.ds(ks, TQ), lo] += _dot(ds, qn[j], TN)
                dkv_acc[pl.ds(ks, TQ), hi] += _dot(p.astype(MXU_DTYPE), dob[j], TN)
                dkr_acc[pl.ds(ks, TQ), :] += _dot(ds, qr[j], TN)
                out.append((dqn + _dot(ds, kn), dqr + _dot(ds, krb)))
            return tuple(out)

        res = lax.fori_loop(0, i + 1, step,
                            tuple((jnp.zeros((TQ, HEAD_DIM), F32), jnp.zeros((TQ, 128), F32)) for _ in HEAD_COLS))
        for (lo, hi), (dqn, dqr) in zip(PAD_COLS, res):
            dq_ref[:, lo] = dqn.astype(dq_ref.dtype)
            dq_ref[:, hi] = _rope_t(dqr, *tabs_i).astype(dq_ref.dtype)

        @pl.when(i == nq - 1)
        def _():
            dkv_ref[...] = dkv_acc[...].astype(dkv_ref.dtype)

        @pl.when((h == HEADS // HEADS_PER_STEP - 1) & (i == nq - 1))
        def _():
            dkr_ref[...] = dkr_acc[...]

    tab = pl.BlockSpec((TQ, 128), lambda h, i: (i, 0))
    qblk = pl.BlockSpec((TQ, HEADS_PER_STEP * HEAD_PAD), lambda h, i: (i, h))
    oblk = pl.BlockSpec((TQ, HEADS_PER_STEP * HEAD_DIM), lambda h, i: (i, h))
    kvblk = pl.BlockSpec((s, HEADS_PER_STEP * HEAD_PAD), lambda h, i: (0, h))
    return _call(
        body, (q, kv, kr, *tabs, do, o, lse), grid=(HEADS // HEADS_PER_STEP, nq),
        in_specs=[qblk, kvblk, pl.BlockSpec((s, 128), lambda h, i: (0, 0)), tab, tab, tab, oblk, oblk,
                  pl.BlockSpec((HEADS_PER_STEP, TQ, 1), lambda h, i: (h, i, 0))],
        out_specs=[qblk, kvblk, pl.BlockSpec((s, 128), lambda h, i: (0, 0))],
        out_shape=[jax.ShapeDtypeStruct((s, HEADS * HEAD_PAD), MXU_DTYPE), jax.ShapeDtypeStruct((s, HEADS * HEAD_PAD), MXU_DTYPE),
                   jax.ShapeDtypeStruct((s, 128), F32)],
        scratch_shapes=[pltpu.VMEM((s, HEADS_PER_STEP * HEAD_PAD), F32), pltpu.VMEM((s, 128), F32)],
        name=name, semantics=("arbitrary", "arbitrary"), comm=comm)


HEADS_PER_STEP = 2
HEAD_COLS = [slice(j * HEAD_DIM, (j + 1) * HEAD_DIM) for j in range(HEADS_PER_STEP)]
PAD_COLS = [(slice(j * HEAD_PAD, j * HEAD_PAD + HEAD_DIM), slice(j * HEAD_PAD + HEAD_DIM, (j + 1) * HEAD_PAD))
            for j in range(HEADS_PER_STEP)]


def _qkv_specs(s):
    groups, width = HEADS // HEADS_PER_STEP, HEADS_PER_STEP * HEAD_DIM
    return [pl.BlockSpec((TQ, width), lambda h, i: (i, h)),
            pl.BlockSpec((s, width), lambda h, i: (0, groups + h)),
            pl.BlockSpec((s, width), lambda h, i: (0, 2 * groups + h))]


def _sb_terms(z):
    sp = jnp.log(1.0 + jnp.exp(-jnp.abs(z)))
    return jnp.minimum(z, 0.0) - sp, jnp.minimum(-z, 0.0) - sp


def _sb_attn_fwd(name, qkv, comm=None):
    s = qkv.shape[0]
    nq = s // TQ
    scale = HEAD_DIM ** -0.5

    def body(q_ref, k_ref, v_ref, o_ref):
        i = pl.program_id(1)
        row, col = _iota2()
        after = (row > col).astype(BF16)
        q = [q_ref[:, hd].astype(MXU_DTYPE) for hd in HEAD_COLS]
        qpos = i * TQ + row

        def step(n, carry):
            ks = pl.multiple_of((i - n) * TQ, TQ)
            strict = (ks + col) < qpos
            out = []
            for hd, qh, (tail, acc) in zip(HEAD_COLS, q, carry):
                z = _dot(qh, k_ref[pl.ds(ks, TQ), hd].astype(MXU_DTYPE), NT) * scale
                lb, l1 = _sb_terms(z)
                l1 = jnp.where(strict, l1, 0.0)
                a = jnp.where(strict, jnp.exp(lb + tail + _exact_dot(l1, after)), 0.0)
                acc = acc + _dot(a.astype(MXU_DTYPE), v_ref[pl.ds(ks, TQ), hd].astype(MXU_DTYPE))
                out.append((tail + jnp.sum(l1, -1, keepdims=True), acc))
            return tuple(out)

        res = lax.fori_loop(0, i + 1, step,
                            tuple((jnp.zeros((TQ, 1), F32), jnp.zeros((TQ, HEAD_DIM), F32)) for _ in HEAD_COLS))
        for hd, (_, acc) in zip(HEAD_COLS, res):
            o_ref[:, hd] = acc.astype(o_ref.dtype)

    return _call(
        body, (qkv, qkv, qkv), grid=(HEADS // HEADS_PER_STEP, nq), in_specs=_qkv_specs(s),
        out_specs=[pl.BlockSpec((TQ, HEADS_PER_STEP * HEAD_DIM), lambda h, i: (i, h))],
        out_shape=[jax.ShapeDtypeStruct((s, HEADS * HEAD_DIM), MXU_DTYPE)],
        name=name, semantics=("parallel", "parallel"), comm=comm)[0]


def _sb_attn_bwd(name, qkv, do, comm=None):
    s = qkv.shape[0]
    nq = s // TQ
    scale = HEAD_DIM ** -0.5

    def body(q_ref, k_ref, v_ref, do_ref, dq_ref, dk_ref, dv_ref, a_buf, dk_acc, dv_acc):
        i = pl.program_id(1)
        row, col = _iota2()
        after = (row > col).astype(BF16)
        before = (row < col).astype(BF16)

        @pl.when(i == 0)
        def _():
            dk_acc[...] = jnp.zeros_like(dk_acc)
            dv_acc[...] = jnp.zeros_like(dv_acc)

        q = [q_ref[:, hd].astype(MXU_DTYPE) for hd in HEAD_COLS]
        dob = [do_ref[:, hd].astype(MXU_DTYPE) for hd in HEAD_COLS]
        qpos = i * TQ + row

        def weights(n, tails):
            kb = i - n
            ks = pl.multiple_of(kb * TQ, TQ)
            strict = (ks + col) < qpos
            out = []
            for j, (hd, tail) in enumerate(zip(HEAD_COLS, tails)):
                z = _dot(q[j], k_ref[pl.ds(ks, TQ), hd].astype(MXU_DTYPE), NT) * scale
                lb, l1 = _sb_terms(z)
                l1 = jnp.where(strict, l1, 0.0)
                a = jnp.where(strict, jnp.exp(lb + tail + _exact_dot(l1, after)), 0.0)
                a_buf[j, kb] = a
                dv_acc[pl.ds(ks, TQ), hd] += _dot(a.astype(MXU_DTYPE), dob[j], TN)
                out.append(tail + jnp.sum(l1, -1, keepdims=True))
            return tuple(out)

        lax.fori_loop(0, i + 1, weights, tuple(jnp.zeros((TQ, 1), F32) for _ in HEAD_COLS))

        def grads(kb, carry):
            ks = pl.multiple_of(kb * TQ, TQ)
            strict = (ks + col) < qpos
            out = []
            for j, (hd, (head, dq)) in enumerate(zip(HEAD_COLS, carry)):
                k = k_ref[pl.ds(ks, TQ), hd].astype(MXU_DTYPE)
                z = _dot(q[j], k, NT) * scale
                e = jnp.exp(-jnp.abs(z))
                beta = jnp.where(z >= 0.0, 1.0, e) / (1.0 + e)
                w = _dot(dob[j], v_ref[pl.ds(ks, TQ), hd].astype(MXU_DTYPE), NT) * a_buf[j, kb]
                dz = jnp.where(strict, w * (1.0 - beta) - beta * (head + _exact_dot(w, before)), 0.0) * scale
                dzb = dz.astype(MXU_DTYPE)
                dk_acc[pl.ds(ks, TQ), hd] += _dot(dzb, q[j], TN)
                out.append((head + jnp.sum(w, -1, keepdims=True), dq + _dot(dzb, k)))
            return tuple(out)

        res = lax.fori_loop(0, i + 1, grads,
                            tuple((jnp.zeros((TQ, 1), F32), jnp.zeros((TQ, HEAD_DIM), F32)) for _ in HEAD_COLS))
        for hd, (_, dq) in zip(HEAD_COLS, res):
            dq_ref[:, hd] = dq.astype(dq_ref.dtype)

        @pl.when(i == nq - 1)
        def _():
            dk_ref[...] = dk_acc[...].astype(dk_ref.dtype)
            dv_ref[...] = dv_acc[...].astype(dv_ref.dtype)

    width = HEADS_PER_STEP * HEAD_DIM
    blk = pl.BlockSpec((TQ, width), lambda h, i: (i, h))
    col_h = pl.BlockSpec((s, width), lambda h, i: (0, h))
    shp = jax.ShapeDtypeStruct((s, HEADS * HEAD_DIM), MXU_DTYPE)
    return _call(
        body, (qkv, qkv, qkv, do), grid=(HEADS // HEADS_PER_STEP, nq), in_specs=_qkv_specs(s) + [blk],
        out_specs=[blk, col_h, col_h], out_shape=[shp, shp, shp],
        scratch_shapes=[pltpu.VMEM((HEADS_PER_STEP, nq, TQ, TQ), F32), pltpu.VMEM((s, width), F32),
                        pltpu.VMEM((s, width), F32)],
        name=name, semantics=("arbitrary", "arbitrary"), comm=comm)


LANES = 128
LAST_REL = (1 << CHUNK_SHIFT) - 1


def _ca_subtiles():
    nb = TQ // LANES
    return [(d, a, b, -d * TQ + (b - a) * LANES) for d in range(CA_TILES) for a in range(nb) for b in range(nb)]


def _ca_bias_tiles(rel_bias):
    table = jnp.pad(jnp.transpose(rel_bias), ((0, 0), (0, 2 * LANES - REL_TABLE)))[:, None]

    def unskew(x, row):
        for b in range(7):
            x = jnp.where((jnp.right_shift(row, b) & 1) == 1, pltpu.roll(x, 1 << b, 1), x)
        return x

    def body(t_ref, o_ref):
        row = lax.broadcasted_iota(jnp.int32, (LANES, LANES), 0)
        col = lax.broadcasted_iota(jnp.int32, (LANES, LANES), 1)
        lane = col[:1]
        lo, hi = t_ref[:, :LANES], t_ref[:, LANES:]
        first = jnp.sum(jnp.where(lane == 0, lo, 0.0), -1, keepdims=True)
        last = jnp.sum(jnp.where(lane == LAST_REL, hi, 0.0), -1, keepdims=True)
        hi = jnp.where(lane <= LAST_REL, hi, last)
        r_lo = unskew(jnp.broadcast_to(lo, (LANES, LANES)), row)
        r_hi = unskew(jnp.broadcast_to(hi, (LANES, LANES)), row)
        upper = col >= row
        for d, a, b, o in _ca_subtiles():
            if o >= LANES:
                piece = jnp.broadcast_to(last, (LANES, LANES))
            elif o == 0:
                piece = jnp.where(upper, r_hi, r_lo)
            elif o == -LANES:
                piece = jnp.where(upper, r_lo, first)
            else:
                piece = jnp.broadcast_to(first, (LANES, LANES))
            o_ref[d, a * LANES:(a + 1) * LANES, b * LANES:(b + 1) * LANES] = piece

    return pl.pallas_call(
        body, grid=(HEADS,), in_specs=[pl.BlockSpec((None, 1, 2 * LANES), lambda h: (h, 0, 0))],
        out_specs=pl.BlockSpec((None, CA_TILES, TQ, TQ), lambda h: (h, 0, 0, 0)),
        out_shape=jax.ShapeDtypeStruct((HEADS, CA_TILES, TQ, TQ), F32),
        name="ca_bias_tiles", compiler_params=_params("parallel"))(table)


def _ca_mask(i, ks, row, col):
    qc = jnp.right_shift(i * TQ + row, CHUNK_SHIFT)
    kc = jnp.right_shift(ks + col, CHUNK_SHIFT)
    return (kc <= qc) & (kc >= qc - CA_LEFT_CHUNKS)


def _ca_attn_fwd(name, qkv, tiles, comm=None):
    s = qkv.shape[0]
    nq = s // TQ
    scale = HEAD_DIM ** -0.5

    def body(q_ref, k_ref, v_ref, bt_ref, o_ref, omx_ref, lse_ref):
        i = pl.program_id(1)
        row, col = _iota2()
        q = [q_ref[:, hd].astype(MXU_DTYPE) for hd in HEAD_COLS]

        def step(kb, carry):
            ks = pl.multiple_of(kb * TQ, TQ)
            mask = _ca_mask(i, ks, row, col)
            t = jnp.minimum(i - kb, CA_TILES - 1)
            out = []
            for j, (hd, (m, l, acc)) in enumerate(zip(HEAD_COLS, carry)):
                sc = _dot(q[j], k_ref[pl.ds(ks, TQ), hd].astype(MXU_DTYPE), NT) * scale + bt_ref[j, t]
                sc = jnp.where(mask, sc, NEG)
                m_new = jnp.maximum(m, jnp.max(sc, -1, keepdims=True))
                p = jnp.exp(sc - m_new)
                corr = jnp.exp(m - m_new)
                l = corr * l + jnp.sum(p, -1, keepdims=True)
                acc = corr * acc + _dot(p.astype(MXU_DTYPE), v_ref[pl.ds(ks, TQ), hd].astype(MXU_DTYPE))
                out.append((m_new, l, acc))
            return tuple(out)

        res = lax.fori_loop(jnp.maximum(i - CA_LEFT_BLOCKS, 0), i + 1, step,
                            tuple((jnp.full((TQ, 1), NEG, F32), jnp.zeros((TQ, 1), F32),
                                   jnp.zeros((TQ, HEAD_DIM), F32)) for _ in HEAD_COLS))
        for j, (hd, (m, l, acc)) in enumerate(zip(HEAD_COLS, res)):
            o = acc / l
            o_ref[:, hd] = o
            omx_ref[:, hd] = o.astype(omx_ref.dtype)
            lse_ref[j] = m + jnp.log(l)

    oblk = pl.BlockSpec((TQ, HEADS_PER_STEP * HEAD_DIM), lambda h, i: (i, h))
    return _call(
        body, (qkv, qkv, qkv, tiles), grid=(HEADS // HEADS_PER_STEP, nq),
        in_specs=_qkv_specs(s) + [pl.BlockSpec((HEADS_PER_STEP, CA_TILES, TQ, TQ), lambda h, i: (h, 0, 0, 0))],
        out_specs=[oblk, oblk, pl.BlockSpec((HEADS_PER_STEP, TQ, 1), lambda h, i: (h, i, 0))],
        out_shape=[jax.ShapeDtypeStruct((s, HEADS * HEAD_DIM), F32),
                   jax.ShapeDtypeStruct((s, HEADS * HEAD_DIM), MXU_DTYPE), jax.ShapeDtypeStruct((HEADS, s, 1), F32)],
        name=name, semantics=("parallel", "parallel"), comm=comm)


def _ca_attn_bwd(name, qkv, tiles, do, o, lse, comm=None):
    s = qkv.shape[0]
    nq = s // TQ
    scale = HEAD_DIM ** -0.5

    def body(q_ref, k_ref, v_ref, bt_ref, do_ref, o_ref, lse_ref, dq_ref, dk_ref, dv_ref, dbt_ref, dk_acc, dv_acc):
        i = pl.program_id(1)
        row, col = _iota2()

        @pl.when(i == 0)
        def _():
            dk_acc[...] = jnp.zeros_like(dk_acc)
            dv_acc[...] = jnp.zeros_like(dv_acc)
            dbt_ref[...] = jnp.zeros_like(dbt_ref)

        q = [q_ref[:, hd].astype(MXU_DTYPE) for hd in HEAD_COLS]
        delta = [jnp.sum(do_ref[:, hd] * o_ref[:, hd], -1, keepdims=True) for hd in HEAD_COLS]
        lse_ = [lse_ref[j] for j in range(HEADS_PER_STEP)]
        dob = [do_ref[:, hd].astype(MXU_DTYPE) for hd in HEAD_COLS]

        def step(kb, dqs):
            ks = pl.multiple_of(kb * TQ, TQ)
            mask = _ca_mask(i, ks, row, col)
            t = jnp.minimum(i - kb, CA_TILES - 1)
            out = []
            for j, (hd, dq) in enumerate(zip(HEAD_COLS, dqs)):
                k = k_ref[pl.ds(ks, TQ), hd].astype(MXU_DTYPE)
                v = v_ref[pl.ds(ks, TQ), hd].astype(MXU_DTYPE)
                sc = _dot(q[j], k, NT) * scale + bt_ref[j, t]
                sc = jnp.where(mask, sc, NEG)
                p = jnp.exp(sc - lse_[j])
                dsc = p * (_dot(dob[j], v, NT) - delta[j])
                dbt_ref[j, t] += dsc
                ds = (dsc * scale).astype(MXU_DTYPE)
                dk_acc[pl.ds(ks, TQ), hd] += _dot(ds, q[j], TN)
                dv_acc[pl.ds(ks, TQ), hd] += _dot(p.astype(MXU_DTYPE), dob[j], TN)
                out.append(dq + _dot(ds, k))
            return tuple(out)

        dqs = lax.fori_loop(jnp.maximum(i - CA_LEFT_BLOCKS, 0), i + 1, step,
                            tuple(jnp.zeros((TQ, HEAD_DIM), F32) for _ in HEAD_COLS))
        for hd, dq in zip(HEAD_COLS, dqs):
            dq_ref[:, hd] = dq.astype(dq_ref.dtype)

        @pl.when(i == nq - 1)
        def _():
            dk_ref[...] = dk_acc[...].astype(dk_ref.dtype)
            dv_ref[...] = dv_acc[...].astype(dv_ref.dtype)

    width = HEADS_PER_STEP * HEAD_DIM
    blk = pl.BlockSpec((TQ, width), lambda h, i: (i, h))
    col_h = pl.BlockSpec((s, width), lambda h, i: (0, h))
    tile = pl.BlockSpec((HEADS_PER_STEP, CA_TILES, TQ, TQ), lambda h, i: (h, 0, 0, 0))
    shp = jax.ShapeDtypeStruct((s, HEADS * HEAD_DIM), MXU_DTYPE)
    return _call(
        body, (qkv, qkv, qkv, tiles, do, o, lse), grid=(HEADS // HEADS_PER_STEP, nq),
        in_specs=_qkv_specs(s) + [tile, blk, blk, pl.BlockSpec((HEADS_PER_STEP, TQ, 1), lambda h, i: (h, i, 0))],
        out_specs=[blk, col_h, col_h, tile],
        out_shape=[shp, shp, shp, jax.ShapeDtypeStruct((HEADS, CA_TILES, TQ, TQ), F32)],
        scratch_shapes=[pltpu.VMEM((s, width), F32), pltpu.VMEM((s, width), F32)],
        name=name, semantics=("arbitrary", "arbitrary"), comm=comm)


def _ca_table_grad(dtiles):
    def skew(x, row):
        for b in range(7):
            x = jnp.where((jnp.right_shift(row, b) & 1) == 1, pltpu.roll(x, LANES - (1 << b), 1), x)
        return x

    def body(t_ref, o_ref):
        row = lax.broadcasted_iota(jnp.int32, (LANES, LANES), 0)
        col = lax.broadcasted_iota(jnp.int32, (LANES, LANES), 1)
        wrapped = (row + col) >= LANES
        lane = col[:1]
        total = lambda x: jnp.sum(jnp.sum(x, 0, keepdims=True), -1, keepdims=True)
        lo = jnp.zeros((1, LANES), F32)
        hi = jnp.zeros((1, LANES), F32)
        for d, a, b, o in _ca_subtiles():
            x = t_ref[d, a * LANES:(a + 1) * LANES, b * LANES:(b + 1) * LANES]
            if o >= LANES:
                hi = hi + jnp.where(lane == LAST_REL, total(x), 0.0)
            elif o <= -2 * LANES:
                lo = lo + jnp.where(lane == 0, total(x), 0.0)
            else:
                y = skew(x, row)
                pos = jnp.sum(jnp.where(wrapped, 0.0, y), 0, keepdims=True)
                neg = jnp.sum(jnp.where(wrapped, y, 0.0), 0, keepdims=True)
                if o == 0:
                    clipped = jnp.sum(jnp.where(lane > LAST_REL, pos, 0.0), -1, keepdims=True)
                    hi = hi + jnp.where(lane <= LAST_REL, pos, 0.0) + jnp.where(lane == LAST_REL, clipped, 0.0)
                    lo = lo + neg
                else:
                    lo = lo + pos + jnp.where(lane == 0, jnp.sum(neg, -1, keepdims=True), 0.0)
        o_ref[:, :LANES] = lo
        o_ref[:, LANES:] = hi

    return pl.pallas_call(
        body, grid=(HEADS,), in_specs=[pl.BlockSpec((None, CA_TILES, TQ, TQ), lambda h: (h, 0, 0, 0))],
        out_specs=pl.BlockSpec((None, 1, 2 * LANES), lambda h: (h, 0, 0)),
        out_shape=jax.ShapeDtypeStruct((HEADS, 1, 2 * LANES), F32),
        name="ca_table_grad", compiler_params=_params("parallel"))(dtiles)


def _row_tile(rows, cols):
    if rows % 128:
        return rows
    tr = 128
    while rows % (2 * tr) == 0 and 2 * tr * cols * 4 <= (1 << 20):
        tr *= 2
    return tr


def _adamw(name, w, g, m, v):
    rows, cols = w.shape
    tr = _row_tile(rows, cols)
    blk = pl.BlockSpec((tr, cols), lambda i: (i, 0))

    def body(w_ref, g_ref, m_ref, v_ref, d_ref, m2_ref, v2_ref):
        g_ = g_ref[...]
        m2 = ADAM_B1 * m_ref[...] + (1.0 - ADAM_B1) * g_
        v2 = ADAM_B2 * v_ref[...] + (1.0 - ADAM_B2) * jnp.square(g_)
        m_hat = m2 / (1.0 - ADAM_B1 ** ADAM_STEP)
        v_hat = v2 / (1.0 - ADAM_B2 ** ADAM_STEP)
        d_ref[...] = -ADAM_LR * (m_hat / (jnp.sqrt(v_hat) + ADAM_EPS) + ADAM_WD * w_ref[...])
        m2_ref[...] = m2
        v2_ref[...] = v2

    shp = jax.ShapeDtypeStruct((rows, cols), F32)
    return pl.pallas_call(body, grid=(rows // tr,), in_specs=[blk] * 4, out_specs=[blk] * 3, out_shape=[shp] * 3,
                          name=name, compiler_params=_params("parallel"))(w, g, m, v)


def _pair_sum(name, g, r1, c, lo, n):
    _, rh, cols = r1.shape
    tr = _row_tile(n, cols)
    nb, off, half = n // tr, lo // tr, rh // tr

    def body(c_ref, g_ref, r_ref, o_ref):
        o_ref[...] = (g_ref[...].astype(F32) + r_ref[...].astype(F32)).astype(o_ref.dtype)

    return pl.pallas_call(
        body, grid_spec=pltpu.PrefetchScalarGridSpec(
            num_scalar_prefetch=1, grid=(N_CHIPS, nb),
            in_specs=[pl.BlockSpec((None, tr, cols), lambda k, i, c_ref: (k, c_ref[0] * half + off + i, 0)),
                      pl.BlockSpec((None, tr, cols), lambda k, i, c_ref: (k, off + i, 0))],
            out_specs=pl.BlockSpec((None, tr, cols), lambda k, i, c_ref: (k, i, 0))),
        out_shape=jax.ShapeDtypeStruct((N_CHIPS, n, cols), WIRE_DTYPE), name=name,
        compiler_params=_params("parallel", "parallel"))(c, g, r1)


def _chip_sum(name, a1, r2, place, total, start, rh, total_rows):
    _, n, cols = a1.shape
    tr = _row_tile(n, cols)
    nb, off, half = n // tr, start // tr, rh // tr

    def body(p_ref, a_ref, r_ref, *rest):
        rest[-1][...] = ((a_ref[...].astype(F32) + r_ref[0].astype(F32)) + r_ref[1].astype(F32)) + r_ref[2].astype(F32)

    return pl.pallas_call(
        body, grid_spec=pltpu.PrefetchScalarGridSpec(
            num_scalar_prefetch=1, grid=(nb,),
            in_specs=[pl.BlockSpec((None, tr, cols), lambda i, p_ref: (p_ref[0], i, 0)),
                      pl.BlockSpec((N_CHIPS - 1, tr, cols), lambda i, p_ref: (0, i, 0))] + ([] if total is None else [ANY]),
            out_specs=pl.BlockSpec((tr, cols), lambda i, p_ref: (off + p_ref[1] * half + i, 0))),
        out_shape=jax.ShapeDtypeStruct((total_rows, cols), F32), name=name,
        input_output_aliases={} if total is None else {3: 0},
        compiler_params=_params("parallel"))(place, a1, r2, *([] if total is None else [total]))


def _cast_place(name, w, chip, l, rows):
    cols = w.shape[1]
    tr = _row_tile(rows, cols)
    nb = rows // tr

    def body(c_ref, w_ref, o_ref):
        o_ref[...] = w_ref[...].astype(o_ref.dtype)

    return pl.pallas_call(
        body, grid_spec=pltpu.PrefetchScalarGridSpec(
            num_scalar_prefetch=1, grid=(nb,),
            in_specs=[pl.BlockSpec((tr, cols), lambda i, c_ref: (l * nb + i, 0))],
            out_specs=pl.BlockSpec((None, tr, cols), lambda i, c_ref: (c_ref[0], i, 0))),
        out_shape=jax.ShapeDtypeStruct((N_CHIPS, rows, cols), MXU_DTYPE), name=name,
        compiler_params=_params("parallel"))(chip, w)


def _place():
    x, y, c = lax.axis_index("x"), lax.axis_index("y"), lax.axis_index("c")
    chips = [(1 - x, y), (x, 1 - y), (1 - x, 1 - y)]
    return x, y, c, chips


def _remote(src, dst, send_sem, recv_sem, to):
    return pltpu.make_async_remote_copy(src_ref=src, dst_ref=dst, send_sem=send_sem, recv_sem=recv_sem,
                                        device_id=to, device_id_type=MESH)


def _in_place(bufs):
    return [jax.ShapeDtypeStruct(b.shape, b.dtype) for b in bufs], {t: t for t in range(len(bufs))}


def _gather(bufs, jobs):
    def copies(srcs, dsts, send, recv):
        x, y, c, chips = _place()
        out = []

        def push(piece, to):
            out.append(_remote(piece, piece, send.at[len(out)], recv.at[len(out)], to))

        for buf, todo in zip(dsts, jobs):
            rh = buf.shape[1] // 2
            for phase, lo, n in todo:
                if phase == 0:
                    for px, py in chips[:2]:
                        push(buf.at[2 * x + y, pl.ds(c * rh + lo, n)], (px, py, c))
                elif phase == 1:
                    push(buf.at[2 * x + (1 - y), pl.ds(c * rh + lo, n // 2)], (1 - x, y, c))
                    push(buf.at[2 * (1 - x) + y, pl.ds(c * rh + lo + n // 2, n // 2)], (x, 1 - y, c))
                else:
                    for px, py in chips:
                        push(buf.at[2 * px + py, pl.ds(c * rh + lo, n)], (x, y, 1 - c))
        return out

    return _Comm(bufs, *_in_place(bufs), copies, sum((2, 2, 3)[phase] for todo in jobs for phase, _, _ in todo))


def _pieces(rh, cols, itemsize):
    us = 3.0 * rh * cols * itemsize / ICI_BYTES_PER_US
    k = max(1, int(round(us / CHUNK_US)))
    while rh % (32 * k):
        k -= 1
        if k <= 1:
            k = 1
            break
    return [(j * (rh // k), rh // k) for j in range(k)], us / k


class _GatherStream:
    def __init__(self, weights, order, dims):
        self.weights, self.order, self.state, self.us, self.credit = weights, order, {}, {}, 0.0
        for key in order:
            _, rows, cols = dims[key[0]]
            cuts, self.us[key] = _pieces(rows // 2, cols, jnp.dtype(MXU_DTYPE).itemsize)
            self.state[key] = [[lo, n, 0] for lo, n in cuts]

    def _comm(self, picks):
        if not picks:
            return None
        keys = list(dict.fromkeys(key for key, _ in picks))
        jobs = [[(self.state[k][j][2], *self.state[k][j][:2]) for k, j in picks if k == key] for key in keys]
        comm = _gather([self.weights[key] for key in keys], jobs)

        def then(out):
            self.weights.update(zip(keys, out))
            for k, j in picks:
                self.state[k][j][2] += 1

        comm.then = then
        return comm

    def carry(self, us):
        todo = [(key, j) for key in self.order for j, piece in enumerate(self.state[key]) if piece[2] < 3]
        picks = [p for p in todo if self.state[p[0]][p[1]][2] > 0]
        self.credit += LINK_SHARE * us - sum(self.us[key] / 4 for key, j in picks if self.state[key][j][2] == 1)
        for key, j in todo:
            if self.state[key][j][2] == 0:
                if self.credit < self.us[key] / 4:
                    break
                picks.append((key, j))
                self.credit -= self.us[key] / 2
        return self._comm(picks)

    def require(self, keys, tag):
        for step in range(3):
            picks = [(key, j) for key in keys for j, piece in enumerate(self.state[key]) if piece[2] < 3]
            if picks:
                _exchange(f"gather_{tag}_{step}", self._comm(picks))
                self.credit = 0.0


def _reduce_pair(grads):
    def copies(srcs, dsts, send, recv):
        x, y, c, _ = _place()
        out = []
        for t, g in enumerate(srcs):
            rh = g.shape[1] // 2
            out.append(_remote(g.at[:, pl.ds((1 - c) * rh, rh)], dsts[t], send.at[t], recv.at[t], (x, y, 1 - c)))
        return out

    return _Comm(grads, [jax.ShapeDtypeStruct((N_CHIPS, g.shape[1] // 2, g.shape[2]), g.dtype) for g in grads], {},
                 copies, len(grads))


def _reduce_chips(parts):
    def copies(srcs, dsts, send, recv):
        x, y, c, chips = _place()
        return [_remote(p.at[2 * px + py], dsts[t].at[j], send.at[3 * t + j], recv.at[3 * t + j], (px, py, c))
                for t, p in enumerate(srcs) for j, (px, py) in enumerate(chips)]

    return _Comm(parts, [jax.ShapeDtypeStruct((N_CHIPS - 1, *p.shape[1:]), p.dtype) for p in parts], {}, copies,
                 3 * len(parts))


def _reduce_share(totals, spans):
    def copies(srcs, dsts, send, recv):
        x, y, c, _ = _place()
        out = []
        for t, (buf, (start, rh)) in enumerate(zip(dsts, spans)):
            mine = buf.at[pl.ds(start + c * rh, rh)]
            out.append(_remote(mine, mine, send.at[t], recv.at[t], (x, y, 1 - c)))
        return out

    return _Comm(totals, *_in_place(totals), copies, len(totals))


def _exchange(name, comm):
    ns, nd = len(comm.srcs), len(comm.dsts)

    def body(*refs):
        send, recv = refs[ns + nd:]
        cps = comm.copies(refs[:ns], refs[ns:ns + nd], send, recv)
        for cp in cps:
            cp.start()
        for cp in cps:
            cp.wait()

    comm.deliver(pl.pallas_call(
        body, in_specs=[ANY] * ns, out_specs=[ANY] * nd, out_shape=comm.dsts,
        scratch_shapes=[pltpu.SemaphoreType.DMA((comm.n,)), pltpu.SemaphoreType.DMA((comm.n,))],
        input_output_aliases=comm.alias, name=name)(*comm.srcs))
    return comm.out


class _ReduceStream:
    def __init__(self, dims, place):
        self.dims, self.place = dims, place
        self.parts, self.pieces, self.left, self.ready, self.total = {}, [], {}, [], {}

    def add(self, key, part):
        self.parts[key] = part

    def _pair(self):
        keys = list(self.parts)
        if not keys:
            return None
        parts = [self.parts.pop(key) for key in keys]
        comm = _reduce_pair(parts)

        def then(out):
            for (name, l), g, r1 in zip(keys, parts, out):
                _, rows, cols = self.dims[name]
                cuts, us = _pieces(rows // 2, cols, jnp.dtype(WIRE_DTYPE).itemsize)
                self.left[name, l] = len(cuts)
                for lo, n in cuts:
                    piece = _pair_sum(f"pair_sum_{name}_{l}_{lo}", g, r1, self.place[1:], lo, n)
                    self.pieces.append(((name, l), lo, piece, us))

        comm.then = then
        return comm

    def _chips(self, us):
        picks, spent = [], 0.0
        while self.pieces and spent + 0.5 * self.pieces[0][3] <= us:
            picks.append(self.pieces.pop(0))
            spent += picks[-1][3]
        if not picks:
            return None
        comm = _reduce_chips([piece for _, _, piece, _ in picks])

        def then(out):
            for ((name, l), lo, piece, _), r2 in zip(picks, out):
                layers, rows, _ = self.dims[name]
                self.total[name] = _chip_sum(f"chip_sum_{name}_{l}_{lo}", piece, r2, self.place, self.total.get(name),
                                             l * rows + lo, rows // 2, layers * rows)
                self.left[name, l] -= 1
                if not self.left[name, l]:
                    self.ready.append((name, l))

        comm.then = then
        return comm

    def _share(self):
        keys = []
        for key in self.ready:
            if key[0] not in [name for name, _ in keys]:
                keys.append(key)
        if not keys:
            return None
        self.ready = [key for key in self.ready if key not in keys]
        spans = [(l * self.dims[name][1], self.dims[name][1] // 2) for name, l in keys]
        comm = _reduce_share([self.total[name] for name, _ in keys], spans)

        def then(out):
            self.total.update(zip([name for name, _ in keys], out))

        comm.then = then
        return comm

    def carry(self, us):
        return _merged(self._pair(), self._share(), self._chips(us))

    def finish(self):
        step = 0
        while self.parts or self.pieces or self.ready:
            _exchange(f"grad_reduce_tail_{step}", self.carry(float("inf")))
            step += 1
        return self.total


def _all_reduce_small(name, pack):
    rows, cols = pack.shape

    def body(p_ref, o_ref, slots, send, recv):
        x, y, c, _ = _place()
        me = 4 * x + 2 * y + c
        slots[me] = p_ref[...]
        cps = []
        for r in range(1, N_DEV):
            to = ((1 - x) if r & 4 else x, (1 - y) if r & 2 else y, (1 - c) if r & 1 else c)
            cp = _remote(p_ref, slots.at[me], send.at[r - 1], recv.at[r - 1], to)
            cp.start()
            cps.append(cp)
        for cp in cps:
            cp.wait()
        acc = slots[0]
        for d in range(1, N_DEV):
            acc = acc + slots[d]
        o_ref[...] = acc

    return pl.pallas_call(
        body, in_specs=[VMEM_SPEC], out_specs=VMEM_SPEC, out_shape=jax.ShapeDtypeStruct((rows, cols), F32),
        scratch_shapes=[pltpu.VMEM((N_DEV, rows, cols), F32), pltpu.SemaphoreType.DMA((N_DEV - 1,)),
                        pltpu.SemaphoreType.DMA((N_DEV - 1,))],
        name=name)(pack)


BIG = ("ffn_w_in", "ffn_w_out", "mla_w_down", "mla_w_uq", "mla_w_ukv", "mla_w_o", "sb_w_qkv", "sb_w_o", "ca_w_qkv",
       "ca_w_o")
COL_SHARDED = {"ffn_w_in": True, "ffn_w_out": False, "mla_w_down": False, "mla_w_uq": True, "mla_w_ukv": True,
               "mla_w_o": False, "sb_w_qkv": True, "sb_w_o": False, "ca_w_qkv": True, "ca_w_o": False}
WEIGHTS = ("ln_mix_g", "ln_mix_b", "ln_ffn_g", "ln_ffn_b", "ffn_w_in", "ffn_w_out", "mla_w_down", "mla_q_norm_g",
           "mla_w_uq", "mla_kv_norm_g", "mla_w_ukv", "mla_w_o", "sb_w_qkv", "sb_w_o", "ca_w_qkv", "ca_rel_bias",
           "ca_w_o")
HEADS_PER_CHIP = HEADS // N_CHIPS


def _mxu_shards(w):
    down = w["mla_w_down"]
    uq = w["mla_w_uq"]
    n, ql = uq.shape[:2]
    lane_pad = 128 - MLA_ROPE
    shaped = dict(w)
    shaped["mla_w_down"] = jnp.pad(down, ((0, 0), (0, 0), (0, lane_pad)))
    shaped["mla_w_uq"] = jnp.pad(uq.reshape(n, ql, HEADS_PER_CHIP, MLA_QK_DIM),
                                 ((0, 0), (0, 0), (0, 0), (0, HEAD_PAD - MLA_QK_DIM))).reshape(n, ql, -1)
    return {k: shaped[k].reshape(-1, shaped[k].shape[-1]) for k in BIG}


def _unpad_grad(name, g, like):
    if name == "mla_w_down":
        g = g[:, :like.shape[-1]]
    elif name == "mla_w_uq":
        g = g.reshape(g.shape[0], HEADS_PER_CHIP, HEAD_PAD)[:, :, :MLA_QK_DIM]
    return g.reshape(like.shape)


class _Lazy(dict):
    def __init__(self, make):
        super().__init__()
        self.make = make

    def __missing__(self, key):
        self[key] = self.make(key)
        return self[key]


def _mixer_keys(i):
    kind, slot = i % 3, i // 3
    if kind == 0:
        return [(k, slot) for k in ("mla_w_down", "mla_w_uq", "mla_w_ukv", "mla_w_o")]
    pre = "sb" if kind == 1 else "ca"
    return [(f"{pre}_w_qkv", slot), (f"{pre}_w_o", slot)]


def _ffn_keys(i):
    return [("ffn_w_in", i), ("ffn_w_out", i)]


def _step(x, target, wl, dims, p, tabs, place):
    dist = place is not None
    order = [key for i in range(DEPTH) for key in _mixer_keys(i) + _ffn_keys(i)]
    gather = _GatherStream(wl, order, dims) if dist else None
    fetch = gather.carry if dist else None

    def weight(key, user):
        if dist:
            gather.require([key], user)
        return wl[key]

    if dist:
        gather.require(_mixer_keys(0), "first")
    saved = []
    h = hm = x
    for i in range(DEPTH):
        kind, slot = i % 3, i // 3
        sv = {"h0m": hm}
        mix = _mixer_keys(i)
        if kind == 0:
            down = _fwd_row(f"mla_down_{i}", hm, weight(mix[0], f"mla_down_{i}"), F32, comm=fetch)
            gq, gkv = p["mla_q_norm_g"][slot][None], p["mla_kv_norm_g"][slot][None]
            cq, ckv, kr = _mla_mid_fwd(down, gq, gkv, tabs)
            q = _fwd_col(f"mla_uq_{i}", cq, weight(mix[1], f"mla_uq_{i}"), F32, comm=fetch)
            kv = _fwd_col(f"mla_ukv_{i}", ckv, weight(mix[2], f"mla_ukv_{i}"), MXU_DTYPE, comm=fetch)
            o, om, lse = _mla_attn_fwd(f"mla_attn_fwd_{i}", q, kv, kr, tabs, comm=fetch)
            sv.update(down=down, gq=gq, gkv=gkv, cq=cq, ckv=ckv, kr=kr, q=q, kv=kv, o=o, lse=lse)
        else:
            pre = "sb" if kind == 1 else "ca"
            qkv = _fwd_col(f"{pre}_qkv_{i}", hm, weight(mix[0], f"{pre}_qkv_{i}"), MXU_DTYPE, comm=fetch)
            if kind == 1:
                om = _sb_attn_fwd(f"sb_attn_fwd_{i}", qkv, comm=fetch)
            else:
                tiles = _ca_bias_tiles(p["ca_rel_bias"][slot])
                o, om, lse = _ca_attn_fwd(f"ca_attn_fwd_{i}", qkv, tiles, comm=fetch)
                sv.update(tiles=tiles, lse=lse, o=o)
            sv.update(qkv=qkv)
        m = _fwd_row(f"mixer_o_{i}", om, weight(mix[-1], f"mixer_o_{i}"), F32, comm=fetch)
        h1, h1m, sv["xh1"], sv["r1"] = _ln_fwd(f"ln_mix_{i}", h, m, p["ln_mix_g"][i][None], p["ln_mix_b"][i][None])
        u, a = _fwd_col(f"ffn_in_{i}", h1m, weight(("ffn_w_in", i), f"ffn_in_{i}"), None, epilogue=_epi_relu2,
                        dtypes=(F32, MXU_DTYPE), comm=fetch)
        y = _fwd_row(f"ffn_out_{i}", a, weight(("ffn_w_out", i), f"ffn_out_{i}"), F32, comm=fetch)
        h, hm, sv["xh2"], sv["r2"] = _ln_fwd(f"ln_ffn_{i}", h1, y, p["ln_ffn_g"][i][None], p["ln_ffn_b"][i][None])
        sv.update(om=om, h1m=h1m, u=u, a=a)
        saved.append(sv)

    reduction = _ReduceStream(dims, place) if dist else None
    carry = reduction.carry if dist else None
    part = {}

    def dw(key, xin, dyin):
        _, rows, cols = dims[key[0]]
        g = _dw(f"dw_{key[0]}_{key[1]}", xin, dyin, rows, cols, COL_SHARDED[key[0]], comm=carry)
        if dist:
            reduction.add(key, g)
        else:
            part[key] = g

    loss, dy = _loss_head(h, target)
    small = {k: [None] * DEPTH for k in ("ln_mix_g", "ln_mix_b", "ln_ffn_g", "ln_ffn_b")}
    n_mla = p["mla_q_norm_g"].shape[0]
    small["mla_q_norm_g"], small["mla_kv_norm_g"] = [None] * n_mla, [None] * n_mla
    for i in reversed(range(DEPTH)):
        kind, slot = i % 3, i // 3
        sv = saved[i]
        mix = _mixer_keys(i)
        dz, dzm, small["ln_ffn_g"][i], small["ln_ffn_b"][i] = _ln_bwd(f"ln_ffn_bwd_{i}", dy, sv["xh2"], sv["r2"],
                                                                      p["ln_ffn_g"][i][None])
        dw(("ffn_w_out", i), sv["a"], dzm)
        du = _dx_row(f"ffn_du_{i}", dzm, wl["ffn_w_out", i], MXU_DTYPE, epilogue=_epi_drelu2, extra=sv["u"], comm=carry)
        dw(("ffn_w_in", i), sv["h1m"], du)
        dy = _dx_col(f"ffn_dh_{i}", du, wl["ffn_w_in", i], F32, epilogue=_epi_residual, extra=dz, comm=carry)
        dz, dzm, small["ln_mix_g"][i], small["ln_mix_b"][i] = _ln_bwd(f"ln_mix_bwd_{i}", dy, sv["xh1"], sv["r1"],
                                                                      p["ln_mix_g"][i][None])
        dw(mix[-1], sv["om"], dzm)
        do = _dx_row(f"mixer_do_{i}", dzm, wl[mix[-1]], F32, comm=carry)
        last = carry if i else None
        if kind == 0:
            dq, dkv, dkr = _mla_attn_bwd(f"mla_attn_bwd_{i}", sv["q"], sv["kv"], sv["kr"], tabs, do, sv["o"],
                                         sv["lse"], comm=carry)
            dw(("mla_w_uq", slot), sv["cq"], dq)
            dcq = _dx_col(f"mla_dcq_{i}", dq, wl["mla_w_uq", slot], F32, comm=carry)
            dw(("mla_w_ukv", slot), sv["ckv"], dkv)
            dckv = _dx_col(f"mla_dckv_{i}", dkv, wl["mla_w_ukv", slot], F32, comm=carry)
            ddown, small["mla_q_norm_g"][slot], small["mla_kv_norm_g"][slot] = _mla_mid_bwd(
                sv["down"], dcq, dckv, dkr, sv["gq"], sv["gkv"], tabs)
            dw(("mla_w_down", slot), sv["h0m"], ddown)
            dy = _dx_row(f"mla_dh_{i}", ddown, wl["mla_w_down", slot], F32, epilogue=_epi_residual, extra=dz,
                         comm=last)
        else:
            pre = "sb" if kind == 1 else "ca"
            if kind == 1:
                dq, dk, dv = _sb_attn_bwd(f"sb_attn_bwd_{i}", sv["qkv"], do, comm=carry)
            else:
                dq, dk, dv, dtiles = _ca_attn_bwd(f"ca_attn_bwd_{i}", sv["qkv"], sv["tiles"], do, sv["o"], sv["lse"],
                                                  comm=carry)
                small["ca_rel_bias"] = [jnp.transpose(_ca_table_grad(dtiles)[:, 0, :REL_TABLE])]
            dqkv = jnp.concatenate([dq, dk, dv], 1)
            dw((f"{pre}_w_qkv", slot), sv["h0m"], dqkv)
            dy = _dx_col(f"{pre}_dh_{i}", dqkv, wl[f"{pre}_w_qkv", slot], F32, epilogue=_epi_residual, extra=dz,
                         comm=last)
    small = {k: jnp.stack([g.reshape(g.shape[-2:]) if k == "ca_rel_bias" else g[0] for g in v]) for k, v in small.items()}
    return loss, dy, (reduction.finish() if dist else part), small


SMALL = ("ln_mix_g", "ln_mix_b", "ln_ffn_g", "ln_ffn_b", "mla_q_norm_g", "mla_kv_norm_g", "ca_rel_bias")


def _pack_small(parts, width):
    flat = jnp.concatenate([parts[k].reshape(-1) for k in SMALL])
    rows = -(-flat.shape[0] // width)
    rows += -rows % 8
    return jnp.pad(flat, (0, rows * width - flat.shape[0])).reshape(rows, width)


def _unpack_small(pack, like):
    flat, out, at = pack.reshape(-1), {}, 0
    for k in SMALL:
        n = int(np.prod(like[k].shape))
        out[k] = flat[at:at + n].reshape(like[k].shape)
        at += n
    return out


def kernel(x, ln_mix_g, ln_mix_b, ln_ffn_g, ln_ffn_b, ffn_w_in, ffn_w_out, mla_w_down, mla_q_norm_g, mla_w_uq, mla_kv_norm_g, mla_w_ukv, mla_w_o, sb_w_qkv, sb_w_o, ca_w_qkv, ca_rel_bias, ca_w_o, loss_target, m_ln_mix_g, m_ln_mix_b, m_ln_ffn_g, m_ln_ffn_b, m_ffn_w_in, m_ffn_w_out, m_mla_w_down, m_mla_q_norm_g, m_mla_w_uq, m_mla_kv_norm_g, m_mla_w_ukv, m_mla_w_o, m_sb_w_qkv, m_sb_w_o, m_ca_w_qkv, m_ca_rel_bias, m_ca_w_o, v_ln_mix_g, v_ln_mix_b, v_ln_ffn_g, v_ln_ffn_b, v_ffn_w_in, v_ffn_w_out, v_mla_w_down, v_mla_q_norm_g, v_mla_w_uq, v_mla_kv_norm_g, v_mla_w_ukv, v_mla_w_o, v_sb_w_qkv, v_sb_w_o, v_ca_w_qkv, v_ca_rel_bias, v_ca_w_o):
    w = dict(zip(WEIGHTS, (ln_mix_g, ln_mix_b, ln_ffn_g, ln_ffn_b, ffn_w_in, ffn_w_out, mla_w_down, mla_q_norm_g,
                           mla_w_uq, mla_kv_norm_g, mla_w_ukv, mla_w_o, sb_w_qkv, sb_w_o, ca_w_qkv, ca_rel_bias,
                           ca_w_o)))
    mom1 = dict(zip(WEIGHTS, (m_ln_mix_g, m_ln_mix_b, m_ln_ffn_g, m_ln_ffn_b, m_ffn_w_in, m_ffn_w_out, m_mla_w_down,
                              m_mla_q_norm_g, m_mla_w_uq, m_mla_kv_norm_g, m_mla_w_ukv, m_mla_w_o, m_sb_w_qkv,
                              m_sb_w_o, m_ca_w_qkv, m_ca_rel_bias, m_ca_w_o)))
    mom2 = dict(zip(WEIGHTS, (v_ln_mix_g, v_ln_mix_b, v_ln_ffn_g, v_ln_ffn_b, v_ffn_w_in, v_ffn_w_out, v_mla_w_down,
                              v_mla_q_norm_g, v_mla_w_uq, v_mla_kv_norm_g, v_mla_w_ukv, v_mla_w_o, v_sb_w_qkv,
                              v_sb_w_o, v_ca_w_qkv, v_ca_rel_bias, v_ca_w_o)))
    xi, yi, ci = lax.axis_index("x"), lax.axis_index("y"), lax.axis_index("c")
    chip = 2 * xi + yi
    d_model = x.shape[-1]

    shards = _mxu_shards(w)
    layers = {k: w[k].shape[0] for k in BIG}
    dims = {k: (layers[k], shards[k].shape[0] // layers[k], shards[k].shape[1]) for k in BIG}
    chip1 = jnp.reshape(chip, (1,)).astype(jnp.int32)
    wl = _Lazy(lambda key: _cast_place(f"cast_{key[0]}_{key[1]}", shards[key[0]], chip1, key[1], dims[key[0]][1]))
    gains = jnp.stack([w["mla_q_norm_g"], w["mla_kv_norm_g"]])
    gains = jnp.where(ci == 0, gains, 0.0)
    placed = lax.dynamic_update_slice_in_dim(jnp.zeros((*gains.shape[:2], N_CHIPS, gains.shape[2]), F32),
                                             gains[:, :, None], chip, 2)
    full_gains = _all_reduce_small("norm_gain_gather", placed.reshape(2 * gains.shape[1], -1))
    full_gains = full_gains.reshape(2, gains.shape[1], -1)
    p = {"ln_mix_g": ln_mix_g, "ln_mix_b": ln_mix_b, "ln_ffn_g": ln_ffn_g, "ln_ffn_b": ln_ffn_b,
         "mla_q_norm_g": full_gains[0], "mla_kv_norm_g": full_gains[1], "ca_rel_bias": ca_rel_bias}

    place = jnp.stack([chip, ci]).astype(jnp.int32)
    loss, grad_x, total, small = _step(x[0], loss_target[0], wl, dims, p, _rope_tables(x.shape[1]), place)
    loss = lax.psum(loss[0, 0], ("x", "y", "c"))

    small = _unpack_small(_all_reduce_small("small_grad_all_reduce", _pack_small(small, d_model)), small)
    grad = {k: small[k] for k in ("ln_mix_g", "ln_mix_b", "ln_ffn_g", "ln_ffn_b", "ca_rel_bias")}
    for k in ("mla_q_norm_g", "mla_kv_norm_g"):
        g = small[k].reshape(small[k].shape[0], N_CHIPS, -1)
        grad[k] = lax.dynamic_index_in_dim(g, chip, 1, keepdims=False)

    for k in BIG:
        grad[k] = _unpad_grad(k, total[k], w[k])

    delta, new_m, new_v = {}, {}, {}
    for k in WEIGHTS:
        flat = lambda a: a.reshape(-1, a.shape[-1])
        dl, m2, v2 = _adamw(f"adamw_{k}", flat(w[k]), flat(grad[k]), flat(mom1[k]), flat(mom2[k]))
        delta[k], new_m[k], new_v[k] = dl.reshape(w[k].shape), m2.reshape(w[k].shape), v2.reshape(w[k].shape)
    return (loss, grad_x[None], *[grad[k] for k in WEIGHTS], *[delta[k] for k in WEIGHTS],
            *[new_m[k] for k in WEIGHTS], *[new_v[k] for k in WEIGHTS])
```
